```python
import math
import jax
import jax.numpy as jnp
from jax import lax
import numpy as np

D_MODEL = 1024
BATCH = 4
SEQ = 8192
DEPTH = 1

GRID_W = 64
CTX_LEN = 256
N_MOD = 6
EPS = 1e-6

SSM_WIDTH = D_MODEL // 2
SSM_GROUP = 16
SSM_GROUPS = SSM_WIDTH // SSM_GROUP
SSM_STATE = 64
STEP_MIN = 1e-3
STEP_MAX = 1e-1

HEAD_DIM = 64
N_HEADS = (D_MODEL // 2) // HEAD_DIM
N_KV_HEADS = N_HEADS // 4
GQA_GROUP = N_HEADS // N_KV_HEADS
ATTN_WIDTH = N_HEADS * HEAD_DIM
KV_WIDTH = N_KV_HEADS * HEAD_DIM
ROPE_THETA = 10000.0
Q_BLOCK = 128

COL_K = SSM_WIDTH
COL_V = COL_K + KV_WIDTH
COL_Q = COL_V + KV_WIDTH
COL_GATE = COL_Q + ATTN_WIDTH
D_IN = COL_GATE + 2 * D_MODEL

N_EXPERTS = 32
TOP_K = 4
D_EXPERT = D_MODEL
SWIGLU_ALPHA = 1.702
SWIGLU_LIMIT = 7.0
MOE_BLOCK = 256

kernel_name = 'hybrid_s5_gqa_moe_dit_block'


def _silu(x):
    return x * jax.nn.sigmoid(x)


def _rmsnorm(x, g):
    xf = x.astype(jnp.float32)
    y = xf * lax.rsqrt(jnp.mean(xf * xf, axis=-1, keepdims=True) + EPS)
    return (y * g.astype(jnp.float32)).astype(x.dtype)


def _modulate(h, shift, scale):
    return h * (1 + scale) + shift


def _scan_combine(left, right):
    a_l, b_l = left
    a_r, b_r = right
    return a_r * a_l, a_r * b_l + b_r


def _diag_scan(lbar, bu, h0, reverse):
    if h0 is not None:
        edge = -1 if reverse else 0
        bu = bu.at[:, edge].add(lbar * h0)
    a = jnp.broadcast_to(lbar, (1,) + bu.shape[1:])
    _, h = lax.associative_scan(_scan_combine, (a, bu), reverse=reverse, axis=1)
    return h


def _s5_discretise(lam_re, lam_im, log_step, b_re, b_im):
    f32 = jnp.float32
    lam = lax.complex(lam_re.astype(f32), lam_im.astype(f32))
    step = jnp.exp(log_step.astype(f32))[:, None]
    lbar = jnp.exp(lam * step)
    bmat = lax.complex(b_re.astype(f32), b_im.astype(f32))
    bbar = ((lbar - 1) / lam)[..., None] * bmat
    return lbar, bbar


def _s5_readout(h, cmat):
    b_, l_ = h.shape[:2]
    return jnp.einsum('blgn,gpn->blgp', h, cmat).real.reshape(b_, l_, SSM_WIDTH)


def _s5_mixer(u, u_c, lam_re, lam_im, log_step, b_re, b_im, c_re, c_im, d_skip, with_ctx_out):
    f32 = jnp.float32
    b_, l_, _ = u.shape
    lc = u_c.shape[1]
    uf = u.astype(f32)
    ucf = u_c.astype(f32)
    ug = uf.reshape(b_, l_, SSM_GROUPS, SSM_GROUP)
    ucg = ucf.reshape(b_, lc, SSM_GROUPS, SSM_GROUP)
    d = d_skip.astype(f32)
    y = uf * d
    y_c = ucf * d if with_ctx_out else None
    for direction, reverse in enumerate((False, True)):
        lbar, bbar = _s5_discretise(lam_re[direction], lam_im[direction], log_step[direction],
                                    b_re[direction], b_im[direction])
        cmat = lax.complex(c_re[direction].astype(f32), c_im[direction].astype(f32))
        h_c = _diag_scan(lbar, jnp.einsum('blgp,gnp->blgn', ucg, bbar), None, reverse)
        h0 = h_c[:, 0] if reverse else h_c[:, -1]
        h = _diag_scan(lbar, jnp.einsum('blgp,gnp->blgn', ug, bbar), h0, reverse)
        y = y + _s5_readout(h, cmat)
        if with_ctx_out:
            y_c = y_c + _s5_readout(h_c, cmat)
    return y.astype(u.dtype), (y_c.astype(u.dtype) if with_ctx_out else None)


def _axial_rope_tables(row_id, col_id):
    half = HEAD_DIM // 2
    inv_freq = ROPE_THETA ** (-jnp.arange(0, half, 2, dtype=jnp.float32) / half)
    ang = jnp.concatenate([row_id.astype(jnp.float32)[:, None] * inv_freq,
                           col_id.astype(jnp.float32)[:, None] * inv_freq], axis=-1)
    return jnp.cos(ang), jnp.sin(ang)


def _rope(x, cos, sin):
    xf = x.astype(jnp.float32)
    x1, x2 = xf[..., 0::2], xf[..., 1::2]
    cos = cos[None, :, None, :]
    sin = sin[None, :, None, :]
    out = jnp.stack([x1 * cos - x2 * sin, x1 * sin + x2 * cos], axis=-1).reshape(x.shape)
    return out.astype(x.dtype)


def _attention(q, k, v):
    b_, lq = q.shape[:2]
    nb = lq // Q_BLOCK
    scale = HEAD_DIM ** -0.5
    qb = q.reshape(b_, nb, Q_BLOCK, N_KV_HEADS, GQA_GROUP, HEAD_DIM).transpose(1, 0, 2, 3, 4, 5)

    def one_block(qblk):
        s = jnp.einsum('bqkgd,bskd->bkgqs', qblk, k).astype(jnp.float32) * scale
        p = jax.nn.softmax(s, axis=-1).astype(v.dtype)
        return jnp.einsum('bkgqs,bskd->bqkgd', p, v)

    o = lax.map(one_block, qb)
    return o.transpose(1, 0, 2, 3, 4, 5).reshape(b_, lq, ATTN_WIDTH)


def _merge_branches(y_ssm, o_attn, gate_logits, w_glu, b_glu, w_ssm_out, w_attn_out, w_out):
    s = jax.nn.gelu(y_ssm)
    s = s * jax.nn.sigmoid(s @ w_glu + b_glu)
    gates = jax.nn.sigmoid(gate_logits)
    merged = gates[..., :D_MODEL] * (s @ w_ssm_out) + gates[..., D_MODEL:] * (o_attn @ w_attn_out)
    return merged @ w_out


def _moe(h, router_w, router_b, w_gate_up, b_gate_up, w_down, b_down):
    t_ = h.shape[0]
    m_ = t_ * TOP_K
    nb = (m_ + N_EXPERTS * (MOE_BLOCK - 1)) // MOE_BLOCK
    logits = (h @ router_w + router_b).astype(jnp.float32)
    top_vals, top_idx = lax.top_k(logits, TOP_K)
    gates = jax.nn.softmax(top_vals, axis=-1)
    flat_e = top_idx.reshape(-1)
    order = jnp.argsort(flat_e)
    sorted_e = flat_e[order]
    sorted_tok = order // TOP_K
    counts = jnp.bincount(flat_e, length=N_EXPERTS)
    starts = jnp.cumsum(counts) - counts
    blk_counts = (counts + MOE_BLOCK - 1) // MOE_BLOCK
    blk_ends = jnp.cumsum(blk_counts)
    pad_starts = (blk_ends - blk_counts) * MOE_BLOCK
    dest = pad_starts[sorted_e] + (jnp.arange(m_) - starts[sorted_e])
    n_slots = nb * MOE_BLOCK
    slot_tok = jnp.full((n_slots,), t_, jnp.int32).at[dest].set(sorted_tok.astype(jnp.int32))
    slot_gate = jnp.zeros((n_slots,), jnp.float32).at[dest].set(gates.reshape(-1)[order])
    block_expert = jnp.minimum(jnp.searchsorted(blk_ends, jnp.arange(nb), side='right'), N_EXPERTS - 1)
    h_pad = jnp.concatenate([h, jnp.zeros((1, h.shape[1]), h.dtype)], axis=0)
    xb = h_pad[slot_tok].reshape(nb, MOE_BLOCK, h.shape[1])

    def expert_block(args):
        xblk, e = args
        gu = xblk @ w_gate_up[e] + b_gate_up[e]
        gate, up = gu[..., :D_EXPERT], gu[..., D_EXPERT:]
        gate = jnp.minimum(gate, SWIGLU_LIMIT)
        up = jnp.clip(up, -SWIGLU_LIMIT, SWIGLU_LIMIT)
        act = (up + 1) * (gate * jax.nn.sigmoid(SWIGLU_ALPHA * gate))
        return act @ w_down[e] + b_down[e]

    yb = lax.map(expert_block, (xb, block_expert)).reshape(n_slots, -1).astype(jnp.float32)
    out = jnp.zeros((t_ + 1, h.shape[1]), jnp.float32).at[slot_tok].add(yb * slot_gate[:, None])
    return out[:t_].astype(h.dtype)


def setup_inputs(seed: int = 0) -> dict:
    key = jax.random.key(seed)
    ks = jax.random.split(key, 40)
    f32 = jnp.float32

    def nrm(k, shape, scale):
        return jax.random.normal(k, shape, f32) * scale

    def gain(k, shape):
        return 1.0 + 0.05 * jax.random.normal(k, shape, f32)

    n_idx = jnp.arange(SSM_STATE, dtype=f32)
    lam_shape = (DEPTH, 2, SSM_GROUPS, SSM_STATE)
    return {
        'x': nrm(ks[0], (BATCH, SEQ, D_MODEL), 1.0),
        'c': nrm(ks[1], (BATCH, D_MODEL), 1.0),
        'ctx': nrm(ks[2], (BATCH, CTX_LEN, D_MODEL), 1.0),
        'c_ctx': nrm(ks[3], (D_MODEL,), 1.0),
        'w_mod': nrm(ks[4], (DEPTH, D_MODEL, N_MOD * D_MODEL), 0.5 * D_MODEL ** -0.5),
        'b_mod': nrm(ks[5], (DEPTH, N_MOD * D_MODEL), 0.02),
        'norm1_g': gain(ks[6], (DEPTH, D_MODEL)),
        'norm2_g': gain(ks[7], (DEPTH, D_MODEL)),
        'w_in': nrm(ks[8], (DEPTH, D_MODEL, D_IN), D_MODEL ** -0.5),
        's5_lam_re': -0.5 + nrm(ks[9], lam_shape, 0.01),
        's5_lam_im': math.pi * n_idx + nrm(ks[10], lam_shape, 0.01),
        's5_log_step': jax.random.uniform(ks[11], (DEPTH, 2, SSM_GROUPS), f32,
                                          minval=math.log(STEP_MIN), maxval=math.log(STEP_MAX)),
        's5_b_re': nrm(ks[12], (DEPTH, 2, SSM_GROUPS, SSM_STATE, SSM_GROUP), (2 * SSM_GROUP) ** -0.5),
        's5_b_im': nrm(ks[13], (DEPTH, 2, SSM_GROUPS, SSM_STATE, SSM_GROUP), (2 * SSM_GROUP) ** -0.5),
        's5_c_re': nrm(ks[14], (DEPTH, 2, SSM_GROUPS, SSM_GROUP, SSM_STATE), SSM_STATE ** -0.5),
        's5_c_im': nrm(ks[15], (DEPTH, 2, SSM_GROUPS, SSM_GROUP, SSM_STATE), SSM_STATE ** -0.5),
        's5_d': nrm(ks[16], (DEPTH, SSM_WIDTH), 1.0),
        'w_glu': nrm(ks[17], (DEPTH, SSM_WIDTH, SSM_WIDTH), SSM_WIDTH ** -0.5),
        'b_glu': nrm(ks[18], (DEPTH, SSM_WIDTH), 0.02),
        'w_ssm_out': nrm(ks[19], (DEPTH, SSM_WIDTH, D_MODEL), SSM_WIDTH ** -0.5),
        'q_norm_g': gain(ks[20], (DEPTH, HEAD_DIM)),
        'k_norm_g': gain(ks[21], (DEPTH, HEAD_DIM)),
        'w_attn_out': nrm(ks[22], (DEPTH, ATTN_WIDTH, D_MODEL), ATTN_WIDTH ** -0.5),
        'w_out': nrm(ks[23], (DEPTH, D_MODEL, D_MODEL), D_MODEL ** -0.5),
        'router_w': nrm(ks[24], (DEPTH, D_MODEL, N_EXPERTS), D_MODEL ** -0.5),
        'router_b': nrm(ks[25], (DEPTH, N_EXPERTS), 0.01),
        'w_gate_up': nrm(ks[26], (DEPTH, N_EXPERTS, D_MODEL, 2 * D_EXPERT), D_MODEL ** -0.5),
        'b_gate_up': nrm(ks[27], (DEPTH, N_EXPERTS, 2 * D_EXPERT), 0.02),
        'w_down': nrm(ks[28], (DEPTH, N_EXPERTS, D_EXPERT, D_MODEL), D_EXPERT ** -0.5),
        'b_down': nrm(ks[29], (DEPTH, N_EXPERTS, D_MODEL), 0.02),
        'final_norm_g': gain(ks[30], (D_MODEL,)),
    }


def reference(x, c, ctx, c_ctx, w_mod, b_mod, norm1_g, norm2_g, w_in,
              s5_lam_re, s5_lam_im, s5_log_step, s5_b_re, s5_b_im, s5_c_re, s5_c_im, s5_d,
              w_glu, b_glu, w_ssm_out, q_norm_g, k_norm_g, w_attn_out, w_out,
              router_w, router_b, w_gate_up, b_gate_up, w_down, b_down, final_norm_g):
    b_, l_, _ = x.shape
    lc = ctx.shape[1]
    rows = l_ // GRID_W
    row_id = jnp.repeat(jnp.arange(rows, dtype=jnp.int32), GRID_W)
    col_id = jnp.arange(rows * GRID_W, dtype=jnp.int32) % GRID_W
    cos, sin = _axial_rope_tables(row_id, col_id)

    for i in range(DEPTH):
        update_ctx = i < DEPTH - 1
        mod = (_silu(c) @ w_mod[i] + b_mod[i])[:, None, :]
        sh1, sc1, g1, sh2, sc2, g2 = jnp.split(mod, N_MOD, axis=-1)
        mod_c = _silu(c_ctx) @ w_mod[i] + b_mod[i]
        sh1c, sc1c, g1c, sh2c, sc2c, g2c = jnp.split(mod_c, N_MOD, axis=-1)

        h = _modulate(_rmsnorm(x, norm1_g[i]), sh1, sc1)
        hc = _modulate(_rmsnorm(ctx, norm1_g[i]), sh1c, sc1c)
        proj = h @ w_in[i]
        n_ctx_cols = D_IN if update_ctx else COL_Q
        proj_c = hc @ w_in[i][:, :n_ctx_cols]

        y_ssm, y_ssm_c = _s5_mixer(proj[..., :SSM_WIDTH], proj_c[..., :SSM_WIDTH],
                                   s5_lam_re[i], s5_lam_im[i], s5_log_step[i],
                                   s5_b_re[i], s5_b_im[i], s5_c_re[i], s5_c_im[i], s5_d[i],
                                   update_ctx)

        q = proj[..., COL_Q:COL_GATE].reshape(b_, l_, N_HEADS, HEAD_DIM)
        k = proj[..., COL_K:COL_V].reshape(b_, l_, N_KV_HEADS, HEAD_DIM)
        v = proj[..., COL_V:COL_Q].reshape(b_, l_, N_KV_HEADS, HEAD_DIM)
        q = _rope(_rmsnorm(q, q_norm_g[i]), cos, sin)
        k = _rope(_rmsnorm(k, k_norm_g[i]), cos, sin)
        k_c = _rmsnorm(proj_c[..., COL_K:COL_V].reshape(b_, lc, N_KV_HEADS, HEAD_DIM), k_norm_g[i])
        v_c = proj_c[..., COL_V:COL_Q].reshape(b_, lc, N_KV_HEADS, HEAD_DIM)
        k_all = jnp.concatenate([k_c, k], axis=1)
        v_all = jnp.concatenate([v_c, v], axis=1)
        o_attn = _attention(q, k_all, v_all)

        x_mix = x + g1 * _merge_branches(y_ssm, o_attn, proj[..., COL_GATE:],
                                         w_glu[i], b_glu[i], w_ssm_out[i], w_attn_out[i], w_out[i])

        h2 = _modulate(_rmsnorm(x_mix, norm2_g[i]), sh2, sc2).reshape(b_ * l_, D_MODEL)
        if update_ctx:
            q_c = _rmsnorm(proj_c[..., COL_Q:COL_GATE].reshape(b_, lc, N_HEADS, HEAD_DIM), q_norm_g[i])
            o_attn_c = _attention(q_c, k_c, v_c)
            ctx_mix = ctx + g1c * _merge_branches(y_ssm_c, o_attn_c, proj_c[..., COL_GATE:],
                                                  w_glu[i], b_glu[i], w_ssm_out[i], w_attn_out[i], w_out[i])
            h2c = _modulate(_rmsnorm(ctx_mix, norm2_g[i]), sh2c, sc2c).reshape(b_ * lc, D_MODEL)
            f = _moe(jnp.concatenate([h2, h2c], axis=0), router_w[i], router_b[i],
                     w_gate_up[i], b_gate_up[i], w_down[i], b_down[i])
            x = x_mix + g2 * f[:b_ * l_].reshape(x.shape)
            ctx = ctx_mix + g2c * f[b_ * l_:].reshape(ctx.shape)
        else:
            f = _moe(h2, router_w[i], router_b[i], w_gate_up[i], b_gate_up[i], w_down[i], b_down[i])
            x = x_mix + g2 * f.reshape(x.shape)

    return _rmsnorm(x, final_norm_g)
```

```python
import functools
import math

import jax
import jax.numpy as jnp
from jax import lax
from jax.experimental import pallas as pl
from jax.experimental.pallas import tpu as pltpu

F32 = jnp.float32
BF16 = jnp.bfloat16

EPS = 1e-6
GRID_W = 64
N_MOD = 6
SSM_GROUP = 16
SSM_STATE = 64
HEAD_DIM = 64
GQA_GROUP = 4
ROPE_THETA = 10000.0
TOP_K = 4
SWIGLU_ALPHA = 1.702
SWIGLU_LIMIT = 7.0

LANES = 128
S5_CHUNK = 32
EXPERT_BLOCK = 512
VMEM_LIMIT = 56 * 1024 * 1024


def _cparams(sem):
    return pltpu.CompilerParams(dimension_semantics=sem, vmem_limit_bytes=VMEM_LIMIT)


def _dot(a, b):
    return jnp.dot(a, b, preferred_element_type=F32)


def _split(a):
    hi = a.astype(BF16)
    lo = (a - hi.astype(F32)).astype(BF16)
    return hi, lo


def _dot3(a, b):
    ah, al = _split(a)
    bh, bl = _split(b)
    return _dot(ah, bh) + _dot(ah, bl) + _dot(al, bh)


def _mod_kernel(c_ref, w_ref, b_ref, o_ref):
    c = c_ref[...]
    s = c * jax.nn.sigmoid(c)
    o_ref[...] = _dot3(s, w_ref[...]) + b_ref[...]


def _modulation(c8, w_mod, b_mod):
    d = c8.shape[1]
    n = w_mod.shape[1]
    tn = 1024
    return pl.pallas_call(
        _mod_kernel,
        out_shape=jax.ShapeDtypeStruct((8, n), F32),
        grid=(n // tn,),
        in_specs=[pl.BlockSpec((8, d), lambda j: (0, 0)),
                  pl.BlockSpec((d, tn), lambda j: (0, j)),
                  pl.BlockSpec((1, tn), lambda j: (0, j))],
        out_specs=pl.BlockSpec((8, tn), lambda j: (0, j)),
        compiler_params=_cparams(("parallel",)),
        name="mod",
    )(c8, w_mod, b_mod.reshape(1, n))


def _head_norm_rope(z, gain, hsum, cos, sin, scale):
    z2h, z2l = _split(z * z)
    ss = _dot(z2h, hsum) + _dot(z2l, hsum)
    zn = z * lax.rsqrt(ss * (1.0 / HEAD_DIM) + EPS) * gain
    lane = lax.broadcasted_iota(jnp.int32, (1, LANES), 1)
    even = (lane & 1) == 0
    outs = []
    for ci in range(z.shape[1] // LANES):
        ch = zn[:, ci * LANES:(ci + 1) * LANES]
        nxt = pltpu.roll(ch, LANES - 1, axis=1)
        prv = pltpu.roll(ch, 1, axis=1)
        sw = jnp.where(even, nxt, prv)
        outs.append((ch * cos + sw * sin) * scale)
    return outs


def _inproj_kernel(with_q, dm, x_ref, mod_ref, g_ref, w_ref, cos_ref, sin_ref, qg_ref, kg_ref,
                   hsum_ref, *out_refs):
    x = x_ref[0]
    ms = jnp.mean(x * x, axis=-1, keepdims=True)
    y = x * lax.rsqrt(ms + EPS) * g_ref[...]
    sh = mod_ref[0, :, 0:dm]
    sc = mod_ref[0, :, dm:2 * dm]
    hb = (y * (1.0 + sc) + sh).astype(BF16)
    cos = cos_ref[...]
    sin = sin_ref[...]
    wu = dm // 2
    wkv = wu // GQA_GROUP
    c_k, c_v, c_q, c_g = wu, wu + wkv, wu + 2 * wkv, 2 * wu + 2 * wkv
    u_ref, k_ref, v_ref = out_refs[:3]
    u_ref[0] = _dot(hb, w_ref[:, 0:c_k]).astype(BF16)
    kz = _dot(hb, w_ref[:, c_k:c_v])
    (kr,) = _head_norm_rope(kz, kg_ref[...], hsum_ref[0:wkv, 0:wkv], cos, sin, 1.0)
    k_ref[0] = kr.astype(BF16)
    v_ref[0] = _dot(hb, w_ref[:, c_v:c_q]).astype(BF16)
    if with_q:
        q_ref, gl_ref = out_refs[3:]
        qz = _dot(hb, w_ref[:, c_q:c_g])
        qs = _head_norm_rope(qz, qg_ref[...], hsum_ref[...], cos, sin, HEAD_DIM ** -0.5)
        for ci, qc in enumerate(qs):
            q_ref[0, :, ci * LANES:(ci + 1) * LANES] = qc.astype(BF16)
        gl_ref[0] = _dot(hb, w_ref[:, c_g:]).astype(BF16)


def _inproj(x, mod3, mod_row0, norm_g, w_in_b, cos, sin, qg, kg, hsum, with_q, tm):
    b_, l_, dm = x.shape
    wu = dm // 2
    wkv = wu // GQA_GROUP
    n_in = w_in_b.shape[1]
    outs = [jax.ShapeDtypeStruct((b_, l_, wu), BF16),
            jax.ShapeDtypeStruct((b_, l_, wkv), BF16),
            jax.ShapeDtypeStruct((b_, l_, wkv), BF16)]
    ospecs = [pl.BlockSpec((1, tm, wu), lambda b, i: (b, i, 0)),
              pl.BlockSpec((1, tm, wkv), lambda b, i: (b, i, 0)),
              pl.BlockSpec((1, tm, wkv), lambda b, i: (b, i, 0))]
    if with_q:
        outs += [jax.ShapeDtypeStruct((b_, l_, wu), BF16),
                 jax.ShapeDtypeStruct((b_, l_, 2 * dm), BF16)]
        ospecs += [pl.BlockSpec((1, tm, wu), lambda b, i: (b, i, 0)),
                   pl.BlockSpec((1, tm, 2 * dm), lambda b, i: (b, i, 0))]
    if mod_row0 is None:
        mod_map = lambda b, i: (b, 0, 0)
    else:
        mod_map = lambda b, i: (mod_row0, 0, 0)
    return pl.pallas_call(
        functools.partial(_inproj_kernel, with_q, dm),
        out_shape=outs,
        grid=(b_, l_ // tm),
        in_specs=[pl.BlockSpec((1, tm, dm), lambda b, i: (b, i, 0)),
                  pl.BlockSpec((1, 1, N_MOD * dm), mod_map),
                  pl.BlockSpec((1, dm), lambda b, i: (0, 0)),
                  pl.BlockSpec((dm, n_in), lambda b, i: (0, 0)),
                  pl.BlockSpec((tm, LANES), lambda b, i: (i, 0)),
                  pl.BlockSpec((tm, LANES), lambda b, i: (i, 0)),
                  pl.BlockSpec((1, wu), lambda b, i: (0, 0)),
                  pl.BlockSpec((1, wkv), lambda b, i: (0, 0)),
                  pl.BlockSpec((wu, wu), lambda b, i: (0, 0))],
        out_specs=ospecs,
        compiler_params=_cparams(("parallel", "parallel")),
        name="inproj_x" if with_q else "inproj_ctx",
    )(x, mod3, norm_g.reshape(1, dm), w_in_b, cos, sin, qg, kg, hsum)


def _cexp_pow(e, lr, li):
    mag = jnp.exp(e * lr)
    ang = e * li
    return mag * jnp.cos(ang), mag * jnp.sin(ang)


def _cmul(ar, ai, br, bi):
    return ar * br - ai * bi, ar * bi + ai * br


def _s5pre_kernel(lamr_r, lami_r, lst_r, lamr_c, lami_c, lst_c, btr_ref, bti_ref, ctr_ref, cti_ref,
                  d_ref, win_ref, wout_ref, m_ref, dec_ref):
    tc = S5_CHUNK
    half = float(tc // 2)
    rows = tc * SSM_GROUP
    gsh = SSM_GROUP.bit_length() - 1
    nsh = SSM_STATE.bit_length() - 1
    s_idx = (lax.broadcasted_iota(jnp.int32, (rows, 1), 0) >> gsh).astype(F32)
    t_idx = (lax.broadcasted_iota(jnp.int32, (1, rows), 1) >> gsh).astype(F32)
    row_i = lax.broadcasted_iota(jnp.int32, (rows, rows), 0)
    col_i = lax.broadcasted_iota(jnp.int32, (rows, rows), 1)
    lane_g = lax.broadcasted_iota(jnp.int32, (1, LANES), 1) >> nsh
    subl_g = lax.broadcasted_iota(jnp.int32, (LANES, 1), 0) >> nsh
    m_acc = [jnp.where(row_i == col_i, d_ref[0, gl], 0.0) for gl in range(2)]
    for dr in range(2):
        step = jnp.exp(lst_r[dr, 0])
        lam_r, lam_i = lamr_r[dr, 0], lami_r[dr, 0]
        lr, li = lam_r * step, lam_i * step
        lb_r, lb_i = _cexp_pow(1.0, lr, li)
        den = lam_r * lam_r + lam_i * lam_i
        nr, ni = lb_r - 1.0, lb_i
        cf_r = (nr * lam_r + ni * lam_i) / den
        cf_i = (ni * lam_r - nr * lam_i) / den
        bb_r, bb_i = _cmul(cf_r, cf_i, btr_ref[dr, 0], bti_ref[dr, 0])
        e_in = (tc - 1.0 - s_idx) if dr == 0 else s_idx
        p_r, p_i = _cexp_pow(e_in, lr, li)
        wi_r, wi_i = _cmul(bb_r, bb_i, p_r, p_i)
        e_a = (half - s_idx) if dr == 0 else (s_idx - half)
        p_r, p_i = _cexp_pow(e_a, lr, li)
        a_r, a_i = _cmul(bb_r, bb_i, p_r, p_i)
        d_r, d_i = _cexp_pow(float(tc), lr, li)
        dec_ref[0, dr, 0] = d_r
        dec_ref[0, dr, 1] = d_i
        stepc = jnp.exp(lst_c[dr, 0])
        lrc, lic = lamr_c[dr, 0] * stepc, lami_c[dr, 0] * stepc
        c_r, c_i = ctr_ref[dr, 0], cti_ref[dr, 0]
        e_b = (t_idx - half) if dr == 0 else (half - t_idx)
        q_r, q_i = _cexp_pow(e_b, lrc, lic)
        bm_r, bm_i = _cmul(c_r, c_i, q_r, q_i)
        e_o = (t_idx + 1.0) if dr == 0 else (tc - t_idx)
        q_r, q_i = _cexp_pow(e_o, lrc, lic)
        wo_r, wo_i = _cmul(c_r, c_i, q_r, q_i)
        if dr == 0:
            mask = (col_i >> gsh) >= (row_i >> gsh)
        else:
            mask = (row_i >> gsh) >= (col_i >> gsh)
        for gl in range(2):
            lsel = lane_g == gl
            ssel = subl_g == gl
            win_ref[0, gl, 2 * dr] = jnp.where(lsel, wi_r, 0.0).astype(BF16)
            win_ref[0, gl, 2 * dr + 1] = jnp.where(lsel, wi_i, 0.0).astype(BF16)
            wout_ref[0, 2 * dr, :, gl * rows:(gl + 1) * rows] = jnp.where(ssel, wo_r, 0.0).astype(BF16)
            wout_ref[0, 2 * dr + 1, :, gl * rows:(gl + 1) * rows] = jnp.where(ssel, -wo_i, 0.0).astype(BF16)
            kmat = (_dot3(jnp.where(lsel, a_r, 0.0), bm_r) - _dot3(jnp.where(lsel, a_i, 0.0), bm_i))
            m_acc[gl] = m_acc[gl] + jnp.where(mask, kmat, 0.0)
    for gl in range(2):
        m_ref[0, gl] = m_acc[gl].astype(BF16)


def _s5_operators(lam_re, lam_im, log_step, b_re, b_im, c_re, c_im, d_skip):
    g = lam_re.shape[1]
    n, p = SSM_STATE, SSM_GROUP
    npair = g // 2
    tc = S5_CHUNK
    rows = tc * p

    def row_form(a):
        return a.reshape(2, npair, 1, 2 * n)

    def col_form(a):
        return a.reshape(2, npair, 2 * n, 1)

    lst = jnp.broadcast_to(log_step[:, :, None], (2, g, n))
    bt = lambda b: jnp.broadcast_to(
        b.reshape(2, npair, 2, n, p).transpose(0, 1, 4, 2, 3)[:, :, None], (2, npair, tc, p, 2, n)
    ).reshape(2, npair, rows, 2 * n)
    ct = lambda c: jnp.broadcast_to(
        c.reshape(2, npair, 2, p, n).transpose(0, 1, 2, 4, 3)[:, :, :, :, None, :], (2, npair, 2, n, tc, p)
    ).reshape(2, npair, 2 * n, rows)
    d2 = jnp.broadcast_to(d_skip.reshape(npair, 2, 1, 1, p), (npair, 2, 1, tc, p)).reshape(npair, 2, 1, rows)

    def spec4(shape):
        return pl.BlockSpec((2, 1) + shape, lambda i: (0, i, 0, 0))

    return pl.pallas_call(
        _s5pre_kernel,
        out_shape=[jax.ShapeDtypeStruct((npair, 2, 4, rows, LANES), BF16),
                   jax.ShapeDtypeStruct((npair, 4, LANES, 2 * rows), BF16),
                   jax.ShapeDtypeStruct((npair, 2, rows, rows), BF16),
                   jax.ShapeDtypeStruct((npair, 2, 2, 1, LANES), F32)],
        grid=(npair,),
        in_specs=[spec4((1, LANES)), spec4((1, LANES)), spec4((1, LANES)),
                  spec4((LANES, 1)), spec4((LANES, 1)), spec4((LANES, 1)),
                  spec4((rows, LANES)), spec4((rows, LANES)),
                  spec4((LANES, rows)), spec4((LANES, rows)),
                  pl.BlockSpec((1, 2, 1, rows), lambda i: (i, 0, 0, 0))],
        out_specs=[pl.BlockSpec((1, 2, 4, rows, LANES), lambda i: (i, 0, 0, 0, 0)),
                   pl.BlockSpec((1, 4, LANES, 2 * rows), lambda i: (i, 0, 0, 0)),
                   pl.BlockSpec((1, 2, rows, rows), lambda i: (i, 0, 0, 0)),
                   pl.BlockSpec((1, 2, 2, 1, LANES), lambda i: (i, 0, 0, 0, 0))],
        compiler_params=_cparams(("parallel",)),
        name="s5pre",
    )(row_form(lam_re), row_form(lam_im), row_form(lst),
      col_form(lam_re), col_form(lam_im), col_form(lst),
      bt(b_re), bt(b_im), ct(c_re), ct(c_im), d2)


def _s5state_kernel(u_ref, win_ref, *s_refs):
    u0 = u_ref[0, 0]
    u1 = u_ref[1, 0]
    for kind in range(4):
        s_refs[kind][...] = _dot(u0, win_ref[0, 0, kind]) + _dot(u1, win_ref[0, 1, kind])


def _s5_states(uf, win):
    g, b_, j, rows = uf.shape
    npair = g // 2
    out = jax.ShapeDtypeStruct((j, b_ * npair * LANES), F32)
    ospec = pl.BlockSpec((j, LANES), lambda pr, b: (0, b * npair + pr))
    return pl.pallas_call(
        _s5state_kernel,
        out_shape=[out] * 4,
        grid=(npair, b_),
        in_specs=[pl.BlockSpec((2, 1, j, rows), lambda pr, b: (pr, b, 0, 0)),
                  pl.BlockSpec((1, 2, 4, rows, LANES), lambda pr, b: (pr, 0, 0, 0, 0))],
        out_specs=[ospec] * 4,
        compiler_params=_cparams(("parallel", "parallel")),
        name="s5state",
    )(uf, win)


def _s5scan_kernel(jc, sfr, sfi, sbr, sbi, dec_ref, hfr, hfi, hbr, hbi):
    j = sfr.shape[0]
    w = sfr.shape[1]
    zero = jnp.zeros((1, w), F32)

    def run(sr, si, hr_out, hi_out, ar, ai, lo, n, reverse, carry):
        def body(i, hc):
            row = (lo + n - 1 - i) if reverse else (lo + i)
            hr, hi = hc
            hr_out[pl.ds(row, 1), :] = hr
            hi_out[pl.ds(row, 1), :] = hi
            nr = ar * hr - ai * hi + sr[pl.ds(row, 1), :]
            ni = ar * hi + ai * hr + si[pl.ds(row, 1), :]
            return nr, ni
        return lax.fori_loop(0, n, body, carry)

    afr, afi = dec_ref[0, 0], dec_ref[0, 1]
    abr, abi = dec_ref[1, 0], dec_ref[1, 1]
    run(sfr, sfi, hfr, hfi, afr, afi, 0, j, False, (zero, zero))
    c = run(sbr, sbi, hbr, hbi, abr, abi, 0, jc, True, (zero, zero))
    run(sbr, sbi, hbr, hbi, abr, abi, jc, j - jc, True, c)


def _s5_scan(states, dec_cols, jc):
    j, wtot = states[0].shape
    tw = 1024
    spec = pl.BlockSpec((j, tw), lambda i: (0, i))
    return pl.pallas_call(
        functools.partial(_s5scan_kernel, jc),
        out_shape=[jax.ShapeDtypeStruct((j, wtot), F32)] * 4,
        grid=(wtot // tw,),
        in_specs=[spec] * 4 + [pl.BlockSpec((2, 2, 1, tw), lambda i: (0, 0, 0, i))],
        out_specs=[spec] * 4,
        compiler_params=_cparams(("parallel",)),
        name="s5scan",
    )(*states, dec_cols)


def _s5out_kernel(u_ref, hfr, hfi, hbr, hbi, wout_ref, m_ref, y_ref):
    rows = u_ref.shape[3]
    y2 = None
    for kind, h in enumerate((hfr, hfi, hbr, hbi)):
        t = _dot(h[...].astype(BF16), wout_ref[0, kind])
        y2 = t if y2 is None else y2 + t
    for gl in range(2):
        y = y2[:, gl * rows:(gl + 1) * rows] + _dot(u_ref[gl, 0], m_ref[0, gl])
        y_ref[gl, 0] = y.astype(BF16)


def _s5_outputs(uf, hprev, wout, m):
    g, b_, j, rows = uf.shape
    npair = g // 2
    hspec = pl.BlockSpec((j, LANES), lambda pr, b: (0, b * npair + pr))
    return pl.pallas_call(
        _s5out_kernel,
        out_shape=jax.ShapeDtypeStruct((g, b_, j, rows), BF16),
        grid=(npair, b_),
        in_specs=[pl.BlockSpec((2, 1, j, rows), lambda pr, b: (pr, b, 0, 0))] + [hspec] * 4 +
                 [pl.BlockSpec((1, 4, LANES, 2 * rows), lambda pr, b: (pr, 0, 0, 0)),
                  pl.BlockSpec((1, 2, rows, rows), lambda pr, b: (pr, 0, 0, 0))],
        out_specs=pl.BlockSpec((2, 1, j, rows), lambda pr, b: (pr, b, 0, 0)),
        compiler_params=_cparams(("parallel", "parallel")),
        name="s5out",
    )(uf, *hprev, wout, m)


def _s5_mixer(u, u_c, ops):
    win, wout, m, dec = ops
    b_, l_, w = u.shape
    lc = u_c.shape[1]
    tc, p = S5_CHUNK, SSM_GROUP
    g = w // p
    npair = g // 2
    jc, jl = lc // tc, l_ // tc
    j = jc + jl
    ua = jnp.concatenate([u_c, u], axis=1)
    uf = ua.reshape(b_, j, tc, g, p).transpose(3, 0, 1, 2, 4).reshape(g, b_, j, tc * p)
    states = _s5_states(uf, win)
    dec_cols = jnp.broadcast_to(dec.transpose(1, 2, 3, 0, 4)[:, :, :, None], (2, 2, 1, b_, npair, LANES))
    dec_cols = dec_cols.reshape(2, 2, 1, b_ * npair * LANES)
    hprev = _s5_scan(states, dec_cols, jc)
    yf = _s5_outputs(uf, hprev, wout, m)
    y = yf[:, :, jc:].reshape(g, b_, jl, tc, p).transpose(1, 2, 3, 0, 4).reshape(b_, l_, w)
    return y


def _attn_kernel(nkv, q_ref, kt_ref, v_ref, o_ref):
    gq, tq, hd = q_ref.shape[2], q_ref.shape[3], q_ref.shape[4]
    q = q_ref[0, 0].reshape(gq * tq, hd)
    m0 = jnp.full((gq * tq, 1), -jnp.inf, F32)
    acc0 = jnp.zeros((gq * tq, LANES), F32)

    def body(jb, carry):
        m_prev, acc = carry
        s = _dot(q, kt_ref[0, 0, jb])
        m_new = jnp.maximum(m_prev, jnp.max(s, axis=-1, keepdims=True))
        alpha = jnp.exp(m_prev - m_new)
        p = jnp.exp(s - m_new).astype(BF16)
        acc = alpha * acc + _dot(p, v_ref[0, 0, jb])
        return m_new, acc

    _, acc = lax.fori_loop(0, nkv, body, (m0, acc0))
    o = acc[:, 0:hd] / acc[:, hd:hd + 1]
    o_ref[0, 0] = o.reshape(gq, tq, hd).astype(BF16)


def _attention(qh, kt, vx, tq):
    b_, kvh, gq, l_, hd = qh.shape
    nkv, tk = kt.shape[2], kt.shape[4]
    return pl.pallas_call(
        functools.partial(_attn_kernel, nkv),
        out_shape=jax.ShapeDtypeStruct((b_, kvh, gq, l_, hd), BF16),
        grid=(b_, kvh, l_ // tq),
        in_specs=[pl.BlockSpec((1, 1, gq, tq, hd), lambda b, h, i: (b, h, 0, i, 0)),
                  pl.BlockSpec((1, 1, nkv, hd, tk), lambda b, h, i: (b, h, 0, 0, 0)),
                  pl.BlockSpec((1, 1, nkv, tk, LANES), lambda b, h, i: (b, h, 0, 0, 0))],
        out_specs=pl.BlockSpec((1, 1, gq, tq, hd), lambda b, h, i: (b, h, 0, i, 0)),
        compiler_params=_cparams(("parallel", "parallel", "parallel")),
        name="attn",
    )(qh, kt, vx)


def _merge_kernel(dm, ne, ys_ref, oa_ref, gl_ref, x_ref, mod_ref, wglu_ref, bglu_ref, wso_ref, wao_ref,
                  wout_ref, n2g_ref, rw_ref, rb_ref, xmix_ref, h2_ref, idx_ref, gate_ref):
    s = jax.nn.gelu(ys_ref[0].astype(F32))
    s = s * jax.nn.sigmoid(_dot(s.astype(BF16), wglu_ref[...]) + bglu_ref[...])
    a = _dot(s.astype(BF16), wso_ref[...])
    bq = _dot(oa_ref[0], wao_ref[...])
    gts = jax.nn.sigmoid(gl_ref[0].astype(F32))
    merged = gts[:, 0:dm] * a + gts[:, dm:2 * dm] * bq
    out = _dot(merged.astype(BF16), wout_ref[...])
    g1 = mod_ref[0, :, 2 * dm:3 * dm]
    sh2 = mod_ref[0, :, 3 * dm:4 * dm]
    sc2 = mod_ref[0, :, 4 * dm:5 * dm]
    xm = x_ref[0] + g1 * out
    xmix_ref[0] = xm
    ms = jnp.mean(xm * xm, axis=-1, keepdims=True)
    h2 = xm * lax.rsqrt(ms + EPS) * n2g_ref[...] * (1.0 + sc2) + sh2
    h2_ref[0] = h2.astype(BF16)
    logits = _dot3(h2, rw_ref[...]) + rb_ref[...]
    lane = lax.broadcasted_iota(jnp.int32, logits.shape, 1).astype(F32)
    neg = jnp.float32(-jnp.inf)
    vals, idxs = [], []
    l = logits
    for _ in range(TOP_K):
        mx = jnp.max(l, axis=-1, keepdims=True)
        ix = jnp.min(jnp.where(l == mx, lane, float(LANES)), axis=-1, keepdims=True)
        vals.append(mx)
        idxs.append(ix)
        l = jnp.where(lane == ix, neg, l)
    es = [jnp.exp(v - vals[0]) for v in vals]
    den = es[0] + es[1] + es[2] + es[3]
    idx_out = jnp.zeros(logits.shape, F32)
    gate_out = jnp.zeros(logits.shape, F32)
    for k in range(TOP_K):
        idx_out = jnp.where(lane == float(k), idxs[k], idx_out)
        gate_out = jnp.where(lane == float(k), es[k] / den, gate_out)
    idx_ref[0] = idx_out.astype(jnp.int32)
    gate_ref[0] = gate_out


def _merge(y_ssm, o_attn, glog, x, mod3, w_glu, b_glu, w_ssm_out, w_attn_out, w_out, norm2_g,
           rw_pad, rb_pad, ne, tm):
    b_, l_, dm = x.shape
    wu = dm // 2
    tok = lambda w: pl.BlockSpec((1, tm, w), lambda b, i: (b, i, 0))
    full = lambda r, c: pl.BlockSpec((r, c), lambda b, i: (0, 0))
    return pl.pallas_call(
        functools.partial(_merge_kernel, dm, ne),
        out_shape=[jax.ShapeDtypeStruct((b_, l_, dm), F32),
                   jax.ShapeDtypeStruct((b_, l_, dm), BF16),
                   jax.ShapeDtypeStruct((b_, l_, LANES), jnp.int32),
                   jax.ShapeDtypeStruct((b_, l_, LANES), F32)],
        grid=(b_, l_ // tm),
        in_specs=[tok(wu), tok(wu), tok(2 * dm), tok(dm),
                  pl.BlockSpec((1, 1, N_MOD * dm), lambda b, i: (b, 0, 0)),
                  full(wu, wu), full(1, wu), full(wu, dm), full(wu, dm), full(dm, dm),
                  full(1, dm), full(dm, LANES), full(1, LANES)],
        out_specs=[tok(dm), tok(dm), tok(LANES), tok(LANES)],
        compiler_params=_cparams(("parallel", "parallel")),
        name="merge",
    )(y_ssm, o_attn, glog, x, mod3, w_glu, b_glu, w_ssm_out, w_attn_out, w_out, norm2_g, rw_pad, rb_pad)


def _expert_kernel(de, be_ref, nu_ref, x_ref, wgu_ref, bgu_ref, wd_ref, bd_ref, y_ref):
    @pl.when(pl.program_id(0) < nu_ref[0])
    def _():
        gu = _dot(x_ref[...], wgu_ref[0]) + bgu_ref[0]
        gate = jnp.minimum(gu[:, 0:de], SWIGLU_LIMIT)
        up = jnp.clip(gu[:, de:2 * de], -SWIGLU_LIMIT, SWIGLU_LIMIT)
        act = (up + 1.0) * (gate * jax.nn.sigmoid(SWIGLU_ALPHA * gate))
        y_ref[...] = (_dot(act.astype(BF16), wd_ref[0]) + bd_ref[0]).astype(BF16)


def _experts(xb, block_expert, n_used, wgu_b, b_gate_up, wd_b, b_down):
    n_slots, dm = xb.shape
    ne, _, de2 = wgu_b.shape
    de = de2 // 2
    nb = n_slots // EXPERT_BLOCK
    return pl.pallas_call(
        functools.partial(_expert_kernel, de),
        out_shape=jax.ShapeDtypeStruct((n_slots, dm), BF16),
        grid_spec=pltpu.PrefetchScalarGridSpec(
            num_scalar_prefetch=2,
            grid=(nb,),
            in_specs=[pl.BlockSpec((EXPERT_BLOCK, dm), lambda i, be, nu: (i, 0)),
                      pl.BlockSpec((1, dm, de2), lambda i, be, nu: (be[i], 0, 0)),
                      pl.BlockSpec((1, 1, de2), lambda i, be, nu: (be[i], 0, 0)),
                      pl.BlockSpec((1, de, dm), lambda i, be, nu: (be[i], 0, 0)),
                      pl.BlockSpec((1, 1, dm), lambda i, be, nu: (be[i], 0, 0))],
            out_specs=pl.BlockSpec((EXPERT_BLOCK, dm), lambda i, be, nu: (i, 0)),
        ),
        compiler_params=_cparams(("arbitrary",)),
        name="expert",
    )(block_expert, n_used, xb, wgu_b, b_gate_up.reshape(ne, 1, de2), wd_b, b_down.reshape(ne, 1, dm))


def _final_kernel(dm, yg_ref, gate_ref, xm_ref, mod_ref, fg_ref, o_ref):
    gates = gate_ref[0]
    f = None
    for k in range(TOP_K):
        t = gates[:, k:k + 1] * yg_ref[k, 0].astype(F32)
        f = t if f is None else f + t
    g2 = mod_ref[0, :, 5 * dm:6 * dm]
    xo = xm_ref[0] + g2 * f
    ms = jnp.mean(xo * xo, axis=-1, keepdims=True)
    o_ref[0] = xo * lax.rsqrt(ms + EPS) * fg_ref[...]


def _final(yg, gates, x_mix, mod3, final_g, tm):
    b_, l_, dm = x_mix.shape
    return pl.pallas_call(
        functools.partial(_final_kernel, dm),
        out_shape=jax.ShapeDtypeStruct((b_, l_, dm), F32),
        grid=(b_, l_ // tm),
        in_specs=[pl.BlockSpec((TOP_K, 1, tm, dm), lambda b, i: (0, b, i, 0)),
                  pl.BlockSpec((1, tm, LANES), lambda b, i: (b, i, 0)),
                  pl.BlockSpec((1, tm, dm), lambda b, i: (b, i, 0)),
                  pl.BlockSpec((1, 1, N_MOD * dm), lambda b, i: (b, 0, 0)),
                  pl.BlockSpec((1, dm), lambda b, i: (0, 0))],
        out_specs=pl.BlockSpec((1, tm, dm), lambda b, i: (b, i, 0)),
        compiler_params=_cparams(("parallel", "parallel")),
        name="final",
    )(yg, gates, x_mix, mod3, final_g.reshape(1, dm))


def _rope_tables(l_):
    half = HEAD_DIM // 2
    inv_freq = ROPE_THETA ** (-jnp.arange(0, half, 2, dtype=F32) / half)
    t = jnp.arange(l_, dtype=jnp.int32)
    row_id, col_id = t // GRID_W, t % GRID_W
    ang = jnp.concatenate([row_id.astype(F32)[:, None] * inv_freq,
                           col_id.astype(F32)[:, None] * inv_freq], axis=-1)
    cos = jnp.repeat(jnp.cos(ang), 2, axis=-1)
    sin = jnp.repeat(jnp.sin(ang), 2, axis=-1)
    sign = jnp.tile(jnp.array([-1.0, 1.0], F32), HEAD_DIM // 2)
    reps = LANES // HEAD_DIM
    return jnp.tile(cos, (1, reps)), jnp.tile(sin * sign, (1, reps))


def _routing(idx, t_, ne):
    m_ = t_ * TOP_K
    blk = EXPERT_BLOCK
    nb = (m_ + ne * (blk - 1)) // blk
    flat_e = idx.reshape(-1)
    order = jnp.argsort(flat_e)
    sorted_e = flat_e[order]
    counts = jnp.bincount(flat_e, length=ne)
    starts = jnp.cumsum(counts) - counts
    blk_counts = (counts + blk - 1) // blk
    blk_ends = jnp.cumsum(blk_counts)
    pad_starts = (blk_ends - blk_counts) * blk
    dest = pad_starts[sorted_e] + (jnp.arange(m_) - starts[sorted_e])
    n_slots = nb * blk
    slot_tok = jnp.zeros((n_slots,), jnp.int32).at[dest].set((order // TOP_K).astype(jnp.int32))
    dest_flat = jnp.zeros((m_,), jnp.int32).at[order].set(dest.astype(jnp.int32))
    block_expert = jnp.minimum(jnp.searchsorted(blk_ends, jnp.arange(nb), side='right'), ne - 1)
    return slot_tok, dest_flat.reshape(t_, TOP_K), block_expert.astype(jnp.int32), blk_ends[-1:].astype(jnp.int32)


def kernel(x, c, ctx, c_ctx, w_mod, b_mod, norm1_g, norm2_g, w_in, s5_lam_re, s5_lam_im, s5_log_step,
           s5_b_re, s5_b_im, s5_c_re, s5_c_im, s5_d, w_glu, b_glu, w_ssm_out, q_norm_g, k_norm_g,
           w_attn_out, w_out, router_w, router_b, w_gate_up, b_gate_up, w_down, b_down, final_norm_g):
    b_, l_, dm = x.shape
    lc = ctx.shape[1]
    depth = w_mod.shape[0]
    assert depth == 1, "single-layer block"
    assert b_ <= 7 and l_ % 512 == 0 and lc % S5_CHUNK == 0
    wu = dm // 2
    wkv = wu // GQA_GROUP
    kvh = wkv // HEAD_DIM
    ne = router_w.shape[-1]
    i = 0

    c8 = jnp.zeros((8, dm), F32).at[:b_].set(c).at[b_].set(c_ctx)
    mod3 = _modulation(c8, w_mod[i], b_mod[i]).reshape(8, 1, N_MOD * dm)

    w_in_b = w_in[i].astype(BF16)
    cos, sin = _rope_tables(l_)
    hsum = jnp.kron(jnp.eye(wu // HEAD_DIM, dtype=F32), jnp.ones((HEAD_DIM, HEAD_DIM), F32)).astype(BF16)
    qg = jnp.tile(q_norm_g[i], wu // HEAD_DIM).reshape(1, wu)
    kg = jnp.tile(k_norm_g[i], wkv // HEAD_DIM).reshape(1, wkv)
    u, k, v, q, glog = _inproj(x, mod3, None, norm1_g[i], w_in_b, cos, sin, qg, kg, hsum, True, 512)
    ones_t = jnp.ones((lc, LANES), F32)
    u_c, k_c, v_c = _inproj(ctx, mod3, b_, norm1_g[i], w_in_b[:, :wu + 2 * wkv], ones_t, 0.0 * ones_t,
                            qg, kg, hsum, False, lc)

    ops = _s5_operators(s5_lam_re[i], s5_lam_im[i], s5_log_step[i], s5_b_re[i], s5_b_im[i],
                        s5_c_re[i], s5_c_im[i], s5_d[i])
    y_ssm = _s5_mixer(u, u_c, ops)

    s_all = lc + l_
    tk = max(t for t in range(LANES, 1024 + 1, LANES) if s_all % t == 0)
    nkv = s_all // tk
    k_all = jnp.concatenate([k_c, k], axis=1).reshape(b_, s_all, kvh, HEAD_DIM)
    v_all = jnp.concatenate([v_c, v], axis=1).reshape(b_, s_all, kvh, HEAD_DIM)
    kt = k_all.reshape(b_, nkv, tk, kvh, HEAD_DIM).transpose(0, 3, 1, 4, 2)
    vx = jnp.concatenate([v_all, jnp.ones_like(v_all)], axis=-1)
    vx = vx.reshape(b_, nkv, tk, kvh, 2 * HEAD_DIM).transpose(0, 3, 1, 2, 4)
    qh = q.reshape(b_, l_, kvh, GQA_GROUP, HEAD_DIM).transpose(0, 2, 3, 1, 4)
    oh = _attention(qh, kt, vx, 256)
    o_attn = oh.transpose(0, 3, 1, 2, 4).reshape(b_, l_, wu)

    rw_pad = jnp.zeros((dm, LANES), F32).at[:, :ne].set(router_w[i])
    rb_pad = jnp.full((1, LANES), -jnp.inf, F32).at[0, :ne].set(router_b[i])
    x_mix, h2, idx, gates = _merge(y_ssm, o_attn, glog, x, mod3, w_glu[i].astype(BF16),
                                   b_glu[i].reshape(1, wu), w_ssm_out[i].astype(BF16),
                                   w_attn_out[i].astype(BF16), w_out[i].astype(BF16),
                                   norm2_g[i].reshape(1, dm), rw_pad, rb_pad, ne, 512)

    t_ = b_ * l_
    slot_tok, dest, block_expert, n_used = _routing(idx[..., :TOP_K].reshape(t_, TOP_K), t_, ne)
    xb = h2.reshape(t_, dm)[slot_tok]
    yb = _experts(xb, block_expert, n_used, w_gate_up[i].astype(BF16), b_gate_up[i],
                  w_down[i].astype(BF16), b_down[i])
    yg = yb[dest.T].reshape(TOP_K, b_, l_, dm)
    return _final(yg, gates, x_mix, mod3, final_norm_g, 512)
```

```python
import functools
import math

import jax
import jax.numpy as jnp
from jax import lax
from jax.experimental import pallas as pl
from jax.experimental.pallas import tpu as pltpu

F32 = jnp.float32
BF16 = jnp.bfloat16

EPS = 1e-6
GRID_W = 64
N_MOD = 6
SSM_GROUP = 16
SSM_STATE = 64
HEAD_DIM = 64
GQA_GROUP = 4
ROPE_THETA = 10000.0
TOP_K = 4
SWIGLU_ALPHA = 1.702
SWIGLU_LIMIT = 7.0
LOG2E = 1.4426950408889634

LANES = 128
S5_CHUNK = 32
EXPERT_BLOCK = 512
ATTN_V_ROWS = 80
ATTN_KEY_CHUNK = 384
ATTN_Q_BLOCK = 256
VMEM_LIMIT = 56 * 1024 * 1024


def _cparams(sem, flags=None):
    return pltpu.CompilerParams(dimension_semantics=sem, vmem_limit_bytes=VMEM_LIMIT, flags=flags)


def _dot(a, b):
    return jnp.dot(a, b, preferred_element_type=F32)


def _split(a):
    hi = a.astype(BF16)
    lo = (a - hi.astype(F32)).astype(BF16)
    return hi, lo


def _dot3(a, b):
    ah, al = _split(a)
    bh, bl = _split(b)
    return _dot(ah, bh) + _dot(ah, bl) + _dot(al, bh)


def _mod_kernel(c_ref, w_ref, b_ref, o_ref):
    c = c_ref[...]
    s = c * jax.nn.sigmoid(c)
    o_ref[...] = _dot3(s, w_ref[...]) + b_ref[...]


def _modulation(c8, w_mod, b_mod):
    d = c8.shape[1]
    n = w_mod.shape[1]
    tn = 1024
    return pl.pallas_call(
        _mod_kernel,
        out_shape=jax.ShapeDtypeStruct((8, n), F32),
        grid=(n // tn,),
        in_specs=[pl.BlockSpec((8, d), lambda j: (0, 0)),
                  pl.BlockSpec((d, tn), lambda j: (0, j)),
                  pl.BlockSpec((1, tn), lambda j: (0, j))],
        out_specs=pl.BlockSpec((8, tn), lambda j: (0, j)),
        compiler_params=_cparams(("parallel",)),
        name="mod",
    )(c8, w_mod, b_mod.reshape(1, n))


def _head_norm_rope(z, gain, hsum, cos, sin, scale):
    z2h, z2l = _split(z * z)
    ss = _dot(z2h, hsum) + _dot(z2l, hsum)
    zn = z * lax.rsqrt(ss * (1.0 / HEAD_DIM) + EPS) * gain
    lane = lax.broadcasted_iota(jnp.int32, (1, LANES), 1)
    even = (lane & 1) == 0
    outs = []
    for ci in range(z.shape[1] // LANES):
        ch = zn[:, ci * LANES:(ci + 1) * LANES]
        nxt = pltpu.roll(ch, LANES - 1, axis=1)
        prv = pltpu.roll(ch, 1, axis=1)
        sw = jnp.where(even, nxt, prv)
        outs.append((ch * cos + sw * sin) * scale)
    return outs


def _inproj_kernel(with_q, dm, x_ref, mod_ref, g_ref, w_ref, cos_ref, sin_ref, qg_ref, kg_ref,
                   hsum_ref, *out_refs):
    x = x_ref[0]
    ms = jnp.mean(x * x, axis=-1, keepdims=True)
    y = x * lax.rsqrt(ms + EPS) * g_ref[...]
    sh = mod_ref[0, :, 0:dm]
    sc = mod_ref[0, :, dm:2 * dm]
    hb = (y * (1.0 + sc) + sh).astype(BF16)
    cos = cos_ref[...]
    sin = sin_ref[...]
    wu = dm // 2
    wkv = wu // GQA_GROUP
    c_k, c_v, c_q, c_g = wu, wu + wkv, wu + 2 * wkv, 2 * wu + 2 * wkv
    u_ref, k_ref, vt_ref = out_refs[:3]
    u_ref[0] = _dot(hb, w_ref[:, 0:c_k]).astype(BF16)
    kz = _dot(hb, w_ref[:, c_k:c_v])
    (kr,) = _head_norm_rope(kz, kg_ref[...], hsum_ref[0:wkv, 0:wkv], cos, sin, 1.0)
    k_ref[0] = kr.astype(BF16)
    vt_ref[0] = _dot(hb, w_ref[:, c_v:c_q]).T.astype(BF16)
    if with_q:
        qt_ref, gl_ref = out_refs[3:]
        qz = _dot(hb, w_ref[:, c_q:c_g])
        qs = _head_norm_rope(qz, qg_ref[...], hsum_ref[...], cos, sin, HEAD_DIM ** -0.5 * LOG2E)
        for ci, qc in enumerate(qs):
            qt_ref[0, ci * LANES:(ci + 1) * LANES, :] = qc.T.astype(BF16)
        gl_ref[0] = _dot(hb, w_ref[:, c_g:]).astype(BF16)


def _inproj(x, mod3, mod_row0, norm_g, w_in_b, cos, sin, qg, kg, hsum, with_q, tm):
    b_, l_, dm = x.shape
    wu = dm // 2
    wkv = wu // GQA_GROUP
    n_in = w_in_b.shape[1]
    outs = [jax.ShapeDtypeStruct((b_, l_, wu), BF16),
            jax.ShapeDtypeStruct((b_, l_, wkv), BF16),
            jax.ShapeDtypeStruct((b_, wkv, l_), BF16)]
    ospecs = [pl.BlockSpec((1, tm, wu), lambda b, i: (b, i, 0)),
              pl.BlockSpec((1, tm, wkv), lambda b, i: (b, i, 0)),
              pl.BlockSpec((1, wkv, tm), lambda b, i: (b, 0, i))]
    if with_q:
        outs += [jax.ShapeDtypeStruct((b_, wu, l_), BF16),
                 jax.ShapeDtypeStruct((b_, l_, 2 * dm), BF16)]
        ospecs += [pl.BlockSpec((1, wu, tm), lambda b, i: (b, 0, i)),
                   pl.BlockSpec((1, tm, 2 * dm), lambda b, i: (b, i, 0))]
    if mod_row0 is None:
        mod_map = lambda b, i: (b, 0, 0)
    else:
        mod_map = lambda b, i: (mod_row0, 0, 0)
    return pl.pallas_call(
        functools.partial(_inproj_kernel, with_q, dm),
        out_shape=outs,
        grid=(b_, l_ // tm),
        in_specs=[pl.BlockSpec((1, tm, dm), lambda b, i: (b, i, 0)),
                  pl.BlockSpec((1, 1, N_MOD * dm), mod_map),
                  pl.BlockSpec((1, dm), lambda b, i: (0, 0)),
                  pl.BlockSpec((dm, n_in), lambda b, i: (0, 0)),
                  pl.BlockSpec((tm, LANES), lambda b, i: (i, 0)),
                  pl.BlockSpec((tm, LANES), lambda b, i: (i, 0)),
                  pl.BlockSpec((1, wu), lambda b, i: (0, 0)),
                  pl.BlockSpec((1, wkv), lambda b, i: (0, 0)),
                  pl.BlockSpec((wu, wu), lambda b, i: (0, 0))],
        out_specs=ospecs,
        compiler_params=_cparams(("parallel", "parallel")),
        name="inproj_x" if with_q else "inproj_ctx",
    )(x, mod3, norm_g.reshape(1, dm), w_in_b, cos, sin, qg, kg, hsum)


def _cexp_pow(e, lr, li):
    mag = jnp.exp(e * lr)
    ang = e * li
    return mag * jnp.cos(ang), mag * jnp.sin(ang)


def _cmul(ar, ai, br, bi):
    return ar * br - ai * bi, ar * bi + ai * br


def _s5pre_kernel(lamr_r, lami_r, lst_r, lamr_c, lami_c, lst_c, btr_ref, bti_ref, ctr_ref, cti_ref,
                  d_ref, win_ref, wout_ref, m_ref, dec_ref):
    tc = S5_CHUNK
    half = float(tc // 2)
    rows = tc * SSM_GROUP
    gsh = SSM_GROUP.bit_length() - 1
    nsh = SSM_STATE.bit_length() - 1
    s_idx = (lax.broadcasted_iota(jnp.int32, (rows, 1), 0) >> gsh).astype(F32)
    t_idx = (lax.broadcasted_iota(jnp.int32, (1, rows), 1) >> gsh).astype(F32)
    row_i = lax.broadcasted_iota(jnp.int32, (rows, rows), 0)
    col_i = lax.broadcasted_iota(jnp.int32, (rows, rows), 1)
    lane_g = lax.broadcasted_iota(jnp.int32, (1, LANES), 1) >> nsh
    subl_g = lax.broadcasted_iota(jnp.int32, (LANES, 1), 0) >> nsh
    m_acc = [jnp.where(row_i == col_i, d_ref[0, gl], 0.0) for gl in range(2)]
    for dr in range(2):
        step = jnp.exp(lst_r[dr, 0])
        lam_r, lam_i = lamr_r[dr, 0], lami_r[dr, 0]
        lr, li = lam_r * step, lam_i * step
        lb_r, lb_i = _cexp_pow(1.0, lr, li)
        den = lam_r * lam_r + lam_i * lam_i
        nr, ni = lb_r - 1.0, lb_i
        cf_r = (nr * lam_r + ni * lam_i) / den
        cf_i = (ni * lam_r - nr * lam_i) / den
        bb_r, bb_i = _cmul(cf_r, cf_i, btr_ref[dr, 0], bti_ref[dr, 0])
        e_in = (tc - 1.0 - s_idx) if dr == 0 else s_idx
        p_r, p_i = _cexp_pow(e_in, lr, li)
        wi_r, wi_i = _cmul(bb_r, bb_i, p_r, p_i)
        e_a = (half - s_idx) if dr == 0 else (s_idx - half)
        p_r, p_i = _cexp_pow(e_a, lr, li)
        a_r, a_i = _cmul(bb_r, bb_i, p_r, p_i)
        d_r, d_i = _cexp_pow(float(tc), lr, li)
        dec_ref[0, dr, 0] = d_r
        dec_ref[0, dr, 1] = d_i
        stepc = jnp.exp(lst_c[dr, 0])
        lrc, lic = lamr_c[dr, 0] * stepc, lami_c[dr, 0] * stepc
        c_r, c_i = ctr_ref[dr, 0], cti_ref[dr, 0]
        e_b = (t_idx - half) if dr == 0 else (half - t_idx)
        q_r, q_i = _cexp_pow(e_b, lrc, lic)
        bm_r, bm_i = _cmul(c_r, c_i, q_r, q_i)
        e_o = (t_idx + 1.0) if dr == 0 else (tc - t_idx)
        q_r, q_i = _cexp_pow(e_o, lrc, lic)
        wo_r, wo_i = _cmul(c_r, c_i, q_r, q_i)
        if dr == 0:
            mask = (col_i >> gsh) >= (row_i >> gsh)
        else:
            mask = (row_i >> gsh) >= (col_i >> gsh)
        for gl in range(2):
            lsel = lane_g == gl
            ssel = subl_g == gl
            win_ref[0, gl, 2 * dr] = jnp.where(lsel, wi_r, 0.0).astype(BF16)
            win_ref[0, gl, 2 * dr + 1] = jnp.where(lsel, wi_i, 0.0).astype(BF16)
            wout_ref[0, 2 * dr, :, gl * rows:(gl + 1) * rows] = jnp.where(ssel, wo_r, 0.0).astype(BF16)
            wout_ref[0, 2 * dr + 1, :, gl * rows:(gl + 1) * rows] = jnp.where(ssel, -wo_i, 0.0).astype(BF16)
            kmat = (_dot3(jnp.where(lsel, a_r, 0.0), bm_r) - _dot3(jnp.where(lsel, a_i, 0.0), bm_i))
            m_acc[gl] = m_acc[gl] + jnp.where(mask, kmat, 0.0)
    for gl in range(2):
        m_ref[0, gl] = m_acc[gl].astype(BF16)


def _s5_operators(lam_re, lam_im, log_step, b_re, b_im, c_re, c_im, d_skip):
    g = lam_re.shape[1]
    n, p = SSM_STATE, SSM_GROUP
    npair = g // 2
    tc = S5_CHUNK
    rows = tc * p

    def row_form(a):
        return a.reshape(2, npair, 1, 2 * n)

    def col_form(a):
        return a.reshape(2, npair, 2 * n, 1)

    lst = jnp.broadcast_to(log_step[:, :, None], (2, g, n))
    bt = lambda b: jnp.broadcast_to(
        b.reshape(2, npair, 2, n, p).transpose(0, 1, 4, 2, 3)[:, :, None], (2, npair, tc, p, 2, n)
    ).reshape(2, npair, rows, 2 * n)
    ct = lambda c: jnp.broadcast_to(
        c.reshape(2, npair, 2, p, n).transpose(0, 1, 2, 4, 3)[:, :, :, :, None, :], (2, npair, 2, n, tc, p)
    ).reshape(2, npair, 2 * n, rows)
    d2 = jnp.broadcast_to(d_skip.reshape(npair, 2, 1, 1, p), (npair, 2, 1, tc, p)).reshape(npair, 2, 1, rows)

    def spec4(shape):
        return pl.BlockSpec((2, 1) + shape, lambda i: (0, i, 0, 0))

    return pl.pallas_call(
        _s5pre_kernel,
        out_shape=[jax.ShapeDtypeStruct((npair, 2, 4, rows, LANES), BF16),
                   jax.ShapeDtypeStruct((npair, 4, LANES, 2 * rows), BF16),
                   jax.ShapeDtypeStruct((npair, 2, rows, rows), BF16),
                   jax.ShapeDtypeStruct((npair, 2, 2, 1, LANES), F32)],
        grid=(npair,),
        in_specs=[spec4((1, LANES)), spec4((1, LANES)), spec4((1, LANES)),
                  spec4((LANES, 1)), spec4((LANES, 1)), spec4((LANES, 1)),
                  spec4((rows, LANES)), spec4((rows, LANES)),
                  spec4((LANES, rows)), spec4((LANES, rows)),
                  pl.BlockSpec((1, 2, 1, rows), lambda i: (i, 0, 0, 0))],
        out_specs=[pl.BlockSpec((1, 2, 4, rows, LANES), lambda i: (i, 0, 0, 0, 0)),
                   pl.BlockSpec((1, 4, LANES, 2 * rows), lambda i: (i, 0, 0, 0)),
                   pl.BlockSpec((1, 2, rows, rows), lambda i: (i, 0, 0, 0)),
                   pl.BlockSpec((1, 2, 2, 1, LANES), lambda i: (i, 0, 0, 0, 0))],
        compiler_params=_cparams(("parallel",)),
        name="s5pre",
    )(row_form(lam_re), row_form(lam_im), row_form(lst),
      col_form(lam_re), col_form(lam_im), col_form(lst),
      bt(b_re), bt(b_im), ct(c_re), ct(c_im), d2)


def _s5state_kernel(u_ref, win_ref, *s_refs):
    u0 = u_ref[0, 0]
    u1 = u_ref[1, 0]
    for kind in range(4):
        s_refs[kind][...] = _dot(u0, win_ref[0, 0, kind]) + _dot(u1, win_ref[0, 1, kind])


def _s5_states(uf, win):
    g, b_, j, rows = uf.shape
    npair = g // 2
    out = jax.ShapeDtypeStruct((j, b_ * npair * LANES), F32)
    ospec = pl.BlockSpec((j, LANES), lambda pr, b: (0, b * npair + pr))
    return pl.pallas_call(
        _s5state_kernel,
        out_shape=[out] * 4,
        grid=(npair, b_),
        in_specs=[pl.BlockSpec((2, 1, j, rows), lambda pr, b: (pr, b, 0, 0)),
                  pl.BlockSpec((1, 2, 4, rows, LANES), lambda pr, b: (pr, 0, 0, 0, 0))],
        out_specs=[ospec] * 4,
        compiler_params=_cparams(("parallel", "parallel")),
        name="s5state",
    )(uf, win)


def _s5scan_kernel(jc, sfr, sfi, sbr, sbi, dec_ref, hfr, hfi, hbr, hbi):
    j = sfr.shape[0]
    w = sfr.shape[1]
    zero = jnp.zeros((1, w), F32)

    def run(sr, si, hr_out, hi_out, ar, ai, lo, n, reverse, carry):
        def body(i, hc):
            row = (lo + n - 1 - i) if reverse else (lo + i)
            hr, hi = hc
            hr_out[pl.ds(row, 1), :] = hr
            hi_out[pl.ds(row, 1), :] = hi
            nr = ar * hr - ai * hi + sr[pl.ds(row, 1), :]
            ni = ar * hi + ai * hr + si[pl.ds(row, 1), :]
            return nr, ni
        return lax.fori_loop(0, n, body, carry)

    afr, afi = dec_ref[0, 0], dec_ref[0, 1]
    abr, abi = dec_ref[1, 0], dec_ref[1, 1]
    run(sfr, sfi, hfr, hfi, afr, afi, 0, j, False, (zero, zero))
    c = run(sbr, sbi, hbr, hbi, abr, abi, 0, jc, True, (zero, zero))
    run(sbr, sbi, hbr, hbi, abr, abi, jc, j - jc, True, c)


def _s5_scan(states, dec_cols, jc):
    j, wtot = states[0].shape
    tw = 1024
    spec = pl.BlockSpec((j, tw), lambda i: (0, i))
    return pl.pallas_call(
        functools.partial(_s5scan_kernel, jc),
        out_shape=[jax.ShapeDtypeStruct((j, wtot), F32)] * 4,
        grid=(wtot // tw,),
        in_specs=[spec] * 4 + [pl.BlockSpec((2, 2, 1, tw), lambda i: (0, 0, 0, i))],
        out_specs=[spec] * 4,
        compiler_params=_cparams(("parallel",)),
        name="s5scan",
    )(*states, dec_cols)


def _s5out_kernel(u_ref, hfr, hfi, hbr, hbi, wout_ref, m_ref, y_ref):
    rows = u_ref.shape[3]
    y2 = None
    for kind, h in enumerate((hfr, hfi, hbr, hbi)):
        t = _dot(h[...].astype(BF16), wout_ref[0, kind])
        y2 = t if y2 is None else y2 + t
    for gl in range(2):
        y = y2[:, gl * rows:(gl + 1) * rows] + _dot(u_ref[gl, 0], m_ref[0, gl])
        y_ref[gl, 0] = y.astype(BF16)


def _s5_outputs(uf, hprev, wout, m):
    g, b_, j, rows = uf.shape
    npair = g // 2
    hspec = pl.BlockSpec((j, LANES), lambda pr, b: (0, b * npair + pr))
    return pl.pallas_call(
        _s5out_kernel,
        out_shape=jax.ShapeDtypeStruct((g, b_, j, rows), BF16),
        grid=(npair, b_),
        in_specs=[pl.BlockSpec((2, 1, j, rows), lambda pr, b: (pr, b, 0, 0))] + [hspec] * 4 +
                 [pl.BlockSpec((1, 4, LANES, 2 * rows), lambda pr, b: (pr, 0, 0, 0)),
                  pl.BlockSpec((1, 2, rows, rows), lambda pr, b: (pr, 0, 0, 0))],
        out_specs=pl.BlockSpec((2, 1, j, rows), lambda pr, b: (pr, b, 0, 0)),
        compiler_params=_cparams(("parallel", "parallel")),
        name="s5out",
    )(uf, *hprev, wout, m)


def _s5_mixer(u, u_c, ops):
    win, wout, m, dec = ops
    b_, l_, w = u.shape
    lc = u_c.shape[1]
    tc, p = S5_CHUNK, SSM_GROUP
    g = w // p
    npair = g // 2
    jc, jl = lc // tc, l_ // tc
    j = jc + jl
    ua = jnp.concatenate([u_c, u], axis=1)
    uf = ua.reshape(b_, j, tc, g, p).transpose(3, 0, 1, 2, 4).reshape(g, b_, j, tc * p)
    states = _s5_states(uf, win)
    dec_cols = jnp.broadcast_to(dec.transpose(1, 2, 3, 0, 4)[:, :, :, None], (2, 2, 1, b_, npair, LANES))
    dec_cols = dec_cols.reshape(2, 2, 1, b_ * npair * LANES)
    hprev = _s5_scan(states, dec_cols, jc)
    yf = _s5_outputs(uf, hprev, wout, m)
    y = yf[:, :, jc:].reshape(g, b_, jl, tc, p).transpose(1, 2, 3, 0, 4).reshape(b_, l_, w)
    return y


def _attn_kernel(nkv, qt_ref, k_ref, vx_ref, o_ref, s0_ref, s1_ref):
    hd = HEAD_DIM
    kvh = pl.program_id(1)
    tq = qt_ref.shape[2]
    gq = qt_ref.shape[1] // hd
    kw = k_ref.shape[3]
    vr = vx_ref.shape[3]
    own = (lax.broadcasted_iota(jnp.int32, (kw, tq), 0) >> (hd.bit_length() - 1)) == kvh
    qp = []
    for g in range(gq):
        qg = qt_ref[0, g * hd:(g + 1) * hd, :].astype(F32)
        q2 = jnp.concatenate([qg] * (kw // hd), axis=0)
        qp.append(jnp.where(own, q2, 0.0).astype(BF16))
    qp = jnp.concatenate(qp, axis=1)
    m0 = jnp.full((1, gq * tq), -jnp.inf, F32)
    acc0 = jnp.zeros((vr, gq * tq), F32)

    def scores(jb, s_ref):
        s = _dot(k_ref[0, jb], qp)
        s_ref[...] = s
        return jnp.max(s, axis=0, keepdims=True)

    def consume(jb, s_ref, cmax, m_prev, acc):
        m_new = jnp.maximum(m_prev, cmax)
        alpha = jnp.exp2(m_prev - m_new)
        p = jnp.exp2(s_ref[...] - m_new).astype(BF16)
        return m_new, alpha * acc + _dot(vx_ref[0, 0, jb], p)

    def body(i, carry):
        m, acc, cmax = carry
        jb = 2 * i
        cmax1 = scores(jb + 1, s1_ref)
        m, acc = consume(jb, s0_ref, cmax, m, acc)
        cmax2 = scores(jb + 2, s0_ref)
        m, acc = consume(jb + 1, s1_ref, cmax1, m, acc)
        return m, acc, cmax2

    npair = (nkv - 1) // 2
    m, acc, cmax = lax.fori_loop(0, npair, body, (m0, acc0, scores(0, s0_ref)))
    if (nkv - 1) % 2:
        cmax1 = scores(nkv - 1, s1_ref)
        m, acc = consume(nkv - 2, s0_ref, cmax, m, acc)
        m, acc = consume(nkv - 1, s1_ref, cmax1, m, acc)
    else:
        m, acc = consume(nkv - 1, s0_ref, cmax, m, acc)
    o = acc[0:hd] / acc[hd:hd + 1]
    ot = jnp.concatenate([o[:, g * tq:(g + 1) * tq] for g in range(gq)], axis=0)
    o_ref[0] = ot.T.astype(BF16)


def _attention(qt, k4, vx, tq):
    b_, wq, l_ = qt.shape
    _, kvh, nkv, vr, tk = vx.shape
    kw = k4.shape[3]
    gw = wq // kvh
    return pl.pallas_call(
        functools.partial(_attn_kernel, nkv),
        out_shape=jax.ShapeDtypeStruct((b_, l_, wq), BF16),
        grid=(b_, kvh, l_ // tq),
        in_specs=[pl.BlockSpec((1, gw, tq), lambda b, h, i: (b, h, i)),
                  pl.BlockSpec((1, nkv, tk, kw), lambda b, h, i: (b, 0, 0, 0)),
                  pl.BlockSpec((1, 1, nkv, vr, tk), lambda b, h, i: (b, h, 0, 0, 0))],
        out_specs=pl.BlockSpec((1, tq, gw), lambda b, h, i: (b, i, h)),
        scratch_shapes=[pltpu.VMEM((tk, (gw // HEAD_DIM) * tq), F32)] * 2,
        compiler_params=_cparams(("parallel", "parallel", "parallel")),
        name="attn",
    )(qt, k4, vx)


def _merge_kernel(dm, ne, ys_ref, oa_ref, gl_ref, x_ref, mod_ref, wglu_ref, bglu_ref, wso_ref, wao_ref,
                  wout_ref, n2g_ref, rw_ref, rb_ref, xmix_ref, h2_ref, idx_ref, gate_ref):
    s = jax.nn.gelu(ys_ref[0].astype(F32))
    s = s * jax.nn.sigmoid(_dot(s.astype(BF16), wglu_ref[...]) + bglu_ref[...])
    a = _dot(s.astype(BF16), wso_ref[...])
    bq = _dot(oa_ref[0], wao_ref[...])
    gts = jax.nn.sigmoid(gl_ref[0].astype(F32))
    merged = gts[:, 0:dm] * a + gts[:, dm:2 * dm] * bq
    out = _dot(merged.astype(BF16), wout_ref[...])
    g1 = mod_ref[0, :, 2 * dm:3 * dm]
    sh2 = mod_ref[0, :, 3 * dm:4 * dm]
    sc2 = mod_ref[0, :, 4 * dm:5 * dm]
    xm = x_ref[0] + g1 * out
    xmix_ref[0] = xm
    ms = jnp.mean(xm * xm, axis=-1, keepdims=True)
    h2 = xm * lax.rsqrt(ms + EPS) * n2g_ref[...] * (1.0 + sc2) + sh2
    h2_ref[0] = h2.astype(BF16)
    logits = _dot3(h2, rw_ref[...]) + rb_ref[...]
    lane = lax.broadcasted_iota(jnp.int32, logits.shape, 1).astype(F32)
    neg = jnp.float32(-jnp.inf)
    vals, idxs = [], []
    l = logits
    for _ in range(TOP_K):
        mx = jnp.max(l, axis=-1, keepdims=True)
        ix = jnp.min(jnp.where(l == mx, lane, float(LANES)), axis=-1, keepdims=True)
        vals.append(mx)
        idxs.append(ix)
        l = jnp.where(lane == ix, neg, l)
    es = [jnp.exp(v - vals[0]) for v in vals]
    den = es[0] + es[1] + es[2] + es[3]
    idx_out = jnp.zeros(logits.shape, F32)
    gate_out = jnp.zeros(logits.shape, F32)
    for k in range(TOP_K):
        idx_out = jnp.where(lane == float(k), idxs[k], idx_out)
        gate_out = jnp.where(lane == float(k), es[k] / den, gate_out)
    idx_ref[0] = idx_out.astype(jnp.int32)
    gate_ref[0] = gate_out


def _merge(y_ssm, o_attn, glog, x, mod3, w_glu, b_glu, w_ssm_out, w_attn_out, w_out, norm2_g,
           rw_pad, rb_pad, ne, tm):
    b_, l_, dm = x.shape
    wu = dm // 2
    tok = lambda w: pl.BlockSpec((1, tm, w), lambda b, i: (b, i, 0))
    full = lambda r, c: pl.BlockSpec((r, c), lambda b, i: (0, 0))
    return pl.pallas_call(
        functools.partial(_merge_kernel, dm, ne),
        out_shape=[jax.ShapeDtypeStruct((b_, l_, dm), F32),
                   jax.ShapeDtypeStruct((b_, l_, dm), BF16),
                   jax.ShapeDtypeStruct((b_, l_, LANES), jnp.int32),
                   jax.ShapeDtypeStruct((b_, l_, LANES), F32)],
        grid=(b_, l_ // tm),
        in_specs=[tok(wu), tok(wu), tok(2 * dm), tok(dm),
                  pl.BlockSpec((1, 1, N_MOD * dm), lambda b, i: (b, 0, 0)),
                  full(wu, wu), full(1, wu), full(wu, dm), full(wu, dm), full(dm, dm),
                  full(1, dm), full(dm, LANES), full(1, LANES)],
        out_specs=[tok(dm), tok(dm), tok(LANES), tok(LANES)],
        compiler_params=_cparams(("parallel", "parallel")),
        name="merge",
    )(y_ssm, o_attn, glog, x, mod3, w_glu, b_glu, w_ssm_out, w_attn_out, w_out, norm2_g, rw_pad, rb_pad)


def _expert_kernel(de, be_ref, nu_ref, x_ref, wgu_ref, bgu_ref, wd_ref, bd_ref, y_ref):
    @pl.when(pl.program_id(0) < nu_ref[0])
    def _():
        gu = _dot(x_ref[...], wgu_ref[0]) + bgu_ref[0]
        gate = jnp.minimum(gu[:, 0:de], SWIGLU_LIMIT)
        up = jnp.clip(gu[:, de:2 * de], -SWIGLU_LIMIT, SWIGLU_LIMIT)
        act = (up + 1.0) * (gate * jax.nn.sigmoid(SWIGLU_ALPHA * gate))
        y_ref[...] = (_dot(act.astype(BF16), wd_ref[0]) + bd_ref[0]).astype(BF16)


def _experts(xb, block_expert, n_used, wgu_b, b_gate_up, wd_b, b_down):
    n_slots, dm = xb.shape
    ne, _, de2 = wgu_b.shape
    de = de2 // 2
    nb = n_slots // EXPERT_BLOCK
    return pl.pallas_call(
        functools.partial(_expert_kernel, de),
        out_shape=jax.ShapeDtypeStruct((n_slots, dm), BF16),
        grid_spec=pltpu.PrefetchScalarGridSpec(
            num_scalar_prefetch=2,
            grid=(nb,),
            in_specs=[pl.BlockSpec((EXPERT_BLOCK, dm), lambda i, be, nu: (i, 0)),
                      pl.BlockSpec((1, dm, de2), lambda i, be, nu: (be[i], 0, 0)),
                      pl.BlockSpec((1, 1, de2), lambda i, be, nu: (be[i], 0, 0)),
                      pl.BlockSpec((1, de, dm), lambda i, be, nu: (be[i], 0, 0)),
                      pl.BlockSpec((1, 1, dm), lambda i, be, nu: (be[i], 0, 0))],
            out_specs=pl.BlockSpec((EXPERT_BLOCK, dm), lambda i, be, nu: (i, 0)),
        ),
        compiler_params=_cparams(("arbitrary",)),
        name="expert",
    )(block_expert, n_used, xb, wgu_b, b_gate_up.reshape(ne, 1, de2), wd_b, b_down.reshape(ne, 1, dm))


def _final_kernel(dm, yg_ref, gate_ref, xm_ref, mod_ref, fg_ref, o_ref):
    gates = gate_ref[0]
    f = None
    for k in range(TOP_K):
        t = gates[:, k:k + 1] * yg_ref[k, 0].astype(F32)
        f = t if f is None else f + t
    g2 = mod_ref[0, :, 5 * dm:6 * dm]
    xo = xm_ref[0] + g2 * f
    ms = jnp.mean(xo * xo, axis=-1, keepdims=True)
    o_ref[0] = xo * lax.rsqrt(ms + EPS) * fg_ref[...]


def _final(yg, gates, x_mix, mod3, final_g, tm):
    b_, l_, dm = x_mix.shape
    return pl.pallas_call(
        functools.partial(_final_kernel, dm),
        out_shape=jax.ShapeDtypeStruct((b_, l_, dm), F32),
        grid=(b_, l_ // tm),
        in_specs=[pl.BlockSpec((TOP_K, 1, tm, dm), lambda b, i: (0, b, i, 0)),
                  pl.BlockSpec((1, tm, LANES), lambda b, i: (b, i, 0)),
                  pl.BlockSpec((1, tm, dm), lambda b, i: (b, i, 0)),
                  pl.BlockSpec((1, 1, N_MOD * dm), lambda b, i: (b, 0, 0)),
                  pl.BlockSpec((1, dm), lambda b, i: (0, 0))],
        out_specs=pl.BlockSpec((1, tm, dm), lambda b, i: (b, i, 0)),
        compiler_params=_cparams(("parallel", "parallel")),
        name="final",
    )(yg, gates, x_mix, mod3, final_g.reshape(1, dm))


def _rope_tables(l_):
    half = HEAD_DIM // 2
    inv_freq = ROPE_THETA ** (-jnp.arange(0, half, 2, dtype=F32) / half)
    t = jnp.arange(l_, dtype=jnp.int32)
    row_id, col_id = t // GRID_W, t % GRID_W
    ang = jnp.concatenate([row_id.astype(F32)[:, None] * inv_freq,
                           col_id.astype(F32)[:, None] * inv_freq], axis=-1)
    cos = jnp.repeat(jnp.cos(ang), 2, axis=-1)
    sin = jnp.repeat(jnp.sin(ang), 2, axis=-1)
    sign = jnp.tile(jnp.array([-1.0, 1.0], F32), HEAD_DIM // 2)
    reps = LANES // HEAD_DIM
    return jnp.tile(cos, (1, reps)), jnp.tile(sin * sign, (1, reps))


def _routing(idx, t_, ne):
    m_ = t_ * TOP_K
    blk = EXPERT_BLOCK
    nb = (m_ + ne * (blk - 1)) // blk
    flat_e = idx.reshape(-1)
    order = jnp.argsort(flat_e)
    sorted_e = flat_e[order]
    counts = jnp.bincount(flat_e, length=ne)
    starts = jnp.cumsum(counts) - counts
    blk_counts = (counts + blk - 1) // blk
    blk_ends = jnp.cumsum(blk_counts)
    pad_starts = (blk_ends - blk_counts) * blk
    dest = pad_starts[sorted_e] + (jnp.arange(m_) - starts[sorted_e])
    n_slots = nb * blk
    slot_tok = jnp.zeros((n_slots,), jnp.int32).at[dest].set((order // TOP_K).astype(jnp.int32))
    dest_flat = jnp.zeros((m_,), jnp.int32).at[order].set(dest.astype(jnp.int32))
    block_expert = jnp.minimum(jnp.searchsorted(blk_ends, jnp.arange(nb), side='right'), ne - 1)
    return slot_tok, dest_flat.reshape(t_, TOP_K), block_expert.astype(jnp.int32), blk_ends[-1:].astype(jnp.int32)


def kernel(x, c, ctx, c_ctx, w_mod, b_mod, norm1_g, norm2_g, w_in, s5_lam_re, s5_lam_im, s5_log_step,
           s5_b_re, s5_b_im, s5_c_re, s5_c_im, s5_d, w_glu, b_glu, w_ssm_out, q_norm_g, k_norm_g,
           w_attn_out, w_out, router_w, router_b, w_gate_up, b_gate_up, w_down, b_down, final_norm_g):
    b_, l_, dm = x.shape
    lc = ctx.shape[1]
    depth = w_mod.shape[0]
    assert depth == 1, "single-layer block"
    assert b_ <= 7 and l_ % 512 == 0 and lc % S5_CHUNK == 0
    wu = dm // 2
    wkv = wu // GQA_GROUP
    kvh = wkv // HEAD_DIM
    ne = router_w.shape[-1]
    i = 0

    c8 = jnp.zeros((8, dm), F32).at[:b_].set(c).at[b_].set(c_ctx)
    mod3 = _modulation(c8, w_mod[i], b_mod[i]).reshape(8, 1, N_MOD * dm)

    w_in_b = w_in[i].astype(BF16)
    cos, sin = _rope_tables(l_)
    hsum = jnp.kron(jnp.eye(wu // HEAD_DIM, dtype=F32), jnp.ones((HEAD_DIM, HEAD_DIM), F32)).astype(BF16)
    qg = jnp.tile(q_norm_g[i], wu // HEAD_DIM).reshape(1, wu)
    kg = jnp.tile(k_norm_g[i], wkv // HEAD_DIM).reshape(1, wkv)
    u, k, vt, qt, glog = _inproj(x, mod3, None, norm1_g[i], w_in_b, cos, sin, qg, kg, hsum, True, 512)
    ones_t = jnp.ones((lc, LANES), F32)
    u_c, k_c, vt_c = _inproj(ctx, mod3, b_, norm1_g[i], w_in_b[:, :wu + 2 * wkv], ones_t, 0.0 * ones_t,
                             qg, kg, hsum, False, lc)

    ops = _s5_operators(s5_lam_re[i], s5_lam_im[i], s5_log_step[i], s5_b_re[i], s5_b_im[i],
                        s5_c_re[i], s5_c_im[i], s5_d[i])
    y_ssm = _s5_mixer(u, u_c, ops)

    s_all = lc + l_
    tk = max(t for t in range(LANES, ATTN_KEY_CHUNK + 1, LANES) if s_all % t == 0)
    nkv = s_all // tk
    k4 = jnp.concatenate([k_c, k], axis=1).reshape(b_, nkv, tk, wkv)
    vt_all = jnp.concatenate([vt_c, vt], axis=2).reshape(b_, kvh, HEAD_DIM, s_all)
    vx = jnp.concatenate([vt_all, jnp.ones((b_, kvh, 1, s_all), BF16),
                          jnp.zeros((b_, kvh, ATTN_V_ROWS - HEAD_DIM - 1, s_all), BF16)], axis=2)
    vx = vx.reshape(b_, kvh, ATTN_V_ROWS, nkv, tk).transpose(0, 1, 3, 2, 4)
    o_attn = _attention(qt, k4, vx, ATTN_Q_BLOCK)

    rw_pad = jnp.zeros((dm, LANES), F32).at[:, :ne].set(router_w[i])
    rb_pad = jnp.full((1, LANES), -jnp.inf, F32).at[0, :ne].set(router_b[i])
    x_mix, h2, idx, gates = _merge(y_ssm, o_attn, glog, x, mod3, w_glu[i].astype(BF16),
                                   b_glu[i].reshape(1, wu), w_ssm_out[i].astype(BF16),
                                   w_attn_out[i].astype(BF16), w_out[i].astype(BF16),
                                   norm2_g[i].reshape(1, dm), rw_pad, rb_pad, ne, 512)

    t_ = b_ * l_
    slot_tok, dest, block_expert, n_used = _routing(idx[..., :TOP_K].reshape(t_, TOP_K), t_, ne)
    xb = h2.reshape(t_, dm)[slot_tok]
    yb = _experts(xb, block_expert, n_used, w_gate_up[i].astype(BF16), b_gate_up[i],
                  w_down[i].astype(BF16), b_down[i])
    yg = yb[dest.T].reshape(TOP_K, b_, l_, dm)
    return _final(yg, gates, x_mix, mod3, final_norm_g, 512)
```

```python
import functools
import math

import jax
import jax.numpy as jnp
from jax import lax
from jax.experimental import pallas as pl
from jax.experimental.pallas import tpu as pltpu

F32 = jnp.float32
BF16 = jnp.bfloat16

EPS = 1e-6
GRID_W = 64
N_MOD = 6
SSM_GROUP = 16
SSM_STATE = 64
HEAD_DIM = 64
GQA_GROUP = 4
ROPE_THETA = 10000.0
TOP_K = 4
SWIGLU_ALPHA = 1.702
SWIGLU_LIMIT = 7.0
LOG2E = 1.4426950408889634

LANES = 128
S5_CHUNK = 32
EXPERT_BLOCK = 512
ATTN_V_ROWS = 80
ATTN_KEY_CHUNK = 384
ATTN_Q_BLOCK = 256
VMEM_LIMIT = 56 * 1024 * 1024


def _cparams(sem, flags=None):
    return pltpu.CompilerParams(dimension_semantics=sem, vmem_limit_bytes=VMEM_LIMIT, flags=flags)


def _dot(a, b):
    return jnp.dot(a, b, preferred_element_type=F32)


def _split(a):
    hi = a.astype(BF16)
    lo = (a - hi.astype(F32)).astype(BF16)
    return hi, lo


def _dot3(a, b):
    ah, al = _split(a)
    bh, bl = _split(b)
    return _dot(ah, bh) + _dot(ah, bl) + _dot(al, bh)


def _mod_kernel(c_ref, w_ref, b_ref, o_ref):
    c = c_ref[...]
    s = c * jax.nn.sigmoid(c)
    o_ref[...] = _dot3(s, w_ref[...]) + b_ref[...]


def _modulation(c8, w_mod, b_mod):
    d = c8.shape[1]
    n = w_mod.shape[1]
    tn = 1024
    return pl.pallas_call(
        _mod_kernel,
        out_shape=jax.ShapeDtypeStruct((8, n), F32),
        grid=(n // tn,),
        in_specs=[pl.BlockSpec((8, d), lambda j: (0, 0)),
                  pl.BlockSpec((d, tn), lambda j: (0, j)),
                  pl.BlockSpec((1, tn), lambda j: (0, j))],
        out_specs=pl.BlockSpec((8, tn), lambda j: (0, j)),
        compiler_params=_cparams(("parallel",)),
        name="mod",
    )(c8, w_mod, b_mod.reshape(1, n))


def _head_norm_rope(z, gain, hsum, cos, sin, scale):
    z2h, z2l = _split(z * z)
    ss = _dot(z2h, hsum) + _dot(z2l, hsum)
    zn = z * lax.rsqrt(ss * (1.0 / HEAD_DIM) + EPS) * gain
    lane = lax.broadcasted_iota(jnp.int32, (1, LANES), 1)
    even = (lane & 1) == 0
    outs = []
    for ci in range(z.shape[1] // LANES):
        ch = zn[:, ci * LANES:(ci + 1) * LANES]
        nxt = pltpu.roll(ch, LANES - 1, axis=1)
        prv = pltpu.roll(ch, 1, axis=1)
        sw = jnp.where(even, nxt, prv)
        outs.append((ch * cos + sw * sin) * scale)
    return outs


def _inproj_kernel(with_q, dm, x_ref, mod_ref, g_ref, w_ref, cos_ref, sin_ref, qg_ref, kg_ref,
                   hsum_ref, *out_refs):
    x = x_ref[0]
    ms = jnp.mean(x * x, axis=-1, keepdims=True)
    y = x * lax.rsqrt(ms + EPS) * g_ref[...]
    sh = mod_ref[0, :, 0:dm]
    sc = mod_ref[0, :, dm:2 * dm]
    hb = (y * (1.0 + sc) + sh).astype(BF16)
    cos = cos_ref[...]
    sin = sin_ref[...]
    wu = dm // 2
    wkv = wu // GQA_GROUP
    c_k, c_v, c_q, c_g = wu, wu + wkv, wu + 2 * wkv, 2 * wu + 2 * wkv
    u_ref, k_ref, vt_ref = out_refs[:3]
    u_ref[0] = _dot(hb, w_ref[:, 0:c_k]).astype(BF16)
    kz = _dot(hb, w_ref[:, c_k:c_v])
    (kr,) = _head_norm_rope(kz, kg_ref[...], hsum_ref[0:wkv, 0:wkv], cos, sin, 1.0)
    k_ref[0] = kr.astype(BF16)
    vt_ref[0] = _dot(hb, w_ref[:, c_v:c_q]).T.astype(BF16)
    if with_q:
        qt_ref, gl_ref = out_refs[3:]
        qz = _dot(hb, w_ref[:, c_q:c_g])
        qs = _head_norm_rope(qz, qg_ref[...], hsum_ref[...], cos, sin, HEAD_DIM ** -0.5 * LOG2E)
        for ci, qc in enumerate(qs):
            qt_ref[0, ci * LANES:(ci + 1) * LANES, :] = qc.T.astype(BF16)
        gl_ref[0] = _dot(hb, w_ref[:, c_g:]).astype(BF16)


def _inproj(x, mod3, mod_row0, norm_g, w_in_b, cos, sin, qg, kg, hsum, with_q, tm):
    b_, l_, dm = x.shape
    wu = dm // 2
    wkv = wu // GQA_GROUP
    n_in = w_in_b.shape[1]
    outs = [jax.ShapeDtypeStruct((b_, l_, wu), BF16),
            jax.ShapeDtypeStruct((b_, l_, wkv), BF16),
            jax.ShapeDtypeStruct((b_, wkv, l_), BF16)]
    ospecs = [pl.BlockSpec((1, tm, wu), lambda b, i: (b, i, 0)),
              pl.BlockSpec((1, tm, wkv), lambda b, i: (b, i, 0)),
              pl.BlockSpec((1, wkv, tm), lambda b, i: (b, 0, i))]
    if with_q:
        outs += [jax.ShapeDtypeStruct((b_, wu, l_), BF16),
                 jax.ShapeDtypeStruct((b_, l_, 2 * dm), BF16)]
        ospecs += [pl.BlockSpec((1, wu, tm), lambda b, i: (b, 0, i)),
                   pl.BlockSpec((1, tm, 2 * dm), lambda b, i: (b, i, 0))]
    if mod_row0 is None:
        mod_map = lambda b, i: (b, 0, 0)
    else:
        mod_map = lambda b, i: (mod_row0, 0, 0)
    return pl.pallas_call(
        functools.partial(_inproj_kernel, with_q, dm),
        out_shape=outs,
        grid=(b_, l_ // tm),
        in_specs=[pl.BlockSpec((1, tm, dm), lambda b, i: (b, i, 0)),
                  pl.BlockSpec((1, 1, N_MOD * dm), mod_map),
                  pl.BlockSpec((1, dm), lambda b, i: (0, 0)),
                  pl.BlockSpec((dm, n_in), lambda b, i: (0, 0)),
                  pl.BlockSpec((tm, LANES), lambda b, i: (i, 0)),
                  pl.BlockSpec((tm, LANES), lambda b, i: (i, 0)),
                  pl.BlockSpec((1, wu), lambda b, i: (0, 0)),
                  pl.BlockSpec((1, wkv), lambda b, i: (0, 0)),
                  pl.BlockSpec((wu, wu), lambda b, i: (0, 0))],
        out_specs=ospecs,
        compiler_params=_cparams(("parallel", "parallel")),
        name="inproj_x" if with_q else "inproj_ctx",
    )(x, mod3, norm_g.reshape(1, dm), w_in_b, cos, sin, qg, kg, hsum)


def _cexp_pow(e, lr, li):
    mag = jnp.exp(e * lr)
    ang = e * li
    return mag * jnp.cos(ang), mag * jnp.sin(ang)


def _cmul(ar, ai, br, bi):
    return ar * br - ai * bi, ar * bi + ai * br


def _s5pre_kernel(lamr_r, lami_r, lst_r, lamr_c, lami_c, lst_c, btr_ref, bti_ref, ctr_ref, cti_ref,
                  d_ref, win_ref, wout_ref, m_ref, dec_ref):
    tc = S5_CHUNK
    half = float(tc // 2)
    rows = tc * SSM_GROUP
    gsh = SSM_GROUP.bit_length() - 1
    nsh = SSM_STATE.bit_length() - 1
    s_idx = (lax.broadcasted_iota(jnp.int32, (rows, 1), 0) >> gsh).astype(F32)
    t_idx = (lax.broadcasted_iota(jnp.int32, (1, rows), 1) >> gsh).astype(F32)
    row_i = lax.broadcasted_iota(jnp.int32, (rows, rows), 0)
    col_i = lax.broadcasted_iota(jnp.int32, (rows, rows), 1)
    lane_g = lax.broadcasted_iota(jnp.int32, (1, LANES), 1) >> nsh
    subl_g = lax.broadcasted_iota(jnp.int32, (LANES, 1), 0) >> nsh
    m_acc = [jnp.where(row_i == col_i, d_ref[0, gl], 0.0) for gl in range(2)]
    for dr in range(2):
        step = jnp.exp(lst_r[dr, 0])
        lam_r, lam_i = lamr_r[dr, 0], lami_r[dr, 0]
        lr, li = lam_r * step, lam_i * step
        lb_r, lb_i = _cexp_pow(1.0, lr, li)
        den = lam_r * lam_r + lam_i * lam_i
        nr, ni = lb_r - 1.0, lb_i
        cf_r = (nr * lam_r + ni * lam_i) / den
        cf_i = (ni * lam_r - nr * lam_i) / den
        bb_r, bb_i = _cmul(cf_r, cf_i, btr_ref[dr, 0], bti_ref[dr, 0])
        e_in = (tc - 1.0 - s_idx) if dr == 0 else s_idx
        p_r, p_i = _cexp_pow(e_in, lr, li)
        wi_r, wi_i = _cmul(bb_r, bb_i, p_r, p_i)
        e_a = (half - s_idx) if dr == 0 else (s_idx - half)
        p_r, p_i = _cexp_pow(e_a, lr, li)
        a_r, a_i = _cmul(bb_r, bb_i, p_r, p_i)
        d_r, d_i = _cexp_pow(float(tc), lr, li)
        dec_ref[0, dr, 0] = d_r
        dec_ref[0, dr, 1] = d_i
        stepc = jnp.exp(lst_c[dr, 0])
        lrc, lic = lamr_c[dr, 0] * stepc, lami_c[dr, 0] * stepc
        c_r, c_i = ctr_ref[dr, 0], cti_ref[dr, 0]
        e_b = (t_idx - half) if dr == 0 else (half - t_idx)
        q_r, q_i = _cexp_pow(e_b, lrc, lic)
        bm_r, bm_i = _cmul(c_r, c_i, q_r, q_i)
        e_o = (t_idx + 1.0) if dr == 0 else (tc - t_idx)
        q_r, q_i = _cexp_pow(e_o, lrc, lic)
        wo_r, wo_i = _cmul(c_r, c_i, q_r, q_i)
        if dr == 0:
            mask = (col_i >> gsh) >= (row_i >> gsh)
        else:
            mask = (row_i >> gsh) >= (col_i >> gsh)
        for gl in range(2):
            lsel = lane_g == gl
            ssel = subl_g == gl
            win_ref[0, gl, 2 * dr] = jnp.where(lsel, wi_r, 0.0).astype(BF16)
            win_ref[0, gl, 2 * dr + 1] = jnp.where(lsel, wi_i, 0.0).astype(BF16)
            wout_ref[0, 2 * dr, :, gl * rows:(gl + 1) * rows] = jnp.where(ssel, wo_r, 0.0).astype(BF16)
            wout_ref[0, 2 * dr + 1, :, gl * rows:(gl + 1) * rows] = jnp.where(ssel, -wo_i, 0.0).astype(BF16)
            kmat = (_dot3(jnp.where(lsel, a_r, 0.0), bm_r) - _dot3(jnp.where(lsel, a_i, 0.0), bm_i))
            m_acc[gl] = m_acc[gl] + jnp.where(mask, kmat, 0.0)
    for gl in range(2):
        m_ref[0, gl] = m_acc[gl].astype(BF16)


def _s5_operators(lam_re, lam_im, log_step, b_re, b_im, c_re, c_im, d_skip):
    g = lam_re.shape[1]
    n, p = SSM_STATE, SSM_GROUP
    npair = g // 2
    tc = S5_CHUNK
    rows = tc * p

    def row_form(a):
        return a.reshape(2, npair, 1, 2 * n)

    def col_form(a):
        return a.reshape(2, npair, 2 * n, 1)

    lst = jnp.broadcast_to(log_step[:, :, None], (2, g, n))
    bt = lambda b: jnp.broadcast_to(
        b.reshape(2, npair, 2, n, p).transpose(0, 1, 4, 2, 3)[:, :, None], (2, npair, tc, p, 2, n)
    ).reshape(2, npair, rows, 2 * n)
    ct = lambda c: jnp.broadcast_to(
        c.reshape(2, npair, 2, p, n).transpose(0, 1, 2, 4, 3)[:, :, :, :, None, :], (2, npair, 2, n, tc, p)
    ).reshape(2, npair, 2 * n, rows)
    d2 = jnp.broadcast_to(d_skip.reshape(npair, 2, 1, 1, p), (npair, 2, 1, tc, p)).reshape(npair, 2, 1, rows)

    def spec4(shape):
        return pl.BlockSpec((2, 1) + shape, lambda i: (0, i, 0, 0))

    return pl.pallas_call(
        _s5pre_kernel,
        out_shape=[jax.ShapeDtypeStruct((npair, 2, 4, rows, LANES), BF16),
                   jax.ShapeDtypeStruct((npair, 4, LANES, 2 * rows), BF16),
                   jax.ShapeDtypeStruct((npair, 2, rows, rows), BF16),
                   jax.ShapeDtypeStruct((npair, 2, 2, 1, LANES), F32)],
        grid=(npair,),
        in_specs=[spec4((1, LANES)), spec4((1, LANES)), spec4((1, LANES)),
                  spec4((LANES, 1)), spec4((LANES, 1)), spec4((LANES, 1)),
                  spec4((rows, LANES)), spec4((rows, LANES)),
                  spec4((LANES, rows)), spec4((LANES, rows)),
                  pl.BlockSpec((1, 2, 1, rows), lambda i: (i, 0, 0, 0))],
        out_specs=[pl.BlockSpec((1, 2, 4, rows, LANES), lambda i: (i, 0, 0, 0, 0)),
                   pl.BlockSpec((1, 4, LANES, 2 * rows), lambda i: (i, 0, 0, 0)),
                   pl.BlockSpec((1, 2, rows, rows), lambda i: (i, 0, 0, 0)),
                   pl.BlockSpec((1, 2, 2, 1, LANES), lambda i: (i, 0, 0, 0, 0))],
        compiler_params=_cparams(("parallel",)),
        name="s5pre",
    )(row_form(lam_re), row_form(lam_im), row_form(lst),
      col_form(lam_re), col_form(lam_im), col_form(lst),
      bt(b_re), bt(b_im), ct(c_re), ct(c_im), d2)


def _s5state_kernel(u_ref, win_ref, *s_refs):
    u0 = u_ref[0, 0]
    u1 = u_ref[1, 0]
    for kind in range(4):
        s_refs[kind][...] = _dot(u0, win_ref[0, 0, kind]) + _dot(u1, win_ref[0, 1, kind])


def _s5_states(uf, win):
    g, b_, j, rows = uf.shape
    npair = g // 2
    out = jax.ShapeDtypeStruct((j, b_ * npair * LANES), F32)
    ospec = pl.BlockSpec((j, LANES), lambda pr, b: (0, b * npair + pr))
    return pl.pallas_call(
        _s5state_kernel,
        out_shape=[out] * 4,
        grid=(npair, b_),
        in_specs=[pl.BlockSpec((2, 1, j, rows), lambda pr, b: (pr, b, 0, 0)),
                  pl.BlockSpec((1, 2, 4, rows, LANES), lambda pr, b: (pr, 0, 0, 0, 0))],
        out_specs=[ospec] * 4,
        compiler_params=_cparams(("parallel", "parallel")),
        name="s5state",
    )(uf, win)


def _s5scan_kernel(jc, sfr, sfi, sbr, sbi, dec_ref, hfr, hfi, hbr, hbi):
    j = sfr.shape[0]
    w = sfr.shape[1]
    zero = jnp.zeros((1, w), F32)

    def run(sr, si, hr_out, hi_out, ar, ai, lo, n, reverse, carry):
        def body(i, hc):
            row = (lo + n - 1 - i) if reverse else (lo + i)
            hr, hi = hc
            hr_out[pl.ds(row, 1), :] = hr
            hi_out[pl.ds(row, 1), :] = hi
            nr = ar * hr - ai * hi + sr[pl.ds(row, 1), :]
            ni = ar * hi + ai * hr + si[pl.ds(row, 1), :]
            return nr, ni
        return lax.fori_loop(0, n, body, carry)

    afr, afi = dec_ref[0, 0], dec_ref[0, 1]
    abr, abi = dec_ref[1, 0], dec_ref[1, 1]
    run(sfr, sfi, hfr, hfi, afr, afi, 0, j, False, (zero, zero))
    c = run(sbr, sbi, hbr, hbi, abr, abi, 0, jc, True, (zero, zero))
    run(sbr, sbi, hbr, hbi, abr, abi, jc, j - jc, True, c)


def _s5_scan(states, dec_cols, jc):
    j, wtot = states[0].shape
    tw = 1024
    spec = pl.BlockSpec((j, tw), lambda i: (0, i))
    return pl.pallas_call(
        functools.partial(_s5scan_kernel, jc),
        out_shape=[jax.ShapeDtypeStruct((j, wtot), F32)] * 4,
        grid=(wtot // tw,),
        in_specs=[spec] * 4 + [pl.BlockSpec((2, 2, 1, tw), lambda i: (0, 0, 0, i))],
        out_specs=[spec] * 4,
        compiler_params=_cparams(("parallel",)),
        name="s5scan",
    )(*states, dec_cols)


def _s5out_kernel(u_ref, hfr, hfi, hbr, hbi, wout_ref, m_ref, y_ref):
    rows = u_ref.shape[3]
    y2 = None
    for kind, h in enumerate((hfr, hfi, hbr, hbi)):
        t = _dot(h[...].astype(BF16), wout_ref[0, kind])
        y2 = t if y2 is None else y2 + t
    for gl in range(2):
        y = y2[:, gl * rows:(gl + 1) * rows] + _dot(u_ref[gl, 0], m_ref[0, gl])
        y_ref[gl, 0] = y.astype(BF16)


def _s5_outputs(uf, hprev, wout, m):
    g, b_, j, rows = uf.shape
    npair = g // 2
    hspec = pl.BlockSpec((j, LANES), lambda pr, b: (0, b * npair + pr))
    return pl.pallas_call(
        _s5out_kernel,
        out_shape=jax.ShapeDtypeStruct((g, b_, j, rows), BF16),
        grid=(npair, b_),
        in_specs=[pl.BlockSpec((2, 1, j, rows), lambda pr, b: (pr, b, 0, 0))] + [hspec] * 4 +
                 [pl.BlockSpec((1, 4, LANES, 2 * rows), lambda pr, b: (pr, 0, 0, 0)),
                  pl.BlockSpec((1, 2, rows, rows), lambda pr, b: (pr, 0, 0, 0))],
        out_specs=pl.BlockSpec((2, 1, j, rows), lambda pr, b: (pr, b, 0, 0)),
        compiler_params=_cparams(("parallel", "parallel")),
        name="s5out",
    )(uf, *hprev, wout, m)


def _s5_mixer(u, u_c, ops):
    win, wout, m, dec = ops
    b_, l_, w = u.shape
    lc = u_c.shape[1]
    tc, p = S5_CHUNK, SSM_GROUP
    g = w // p
    npair = g // 2
    jc, jl = lc // tc, l_ // tc
    j = jc + jl
    ua = jnp.concatenate([u_c, u], axis=1)
    uf = ua.reshape(b_, j, tc, g, p).transpose(3, 0, 1, 2, 4).reshape(g, b_, j, tc * p)
    states = _s5_states(uf, win)
    dec_cols = jnp.broadcast_to(dec.transpose(1, 2, 3, 0, 4)[:, :, :, None], (2, 2, 1, b_, npair, LANES))
    dec_cols = dec_cols.reshape(2, 2, 1, b_ * npair * LANES)
    hprev = _s5_scan(states, dec_cols, jc)
    yf = _s5_outputs(uf, hprev, wout, m)
    y = yf[:, :, jc:].reshape(g, b_, jl, tc, p).transpose(1, 2, 3, 0, 4).reshape(b_, l_, w)
    return y


def _attn_kernel(nkv, qt_ref, k_ref, vx_ref, o_ref, s0_ref, s1_ref):
    hd = HEAD_DIM
    kvh = pl.program_id(1)
    tq = qt_ref.shape[2]
    gq = qt_ref.shape[1] // hd
    kw = k_ref.shape[3]
    vr = vx_ref.shape[3]
    own = (lax.broadcasted_iota(jnp.int32, (kw, tq), 0) >> (hd.bit_length() - 1)) == kvh
    qp = []
    for g in range(gq):
        qg = qt_ref[0, g * hd:(g + 1) * hd, :].astype(F32)
        q2 = jnp.concatenate([qg] * (kw // hd), axis=0)
        qp.append(jnp.where(own, q2, 0.0).astype(BF16))
    qp = jnp.concatenate(qp, axis=1)
    m0 = jnp.full((1, gq * tq), -jnp.inf, F32)
    acc0 = jnp.zeros((vr, gq * tq), F32)

    def scores(jb, s_ref):
        s = _dot(k_ref[0, jb], qp)
        s_ref[...] = s
        return jnp.max(s, axis=0, keepdims=True)

    def consume(jb, s_ref, cmax, m_prev, acc):
        m_new = jnp.maximum(m_prev, cmax)
        alpha = jnp.exp2(m_prev - m_new)
        p = jnp.exp2(s_ref[...] - m_new).astype(BF16)
        return m_new, alpha * acc + _dot(vx_ref[0, 0, jb], p)

    def body(i, carry):
        m, acc, cmax = carry
        jb = 2 * i
        cmax1 = scores(jb + 1, s1_ref)
        m, acc = consume(jb, s0_ref, cmax, m, acc)
        cmax2 = scores(jb + 2, s0_ref)
        m, acc = consume(jb + 1, s1_ref, cmax1, m, acc)
        return m, acc, cmax2

    npair = (nkv - 1) // 2
    m, acc, cmax = lax.fori_loop(0, npair, body, (m0, acc0, scores(0, s0_ref)))
    if (nkv - 1) % 2:
        cmax1 = scores(nkv - 1, s1_ref)
        m, acc = consume(nkv - 2, s0_ref, cmax, m, acc)
        m, acc = consume(nkv - 1, s1_ref, cmax1, m, acc)
    else:
        m, acc = consume(nkv - 1, s0_ref, cmax, m, acc)
    o = acc[0:hd] / acc[hd:hd + 1]
    ot = jnp.concatenate([o[:, g * tq:(g + 1) * tq] for g in range(gq)], axis=0)
    o_ref[0] = ot.T.astype(BF16)


def _attention(qt, k4, vx, tq):
    b_, wq, l_ = qt.shape
    _, kvh, nkv, vr, tk = vx.shape
    kw = k4.shape[3]
    gw = wq // kvh
    return pl.pallas_call(
        functools.partial(_attn_kernel, nkv),
        out_shape=jax.ShapeDtypeStruct((b_, l_, wq), BF16),
        grid=(b_, kvh, l_ // tq),
        in_specs=[pl.BlockSpec((1, gw, tq), lambda b, h, i: (b, h, i)),
                  pl.BlockSpec((1, nkv, tk, kw), lambda b, h, i: (b, 0, 0, 0)),
                  pl.BlockSpec((1, 1, nkv, vr, tk), lambda b, h, i: (b, h, 0, 0, 0))],
        out_specs=pl.BlockSpec((1, tq, gw), lambda b, h, i: (b, i, h)),
        scratch_shapes=[pltpu.VMEM((tk, (gw // HEAD_DIM) * tq), F32)] * 2,
        compiler_params=_cparams(("parallel", "parallel", "parallel")),
        name="attn",
    )(qt, k4, vx)


def _merge_kernel(dm, ne, ys_ref, oa_ref, gl_ref, x_ref, mod_ref, wglu_ref, bglu_ref, wso_ref, wao_ref,
                  wout_ref, n2g_ref, rw_ref, rb_ref, xmix_ref, h2_ref, idx_ref, gate_ref):
    s = jax.nn.gelu(ys_ref[0].astype(F32))
    s = s * jax.nn.sigmoid(_dot(s.astype(BF16), wglu_ref[...]) + bglu_ref[...])
    a = _dot(s.astype(BF16), wso_ref[...])
    bq = _dot(oa_ref[0], wao_ref[...])
    gts = jax.nn.sigmoid(gl_ref[0].astype(F32))
    merged = gts[:, 0:dm] * a + gts[:, dm:2 * dm] * bq
    out = _dot(merged.astype(BF16), wout_ref[...])
    g1 = mod_ref[0, :, 2 * dm:3 * dm]
    sh2 = mod_ref[0, :, 3 * dm:4 * dm]
    sc2 = mod_ref[0, :, 4 * dm:5 * dm]
    xm = x_ref[0] + g1 * out
    xmix_ref[0] = xm
    ms = jnp.mean(xm * xm, axis=-1, keepdims=True)
    h2 = xm * lax.rsqrt(ms + EPS) * n2g_ref[...] * (1.0 + sc2) + sh2
    h2_ref[0] = h2.astype(BF16)
    logits = _dot3(h2, rw_ref[...]) + rb_ref[...]
    lane = lax.broadcasted_iota(jnp.int32, logits.shape, 1).astype(F32)
    neg = jnp.float32(-jnp.inf)
    vals, idxs = [], []
    l = logits
    for _ in range(TOP_K):
        mx = jnp.max(l, axis=-1, keepdims=True)
        ix = jnp.min(jnp.where(l == mx, lane, float(LANES)), axis=-1, keepdims=True)
        vals.append(mx)
        idxs.append(ix)
        l = jnp.where(lane == ix, neg, l)
    es = [jnp.exp(v - vals[0]) for v in vals]
    den = es[0] + es[1] + es[2] + es[3]
    idx_out = jnp.zeros(logits.shape, F32)
    gate_out = jnp.zeros(logits.shape, F32)
    for k in range(TOP_K):
        idx_out = jnp.where(lane == float(k), idxs[k], idx_out)
        gate_out = jnp.where(lane == float(k), es[k] / den, gate_out)
    idx_ref[0] = idx_out.astype(jnp.int32)
    gate_ref[0] = gate_out


def _merge(y_ssm, o_attn, glog, x, mod3, w_glu, b_glu, w_ssm_out, w_attn_out, w_out, norm2_g,
           rw_pad, rb_pad, ne, tm):
    b_, l_, dm = x.shape
    wu = dm // 2
    tok = lambda w: pl.BlockSpec((1, tm, w), lambda b, i: (b, i, 0))
    full = lambda r, c: pl.BlockSpec((r, c), lambda b, i: (0, 0))
    return pl.pallas_call(
        functools.partial(_merge_kernel, dm, ne),
        out_shape=[jax.ShapeDtypeStruct((b_, l_, dm), F32),
                   jax.ShapeDtypeStruct((b_, l_, dm), BF16),
                   jax.ShapeDtypeStruct((b_, l_, LANES), jnp.int32),
                   jax.ShapeDtypeStruct((b_, l_, LANES), F32)],
        grid=(b_, l_ // tm),
        in_specs=[tok(wu), tok(wu), tok(2 * dm), tok(dm),
                  pl.BlockSpec((1, 1, N_MOD * dm), lambda b, i: (b, 0, 0)),
                  full(wu, wu), full(1, wu), full(wu, dm), full(wu, dm), full(dm, dm),
                  full(1, dm), full(dm, LANES), full(1, LANES)],
        out_specs=[tok(dm), tok(dm), tok(LANES), tok(LANES)],
        compiler_params=_cparams(("parallel", "parallel")),
        name="merge",
    )(y_ssm, o_attn, glog, x, mod3, w_glu, b_glu, w_ssm_out, w_attn_out, w_out, norm2_g, rw_pad, rb_pad)


def _expert_kernel(de, be_ref, nu_ref, x_ref, wgu_ref, bgu_ref, wd_ref, bd_ref, y_ref, wgu_b, wd_b):
    i = pl.program_id(0)
    used = i < nu_ref[0]
    new_expert = (i == 0) | (be_ref[i] != be_ref[jnp.maximum(i - 1, 0)])

    @pl.when(used & new_expert)
    def _():
        wgu_b[...] = wgu_ref[0].astype(BF16)
        wd_b[...] = wd_ref[0].astype(BF16)

    @pl.when(used)
    def _():
        gu = _dot(x_ref[...], wgu_b[...]) + bgu_ref[0]
        gate = jnp.minimum(gu[:, 0:de], SWIGLU_LIMIT)
        up = jnp.clip(gu[:, de:2 * de], -SWIGLU_LIMIT, SWIGLU_LIMIT)
        act = (up + 1.0) * (gate * jax.nn.sigmoid(SWIGLU_ALPHA * gate))
        y_ref[...] = (_dot(act.astype(BF16), wd_b[...]) + bd_ref[0]).astype(BF16)

    @pl.when(jnp.logical_not(used))
    def _():
        y_ref[...] = jnp.zeros(y_ref.shape, y_ref.dtype)


def _experts(xb, block_expert, n_used, w_gate_up, b_gate_up, w_down, b_down):
    n_slots, dm = xb.shape
    ne, _, de2 = w_gate_up.shape
    de = de2 // 2
    nb = n_slots // EXPERT_BLOCK
    return pl.pallas_call(
        functools.partial(_expert_kernel, de),
        out_shape=jax.ShapeDtypeStruct((n_slots, dm), BF16),
        grid_spec=pltpu.PrefetchScalarGridSpec(
            num_scalar_prefetch=2,
            grid=(nb,),
            in_specs=[pl.BlockSpec((EXPERT_BLOCK, dm), lambda i, be, nu: (i, 0)),
                      pl.BlockSpec((1, dm, de2), lambda i, be, nu: (be[i], 0, 0)),
                      pl.BlockSpec((1, 1, de2), lambda i, be, nu: (be[i], 0, 0)),
                      pl.BlockSpec((1, de, dm), lambda i, be, nu: (be[i], 0, 0)),
                      pl.BlockSpec((1, 1, dm), lambda i, be, nu: (be[i], 0, 0))],
            out_specs=pl.BlockSpec((EXPERT_BLOCK, dm), lambda i, be, nu: (i, 0)),
            scratch_shapes=[pltpu.VMEM((dm, de2), BF16), pltpu.VMEM((de, dm), BF16)],
        ),
        compiler_params=_cparams(("arbitrary",)),
        name="expert",
    )(block_expert, n_used, xb, w_gate_up, b_gate_up.reshape(ne, 1, de2), w_down, b_down.reshape(ne, 1, dm))


def _final_kernel(dm, yg_ref, gate_ref, xm_ref, mod_ref, fg_ref, o_ref):
    gates = gate_ref[0]
    f = None
    for k in range(TOP_K):
        t = gates[:, k:k + 1] * yg_ref[0, :, k * dm:(k + 1) * dm].astype(F32)
        f = t if f is None else f + t
    g2 = mod_ref[0, :, 5 * dm:6 * dm]
    xo = xm_ref[0] + g2 * f
    ms = jnp.mean(xo * xo, axis=-1, keepdims=True)
    o_ref[0] = xo * lax.rsqrt(ms + EPS) * fg_ref[...]


def _final(yg, gates, x_mix, mod3, final_g, tm):
    b_, l_, dm = x_mix.shape
    return pl.pallas_call(
        functools.partial(_final_kernel, dm),
        out_shape=jax.ShapeDtypeStruct((b_, l_, dm), F32),
        grid=(b_, l_ // tm),
        in_specs=[pl.BlockSpec((1, tm, TOP_K * dm), lambda b, i: (b, i, 0)),
                  pl.BlockSpec((1, tm, LANES), lambda b, i: (b, i, 0)),
                  pl.BlockSpec((1, tm, dm), lambda b, i: (b, i, 0)),
                  pl.BlockSpec((1, 1, N_MOD * dm), lambda b, i: (b, 0, 0)),
                  pl.BlockSpec((1, dm), lambda b, i: (0, 0))],
        out_specs=pl.BlockSpec((1, tm, dm), lambda b, i: (b, i, 0)),
        compiler_params=_cparams(("parallel", "parallel")),
        name="final",
    )(yg, gates, x_mix, mod3, final_g.reshape(1, dm))


def _rope_tables(l_):
    half = HEAD_DIM // 2
    inv_freq = ROPE_THETA ** (-jnp.arange(0, half, 2, dtype=F32) / half)
    t = jnp.arange(l_, dtype=jnp.int32)
    row_id, col_id = t // GRID_W, t % GRID_W
    ang = jnp.concatenate([row_id.astype(F32)[:, None] * inv_freq,
                           col_id.astype(F32)[:, None] * inv_freq], axis=-1)
    cos = jnp.repeat(jnp.cos(ang), 2, axis=-1)
    sin = jnp.repeat(jnp.sin(ang), 2, axis=-1)
    sign = jnp.tile(jnp.array([-1.0, 1.0], F32), HEAD_DIM // 2)
    reps = LANES // HEAD_DIM
    return jnp.tile(cos, (1, reps)), jnp.tile(sin * sign, (1, reps))


def _routing(idx, t_, ne):
    m_ = t_ * TOP_K
    blk = EXPERT_BLOCK
    nb = (m_ + ne * (blk - 1)) // blk
    flat_e = idx.reshape(-1)
    onehot = flat_e[:, None] == jnp.arange(ne, dtype=jnp.int32)[None, :]
    counts = jnp.sum(onehot.astype(jnp.int32), axis=0)
    starts = jnp.cumsum(counts) - counts
    blk_counts = (counts + blk - 1) // blk
    blk_ends = jnp.cumsum(blk_counts)
    shift = (blk_ends - blk_counts) * blk - starts
    order = jnp.argsort(flat_e)
    inv = jnp.argsort(order)
    dest = inv.astype(jnp.int32) + jnp.sum(jnp.where(onehot, shift[None, :], 0), axis=1)
    block_expert = jnp.minimum(jnp.sum(jnp.arange(nb)[:, None] >= blk_ends[None, :], axis=1), ne - 1)
    slot_src = jnp.arange(nb * blk, dtype=jnp.int32) - jnp.repeat(shift[block_expert], blk)
    slot_tok = order[jnp.clip(slot_src, 0, m_ - 1)] // TOP_K
    return (slot_tok.astype(jnp.int32), dest.astype(jnp.int32), block_expert.astype(jnp.int32),
            blk_ends[-1:].astype(jnp.int32))


def kernel(x, c, ctx, c_ctx, w_mod, b_mod, norm1_g, norm2_g, w_in, s5_lam_re, s5_lam_im, s5_log_step,
           s5_b_re, s5_b_im, s5_c_re, s5_c_im, s5_d, w_glu, b_glu, w_ssm_out, q_norm_g, k_norm_g,
           w_attn_out, w_out, router_w, router_b, w_gate_up, b_gate_up, w_down, b_down, final_norm_g):
    b_, l_, dm = x.shape
    lc = ctx.shape[1]
    depth = w_mod.shape[0]
    assert depth == 1, "single-layer block"
    assert b_ <= 7 and l_ % 512 == 0 and lc % S5_CHUNK == 0
    wu = dm // 2
    wkv = wu // GQA_GROUP
    kvh = wkv // HEAD_DIM
    ne = router_w.shape[-1]
    i = 0

    c8 = jnp.zeros((8, dm), F32).at[:b_].set(c).at[b_].set(c_ctx)
    mod3 = _modulation(c8, w_mod[i], b_mod[i]).reshape(8, 1, N_MOD * dm)

    w_in_b = w_in[i].astype(BF16)
    cos, sin = _rope_tables(l_)
    hsum = jnp.kron(jnp.eye(wu // HEAD_DIM, dtype=F32), jnp.ones((HEAD_DIM, HEAD_DIM), F32)).astype(BF16)
    qg = jnp.tile(q_norm_g[i], wu // HEAD_DIM).reshape(1, wu)
    kg = jnp.tile(k_norm_g[i], wkv // HEAD_DIM).reshape(1, wkv)
    u, k, vt, qt, glog = _inproj(x, mod3, None, norm1_g[i], w_in_b, cos, sin, qg, kg, hsum, True, 512)
    ones_t = jnp.ones((lc, LANES), F32)
    u_c, k_c, vt_c = _inproj(ctx, mod3, b_, norm1_g[i], w_in_b[:, :wu + 2 * wkv], ones_t, 0.0 * ones_t,
                             qg, kg, hsum, False, lc)

    ops = _s5_operators(s5_lam_re[i], s5_lam_im[i], s5_log_step[i], s5_b_re[i], s5_b_im[i],
                        s5_c_re[i], s5_c_im[i], s5_d[i])
    y_ssm = _s5_mixer(u, u_c, ops)

    s_all = lc + l_
    tk = max(t for t in range(LANES, ATTN_KEY_CHUNK + 1, LANES) if s_all % t == 0)
    nkv = s_all // tk
    k4 = jnp.concatenate([k_c, k], axis=1).reshape(b_, nkv, tk, wkv)
    vt_all = jnp.concatenate([vt_c, vt], axis=2).reshape(b_, kvh, HEAD_DIM, s_all)
    vx = jnp.concatenate([vt_all, jnp.ones((b_, kvh, 1, s_all), BF16),
                          jnp.zeros((b_, kvh, ATTN_V_ROWS - HEAD_DIM - 1, s_all), BF16)], axis=2)
    vx = vx.reshape(b_, kvh, ATTN_V_ROWS, nkv, tk).transpose(0, 1, 3, 2, 4)
    o_attn = _attention(qt, k4, vx, ATTN_Q_BLOCK)

    rw_pad = jnp.zeros((dm, LANES), F32).at[:, :ne].set(router_w[i])
    rb_pad = jnp.full((1, LANES), -jnp.inf, F32).at[0, :ne].set(router_b[i])
    x_mix, h2, idx, gates = _merge(y_ssm, o_attn, glog, x, mod3, w_glu[i].astype(BF16),
                                   b_glu[i].reshape(1, wu), w_ssm_out[i].astype(BF16),
                                   w_attn_out[i].astype(BF16), w_out[i].astype(BF16),
                                   norm2_g[i].reshape(1, dm), rw_pad, rb_pad, ne, 512)

    t_ = b_ * l_
    slot_tok, dest, block_expert, n_used = _routing(idx[..., :TOP_K].reshape(t_, TOP_K), t_, ne)
    xb = h2.reshape(t_, dm)[slot_tok]
    yb = _experts(xb, block_expert, n_used, w_gate_up[i], b_gate_up[i], w_down[i], b_down[i])
    yg = yb[dest].reshape(b_, l_, TOP_K * dm)
    return _final(yg, gates, x_mix, mod3, final_norm_g, 512)
```

```python
import functools
import math

import jax
import jax.numpy as jnp
from jax import lax
from jax.experimental import pallas as pl
from jax.experimental.pallas import tpu as pltpu
from jax.experimental.pallas import tpu_sc as plsc

F32 = jnp.float32
BF16 = jnp.bfloat16

EPS = 1e-6
GRID_W = 64
N_MOD = 6
SSM_GROUP = 16
SSM_STATE = 64
HEAD_DIM = 64
GQA_GROUP = 4
ROPE_THETA = 10000.0
TOP_K = 4
SWIGLU_ALPHA = 1.702
SWIGLU_LIMIT = 7.0
LOG2E = 1.4426950408889634

LANES = 128
S5_CHUNK = 32
EXPERT_BLOCK = 512
ATTN_V_ROWS = 80
ATTN_KEY_CHUNK = 384
ATTN_Q_BLOCK = 256
SC_GATHER_WINDOW = 128
SC_WORKERS = 32
SLOT_BLOCK_MULTIPLE = SC_GATHER_WINDOW * SC_WORKERS // EXPERT_BLOCK
VMEM_LIMIT = 56 * 1024 * 1024


def _cparams(sem, flags=None):
    return pltpu.CompilerParams(dimension_semantics=sem, vmem_limit_bytes=VMEM_LIMIT, flags=flags)


def _dot(a, b):
    return jnp.dot(a, b, preferred_element_type=F32)


def _split(a):
    hi = a.astype(BF16)
    lo = (a - hi.astype(F32)).astype(BF16)
    return hi, lo


def _pack2(lo, hi):
    lo_w = lax.bitcast_convert_type(lo.astype(BF16).astype(F32), jnp.uint32) >> 16
    hi_w = lax.bitcast_convert_type(hi.astype(BF16).astype(F32), jnp.uint32) & jnp.uint32(0xFFFF0000)
    return lo_w | hi_w


def _unpack2(words):
    lo = lax.bitcast_convert_type(words << 16, F32)
    hi = lax.bitcast_convert_type(words & jnp.uint32(0xFFFF0000), F32)
    return lo, hi


def _dot3(a, b):
    ah, al = _split(a)
    bh, bl = _split(b)
    return _dot(ah, bh) + _dot(ah, bl) + _dot(al, bh)


def _mod_kernel(c_ref, w_ref, b_ref, o_ref):
    c = c_ref[...]
    s = c * jax.nn.sigmoid(c)
    o_ref[...] = _dot3(s, w_ref[...]) + b_ref[...]


def _modulation(c8, w_mod, b_mod):
    d = c8.shape[1]
    n = w_mod.shape[1]
    tn = 1024
    return pl.pallas_call(
        _mod_kernel,
        out_shape=jax.ShapeDtypeStruct((8, n), F32),
        grid=(n // tn,),
        in_specs=[pl.BlockSpec((8, d), lambda j: (0, 0)),
                  pl.BlockSpec((d, tn), lambda j: (0, j)),
                  pl.BlockSpec((1, tn), lambda j: (0, j))],
        out_specs=pl.BlockSpec((8, tn), lambda j: (0, j)),
        compiler_params=_cparams(("parallel",)),
        name="mod",
    )(c8, w_mod, b_mod.reshape(1, n))


def _head_norm_rope(z, gain, hsum, cos, sin, scale):
    z2h, z2l = _split(z * z)
    ss = _dot(z2h, hsum) + _dot(z2l, hsum)
    zn = z * lax.rsqrt(ss * (1.0 / HEAD_DIM) + EPS) * gain
    lane = lax.broadcasted_iota(jnp.int32, (1, LANES), 1)
    even = (lane & 1) == 0
    outs = []
    for ci in range(z.shape[1] // LANES):
        ch = zn[:, ci * LANES:(ci + 1) * LANES]
        nxt = pltpu.roll(ch, LANES - 1, axis=1)
        prv = pltpu.roll(ch, 1, axis=1)
        sw = jnp.where(even, nxt, prv)
        outs.append((ch * cos + sw * sin) * scale)
    return outs


def _inproj_kernel(with_q, dm, x_ref, mod_ref, g_ref, w_ref, cos_ref, sin_ref, qg_ref, kg_ref,
                   hsum_ref, *out_refs):
    x = x_ref[0]
    ms = jnp.mean(x * x, axis=-1, keepdims=True)
    y = x * lax.rsqrt(ms + EPS) * g_ref[...]
    sh = mod_ref[0, :, 0:dm]
    sc = mod_ref[0, :, dm:2 * dm]
    hb = (y * (1.0 + sc) + sh).astype(BF16)
    cos = cos_ref[...]
    sin = sin_ref[...]
    wu = dm // 2
    wkv = wu // GQA_GROUP
    c_k, c_v, c_q, c_g = wu, wu + wkv, wu + 2 * wkv, 2 * wu + 2 * wkv
    u_ref, k_ref, vt_ref = out_refs[:3]
    u_ref[0] = _dot(hb, w_ref[:, 0:c_k]).astype(BF16)
    kz = _dot(hb, w_ref[:, c_k:c_v])
    (kr,) = _head_norm_rope(kz, kg_ref[...], hsum_ref[0:wkv, 0:wkv], cos, sin, 1.0)
    k_ref[0] = kr.astype(BF16)
    vt_ref[0] = _dot(hb, w_ref[:, c_v:c_q]).T.astype(BF16)
    if with_q:
        qt_ref, gl_ref = out_refs[3:]
        qz = _dot(hb, w_ref[:, c_q:c_g])
        qs = _head_norm_rope(qz, qg_ref[...], hsum_ref[...], cos, sin, HEAD_DIM ** -0.5 * LOG2E)
        for ci, qc in enumerate(qs):
            qt_ref[0, ci * LANES:(ci + 1) * LANES, :] = qc.T.astype(BF16)
        gl_ref[0] = _dot(hb, w_ref[:, c_g:]).astype(BF16)


def _inproj(x, mod3, mod_row0, norm_g, w_in_b, cos, sin, qg, kg, hsum, with_q, tm):
    b_, l_, dm = x.shape
    wu = dm // 2
    wkv = wu // GQA_GROUP
    n_in = w_in_b.shape[1]
    outs = [jax.ShapeDtypeStruct((b_, l_, wu), BF16),
            jax.ShapeDtypeStruct((b_, l_, wkv), BF16),
            jax.ShapeDtypeStruct((b_, wkv, l_), BF16)]
    ospecs = [pl.BlockSpec((1, tm, wu), lambda b, i: (b, i, 0)),
              pl.BlockSpec((1, tm, wkv), lambda b, i: (b, i, 0)),
              pl.BlockSpec((1, wkv, tm), lambda b, i: (b, 0, i))]
    if with_q:
        outs += [jax.ShapeDtypeStruct((b_, wu, l_), BF16),
                 jax.ShapeDtypeStruct((b_, l_, 2 * dm), BF16)]
        ospecs += [pl.BlockSpec((1, wu, tm), lambda b, i: (b, 0, i)),
                   pl.BlockSpec((1, tm, 2 * dm), lambda b, i: (b, i, 0))]
    if mod_row0 is None:
        mod_map = lambda b, i: (b, 0, 0)
    else:
        mod_map = lambda b, i: (mod_row0, 0, 0)
    return pl.pallas_call(
        functools.partial(_inproj_kernel, with_q, dm),
        out_shape=outs,
        grid=(b_, l_ // tm),
        in_specs=[pl.BlockSpec((1, tm, dm), lambda b, i: (b, i, 0)),
                  pl.BlockSpec((1, 1, N_MOD * dm), mod_map),
                  pl.BlockSpec((1, dm), lambda b, i: (0, 0)),
                  pl.BlockSpec((dm, n_in), lambda b, i: (0, 0)),
                  pl.BlockSpec((tm, LANES), lambda b, i: (i, 0)),
                  pl.BlockSpec((tm, LANES), lambda b, i: (i, 0)),
                  pl.BlockSpec((1, wu), lambda b, i: (0, 0)),
                  pl.BlockSpec((1, wkv), lambda b, i: (0, 0)),
                  pl.BlockSpec((wu, wu), lambda b, i: (0, 0))],
        out_specs=ospecs,
        compiler_params=_cparams(("parallel", "parallel")),
        name="inproj_x" if with_q else "inproj_ctx",
    )(x, mod3, norm_g.reshape(1, dm), w_in_b, cos, sin, qg, kg, hsum)


def _cexp_pow(e, lr, li):
    mag = jnp.exp(e * lr)
    ang = e * li
    return mag * jnp.cos(ang), mag * jnp.sin(ang)


def _cmul(ar, ai, br, bi):
    return ar * br - ai * bi, ar * bi + ai * br


def _s5pre_kernel(lamr_r, lami_r, lst_r, lamr_c, lami_c, lst_c, btr_ref, bti_ref, ctr_ref, cti_ref,
                  d_ref, win_ref, wout_ref, m_ref, dec_ref):
    tc = S5_CHUNK
    half = float(tc // 2)
    rows = tc * SSM_GROUP
    gsh = SSM_GROUP.bit_length() - 1
    nsh = SSM_STATE.bit_length() - 1
    s_idx = (lax.broadcasted_iota(jnp.int32, (rows, 1), 0) >> gsh).astype(F32)
    t_idx = (lax.broadcasted_iota(jnp.int32, (1, rows), 1) >> gsh).astype(F32)
    row_i = lax.broadcasted_iota(jnp.int32, (rows, rows), 0)
    col_i = lax.broadcasted_iota(jnp.int32, (rows, rows), 1)
    lane_g = lax.broadcasted_iota(jnp.int32, (1, LANES), 1) >> nsh
    subl_g = lax.broadcasted_iota(jnp.int32, (LANES, 1), 0) >> nsh
    m_acc = [jnp.where(row_i == col_i, d_ref[0, gl], 0.0) for gl in range(2)]
    for dr in range(2):
        step = jnp.exp(lst_r[dr, 0])
        lam_r, lam_i = lamr_r[dr, 0], lami_r[dr, 0]
        lr, li = lam_r * step, lam_i * step
        lb_r, lb_i = _cexp_pow(1.0, lr, li)
        den = lam_r * lam_r + lam_i * lam_i
        nr, ni = lb_r - 1.0, lb_i
        cf_r = (nr * lam_r + ni * lam_i) / den
        cf_i = (ni * lam_r - nr * lam_i) / den
        bb_r, bb_i = _cmul(cf_r, cf_i, btr_ref[dr, 0], bti_ref[dr, 0])
        e_in = (tc - 1.0 - s_idx) if dr == 0 else s_idx
        p_r, p_i = _cexp_pow(e_in, lr, li)
        wi_r, wi_i = _cmul(bb_r, bb_i, p_r, p_i)
        e_a = (half - s_idx) if dr == 0 else (s_idx - half)
        p_r, p_i = _cexp_pow(e_a, lr, li)
        a_r, a_i = _cmul(bb_r, bb_i, p_r, p_i)
        d_r, d_i = _cexp_pow(float(tc), lr, li)
        dec_ref[0, dr, 0] = d_r
        dec_ref[0, dr, 1] = d_i
        stepc = jnp.exp(lst_c[dr, 0])
        lrc, lic = lamr_c[dr, 0] * stepc, lami_c[dr, 0] * stepc
        c_r, c_i = ctr_ref[dr, 0], cti_ref[dr, 0]
        e_b = (t_idx - half) if dr == 0 else (half - t_idx)
        q_r, q_i = _cexp_pow(e_b, lrc, lic)
        bm_r, bm_i = _cmul(c_r, c_i, q_r, q_i)
        e_o = (t_idx + 1.0) if dr == 0 else (tc - t_idx)
        q_r, q_i = _cexp_pow(e_o, lrc, lic)
        wo_r, wo_i = _cmul(c_r, c_i, q_r, q_i)
        if dr == 0:
            mask = (col_i >> gsh) >= (row_i >> gsh)
        else:
            mask = (row_i >> gsh) >= (col_i >> gsh)
        for gl in range(2):
            lsel = lane_g == gl
            ssel = subl_g == gl
            win_ref[0, gl, 2 * dr] = jnp.where(lsel, wi_r, 0.0).astype(BF16)
            win_ref[0, gl, 2 * dr + 1] = jnp.where(lsel, wi_i, 0.0).astype(BF16)
            wout_ref[0, 2 * dr, :, gl * rows:(gl + 1) * rows] = jnp.where(ssel, wo_r, 0.0).astype(BF16)
            wout_ref[0, 2 * dr + 1, :, gl * rows:(gl + 1) * rows] = jnp.where(ssel, -wo_i, 0.0).astype(BF16)
            kmat = (_dot3(jnp.where(lsel, a_r, 0.0), bm_r) - _dot3(jnp.where(lsel, a_i, 0.0), bm_i))
            m_acc[gl] = m_acc[gl] + jnp.where(mask, kmat, 0.0)
    for gl in range(2):
        m_ref[0, gl] = m_acc[gl].astype(BF16)


def _s5_operators(lam_re, lam_im, log_step, b_re, b_im, c_re, c_im, d_skip):
    g = lam_re.shape[1]
    n, p = SSM_STATE, SSM_GROUP
    npair = g // 2
    tc = S5_CHUNK
    rows = tc * p

    def row_form(a):
        return a.reshape(2, npair, 1, 2 * n)

    def col_form(a):
        return a.reshape(2, npair, 2 * n, 1)

    lst = jnp.broadcast_to(log_step[:, :, None], (2, g, n))
    bt = lambda b: jnp.broadcast_to(
        b.reshape(2, npair, 2, n, p).transpose(0, 1, 4, 2, 3)[:, :, None], (2, npair, tc, p, 2, n)
    ).reshape(2, npair, rows, 2 * n)
    ct = lambda c: jnp.broadcast_to(
        c.reshape(2, npair, 2, p, n).transpose(0, 1, 2, 4, 3)[:, :, :, :, None, :], (2, npair, 2, n, tc, p)
    ).reshape(2, npair, 2 * n, rows)
    d2 = jnp.broadcast_to(d_skip.reshape(npair, 2, 1, 1, p), (npair, 2, 1, tc, p)).reshape(npair, 2, 1, rows)

    def spec4(shape):
        return pl.BlockSpec((2, 1) + shape, lambda i: (0, i, 0, 0))

    return pl.pallas_call(
        _s5pre_kernel,
        out_shape=[jax.ShapeDtypeStruct((npair, 2, 4, rows, LANES), BF16),
                   jax.ShapeDtypeStruct((npair, 4, LANES, 2 * rows), BF16),
                   jax.ShapeDtypeStruct((npair, 2, rows, rows), BF16),
                   jax.ShapeDtypeStruct((npair, 2, 2, 1, LANES), F32)],
        grid=(npair,),
        in_specs=[spec4((1, LANES)), spec4((1, LANES)), spec4((1, LANES)),
                  spec4((LANES, 1)), spec4((LANES, 1)), spec4((LANES, 1)),
                  spec4((rows, LANES)), spec4((rows, LANES)),
                  spec4((LANES, rows)), spec4((LANES, rows)),
                  pl.BlockSpec((1, 2, 1, rows), lambda i: (i, 0, 0, 0))],
        out_specs=[pl.BlockSpec((1, 2, 4, rows, LANES), lambda i: (i, 0, 0, 0, 0)),
                   pl.BlockSpec((1, 4, LANES, 2 * rows), lambda i: (i, 0, 0, 0)),
                   pl.BlockSpec((1, 2, rows, rows), lambda i: (i, 0, 0, 0)),
                   pl.BlockSpec((1, 2, 2, 1, LANES), lambda i: (i, 0, 0, 0, 0))],
        compiler_params=_cparams(("parallel",)),
        name="s5pre",
    )(row_form(lam_re), row_form(lam_im), row_form(lst),
      col_form(lam_re), col_form(lam_im), col_form(lst),
      bt(b_re), bt(b_im), ct(c_re), ct(c_im), d2)


def _s5state_kernel(u_ref, win_ref, *s_refs):
    u0 = u_ref[0, 0]
    u1 = u_ref[1, 0]
    for kind in range(4):
        s_refs[kind][...] = _dot(u0, win_ref[0, 0, kind]) + _dot(u1, win_ref[0, 1, kind])


def _s5_states(uf, win):
    g, b_, j, rows = uf.shape
    npair = g // 2
    out = jax.ShapeDtypeStruct((j, b_ * npair * LANES), F32)
    ospec = pl.BlockSpec((j, LANES), lambda pr, b: (0, b * npair + pr))
    return pl.pallas_call(
        _s5state_kernel,
        out_shape=[out] * 4,
        grid=(npair, b_),
        in_specs=[pl.BlockSpec((2, 1, j, rows), lambda pr, b: (pr, b, 0, 0)),
                  pl.BlockSpec((1, 2, 4, rows, LANES), lambda pr, b: (pr, 0, 0, 0, 0))],
        out_specs=[ospec] * 4,
        compiler_params=_cparams(("parallel", "parallel")),
        name="s5state",
    )(uf, win)


def _s5scan_kernel(jc, sfr, sfi, sbr, sbi, dec_ref, hfr, hfi, hbr, hbi):
    j = sfr.shape[0]
    w = sfr.shape[1]
    zero = jnp.zeros((1, w), F32)

    def run(sr, si, hr_out, hi_out, ar, ai, lo, n, reverse, carry):
        def body(i, hc):
            row = (lo + n - 1 - i) if reverse else (lo + i)
            hr, hi = hc
            hr_out[pl.ds(row, 1), :] = hr
            hi_out[pl.ds(row, 1), :] = hi
            nr = ar * hr - ai * hi + sr[pl.ds(row, 1), :]
            ni = ar * hi + ai * hr + si[pl.ds(row, 1), :]
            return nr, ni
        return lax.fori_loop(0, n, body, carry)

    afr, afi = dec_ref[0, 0], dec_ref[0, 1]
    abr, abi = dec_ref[1, 0], dec_ref[1, 1]
    run(sfr, sfi, hfr, hfi, afr, afi, 0, j, False, (zero, zero))
    c = run(sbr, sbi, hbr, hbi, abr, abi, 0, jc, True, (zero, zero))
    run(sbr, sbi, hbr, hbi, abr, abi, jc, j - jc, True, c)


def _s5_scan(states, dec_cols, jc):
    j, wtot = states[0].shape
    tw = 1024
    spec = pl.BlockSpec((j, tw), lambda i: (0, i))
    return pl.pallas_call(
        functools.partial(_s5scan_kernel, jc),
        out_shape=[jax.ShapeDtypeStruct((j, wtot), F32)] * 4,
        grid=(wtot // tw,),
        in_specs=[spec] * 4 + [pl.BlockSpec((2, 2, 1, tw), lambda i: (0, 0, 0, i))],
        out_specs=[spec] * 4,
        compiler_params=_cparams(("parallel",)),
        name="s5scan",
    )(*states, dec_cols)


def _s5out_kernel(u_ref, hfr, hfi, hbr, hbi, wout_ref, m_ref, y_ref):
    rows = u_ref.shape[3]
    y2 = None
    for kind, h in enumerate((hfr, hfi, hbr, hbi)):
        t = _dot(h[...].astype(BF16), wout_ref[0, kind])
        y2 = t if y2 is None else y2 + t
    for gl in range(2):
        y = y2[:, gl * rows:(gl + 1) * rows] + _dot(u_ref[gl, 0], m_ref[0, gl])
        y_ref[gl, 0] = y.astype(BF16)


def _s5_outputs(uf, hprev, wout, m):
    g, b_, j, rows = uf.shape
    npair = g // 2
    hspec = pl.BlockSpec((j, LANES), lambda pr, b: (0, b * npair + pr))
    return pl.pallas_call(
        _s5out_kernel,
        out_shape=jax.ShapeDtypeStruct((g, b_, j, rows), BF16),
        grid=(npair, b_),
        in_specs=[pl.BlockSpec((2, 1, j, rows), lambda pr, b: (pr, b, 0, 0))] + [hspec] * 4 +
                 [pl.BlockSpec((1, 4, LANES, 2 * rows), lambda pr, b: (pr, 0, 0, 0)),
                  pl.BlockSpec((1, 2, rows, rows), lambda pr, b: (pr, 0, 0, 0))],
        out_specs=pl.BlockSpec((2, 1, j, rows), lambda pr, b: (pr, b, 0, 0)),
        compiler_params=_cparams(("parallel", "parallel")),
        name="s5out",
    )(uf, *hprev, wout, m)


def _s5_mixer(u, u_c, ops):
    win, wout, m, dec = ops
    b_, l_, w = u.shape
    lc = u_c.shape[1]
    tc, p = S5_CHUNK, SSM_GROUP
    g = w // p
    npair = g // 2
    jc, jl = lc // tc, l_ // tc
    j = jc + jl
    ua = jnp.concatenate([u_c, u], axis=1)
    uf = ua.reshape(b_, j, tc, g, p).transpose(3, 0, 1, 2, 4).reshape(g, b_, j, tc * p)
    states = _s5_states(uf, win)
    dec_cols = jnp.broadcast_to(dec.transpose(1, 2, 3, 0, 4)[:, :, :, None], (2, 2, 1, b_, npair, LANES))
    dec_cols = dec_cols.reshape(2, 2, 1, b_ * npair * LANES)
    hprev = _s5_scan(states, dec_cols, jc)
    yf = _s5_outputs(uf, hprev, wout, m)
    y = yf[:, :, jc:].reshape(g, b_, jl, tc, p).transpose(1, 2, 3, 0, 4).reshape(b_, l_, w)
    return y


def _attn_kernel(nkv, qt_ref, k_ref, vx_ref, o_ref, s0_ref, s1_ref):
    hd = HEAD_DIM
    kvh = pl.program_id(1)
    tq = qt_ref.shape[2]
    gq = qt_ref.shape[1] // hd
    kw = k_ref.shape[3]
    vr = vx_ref.shape[3]
    own = (lax.broadcasted_iota(jnp.int32, (kw, tq), 0) >> (hd.bit_length() - 1)) == kvh
    qp = []
    for g in range(gq):
        qg = qt_ref[0, g * hd:(g + 1) * hd, :].astype(F32)
        q2 = jnp.concatenate([qg] * (kw // hd), axis=0)
        qp.append(jnp.where(own, q2, 0.0).astype(BF16))
    qp = jnp.concatenate(qp, axis=1)
    m0 = jnp.full((1, gq * tq), -jnp.inf, F32)
    acc0 = jnp.zeros((vr, gq * tq), F32)

    def scores(jb, s_ref):
        s = _dot(k_ref[0, jb], qp)
        s_ref[...] = s
        return jnp.max(s, axis=0, keepdims=True)

    def consume(jb, s_ref, cmax, m_prev, acc):
        m_new = jnp.maximum(m_prev, cmax)
        alpha = jnp.exp2(m_prev - m_new)
        p = jnp.exp2(s_ref[...] - m_new).astype(BF16)
        return m_new, alpha * acc + _dot(vx_ref[0, 0, jb], p)

    def body(i, carry):
        m, acc, cmax = carry
        jb = 2 * i
        cmax1 = scores(jb + 1, s1_ref)
        m, acc = consume(jb, s0_ref, cmax, m, acc)
        cmax2 = scores(jb + 2, s0_ref)
        m, acc = consume(jb + 1, s1_ref, cmax1, m, acc)
        return m, acc, cmax2

    npair = (nkv - 1) // 2
    m, acc, cmax = lax.fori_loop(0, npair, body, (m0, acc0, scores(0, s0_ref)))
    if (nkv - 1) % 2:
        cmax1 = scores(nkv - 1, s1_ref)
        m, acc = consume(nkv - 2, s0_ref, cmax, m, acc)
        m, acc = consume(nkv - 1, s1_ref, cmax1, m, acc)
    else:
        m, acc = consume(nkv - 1, s0_ref, cmax, m, acc)
    o = acc[0:hd] / acc[hd:hd + 1]
    ot = jnp.concatenate([o[:, g * tq:(g + 1) * tq] for g in range(gq)], axis=0)
    o_ref[0] = ot.T.astype(BF16)


def _attention(qt, k4, vx, tq):
    b_, wq, l_ = qt.shape
    _, kvh, nkv, vr, tk = vx.shape
    kw = k4.shape[3]
    gw = wq // kvh
    return pl.pallas_call(
        functools.partial(_attn_kernel, nkv),
        out_shape=jax.ShapeDtypeStruct((b_, l_, wq), BF16),
        grid=(b_, kvh, l_ // tq),
        in_specs=[pl.BlockSpec((1, gw, tq), lambda b, h, i: (b, h, i)),
                  pl.BlockSpec((1, nkv, tk, kw), lambda b, h, i: (b, 0, 0, 0)),
                  pl.BlockSpec((1, 1, nkv, vr, tk), lambda b, h, i: (b, h, 0, 0, 0))],
        out_specs=pl.BlockSpec((1, tq, gw), lambda b, h, i: (b, i, h)),
        scratch_shapes=[pltpu.VMEM((tk, (gw // HEAD_DIM) * tq), F32)] * 2,
        compiler_params=_cparams(("parallel", "parallel", "parallel")),
        name="attn",
    )(qt, k4, vx)


def _merge_kernel(dm, ne, ys_ref, oa_ref, gl_ref, x_ref, mod_ref, wglu_ref, bglu_ref, wso_ref, wao_ref,
                  wout_ref, n2g_ref, rw_ref, rb_ref, xmix_ref, h2_ref, idx_ref, gate_ref):
    s = jax.nn.gelu(ys_ref[0].astype(F32))
    s = s * jax.nn.sigmoid(_dot(s.astype(BF16), wglu_ref[...]) + bglu_ref[...])
    a = _dot(s.astype(BF16), wso_ref[...])
    bq = _dot(oa_ref[0], wao_ref[...])
    gts = jax.nn.sigmoid(gl_ref[0].astype(F32))
    merged = gts[:, 0:dm] * a + gts[:, dm:2 * dm] * bq
    out = _dot(merged.astype(BF16), wout_ref[...])
    g1 = mod_ref[0, :, 2 * dm:3 * dm]
    sh2 = mod_ref[0, :, 3 * dm:4 * dm]
    sc2 = mod_ref[0, :, 4 * dm:5 * dm]
    xm = x_ref[0] + g1 * out
    xmix_ref[0] = xm
    ms = jnp.mean(xm * xm, axis=-1, keepdims=True)
    h2 = xm * lax.rsqrt(ms + EPS) * n2g_ref[...] * (1.0 + sc2) + sh2
    h2_ref[0] = _pack2(h2[:, 0:dm // 2], h2[:, dm // 2:dm])
    logits = _dot3(h2, rw_ref[...]) + rb_ref[...]
    lane = lax.broadcasted_iota(jnp.int32, logits.shape, 1).astype(F32)
    neg = jnp.float32(-jnp.inf)
    vals, idxs = [], []
    l = logits
    for _ in range(TOP_K):
        mx = jnp.max(l, axis=-1, keepdims=True)
        ix = jnp.min(jnp.where(l == mx, lane, float(LANES)), axis=-1, keepdims=True)
        vals.append(mx)
        idxs.append(ix)
        l = jnp.where(lane == ix, neg, l)
    es = [jnp.exp(v - vals[0]) for v in vals]
    den = es[0] + es[1] + es[2] + es[3]
    idx_out = jnp.zeros(logits.shape, F32)
    gate_out = jnp.zeros(logits.shape, F32)
    for k in range(TOP_K):
        idx_out = jnp.where(lane == float(k), idxs[k], idx_out)
        gate_out = jnp.where(lane == float(k), es[k] / den, gate_out)
    idx_ref[0] = idx_out.astype(jnp.int32)
    gate_ref[0] = gate_out


def _merge(y_ssm, o_attn, glog, x, mod3, w_glu, b_glu, w_ssm_out, w_attn_out, w_out, norm2_g,
           rw_pad, rb_pad, ne, tm):
    b_, l_, dm = x.shape
    wu = dm // 2
    tok = lambda w: pl.BlockSpec((1, tm, w), lambda b, i: (b, i, 0))
    full = lambda r, c: pl.BlockSpec((r, c), lambda b, i: (0, 0))
    return pl.pallas_call(
        functools.partial(_merge_kernel, dm, ne),
        out_shape=[jax.ShapeDtypeStruct((b_, l_, dm), F32),
                   jax.ShapeDtypeStruct((b_, l_, dm // 2), jnp.uint32),
                   jax.ShapeDtypeStruct((b_, l_, LANES), jnp.int32),
                   jax.ShapeDtypeStruct((b_, l_, LANES), F32)],
        grid=(b_, l_ // tm),
        in_specs=[tok(wu), tok(wu), tok(2 * dm), tok(dm),
                  pl.BlockSpec((1, 1, N_MOD * dm), lambda b, i: (b, 0, 0)),
                  full(wu, wu), full(1, wu), full(wu, dm), full(wu, dm), full(dm, dm),
                  full(1, dm), full(dm, LANES), full(1, LANES)],
        out_specs=[tok(dm), tok(dm // 2), tok(LANES), tok(LANES)],
        compiler_params=_cparams(("parallel", "parallel")),
        name="merge",
    )(y_ssm, o_attn, glog, x, mod3, w_glu, b_glu, w_ssm_out, w_attn_out, w_out, norm2_g, rw_pad, rb_pad)


def _expert_kernel(de, be_ref, nu_ref, x_ref, wgu_ref, bgu_ref, wd_ref, bd_ref, y_ref, wgu_b, wd_b):
    i = pl.program_id(0)
    used = i < nu_ref[0]
    new_expert = (i == 0) | (be_ref[i] != be_ref[jnp.maximum(i - 1, 0)])

    @pl.when(used & new_expert)
    def _():
        wgu_b[...] = wgu_ref[0].astype(BF16)
        wd_b[...] = wd_ref[0].astype(BF16)

    @pl.when(used)
    def _():
        x_lo, x_hi = _unpack2(x_ref[...])
        xb = jnp.concatenate([x_lo, x_hi], axis=1).astype(BF16)
        gu = _dot(xb, wgu_b[...]) + bgu_ref[0]
        gate = jnp.minimum(gu[:, 0:de], SWIGLU_LIMIT)
        up = jnp.clip(gu[:, de:2 * de], -SWIGLU_LIMIT, SWIGLU_LIMIT)
        act = (up + 1.0) * (gate * jax.nn.sigmoid(SWIGLU_ALPHA * gate))
        y = _dot(act.astype(BF16), wd_b[...]) + bd_ref[0]
        half = y.shape[1] // 2
        y_ref[...] = _pack2(y[:, 0:half], y[:, half:])

    @pl.when(jnp.logical_not(used))
    def _():
        y_ref[...] = jnp.zeros(y_ref.shape, y_ref.dtype)


def _experts(xb, block_expert, n_used, w_gate_up, b_gate_up, w_down, b_down):
    n_slots, dmh = xb.shape
    dm = 2 * dmh
    ne, _, de2 = w_gate_up.shape
    de = de2 // 2
    nb = n_slots // EXPERT_BLOCK
    return pl.pallas_call(
        functools.partial(_expert_kernel, de),
        out_shape=jax.ShapeDtypeStruct((n_slots, dmh), jnp.uint32),
        grid_spec=pltpu.PrefetchScalarGridSpec(
            num_scalar_prefetch=2,
            grid=(nb,),
            in_specs=[pl.BlockSpec((EXPERT_BLOCK, dmh), lambda i, be, nu: (i, 0)),
                      pl.BlockSpec((1, dm, de2), lambda i, be, nu: (be[i], 0, 0)),
                      pl.BlockSpec((1, 1, de2), lambda i, be, nu: (be[i], 0, 0)),
                      pl.BlockSpec((1, de, dm), lambda i, be, nu: (be[i], 0, 0)),
                      pl.BlockSpec((1, 1, dm), lambda i, be, nu: (be[i], 0, 0))],
            out_specs=pl.BlockSpec((EXPERT_BLOCK, dmh), lambda i, be, nu: (i, 0)),
            scratch_shapes=[pltpu.VMEM((dm, de2), BF16), pltpu.VMEM((de, dm), BF16)],
        ),
        compiler_params=_cparams(("arbitrary",)),
        name="expert",
    )(block_expert, n_used, xb, w_gate_up, b_gate_up.reshape(ne, 1, de2), w_down, b_down.reshape(ne, 1, dm))


def _sc_gather_rows(table, idx):
    n = idx.shape[0]
    w = table.shape[1]
    mesh = plsc.VectorSubcoreMesh(core_axis_name="c", subcore_axis_name="s")
    nc, nw = mesh.num_cores, mesh.num_cores * mesh.num_subcores
    win = SC_GATHER_WINDOW
    assert n % (win * nw) == 0
    per_worker = n // nw

    @functools.partial(
        pl.kernel, out_type=jax.ShapeDtypeStruct((n, w), table.dtype), mesh=mesh,
        scratch_types=[pltpu.VMEM((win,), jnp.int32), pltpu.VMEM((win, w), table.dtype),
                       pltpu.SemaphoreType.DMA])
    def gather(x_hbm, i_hbm, o_hbm, idx_v, rows_v, sem):
        base = (lax.axis_index("s") * nc + lax.axis_index("c")) * per_worker

        @pl.loop(0, per_worker // win)
        def _(j):
            off = base + j * win
            pltpu.sync_copy(i_hbm.at[pl.ds(off, win)], idx_v)
            pltpu.async_copy(x_hbm.at[idx_v], rows_v, sem).wait()
            pltpu.sync_copy(rows_v, o_hbm.at[pl.ds(off, win)])

    return gather(table, idx)


def _final_kernel(dm, yg_ref, gate_ref, xm_ref, mod_ref, fg_ref, o_ref):
    gates = gate_ref[0]
    f_lo = f_hi = None
    for k in range(TOP_K):
        y_lo, y_hi = _unpack2(yg_ref[k, 0])
        gk = gates[:, k:k + 1]
        f_lo = gk * y_lo if f_lo is None else f_lo + gk * y_lo
        f_hi = gk * y_hi if f_hi is None else f_hi + gk * y_hi
    f = jnp.concatenate([f_lo, f_hi], axis=1)
    g2 = mod_ref[0, :, 5 * dm:6 * dm]
    xo = xm_ref[0] + g2 * f
    ms = jnp.mean(xo * xo, axis=-1, keepdims=True)
    o_ref[0] = xo * lax.rsqrt(ms + EPS) * fg_ref[...]


def _final(yg, gates, x_mix, mod3, final_g, tm):
    b_, l_, dm = x_mix.shape
    return pl.pallas_call(
        functools.partial(_final_kernel, dm),
        out_shape=jax.ShapeDtypeStruct((b_, l_, dm), F32),
        grid=(b_, l_ // tm),
        in_specs=[pl.BlockSpec((TOP_K, 1, tm, dm // 2), lambda b, i: (0, b, i, 0)),
                  pl.BlockSpec((1, tm, LANES), lambda b, i: (b, i, 0)),
                  pl.BlockSpec((1, tm, dm), lambda b, i: (b, i, 0)),
                  pl.BlockSpec((1, 1, N_MOD * dm), lambda b, i: (b, 0, 0)),
                  pl.BlockSpec((1, dm), lambda b, i: (0, 0))],
        out_specs=pl.BlockSpec((1, tm, dm), lambda b, i: (b, i, 0)),
        compiler_params=_cparams(("parallel", "parallel")),
        name="final",
    )(yg, gates, x_mix, mod3, final_g.reshape(1, dm))


def _rope_tables(l_):
    half = HEAD_DIM // 2
    inv_freq = ROPE_THETA ** (-jnp.arange(0, half, 2, dtype=F32) / half)
    t = jnp.arange(l_, dtype=jnp.int32)
    row_id, col_id = t // GRID_W, t % GRID_W
    ang = jnp.concatenate([row_id.astype(F32)[:, None] * inv_freq,
                           col_id.astype(F32)[:, None] * inv_freq], axis=-1)
    cos = jnp.repeat(jnp.cos(ang), 2, axis=-1)
    sin = jnp.repeat(jnp.sin(ang), 2, axis=-1)
    sign = jnp.tile(jnp.array([-1.0, 1.0], F32), HEAD_DIM // 2)
    reps = LANES // HEAD_DIM
    return jnp.tile(cos, (1, reps)), jnp.tile(sin * sign, (1, reps))


def _routing(idx, t_, ne):
    m_ = t_ * TOP_K
    blk = EXPERT_BLOCK
    nb = (m_ + ne * (blk - 1)) // blk
    nb = -(-nb // SLOT_BLOCK_MULTIPLE) * SLOT_BLOCK_MULTIPLE
    flat_e = idx.reshape(-1)
    onehot = flat_e[:, None] == jnp.arange(ne, dtype=jnp.int32)[None, :]
    counts = jnp.sum(onehot.astype(jnp.int32), axis=0)
    starts = jnp.cumsum(counts) - counts
    blk_counts = (counts + blk - 1) // blk
    blk_ends = jnp.cumsum(blk_counts)
    shift = (blk_ends - blk_counts) * blk - starts
    order = jnp.argsort(flat_e)
    inv = jnp.argsort(order)
    dest = inv.astype(jnp.int32) + jnp.sum(jnp.where(onehot, shift[None, :], 0), axis=1)
    block_expert = jnp.minimum(jnp.sum(jnp.arange(nb)[:, None] >= blk_ends[None, :], axis=1), ne - 1)
    slot_src = jnp.arange(nb * blk, dtype=jnp.int32) - jnp.repeat(shift[block_expert], blk)
    slot_tok = order[jnp.clip(slot_src, 0, m_ - 1)] // TOP_K
    return (slot_tok.astype(jnp.int32), dest.astype(jnp.int32), block_expert.astype(jnp.int32),
            blk_ends[-1:].astype(jnp.int32))


def kernel(x, c, ctx, c_ctx, w_mod, b_mod, norm1_g, norm2_g, w_in, s5_lam_re, s5_lam_im, s5_log_step,
           s5_b_re, s5_b_im, s5_c_re, s5_c_im, s5_d, w_glu, b_glu, w_ssm_out, q_norm_g, k_norm_g,
           w_attn_out, w_out, router_w, router_b, w_gate_up, b_gate_up, w_down, b_down, final_norm_g):
    b_, l_, dm = x.shape
    lc = ctx.shape[1]
    depth = w_mod.shape[0]
    assert depth == 1, "single-layer block"
    assert b_ <= 7 and l_ % 512 == 0 and lc % S5_CHUNK == 0
    wu = dm // 2
    wkv = wu // GQA_GROUP
    kvh = wkv // HEAD_DIM
    ne = router_w.shape[-1]
    i = 0

    c8 = jnp.zeros((8, dm), F32).at[:b_].set(c).at[b_].set(c_ctx)
    mod3 = _modulation(c8, w_mod[i], b_mod[i]).reshape(8, 1, N_MOD * dm)

    w_in_b = w_in[i].astype(BF16)
    cos, sin = _rope_tables(l_)
    hsum = jnp.kron(jnp.eye(wu // HEAD_DIM, dtype=F32), jnp.ones((HEAD_DIM, HEAD_DIM), F32)).astype(BF16)
    qg = jnp.tile(q_norm_g[i], wu // HEAD_DIM).reshape(1, wu)
    kg = jnp.tile(k_norm_g[i], wkv // HEAD_DIM).reshape(1, wkv)
    u, k, vt, qt, glog = _inproj(x, mod3, None, norm1_g[i], w_in_b, cos, sin, qg, kg, hsum, True, 512)
    ones_t = jnp.ones((lc, LANES), F32)
    u_c, k_c, vt_c = _inproj(ctx, mod3, b_, norm1_g[i], w_in_b[:, :wu + 2 * wkv], ones_t, 0.0 * ones_t,
                             qg, kg, hsum, False, lc)

    ops = _s5_operators(s5_lam_re[i], s5_lam_im[i], s5_log_step[i], s5_b_re[i], s5_b_im[i],
                        s5_c_re[i], s5_c_im[i], s5_d[i])
    y_ssm = _s5_mixer(u, u_c, ops)

    s_all = lc + l_
    tk = max(t for t in range(LANES, ATTN_KEY_CHUNK + 1, LANES) if s_all % t == 0)
    nkv = s_all // tk
    k4 = jnp.concatenate([k_c, k], axis=1).reshape(b_, nkv, tk, wkv)
    vt_all = jnp.concatenate([vt_c, vt], axis=2).reshape(b_, kvh, HEAD_DIM, s_all)
    vx = jnp.concatenate([vt_all, jnp.ones((b_, kvh, 1, s_all), BF16),
                          jnp.zeros((b_, kvh, ATTN_V_ROWS - HEAD_DIM - 1, s_all), BF16)], axis=2)
    vx = vx.reshape(b_, kvh, ATTN_V_ROWS, nkv, tk).transpose(0, 1, 3, 2, 4)
    o_attn = _attention(qt, k4, vx, ATTN_Q_BLOCK)

    rw_pad = jnp.zeros((dm, LANES), F32).at[:, :ne].set(router_w[i])
    rb_pad = jnp.full((1, LANES), -jnp.inf, F32).at[0, :ne].set(router_b[i])
    x_mix, h2, idx, gates = _merge(y_ssm, o_attn, glog, x, mod3, w_glu[i].astype(BF16),
                                   b_glu[i].reshape(1, wu), w_ssm_out[i].astype(BF16),
                                   w_attn_out[i].astype(BF16), w_out[i].astype(BF16),
                                   norm2_g[i].reshape(1, dm), rw_pad, rb_pad, ne, 512)

    t_ = b_ * l_
    slot_tok, dest, block_expert, n_used = _routing(idx[..., :TOP_K].reshape(t_, TOP_K), t_, ne)
    xb = _sc_gather_rows(h2.reshape(t_, dm // 2), slot_tok)
    yb = _experts(xb, block_expert, n_used, w_gate_up[i], b_gate_up[i], w_down[i], b_down[i])
    dest_kmajor = dest.reshape(t_, TOP_K).T.reshape(-1)
    yg = _sc_gather_rows(yb, dest_kmajor).reshape(TOP_K, b_, l_, dm // 2)
    return _final(yg, gates, x_mix, mod3, final_norm_g, 512)
```

```python
import functools
import math

import jax
import jax.numpy as jnp
from jax import lax
from jax.experimental import pallas as pl
from jax.experimental.pallas import tpu as pltpu
from jax.experimental.pallas import tpu_sc as plsc

F32 = jnp.float32
BF16 = jnp.bfloat16

EPS = 1e-6
GRID_W = 64
N_MOD = 6
SSM_GROUP = 16
SSM_STATE = 64
HEAD_DIM = 64
GQA_GROUP = 4
ROPE_THETA = 10000.0
TOP_K = 4
SWIGLU_ALPHA = 1.702
SWIGLU_LIMIT = 7.0
LOG2E = 1.4426950408889634

LANES = 128
S5_CHUNK = 32
EXPERT_BLOCK = 512
ATTN_V_ROWS = 80
ATTN_KEY_CHUNK = 384
ATTN_Q_BLOCK = 256
ATTN_STEPS_PER_TRIP = 4
SC_GATHER_WINDOW = 128
SC_WORKERS = 32
SLOT_BLOCK_MULTIPLE = SC_GATHER_WINDOW * SC_WORKERS // EXPERT_BLOCK
VMEM_LIMIT = 56 * 1024 * 1024


def _cparams(sem, flags=None):
    return pltpu.CompilerParams(dimension_semantics=sem, vmem_limit_bytes=VMEM_LIMIT, flags=flags)


def _dot(a, b):
    return jnp.dot(a, b, preferred_element_type=F32)


def _split(a):
    hi = a.astype(BF16)
    lo = (a - hi.astype(F32)).astype(BF16)
    return hi, lo


def _pack2(lo, hi):
    lo_w = lax.bitcast_convert_type(lo.astype(BF16).astype(F32), jnp.uint32) >> 16
    hi_w = lax.bitcast_convert_type(hi.astype(BF16).astype(F32), jnp.uint32) & jnp.uint32(0xFFFF0000)
    return lo_w | hi_w


def _unpack2(words):
    lo = lax.bitcast_convert_type(words << 16, F32)
    hi = lax.bitcast_convert_type(words & jnp.uint32(0xFFFF0000), F32)
    return lo, hi


def _dot3(a, b):
    ah, al = _split(a)
    bh, bl = _split(b)
    return _dot(ah, bh) + _dot(ah, bl) + _dot(al, bh)


def _mod_kernel(c_ref, w_ref, b_ref, o_ref):
    c = c_ref[...]
    s = c * jax.nn.sigmoid(c)
    o_ref[...] = _dot3(s, w_ref[...]) + b_ref[...]


def _modulation(c8, w_mod, b_mod):
    d = c8.shape[1]
    n = w_mod.shape[1]
    tn = 1024
    return pl.pallas_call(
        _mod_kernel,
        out_shape=jax.ShapeDtypeStruct((8, n), F32),
        grid=(n // tn,),
        in_specs=[pl.BlockSpec((8, d), lambda j: (0, 0)),
                  pl.BlockSpec((d, tn), lambda j: (0, j)),
                  pl.BlockSpec((1, tn), lambda j: (0, j))],
        out_specs=pl.BlockSpec((8, tn), lambda j: (0, j)),
        compiler_params=_cparams(("parallel",)),
        name="mod",
    )(c8, w_mod, b_mod.reshape(1, n))


def _head_norm_rope(z, gain, hsum, cos, sin, scale):
    z2h, z2l = _split(z * z)
    ss = _dot(z2h, hsum) + _dot(z2l, hsum)
    zn = z * lax.rsqrt(ss * (1.0 / HEAD_DIM) + EPS) * gain
    lane = lax.broadcasted_iota(jnp.int32, (1, LANES), 1)
    even = (lane & 1) == 0
    outs = []
    for ci in range(z.shape[1] // LANES):
        ch = zn[:, ci * LANES:(ci + 1) * LANES]
        nxt = pltpu.roll(ch, LANES - 1, axis=1)
        prv = pltpu.roll(ch, 1, axis=1)
        sw = jnp.where(even, nxt, prv)
        outs.append((ch * cos + sw * sin) * scale)
    return outs


def _inproj_kernel(with_q, dm, x_ref, mod_ref, g_ref, w_ref, cos_ref, sin_ref, qg_ref, kg_ref,
                   hsum_ref, *out_refs):
    x = x_ref[0]
    ms = jnp.mean(x * x, axis=-1, keepdims=True)
    y = x * lax.rsqrt(ms + EPS) * g_ref[...]
    sh = mod_ref[0, :, 0:dm]
    sc = mod_ref[0, :, dm:2 * dm]
    hb = (y * (1.0 + sc) + sh).astype(BF16)
    cos = cos_ref[...]
    sin = sin_ref[...]
    wu = dm // 2
    wkv = wu // GQA_GROUP
    c_k, c_v, c_q, c_g = wu, wu + wkv, wu + 2 * wkv, 2 * wu + 2 * wkv
    u_ref, k_ref, vt_ref = out_refs[:3]
    u_ref[0] = _dot(hb, w_ref[:, 0:c_k]).astype(BF16)
    kz = _dot(hb, w_ref[:, c_k:c_v])
    (kr,) = _head_norm_rope(kz, kg_ref[...], hsum_ref[0:wkv, 0:wkv], cos, sin, 1.0)
    k_ref[0] = kr.astype(BF16)
    vt_ref[0] = _dot(hb, w_ref[:, c_v:c_q]).T.astype(BF16)
    if with_q:
        qt_ref, gl_ref = out_refs[3:]
        qz = _dot(hb, w_ref[:, c_q:c_g])
        qs = _head_norm_rope(qz, qg_ref[...], hsum_ref[...], cos, sin, HEAD_DIM ** -0.5 * LOG2E)
        for ci, qc in enumerate(qs):
            qt_ref[0, ci * LANES:(ci + 1) * LANES, :] = qc.T.astype(BF16)
        gl_ref[0] = _dot(hb, w_ref[:, c_g:]).astype(BF16)


def _inproj(x, mod3, mod_row0, norm_g, w_in_b, cos, sin, qg, kg, hsum, with_q, tm):
    b_, l_, dm = x.shape
    wu = dm // 2
    wkv = wu // GQA_GROUP
    n_in = w_in_b.shape[1]
    outs = [jax.ShapeDtypeStruct((b_, l_, wu), BF16),
            jax.ShapeDtypeStruct((b_, l_, wkv), BF16),
            jax.ShapeDtypeStruct((b_, wkv, l_), BF16)]
    ospecs = [pl.BlockSpec((1, tm, wu), lambda b, i: (b, i, 0)),
              pl.BlockSpec((1, tm, wkv), lambda b, i: (b, i, 0)),
              pl.BlockSpec((1, wkv, tm), lambda b, i: (b, 0, i))]
    if with_q:
        outs += [jax.ShapeDtypeStruct((b_, wu, l_), BF16),
                 jax.ShapeDtypeStruct((b_, l_, 2 * dm), BF16)]
        ospecs += [pl.BlockSpec((1, wu, tm), lambda b, i: (b, 0, i)),
                   pl.BlockSpec((1, tm, 2 * dm), lambda b, i: (b, i, 0))]
    if mod_row0 is None:
        mod_map = lambda b, i: (b, 0, 0)
    else:
        mod_map = lambda b, i: (mod_row0, 0, 0)
    return pl.pallas_call(
        functools.partial(_inproj_kernel, with_q, dm),
        out_shape=outs,
        grid=(b_, l_ // tm),
        in_specs=[pl.BlockSpec((1, tm, dm), lambda b, i: (b, i, 0)),
                  pl.BlockSpec((1, 1, N_MOD * dm), mod_map),
                  pl.BlockSpec((1, dm), lambda b, i: (0, 0)),
                  pl.BlockSpec((dm, n_in), lambda b, i: (0, 0)),
                  pl.BlockSpec((tm, LANES), lambda b, i: (i, 0)),
                  pl.BlockSpec((tm, LANES), lambda b, i: (i, 0)),
                  pl.BlockSpec((1, wu), lambda b, i: (0, 0)),
                  pl.BlockSpec((1, wkv), lambda b, i: (0, 0)),
                  pl.BlockSpec((wu, wu), lambda b, i: (0, 0))],
        out_specs=ospecs,
        compiler_params=_cparams(("parallel", "parallel")),
        name="inproj_x" if with_q else "inproj_ctx",
    )(x, mod3, norm_g.reshape(1, dm), w_in_b, cos, sin, qg, kg, hsum)


def _cexp_pow(e, lr, li):
    mag = jnp.exp(e * lr)
    ang = e * li
    return mag * jnp.cos(ang), mag * jnp.sin(ang)


def _cmul(ar, ai, br, bi):
    return ar * br - ai * bi, ar * bi + ai * br


def _s5pre_kernel(lamr_r, lami_r, lst_r, lamr_c, lami_c, lst_c, btr_ref, bti_ref, ctr_ref, cti_ref,
                  d_ref, win_ref, wout_ref, m_ref, dec_ref):
    tc = S5_CHUNK
    half = float(tc // 2)
    rows = tc * SSM_GROUP
    gsh = SSM_GROUP.bit_length() - 1
    nsh = SSM_STATE.bit_length() - 1
    s_idx = (lax.broadcasted_iota(jnp.int32, (rows, 1), 0) >> gsh).astype(F32)
    t_idx = (lax.broadcasted_iota(jnp.int32, (1, rows), 1) >> gsh).astype(F32)
    row_i = lax.broadcasted_iota(jnp.int32, (rows, rows), 0)
    col_i = lax.broadcasted_iota(jnp.int32, (rows, rows), 1)
    lane_g = lax.broadcasted_iota(jnp.int32, (1, LANES), 1) >> nsh
    subl_g = lax.broadcasted_iota(jnp.int32, (LANES, 1), 0) >> nsh
    m_acc = [jnp.where(row_i == col_i, d_ref[0, gl], 0.0) for gl in range(2)]
    for dr in range(2):
        step = jnp.exp(lst_r[dr, 0])
        lam_r, lam_i = lamr_r[dr, 0], lami_r[dr, 0]
        lr, li = lam_r * step, lam_i * step
        lb_r, lb_i = _cexp_pow(1.0, lr, li)
        den = lam_r * lam_r + lam_i * lam_i
        nr, ni = lb_r - 1.0, lb_i
        cf_r = (nr * lam_r + ni * lam_i) / den
        cf_i = (ni * lam_r - nr * lam_i) / den
        bb_r, bb_i = _cmul(cf_r, cf_i, btr_ref[dr, 0], bti_ref[dr, 0])
        e_in = (tc - 1.0 - s_idx) if dr == 0 else s_idx
        p_r, p_i = _cexp_pow(e_in, lr, li)
        wi_r, wi_i = _cmul(bb_r, bb_i, p_r, p_i)
        e_a = (half - s_idx) if dr == 0 else (s_idx - half)
        p_r, p_i = _cexp_pow(e_a, lr, li)
        a_r, a_i = _cmul(bb_r, bb_i, p_r, p_i)
        d_r, d_i = _cexp_pow(float(tc), lr, li)
        dec_ref[0, dr, 0] = d_r
        dec_ref[0, dr, 1] = d_i
        stepc = jnp.exp(lst_c[dr, 0])
        lrc, lic = lamr_c[dr, 0] * stepc, lami_c[dr, 0] * stepc
        c_r, c_i = ctr_ref[dr, 0], cti_ref[dr, 0]
        e_b = (t_idx - half) if dr == 0 else (half - t_idx)
        q_r, q_i = _cexp_pow(e_b, lrc, lic)
        bm_r, bm_i = _cmul(c_r, c_i, q_r, q_i)
        e_o = (t_idx + 1.0) if dr == 0 else (tc - t_idx)
        q_r, q_i = _cexp_pow(e_o, lrc, lic)
        wo_r, wo_i = _cmul(c_r, c_i, q_r, q_i)
        if dr == 0:
            mask = (col_i >> gsh) >= (row_i >> gsh)
        else:
            mask = (row_i >> gsh) >= (col_i >> gsh)
        for gl in range(2):
            lsel = lane_g == gl
            ssel = subl_g == gl
            win_ref[0, gl, 2 * dr] = jnp.where(lsel, wi_r, 0.0).astype(BF16)
            win_ref[0, gl, 2 * dr + 1] = jnp.where(lsel, wi_i, 0.0).astype(BF16)
            wout_ref[0, 2 * dr, :, gl * rows:(gl + 1) * rows] = jnp.where(ssel, wo_r, 0.0).astype(BF16)
            wout_ref[0, 2 * dr + 1, :, gl * rows:(gl + 1) * rows] = jnp.where(ssel, -wo_i, 0.0).astype(BF16)
            kmat = (_dot3(jnp.where(lsel, a_r, 0.0), bm_r) - _dot3(jnp.where(lsel, a_i, 0.0), bm_i))
            m_acc[gl] = m_acc[gl] + jnp.where(mask, kmat, 0.0)
    for gl in range(2):
        m_ref[0, gl] = m_acc[gl].astype(BF16)


def _s5_operators(lam_re, lam_im, log_step, b_re, b_im, c_re, c_im, d_skip):
    g = lam_re.shape[1]
    n, p = SSM_STATE, SSM_GROUP
    npair = g // 2
    tc = S5_CHUNK
    rows = tc * p

    def row_form(a):
        return a.reshape(2, npair, 1, 2 * n)

    def col_form(a):
        return a.reshape(2, npair, 2 * n, 1)

    lst = jnp.broadcast_to(log_step[:, :, None], (2, g, n))
    bt = lambda b: jnp.broadcast_to(
        b.reshape(2, npair, 2, n, p).transpose(0, 1, 4, 2, 3)[:, :, None], (2, npair, tc, p, 2, n)
    ).reshape(2, npair, rows, 2 * n)
    ct = lambda c: jnp.broadcast_to(
        c.reshape(2, npair, 2, p, n).transpose(0, 1, 2, 4, 3)[:, :, :, :, None, :], (2, npair, 2, n, tc, p)
    ).reshape(2, npair, 2 * n, rows)
    d2 = jnp.broadcast_to(d_skip.reshape(npair, 2, 1, 1, p), (npair, 2, 1, tc, p)).reshape(npair, 2, 1, rows)

    def spec4(shape):
        return pl.BlockSpec((2, 1) + shape, lambda i: (0, i, 0, 0))

    return pl.pallas_call(
        _s5pre_kernel,
        out_shape=[jax.ShapeDtypeStruct((npair, 2, 4, rows, LANES), BF16),
                   jax.ShapeDtypeStruct((npair, 4, LANES, 2 * rows), BF16),
                   jax.ShapeDtypeStruct((npair, 2, rows, rows), BF16),
                   jax.ShapeDtypeStruct((npair, 2, 2, 1, LANES), F32)],
        grid=(npair,),
        in_specs=[spec4((1, LANES)), spec4((1, LANES)), spec4((1, LANES)),
                  spec4((LANES, 1)), spec4((LANES, 1)), spec4((LANES, 1)),
                  spec4((rows, LANES)), spec4((rows, LANES)),
                  spec4((LANES, rows)), spec4((LANES, rows)),
                  pl.BlockSpec((1, 2, 1, rows), lambda i: (i, 0, 0, 0))],
        out_specs=[pl.BlockSpec((1, 2, 4, rows, LANES), lambda i: (i, 0, 0, 0, 0)),
                   pl.BlockSpec((1, 4, LANES, 2 * rows), lambda i: (i, 0, 0, 0)),
                   pl.BlockSpec((1, 2, rows, rows), lambda i: (i, 0, 0, 0)),
                   pl.BlockSpec((1, 2, 2, 1, LANES), lambda i: (i, 0, 0, 0, 0))],
        compiler_params=_cparams(("parallel",)),
        name="s5pre",
    )(row_form(lam_re), row_form(lam_im), row_form(lst),
      col_form(lam_re), col_form(lam_im), col_form(lst),
      bt(b_re), bt(b_im), ct(c_re), ct(c_im), d2)


def _s5state_kernel(u_ref, win_ref, *s_refs):
    u0 = u_ref[0, 0]
    u1 = u_ref[1, 0]
    for kind in range(4):
        s_refs[kind][...] = _dot(u0, win_ref[0, 0, kind]) + _dot(u1, win_ref[0, 1, kind])


def _s5_states(uf, win):
    g, b_, j, rows = uf.shape
    npair = g // 2
    out = jax.ShapeDtypeStruct((j, b_ * npair * LANES), F32)
    ospec = pl.BlockSpec((j, LANES), lambda pr, b: (0, b * npair + pr))
    return pl.pallas_call(
        _s5state_kernel,
        out_shape=[out] * 4,
        grid=(npair, b_),
        in_specs=[pl.BlockSpec((2, 1, j, rows), lambda pr, b: (pr, b, 0, 0)),
                  pl.BlockSpec((1, 2, 4, rows, LANES), lambda pr, b: (pr, 0, 0, 0, 0))],
        out_specs=[ospec] * 4,
        compiler_params=_cparams(("parallel", "parallel")),
        name="s5state",
    )(uf, win)


def _s5scan_kernel(jc, sfr, sfi, sbr, sbi, dec_ref, hfr, hfi, hbr, hbi):
    j = sfr.shape[0]
    w = sfr.shape[1]
    zero = jnp.zeros((1, w), F32)

    def run(sr, si, hr_out, hi_out, ar, ai, lo, n, reverse, carry):
        def body(i, hc):
            row = (lo + n - 1 - i) if reverse else (lo + i)
            hr, hi = hc
            hr_out[pl.ds(row, 1), :] = hr
            hi_out[pl.ds(row, 1), :] = hi
            nr = ar * hr - ai * hi + sr[pl.ds(row, 1), :]
            ni = ar * hi + ai * hr + si[pl.ds(row, 1), :]
            return nr, ni
        return lax.fori_loop(0, n, body, carry)

    afr, afi = dec_ref[0, 0], dec_ref[0, 1]
    abr, abi = dec_ref[1, 0], dec_ref[1, 1]
    run(sfr, sfi, hfr, hfi, afr, afi, 0, j, False, (zero, zero))
    c = run(sbr, sbi, hbr, hbi, abr, abi, 0, jc, True, (zero, zero))
    run(sbr, sbi, hbr, hbi, abr, abi, jc, j - jc, True, c)


def _s5_scan(states, dec_cols, jc):
    j, wtot = states[0].shape
    tw = 1024
    spec = pl.BlockSpec((j, tw), lambda i: (0, i))
    return pl.pallas_call(
        functools.partial(_s5scan_kernel, jc),
        out_shape=[jax.ShapeDtypeStruct((j, wtot), F32)] * 4,
        grid=(wtot // tw,),
        in_specs=[spec] * 4 + [pl.BlockSpec((2, 2, 1, tw), lambda i: (0, 0, 0, i))],
        out_specs=[spec] * 4,
        compiler_params=_cparams(("parallel",)),
        name="s5scan",
    )(*states, dec_cols)


def _s5out_kernel(u_ref, hfr, hfi, hbr, hbi, wout_ref, m_ref, y_ref):
    rows = u_ref.shape[3]
    y2 = None
    for kind, h in enumerate((hfr, hfi, hbr, hbi)):
        t = _dot(h[...].astype(BF16), wout_ref[0, kind])
        y2 = t if y2 is None else y2 + t
    for gl in range(2):
        y = y2[:, gl * rows:(gl + 1) * rows] + _dot(u_ref[gl, 0], m_ref[0, gl])
        y_ref[gl, 0] = y.astype(BF16)


def _s5_outputs(uf, hprev, wout, m):
    g, b_, j, rows = uf.shape
    npair = g // 2
    hspec = pl.BlockSpec((j, LANES), lambda pr, b: (0, b * npair + pr))
    return pl.pallas_call(
        _s5out_kernel,
        out_shape=jax.ShapeDtypeStruct((g, b_, j, rows), BF16),
        grid=(npair, b_),
        in_specs=[pl.BlockSpec((2, 1, j, rows), lambda pr, b: (pr, b, 0, 0))] + [hspec] * 4 +
                 [pl.BlockSpec((1, 4, LANES, 2 * rows), lambda pr, b: (pr, 0, 0, 0)),
                  pl.BlockSpec((1, 2, rows, rows), lambda pr, b: (pr, 0, 0, 0))],
        out_specs=pl.BlockSpec((2, 1, j, rows), lambda pr, b: (pr, b, 0, 0)),
        compiler_params=_cparams(("parallel", "parallel")),
        name="s5out",
    )(uf, *hprev, wout, m)


def _s5_mixer(u, u_c, ops):
    win, wout, m, dec = ops
    b_, l_, w = u.shape
    lc = u_c.shape[1]
    tc, p = S5_CHUNK, SSM_GROUP
    g = w // p
    npair = g // 2
    jc, jl = lc // tc, l_ // tc
    j = jc + jl
    ua = jnp.concatenate([u_c, u], axis=1)
    uf = ua.reshape(b_, j, tc, g, p).transpose(3, 0, 1, 2, 4).reshape(g, b_, j, tc * p)
    states = _s5_states(uf, win)
    dec_cols = jnp.broadcast_to(dec.transpose(1, 2, 3, 0, 4)[:, :, :, None], (2, 2, 1, b_, npair, LANES))
    dec_cols = dec_cols.reshape(2, 2, 1, b_ * npair * LANES)
    hprev = _s5_scan(states, dec_cols, jc)
    yf = _s5_outputs(uf, hprev, wout, m)
    y = yf[:, :, jc:].reshape(g, b_, jl, tc, p).transpose(1, 2, 3, 0, 4).reshape(b_, l_, w)
    return y


def _attn_kernel(nkv, qt_ref, k_ref, vx_ref, o_ref, s0_ref, s1_ref):
    hd = HEAD_DIM
    kvh = pl.program_id(1)
    tq = qt_ref.shape[2]
    gq = qt_ref.shape[1] // hd
    kw = k_ref.shape[3]
    vr = vx_ref.shape[3]
    own = (lax.broadcasted_iota(jnp.int32, (kw, tq), 0) >> (hd.bit_length() - 1)) == kvh
    qp = []
    for g in range(gq):
        qg = qt_ref[0, g * hd:(g + 1) * hd, :].astype(F32)
        q2 = jnp.concatenate([qg] * (kw // hd), axis=0)
        qp.append(jnp.where(own, q2, 0.0).astype(BF16))
    qp = jnp.concatenate(qp, axis=1)
    m0 = jnp.full((1, gq * tq), -jnp.inf, F32)
    acc0 = jnp.zeros((vr, gq * tq), F32)

    def scores(jb, s_ref):
        s = _dot(k_ref[0, jb], qp)
        s_ref[...] = s.astype(BF16)
        return jnp.max(s, axis=0, keepdims=True)

    def consume(jb, s_ref, cmax, m_prev, acc):
        m_new = jnp.maximum(m_prev, cmax).astype(BF16).astype(F32)
        alpha = jnp.exp2(m_prev - m_new)
        p = jnp.exp2(s_ref[...] - m_new.astype(BF16))
        return m_new, alpha * acc + _dot(vx_ref[0, 0, jb], p)

    bufs = (s0_ref, s1_ref)

    def steps(j0, count, m, acc, cmax):
        for u in range(count):
            cnext = scores(j0 + u + 1, bufs[(u + 1) % 2])
            m, acc = consume(j0 + u, bufs[u % 2], cmax, m, acc)
            cmax = cnext
        return m, acc, cmax

    unroll = ATTN_STEPS_PER_TRIP
    trips = (nkv - 1) // unroll
    carry = lax.fori_loop(0, trips, lambda i, c: steps(unroll * i, unroll, *c), (m0, acc0, scores(0, s0_ref)))
    m, acc, cmax = steps(trips * unroll, (nkv - 1) - trips * unroll, *carry)
    m, acc = consume(nkv - 1, bufs[(nkv - 1) % 2], cmax, m, acc)
    o = acc[0:hd] / acc[hd:hd + 1]
    ot = jnp.concatenate([o[:, g * tq:(g + 1) * tq] for g in range(gq)], axis=0)
    o_ref[0] = ot.T.astype(BF16)


def _attention(qt, k4, vx, tq):
    b_, wq, l_ = qt.shape
    _, kvh, nkv, vr, tk = vx.shape
    kw = k4.shape[3]
    gw = wq // kvh
    return pl.pallas_call(
        functools.partial(_attn_kernel, nkv),
        out_shape=jax.ShapeDtypeStruct((b_, l_, wq), BF16),
        grid=(b_, kvh, l_ // tq),
        in_specs=[pl.BlockSpec((1, gw, tq), lambda b, h, i: (b, h, i)),
                  pl.BlockSpec((1, nkv, tk, kw), lambda b, h, i: (b, 0, 0, 0)),
                  pl.BlockSpec((1, 1, nkv, vr, tk), lambda b, h, i: (b, h, 0, 0, 0))],
        out_specs=pl.BlockSpec((1, tq, gw), lambda b, h, i: (b, i, h)),
        scratch_shapes=[pltpu.VMEM((tk, (gw // HEAD_DIM) * tq), BF16)] * 2,
        compiler_params=_cparams(("parallel", "parallel", "parallel")),
        name="attn",
    )(qt, k4, vx)


def _merge_kernel(dm, ne, ys_ref, oa_ref, gl_ref, x_ref, mod_ref, wglu_ref, bglu_ref, wso_ref, wao_ref,
                  wout_ref, n2g_ref, rw_ref, rb_ref, ltri_ref, xmix_ref, h2_ref, idx_ref, gate_ref, rank_ref,
                  cnt_out_ref, cnt_ref):
    s = jax.nn.gelu(ys_ref[0].astype(F32))
    s = s * jax.nn.sigmoid(_dot(s.astype(BF16), wglu_ref[...]) + bglu_ref[...])
    a = _dot(s.astype(BF16), wso_ref[...])
    bq = _dot(oa_ref[0], wao_ref[...])
    gts = jax.nn.sigmoid(gl_ref[0].astype(F32))
    merged = gts[:, 0:dm] * a + gts[:, dm:2 * dm] * bq
    out = _dot(merged.astype(BF16), wout_ref[...])
    g1 = mod_ref[0, :, 2 * dm:3 * dm]
    sh2 = mod_ref[0, :, 3 * dm:4 * dm]
    sc2 = mod_ref[0, :, 4 * dm:5 * dm]
    xm = x_ref[0] + g1 * out
    xmix_ref[0] = xm
    ms = jnp.mean(xm * xm, axis=-1, keepdims=True)
    h2 = xm * lax.rsqrt(ms + EPS) * n2g_ref[...] * (1.0 + sc2) + sh2
    h2_ref[0] = _pack2(h2[:, 0:dm // 2], h2[:, dm // 2:dm])
    logits = _dot3(h2, rw_ref[...]) + rb_ref[...]
    lane = lax.broadcasted_iota(jnp.int32, logits.shape, 1).astype(F32)
    neg = jnp.float32(-jnp.inf)
    vals, idxs = [], []
    l = logits
    for _ in range(TOP_K):
        mx = jnp.max(l, axis=-1, keepdims=True)
        ix = jnp.min(jnp.where(l == mx, lane, float(LANES)), axis=-1, keepdims=True)
        vals.append(mx)
        idxs.append(ix)
        l = jnp.where(lane == ix, neg, l)
    es = [jnp.exp(v - vals[0]) for v in vals]
    den = es[0] + es[1] + es[2] + es[3]
    @pl.when((pl.program_id(0) == 0) & (pl.program_id(1) == 0))
    def _():
        cnt_ref[...] = jnp.zeros(cnt_ref.shape, F32)

    onehot = jnp.zeros(logits.shape, F32)
    for k in range(TOP_K):
        onehot = onehot + jnp.where(lane == idxs[k], 1.0, 0.0)
    prior = _dot(ltri_ref[...], onehot.astype(BF16)) + cnt_ref[...]
    cnt_ref[...] = cnt_ref[...] + jnp.sum(onehot, axis=0, keepdims=True)
    cnt_out_ref[...] = cnt_ref[...]
    idx_out = jnp.zeros(logits.shape, F32)
    gate_out = jnp.zeros(logits.shape, F32)
    rank_out = jnp.zeros(logits.shape, F32)
    for k in range(TOP_K):
        rk = jnp.sum(jnp.where(lane == idxs[k], prior, 0.0), axis=-1, keepdims=True)
        idx_out = jnp.where(lane == float(k), idxs[k], idx_out)
        gate_out = jnp.where(lane == float(k), es[k] / den, gate_out)
        rank_out = jnp.where(lane == float(k), rk, rank_out)
    idx_ref[0] = idx_out.astype(jnp.int32)
    gate_ref[0] = gate_out
    rank_ref[0] = rank_out.astype(jnp.int32)


def _merge(y_ssm, o_attn, glog, x, mod3, w_glu, b_glu, w_ssm_out, w_attn_out, w_out, norm2_g,
           rw_pad, rb_pad, ne, tm):
    b_, l_, dm = x.shape
    wu = dm // 2
    tok = lambda w: pl.BlockSpec((1, tm, w), lambda b, i: (b, i, 0))
    full = lambda r, c: pl.BlockSpec((r, c), lambda b, i: (0, 0))
    ltri = (jnp.arange(tm)[:, None] > jnp.arange(tm)[None, :]).astype(BF16)
    return pl.pallas_call(
        functools.partial(_merge_kernel, dm, ne),
        out_shape=[jax.ShapeDtypeStruct((b_, l_, dm), F32),
                   jax.ShapeDtypeStruct((b_, l_, dm // 2), jnp.uint32),
                   jax.ShapeDtypeStruct((b_, l_, LANES), jnp.int32),
                   jax.ShapeDtypeStruct((b_, l_, LANES), F32),
                   jax.ShapeDtypeStruct((b_, l_, LANES), jnp.int32),
                   jax.ShapeDtypeStruct((1, LANES), F32)],
        grid=(b_, l_ // tm),
        in_specs=[tok(wu), tok(wu), tok(2 * dm), tok(dm),
                  pl.BlockSpec((1, 1, N_MOD * dm), lambda b, i: (b, 0, 0)),
                  full(wu, wu), full(1, wu), full(wu, dm), full(wu, dm), full(dm, dm),
                  full(1, dm), full(dm, LANES), full(1, LANES), full(tm, tm)],
        out_specs=[tok(dm), tok(dm // 2), tok(LANES), tok(LANES), tok(LANES), full(1, LANES)],
        scratch_shapes=[pltpu.VMEM((1, LANES), F32)],
        compiler_params=_cparams(("arbitrary", "arbitrary")),
        name="merge",
    )(y_ssm, o_attn, glog, x, mod3, w_glu, b_glu, w_ssm_out, w_attn_out, w_out, norm2_g, rw_pad, rb_pad, ltri)


def _expert_kernel(de, be_ref, nv_ref, x_ref, wgu_ref, bgu_ref, wd_ref, bd_ref, y_ref, wgu_b, wd_b):
    i = pl.program_id(0)
    n_valid = nv_ref[i]
    used = n_valid > 0
    new_expert = (i == 0) | (be_ref[i] != be_ref[jnp.maximum(i - 1, 0)])

    @pl.when(used & new_expert)
    def _():
        wgu_b[...] = wgu_ref[0].astype(BF16)
        wd_b[...] = wd_ref[0].astype(BF16)

    @pl.when(used)
    def _():
        row = lax.broadcasted_iota(jnp.int32, (x_ref.shape[0], 1), 0)
        x_lo, x_hi = _unpack2(jnp.where(row < n_valid, x_ref[...], jnp.uint32(0)))
        xb = jnp.concatenate([x_lo, x_hi], axis=1).astype(BF16)
        gu = _dot(xb, wgu_b[...]) + bgu_ref[0]
        gate = jnp.minimum(gu[:, 0:de], SWIGLU_LIMIT)
        up = jnp.clip(gu[:, de:2 * de], -SWIGLU_LIMIT, SWIGLU_LIMIT)
        act = (up + 1.0) * (gate * jax.nn.sigmoid(SWIGLU_ALPHA * gate))
        y = _dot(act.astype(BF16), wd_b[...]) + bd_ref[0]
        half = y.shape[1] // 2
        y_ref[...] = _pack2(y[:, 0:half], y[:, half:])

    @pl.when(jnp.logical_not(used))
    def _():
        y_ref[...] = jnp.zeros(y_ref.shape, y_ref.dtype)


def _experts(xb, block_expert, block_valid, w_gate_up, b_gate_up, w_down, b_down):
    n_slots, dmh = xb.shape
    dm = 2 * dmh
    ne, _, de2 = w_gate_up.shape
    de = de2 // 2
    nb = n_slots // EXPERT_BLOCK
    return pl.pallas_call(
        functools.partial(_expert_kernel, de),
        out_shape=jax.ShapeDtypeStruct((n_slots, dmh), jnp.uint32),
        grid_spec=pltpu.PrefetchScalarGridSpec(
            num_scalar_prefetch=2,
            grid=(nb,),
            in_specs=[pl.BlockSpec((EXPERT_BLOCK, dmh), lambda i, be, nu: (i, 0)),
                      pl.BlockSpec((1, dm, de2), lambda i, be, nu: (be[i], 0, 0)),
                      pl.BlockSpec((1, 1, de2), lambda i, be, nu: (be[i], 0, 0)),
                      pl.BlockSpec((1, de, dm), lambda i, be, nu: (be[i], 0, 0)),
                      pl.BlockSpec((1, 1, dm), lambda i, be, nu: (be[i], 0, 0))],
            out_specs=pl.BlockSpec((EXPERT_BLOCK, dmh), lambda i, be, nu: (i, 0)),
            scratch_shapes=[pltpu.VMEM((dm, de2), BF16), pltpu.VMEM((de, dm), BF16)],
        ),
        compiler_params=_cparams(("arbitrary",)),
        name="expert",
    )(block_expert, block_valid, xb, w_gate_up, b_gate_up.reshape(ne, 1, de2), w_down, b_down.reshape(ne, 1, dm))


def _sc_gather_rows(table, idx):
    n = idx.shape[0]
    w = table.shape[1]
    mesh = plsc.VectorSubcoreMesh(core_axis_name="c", subcore_axis_name="s")
    nc, nw = mesh.num_cores, mesh.num_cores * mesh.num_subcores
    win = SC_GATHER_WINDOW
    assert n % (win * nw) == 0
    per_worker = n // nw

    @functools.partial(
        pl.kernel, out_type=jax.ShapeDtypeStruct((n, w), table.dtype), mesh=mesh,
        scratch_types=[pltpu.VMEM((win,), jnp.int32), pltpu.VMEM((win, w), table.dtype),
                       pltpu.SemaphoreType.DMA])
    def gather(x_hbm, i_hbm, o_hbm, idx_v, rows_v, sem):
        base = (lax.axis_index("s") * nc + lax.axis_index("c")) * per_worker

        @pl.loop(0, per_worker // win)
        def _(j):
            off = base + j * win
            pltpu.sync_copy(i_hbm.at[pl.ds(off, win)], idx_v)
            pltpu.async_copy(x_hbm.at[idx_v], rows_v, sem).wait()
            pltpu.sync_copy(rows_v, o_hbm.at[pl.ds(off, win)])

    return gather(table, idx)


def _sc_scatter_rows(rows, dest, n_out):
    t_, w = rows.shape
    kk = dest.shape[0] // t_
    mesh = plsc.VectorSubcoreMesh(core_axis_name="c", subcore_axis_name="s")
    nc, nw = mesh.num_cores, mesh.num_cores * mesh.num_subcores
    win = SC_GATHER_WINDOW
    assert t_ % (win * nw) == 0
    per_worker = t_ // nw

    @functools.partial(
        pl.kernel, out_type=jax.ShapeDtypeStruct((n_out, w), rows.dtype), mesh=mesh,
        scratch_types=[pltpu.VMEM((win,), jnp.int32)] * kk + [pltpu.VMEM((win, w), rows.dtype),
                                                              pltpu.SemaphoreType.DMA])
    def scatter(x_hbm, d_hbm, o_hbm, *scratch):
        idx_vs, rows_v, sem = scratch[:kk], scratch[kk], scratch[kk + 1]
        base = (lax.axis_index("s") * nc + lax.axis_index("c")) * per_worker

        @pl.loop(0, per_worker // win)
        def _(j):
            off = base + j * win
            pltpu.sync_copy(x_hbm.at[pl.ds(off, win)], rows_v)
            for k in range(kk):
                pltpu.sync_copy(d_hbm.at[pl.ds(k * t_ + off, win)], idx_vs[k])
            copies = [pltpu.async_copy(rows_v, o_hbm.at[idx_vs[k]], sem) for k in range(kk)]
            for cp in copies:
                cp.wait()

    return scatter(rows, dest)


def _final_kernel(dm, yg_ref, gate_ref, xm_ref, mod_ref, fg_ref, o_ref):
    gates = gate_ref[0]
    f_lo = f_hi = None
    for k in range(TOP_K):
        y_lo, y_hi = _unpack2(yg_ref[k, 0])
        gk = gates[:, k:k + 1]
        f_lo = gk * y_lo if f_lo is None else f_lo + gk * y_lo
        f_hi = gk * y_hi if f_hi is None else f_hi + gk * y_hi
    f = jnp.concatenate([f_lo, f_hi], axis=1)
    g2 = mod_ref[0, :, 5 * dm:6 * dm]
    xo = xm_ref[0] + g2 * f
    ms = jnp.mean(xo * xo, axis=-1, keepdims=True)
    o_ref[0] = xo * lax.rsqrt(ms + EPS) * fg_ref[...]


def _final(yg, gates, x_mix, mod3, final_g, tm):
    b_, l_, dm = x_mix.shape
    return pl.pallas_call(
        functools.partial(_final_kernel, dm),
        out_shape=jax.ShapeDtypeStruct((b_, l_, dm), F32),
        grid=(b_, l_ // tm),
        in_specs=[pl.BlockSpec((TOP_K, 1, tm, dm // 2), lambda b, i: (0, b, i, 0)),
                  pl.BlockSpec((1, tm, LANES), lambda b, i: (b, i, 0)),
                  pl.BlockSpec((1, tm, dm), lambda b, i: (b, i, 0)),
                  pl.BlockSpec((1, 1, N_MOD * dm), lambda b, i: (b, 0, 0)),
                  pl.BlockSpec((1, dm), lambda b, i: (0, 0))],
        out_specs=pl.BlockSpec((1, tm, dm), lambda b, i: (b, i, 0)),
        compiler_params=_cparams(("parallel", "parallel")),
        name="final",
    )(yg, gates, x_mix, mod3, final_g.reshape(1, dm))


def _rope_tables(l_):
    half = HEAD_DIM // 2
    inv_freq = ROPE_THETA ** (-jnp.arange(0, half, 2, dtype=F32) / half)
    t = jnp.arange(l_, dtype=jnp.int32)
    row_id, col_id = t // GRID_W, t % GRID_W
    ang = jnp.concatenate([row_id.astype(F32)[:, None] * inv_freq,
                           col_id.astype(F32)[:, None] * inv_freq], axis=-1)
    cos = jnp.repeat(jnp.cos(ang), 2, axis=-1)
    sin = jnp.repeat(jnp.sin(ang), 2, axis=-1)
    sign = jnp.tile(jnp.array([-1.0, 1.0], F32), HEAD_DIM // 2)
    reps = LANES // HEAD_DIM
    return jnp.tile(cos, (1, reps)), jnp.tile(sin * sign, (1, reps))


def _routing(idx, rank, counts, t_, ne):
    m_ = t_ * TOP_K
    blk = EXPERT_BLOCK
    nb = (m_ + ne * (blk - 1)) // blk
    nb = -(-nb // SLOT_BLOCK_MULTIPLE) * SLOT_BLOCK_MULTIPLE
    blk_counts = (counts + blk - 1) // blk
    blk_ends = jnp.cumsum(blk_counts)
    first_slot = (blk_ends - blk_counts) * blk
    onehot = idx[:, :, None] == jnp.arange(ne, dtype=jnp.int32)[None, None, :]
    dest = rank + jnp.sum(jnp.where(onehot, first_slot[None, None, :], 0), axis=2)
    blocks = jnp.arange(nb, dtype=jnp.int32)
    block_expert = jnp.minimum(jnp.sum(blocks[:, None] >= blk_ends[None, :], axis=1), ne - 1)
    in_expert = blocks * blk - first_slot[block_expert]
    block_valid = jnp.clip(counts[block_expert] - in_expert, 0, blk)
    block_valid = jnp.where(blocks < blk_ends[-1], block_valid, 0)
    return dest.astype(jnp.int32), block_expert.astype(jnp.int32), block_valid.astype(jnp.int32), nb * blk


def kernel(x, c, ctx, c_ctx, w_mod, b_mod, norm1_g, norm2_g, w_in, s5_lam_re, s5_lam_im, s5_log_step,
           s5_b_re, s5_b_im, s5_c_re, s5_c_im, s5_d, w_glu, b_glu, w_ssm_out, q_norm_g, k_norm_g,
           w_attn_out, w_out, router_w, router_b, w_gate_up, b_gate_up, w_down, b_down, final_norm_g):
    b_, l_, dm = x.shape
    lc = ctx.shape[1]
    depth = w_mod.shape[0]
    assert depth == 1, "single-layer block"
    assert b_ <= 7 and l_ % 512 == 0 and lc % S5_CHUNK == 0
    wu = dm // 2
    wkv = wu // GQA_GROUP
    kvh = wkv // HEAD_DIM
    ne = router_w.shape[-1]
    i = 0

    c8 = jnp.zeros((8, dm), F32).at[:b_].set(c).at[b_].set(c_ctx)
    mod3 = _modulation(c8, w_mod[i], b_mod[i]).reshape(8, 1, N_MOD * dm)

    w_in_b = w_in[i].astype(BF16)
    cos, sin = _rope_tables(l_)
    hsum = jnp.kron(jnp.eye(wu // HEAD_DIM, dtype=F32), jnp.ones((HEAD_DIM, HEAD_DIM), F32)).astype(BF16)
    qg = jnp.tile(q_norm_g[i], wu // HEAD_DIM).reshape(1, wu)
    kg = jnp.tile(k_norm_g[i], wkv // HEAD_DIM).reshape(1, wkv)
    u, k, vt, qt, glog = _inproj(x, mod3, None, norm1_g[i], w_in_b, cos, sin, qg, kg, hsum, True, 512)
    ones_t = jnp.ones((lc, LANES), F32)
    u_c, k_c, vt_c = _inproj(ctx, mod3, b_, norm1_g[i], w_in_b[:, :wu + 2 * wkv], ones_t, 0.0 * ones_t,
                             qg, kg, hsum, False, lc)

    ops = _s5_operators(s5_lam_re[i], s5_lam_im[i], s5_log_step[i], s5_b_re[i], s5_b_im[i],
                        s5_c_re[i], s5_c_im[i], s5_d[i])
    y_ssm = _s5_mixer(u, u_c, ops)

    s_all = lc + l_
    tk = max(t for t in range(LANES, ATTN_KEY_CHUNK + 1, LANES) if s_all % t == 0)
    nkv = s_all // tk
    k4 = jnp.concatenate([k_c, k], axis=1).reshape(b_, nkv, tk, wkv)
    vt_all = jnp.concatenate([vt_c, vt], axis=2).reshape(b_, kvh, HEAD_DIM, s_all)
    vx = jnp.concatenate([vt_all, jnp.ones((b_, kvh, 1, s_all), BF16),
                          jnp.zeros((b_, kvh, ATTN_V_ROWS - HEAD_DIM - 1, s_all), BF16)], axis=2)
    vx = vx.reshape(b_, kvh, ATTN_V_ROWS, nkv, tk).transpose(0, 1, 3, 2, 4)
    o_attn = _attention(qt, k4, vx, ATTN_Q_BLOCK)

    rw_pad = jnp.zeros((dm, LANES), F32).at[:, :ne].set(router_w[i])
    rb_pad = jnp.full((1, LANES), -jnp.inf, F32).at[0, :ne].set(router_b[i])
    x_mix, h2, idx, gates, rank, counts = _merge(
        y_ssm, o_attn, glog, x, mod3, w_glu[i].astype(BF16), b_glu[i].reshape(1, wu),
        w_ssm_out[i].astype(BF16), w_attn_out[i].astype(BF16), w_out[i].astype(BF16),
        norm2_g[i].reshape(1, dm), rw_pad, rb_pad, ne, 512)

    t_ = b_ * l_
    dest, block_expert, block_valid, n_slots = _routing(
        idx[..., :TOP_K].reshape(t_, TOP_K), rank[..., :TOP_K].reshape(t_, TOP_K),
        counts[0, :ne].astype(jnp.int32), t_, ne)
    dest_kmajor = dest.T.reshape(-1)
    xb = _sc_scatter_rows(h2.reshape(t_, dm // 2), dest_kmajor, n_slots)
    yb = _experts(xb, block_expert, block_valid, w_gate_up[i], b_gate_up[i], w_down[i], b_down[i])
    yg = _sc_gather_rows(yb, dest_kmajor).reshape(TOP_K, b_, l_, dm // 2)
    return _final(yg, gates, x_mix, mod3, final_norm_g, 512)
```

```python
import functools
import math

import jax
import jax.numpy as jnp
from jax import lax
from jax.experimental import pallas as pl
from jax.experimental.pallas import tpu as pltpu
from jax.experimental.pallas import tpu_sc as plsc

F32 = jnp.float32
BF16 = jnp.bfloat16

EPS = 1e-6
GRID_W = 64
N_MOD = 6
SSM_GROUP = 16
SSM_STATE = 64
HEAD_DIM = 64
GQA_GROUP = 4
ROPE_THETA = 10000.0
TOP_K = 4
SWIGLU_ALPHA = 1.702
SWIGLU_LIMIT = 7.0
LOG2E = 1.4426950408889634

LANES = 128
S5_CHUNK = 32
S5_POW_ROWS = 56
EXPERT_BLOCK = 512
ATTN_V_ROWS = 80
ATTN_KEY_CHUNK = 384
ATTN_Q_BLOCK = 256
ATTN_STEPS_PER_TRIP = 10
SC_GATHER_WINDOW = 128
SC_WORKERS = 32
SLOT_BLOCK_MULTIPLE = SC_GATHER_WINDOW * SC_WORKERS // EXPERT_BLOCK
VMEM_LIMIT = 56 * 1024 * 1024


def _cparams(sem, flags=None):
    return pltpu.CompilerParams(dimension_semantics=sem, vmem_limit_bytes=VMEM_LIMIT, flags=flags)


def _dot(a, b):
    return jnp.dot(a, b, preferred_element_type=F32)


def _split(a):
    hi = a.astype(BF16)
    lo = (a - hi.astype(F32)).astype(BF16)
    return hi, lo


def _pack2(lo, hi):
    lo_w = lax.bitcast_convert_type(lo.astype(BF16).astype(F32), jnp.uint32) >> 16
    hi_w = lax.bitcast_convert_type(hi.astype(BF16).astype(F32), jnp.uint32) & jnp.uint32(0xFFFF0000)
    return lo_w | hi_w


def _unpack2(words):
    lo = lax.bitcast_convert_type(words << 16, F32)
    hi = lax.bitcast_convert_type(words & jnp.uint32(0xFFFF0000), F32)
    return lo, hi


def _dot3(a, b):
    ah, al = _split(a)
    bh, bl = _split(b)
    return _dot(ah, bh) + _dot(ah, bl) + _dot(al, bh)


def _mod_kernel(c_ref, w_ref, b_ref, o_ref):
    c = c_ref[...]
    s = c * jax.nn.sigmoid(c)
    o_ref[...] = _dot3(s, w_ref[...]) + b_ref[...]


def _modulation(c8, w_mod, b_mod):
    d = c8.shape[1]
    n = w_mod.shape[1]
    tn = 1024
    return pl.pallas_call(
        _mod_kernel,
        out_shape=jax.ShapeDtypeStruct((8, n), F32),
        grid=(n // tn,),
        in_specs=[pl.BlockSpec((8, d), lambda j: (0, 0)),
                  pl.BlockSpec((d, tn), lambda j: (0, j)),
                  pl.BlockSpec((1, tn), lambda j: (0, j))],
        out_specs=pl.BlockSpec((8, tn), lambda j: (0, j)),
        compiler_params=_cparams(("parallel",)),
        name="mod",
    )(c8, w_mod, b_mod.reshape(1, n))


def _head_norm_rope(z, gain, hsum, cos, sin, scale):
    ss = _dot((z * z).astype(BF16), hsum)
    zn = z * lax.rsqrt(ss * (1.0 / HEAD_DIM) + EPS) * gain
    lane = lax.broadcasted_iota(jnp.int32, (1, LANES), 1)
    even = (lane & 1) == 0
    outs = []
    for ci in range(z.shape[1] // LANES):
        ch = zn[:, ci * LANES:(ci + 1) * LANES]
        nxt = pltpu.roll(ch, LANES - 1, axis=1)
        prv = pltpu.roll(ch, 1, axis=1)
        sw = jnp.where(even, nxt, prv)
        outs.append((ch * cos + sw * sin) * scale)
    return outs


def _inproj_kernel(with_q, dm, x_ref, mod_ref, g_ref, w_ref, cos_ref, sin_ref, qg_ref, kg_ref,
                   hsum_ref, *out_refs):
    x = x_ref[0]
    ms = jnp.mean(x * x, axis=-1, keepdims=True)
    y = x * lax.rsqrt(ms + EPS) * g_ref[...]
    sh = mod_ref[0, :, 0:dm]
    sc = mod_ref[0, :, dm:2 * dm]
    hb = (y * (1.0 + sc) + sh).astype(BF16)
    cos = cos_ref[...]
    sin = sin_ref[...]
    wu = dm // 2
    wkv = wu // GQA_GROUP
    c_k, c_v, c_q, c_g = wu, wu + wkv, wu + 2 * wkv, 2 * wu + 2 * wkv
    u_ref, k_ref, vt_ref = out_refs[:3]
    u_ref[0] = _dot(hb, w_ref[:, 0:c_k]).astype(BF16)
    kz = _dot(hb, w_ref[:, c_k:c_v])
    (kr,) = _head_norm_rope(kz, kg_ref[...], hsum_ref[0:wkv, 0:wkv], cos, sin, 1.0)
    k_ref[0] = kr.astype(BF16)
    vt_ref[0] = _dot(hb, w_ref[:, c_v:c_q]).T.astype(BF16)
    if with_q:
        qt_ref, gl_ref = out_refs[3:]
        qz = _dot(hb, w_ref[:, c_q:c_g])
        qs = _head_norm_rope(qz, qg_ref[...], hsum_ref[...], cos, sin, HEAD_DIM ** -0.5 * LOG2E)
        for ci, qc in enumerate(qs):
            qt_ref[0, ci * LANES:(ci + 1) * LANES, :] = qc.T.astype(BF16)
        gl_ref[0] = _dot(hb, w_ref[:, c_g:]).astype(BF16)


def _inproj(x, mod3, mod_row0, norm_g, w_in_b, cos, sin, qg, kg, hsum, with_q, tm):
    b_, l_, dm = x.shape
    wu = dm // 2
    wkv = wu // GQA_GROUP
    n_in = w_in_b.shape[1]
    outs = [jax.ShapeDtypeStruct((b_, l_, wu), BF16),
            jax.ShapeDtypeStruct((b_, l_, wkv), BF16),
            jax.ShapeDtypeStruct((b_, wkv, l_), BF16)]
    ospecs = [pl.BlockSpec((1, tm, wu), lambda b, i: (b, i, 0)),
              pl.BlockSpec((1, tm, wkv), lambda b, i: (b, i, 0)),
              pl.BlockSpec((1, wkv, tm), lambda b, i: (b, 0, i))]
    if with_q:
        outs += [jax.ShapeDtypeStruct((b_, wu, l_), BF16),
                 jax.ShapeDtypeStruct((b_, l_, 2 * dm), BF16)]
        ospecs += [pl.BlockSpec((1, wu, tm), lambda b, i: (b, 0, i)),
                   pl.BlockSpec((1, tm, 2 * dm), lambda b, i: (b, i, 0))]
    if mod_row0 is None:
        mod_map = lambda b, i: (b, 0, 0)
    else:
        mod_map = lambda b, i: (mod_row0, 0, 0)
    return pl.pallas_call(
        functools.partial(_inproj_kernel, with_q, dm),
        out_shape=outs,
        grid=(b_, l_ // tm),
        in_specs=[pl.BlockSpec((1, tm, dm), lambda b, i: (b, i, 0)),
                  pl.BlockSpec((1, 1, N_MOD * dm), mod_map),
                  pl.BlockSpec((1, dm), lambda b, i: (0, 0)),
                  pl.BlockSpec((dm, n_in), lambda b, i: (0, 0)),
                  pl.BlockSpec((tm, LANES), lambda b, i: (i, 0)),
                  pl.BlockSpec((tm, LANES), lambda b, i: (i, 0)),
                  pl.BlockSpec((1, wu), lambda b, i: (0, 0)),
                  pl.BlockSpec((1, wkv), lambda b, i: (0, 0)),
                  pl.BlockSpec((wu, wu), lambda b, i: (0, 0))],
        out_specs=ospecs,
        compiler_params=_cparams(("parallel", "parallel")),
        name="inproj_x" if with_q else "inproj_ctx",
    )(x, mod3, norm_g.reshape(1, dm), w_in_b, cos, sin, qg, kg, hsum)


def _cexp_pow(e, lr, li):
    mag = jnp.exp(e * lr)
    ang = e * li
    return mag * jnp.cos(ang), mag * jnp.sin(ang)


def _cmul(ar, ai, br, bi):
    return ar * br - ai * bi, ar * bi + ai * br


def _s5pre_kernel(lamr_r, lami_r, lst_r, lamr_c, lami_c, lst_c, btr_ref, bti_ref, ctr_ref, cti_ref,
                  d_ref, win_ref, wout_ref, m_ref, dec_ref):
    tc = S5_CHUNK
    half = tc // 2
    p = SSM_GROUP
    rows = tc * p
    gsh = SSM_GROUP.bit_length() - 1
    nsh = SSM_STATE.bit_length() - 1
    row_i = lax.broadcasted_iota(jnp.int32, (rows, rows), 0)
    col_i = lax.broadcasted_iota(jnp.int32, (rows, rows), 1)
    lane_g = lax.broadcasted_iota(jnp.int32, (1, LANES), 1) >> nsh
    subl_g = lax.broadcasted_iota(jnp.int32, (LANES, 1), 0) >> nsh
    e_rows = lax.broadcasted_iota(jnp.int32, (S5_POW_ROWS, 1), 0) - half
    e_lanes = lax.broadcasted_iota(jnp.int32, (1, LANES), 1) - half
    sel_e = lax.broadcasted_iota(jnp.int32, (LANES, rows), 0) - half
    sel_t = lax.broadcasted_iota(jnp.int32, (LANES, rows), 1) >> gsh
    sel_p = (lax.broadcasted_iota(jnp.int32, (LANES, rows), 0)
             == (lax.broadcasted_iota(jnp.int32, (LANES, rows), 1) & (p - 1))).astype(BF16)

    def spread_lanes(table, sel):
        hi, lo = _split(table)
        return _dot(hi, sel) + _dot(lo, sel)

    def spread_rows(table, e_of_s):
        return jnp.concatenate([jnp.broadcast_to(table[e_of_s(s) + half:e_of_s(s) + half + 1, :], (p, LANES))
                                for s in range(tc)], axis=0)

    m_acc = [jnp.where(row_i == col_i, d_ref[0, gl], 0.0) for gl in range(2)]
    for dr in range(2):
        step = jnp.exp(lst_r[dr, 0])
        lam_r, lam_i = lamr_r[dr, 0], lami_r[dr, 0]
        lr, li = lam_r * step, lam_i * step
        pw_r, pw_i = _cexp_pow(e_rows.astype(F32), lr, li)
        lb_r, lb_i = pw_r[half + 1:half + 2, :], pw_i[half + 1:half + 2, :]
        den = lam_r * lam_r + lam_i * lam_i
        nr, ni = lb_r - 1.0, lb_i
        cf_r = (nr * lam_r + ni * lam_i) / den
        cf_i = (ni * lam_r - nr * lam_i) / den
        bt_r = jnp.concatenate([btr_ref[dr, 0]] * tc, axis=0)
        bt_i = jnp.concatenate([bti_ref[dr, 0]] * tc, axis=0)
        bb_r, bb_i = _cmul(cf_r, cf_i, bt_r, bt_i)
        e_in = (lambda s: tc - 1 - s) if dr == 0 else (lambda s: s)
        wi_r, wi_i = _cmul(bb_r, bb_i, spread_rows(pw_r, e_in), spread_rows(pw_i, e_in))
        e_a = (lambda s: half - s) if dr == 0 else (lambda s: s - half)
        a_r, a_i = _cmul(bb_r, bb_i, spread_rows(pw_r, e_a), spread_rows(pw_i, e_a))
        dec_ref[0, dr, 0] = pw_r[tc + half:tc + half + 1, :]
        dec_ref[0, dr, 1] = pw_i[tc + half:tc + half + 1, :]
        stepc = jnp.exp(lst_c[dr, 0])
        lrc, lic = lamr_c[dr, 0] * stepc, lami_c[dr, 0] * stepc
        qw_r, qw_i = _cexp_pow(e_lanes.astype(F32), lrc, lic)
        qw_r = jnp.where(e_lanes <= tc, qw_r, 0.0)
        qw_i = jnp.where(e_lanes <= tc, qw_i, 0.0)
        c_r, c_i = spread_lanes(ctr_ref[dr, 0], sel_p), spread_lanes(cti_ref[dr, 0], sel_p)
        sel_b = (sel_e == (sel_t - half if dr == 0 else half - sel_t)).astype(BF16)
        bm_r, bm_i = _cmul(c_r, c_i, spread_lanes(qw_r, sel_b), spread_lanes(qw_i, sel_b))
        sel_o = (sel_e == (sel_t + 1 if dr == 0 else tc - sel_t)).astype(BF16)
        wo_r, wo_i = _cmul(c_r, c_i, spread_lanes(qw_r, sel_o), spread_lanes(qw_i, sel_o))
        if dr == 0:
            mask = (col_i >> gsh) >= (row_i >> gsh)
        else:
            mask = (row_i >> gsh) >= (col_i >> gsh)
        for gl in range(2):
            lsel = lane_g == gl
            ssel = subl_g == gl
            win_ref[0, gl, 2 * dr] = jnp.where(lsel, wi_r, 0.0).astype(BF16)
            win_ref[0, gl, 2 * dr + 1] = jnp.where(lsel, wi_i, 0.0).astype(BF16)
            wout_ref[0, 2 * dr, :, gl * rows:(gl + 1) * rows] = jnp.where(ssel, wo_r, 0.0).astype(BF16)
            wout_ref[0, 2 * dr + 1, :, gl * rows:(gl + 1) * rows] = jnp.where(ssel, -wo_i, 0.0).astype(BF16)
            kmat = (_dot3(jnp.where(lsel, a_r, 0.0), bm_r) - _dot3(jnp.where(lsel, a_i, 0.0), bm_i))
            m_acc[gl] = m_acc[gl] + jnp.where(mask, kmat, 0.0)
    for gl in range(2):
        m_ref[0, gl] = m_acc[gl].astype(BF16)


def _s5_operators(lam_re, lam_im, log_step, b_re, b_im, c_re, c_im, d_skip):
    g = lam_re.shape[1]
    n, p = SSM_STATE, SSM_GROUP
    npair = g // 2
    tc = S5_CHUNK
    rows = tc * p

    def row_form(a):
        return a.reshape(2, npair, 1, 2 * n)

    def col_form(a):
        return a.reshape(2, npair, 2 * n, 1)

    lst = jnp.broadcast_to(log_step[:, :, None], (2, g, n))
    bt = lambda b: b.reshape(2, npair, 2, n, p).transpose(0, 1, 4, 2, 3).reshape(2, npair, p, 2 * n)
    ct = lambda c: jnp.pad(c.reshape(2, npair, 2, p, n).transpose(0, 1, 2, 4, 3).reshape(2, npair, 2 * n, p),
                           ((0, 0), (0, 0), (0, 0), (0, LANES - p)))
    d2 = jnp.broadcast_to(d_skip.reshape(npair, 2, 1, 1, p), (npair, 2, 1, tc, p)).reshape(npair, 2, 1, rows)

    def spec4(shape):
        return pl.BlockSpec((2, 1) + shape, lambda i: (0, i, 0, 0))

    return pl.pallas_call(
        _s5pre_kernel,
        out_shape=[jax.ShapeDtypeStruct((npair, 2, 4, rows, LANES), BF16),
                   jax.ShapeDtypeStruct((npair, 4, LANES, 2 * rows), BF16),
                   jax.ShapeDtypeStruct((npair, 2, rows, rows), BF16),
                   jax.ShapeDtypeStruct((npair, 2, 2, 1, LANES), F32)],
        grid=(npair,),
        in_specs=[spec4((1, LANES)), spec4((1, LANES)), spec4((1, LANES)),
                  spec4((LANES, 1)), spec4((LANES, 1)), spec4((LANES, 1)),
                  spec4((p, LANES)), spec4((p, LANES)),
                  spec4((LANES, LANES)), spec4((LANES, LANES)),
                  pl.BlockSpec((1, 2, 1, rows), lambda i: (i, 0, 0, 0))],
        out_specs=[pl.BlockSpec((1, 2, 4, rows, LANES), lambda i: (i, 0, 0, 0, 0)),
                   pl.BlockSpec((1, 4, LANES, 2 * rows), lambda i: (i, 0, 0, 0)),
                   pl.BlockSpec((1, 2, rows, rows), lambda i: (i, 0, 0, 0)),
                   pl.BlockSpec((1, 2, 2, 1, LANES), lambda i: (i, 0, 0, 0, 0))],
        compiler_params=_cparams(("parallel",)),
        name="s5pre",
    )(row_form(lam_re), row_form(lam_im), row_form(lst),
      col_form(lam_re), col_form(lam_im), col_form(lst),
      bt(b_re), bt(b_im), ct(c_re), ct(c_im), d2)


def _s5state_kernel(u_ref, win_ref, *s_refs):
    u0 = u_ref[0, 0]
    u1 = u_ref[1, 0]
    for kind in range(4):
        s_refs[kind][...] = _dot(u0, win_ref[0, 0, kind]) + _dot(u1, win_ref[0, 1, kind])


def _s5_states(uf, win):
    g, b_, j, rows = uf.shape
    npair = g // 2
    out = jax.ShapeDtypeStruct((j, b_ * npair * LANES), F32)
    ospec = pl.BlockSpec((j, LANES), lambda pr, b: (0, b * npair + pr))
    return pl.pallas_call(
        _s5state_kernel,
        out_shape=[out] * 4,
        grid=(npair, b_),
        in_specs=[pl.BlockSpec((2, 1, j, rows), lambda pr, b: (pr, b, 0, 0)),
                  pl.BlockSpec((1, 2, 4, rows, LANES), lambda pr, b: (pr, 0, 0, 0, 0))],
        out_specs=[ospec] * 4,
        compiler_params=_cparams(("parallel", "parallel")),
        name="s5state",
    )(uf, win)


def _s5scan_kernel(jc, sfr, sfi, sbr, sbi, dec_ref, hfr, hfi, hbr, hbi):
    j = sfr.shape[0]
    w = sfr.shape[1]
    zero = jnp.zeros((1, w), F32)

    def run(sr, si, hr_out, hi_out, ar, ai, lo, n, reverse, carry):
        def body(i, hc):
            row = (lo + n - 1 - i) if reverse else (lo + i)
            hr, hi = hc
            hr_out[pl.ds(row, 1), :] = hr
            hi_out[pl.ds(row, 1), :] = hi
            nr = ar * hr - ai * hi + sr[pl.ds(row, 1), :]
            ni = ar * hi + ai * hr + si[pl.ds(row, 1), :]
            return nr, ni
        return lax.fori_loop(0, n, body, carry)

    afr, afi = dec_ref[0, 0], dec_ref[0, 1]
    abr, abi = dec_ref[1, 0], dec_ref[1, 1]
    run(sfr, sfi, hfr, hfi, afr, afi, 0, j, False, (zero, zero))
    c = run(sbr, sbi, hbr, hbi, abr, abi, 0, jc, True, (zero, zero))
    run(sbr, sbi, hbr, hbi, abr, abi, jc, j - jc, True, c)


def _s5_scan(states, dec_cols, jc):
    j, wtot = states[0].shape
    tw = 1024
    spec = pl.BlockSpec((j, tw), lambda i: (0, i))
    return pl.pallas_call(
        functools.partial(_s5scan_kernel, jc),
        out_shape=[jax.ShapeDtypeStruct((j, wtot), F32)] * 4,
        grid=(wtot // tw,),
        in_specs=[spec] * 4 + [pl.BlockSpec((2, 2, 1, tw), lambda i: (0, 0, 0, i))],
        out_specs=[spec] * 4,
        compiler_params=_cparams(("parallel",)),
        name="s5scan",
    )(*states, dec_cols)


def _s5out_kernel(u_ref, hfr, hfi, hbr, hbi, wout_ref, m_ref, y_ref):
    rows = u_ref.shape[3]
    y2 = None
    for kind, h in enumerate((hfr, hfi, hbr, hbi)):
        t = _dot(h[...].astype(BF16), wout_ref[0, kind])
        y2 = t if y2 is None else y2 + t
    for gl in range(2):
        y = y2[:, gl * rows:(gl + 1) * rows] + _dot(u_ref[gl, 0], m_ref[0, gl])
        y_ref[gl, 0] = y.astype(BF16)


def _s5_outputs(uf, hprev, wout, m):
    g, b_, j, rows = uf.shape
    npair = g // 2
    hspec = pl.BlockSpec((j, LANES), lambda pr, b: (0, b * npair + pr))
    return pl.pallas_call(
        _s5out_kernel,
        out_shape=jax.ShapeDtypeStruct((g, b_, j, rows), BF16),
        grid=(npair, b_),
        in_specs=[pl.BlockSpec((2, 1, j, rows), lambda pr, b: (pr, b, 0, 0))] + [hspec] * 4 +
                 [pl.BlockSpec((1, 4, LANES, 2 * rows), lambda pr, b: (pr, 0, 0, 0)),
                  pl.BlockSpec((1, 2, rows, rows), lambda pr, b: (pr, 0, 0, 0))],
        out_specs=pl.BlockSpec((2, 1, j, rows), lambda pr, b: (pr, b, 0, 0)),
        compiler_params=_cparams(("parallel", "parallel")),
        name="s5out",
    )(uf, *hprev, wout, m)


def _s5_mixer(u, u_c, ops):
    win, wout, m, dec = ops
    b_, l_, w = u.shape
    lc = u_c.shape[1]
    tc, p = S5_CHUNK, SSM_GROUP
    g = w // p
    npair = g // 2
    jc, jl = lc // tc, l_ // tc
    j = jc + jl
    ua = jnp.concatenate([u_c, u], axis=1)
    uf = ua.reshape(b_, j, tc, g, p).transpose(3, 0, 1, 2, 4).reshape(g, b_, j, tc * p)
    states = _s5_states(uf, win)
    dec_cols = jnp.broadcast_to(dec.transpose(1, 2, 3, 0, 4)[:, :, :, None], (2, 2, 1, b_, npair, LANES))
    dec_cols = dec_cols.reshape(2, 2, 1, b_ * npair * LANES)
    hprev = _s5_scan(states, dec_cols, jc)
    yf = _s5_outputs(uf, hprev, wout, m)
    y = yf[:, :, jc:].reshape(g, b_, jl, tc, p).transpose(1, 2, 3, 0, 4).reshape(b_, l_, w)
    return y


def _attn_kernel(nkv, qt_ref, k_ref, vx_ref, o_ref, s0_ref, s1_ref):
    hd = HEAD_DIM
    kvh = pl.program_id(1)
    tq = qt_ref.shape[2]
    gq = qt_ref.shape[1] // hd
    kw = k_ref.shape[3]
    vr = vx_ref.shape[3]
    own = (lax.broadcasted_iota(jnp.int32, (kw, tq), 0) >> (hd.bit_length() - 1)) == kvh
    qp = []
    for g in range(gq):
        qg = qt_ref[0, g * hd:(g + 1) * hd, :].astype(F32)
        q2 = jnp.concatenate([qg] * (kw // hd), axis=0)
        qp.append(jnp.where(own, q2, 0.0).astype(BF16))
    qp = jnp.concatenate(qp, axis=1)
    m0 = jnp.full((1, gq * tq), -jnp.inf, F32)
    acc0 = jnp.zeros((vr, gq * tq), F32)

    def scores(jb, s_ref):
        s = _dot(k_ref[0, jb], qp)
        s_ref[...] = s
        return jnp.max(s, axis=0, keepdims=True)

    def consume(jb, s_ref, cmax, m_prev, acc):
        m_new = jnp.maximum(m_prev, cmax)
        alpha = jnp.exp2(m_prev - m_new)
        p = jnp.exp2(s_ref[...] - m_new).astype(BF16)
        return m_new, alpha * acc + _dot(vx_ref[0, 0, jb], p)

    bufs = (s0_ref, s1_ref)

    def steps(j0, count, m, acc, cmax):
        for u in range(count):
            cnext = scores(j0 + u + 1, bufs[(u + 1) % 2])
            m, acc = consume(j0 + u, bufs[u % 2], cmax, m, acc)
            cmax = cnext
        return m, acc, cmax

    unroll = ATTN_STEPS_PER_TRIP
    trips = (nkv - 1) // unroll
    carry = lax.fori_loop(0, trips, lambda i, c: steps(unroll * i, unroll, *c), (m0, acc0, scores(0, s0_ref)))
    m, acc, cmax = steps(trips * unroll, (nkv - 1) - trips * unroll, *carry)
    m, acc = consume(nkv - 1, bufs[(nkv - 1) % 2], cmax, m, acc)
    o = acc[0:hd] / acc[hd:hd + 1]
    ot = jnp.concatenate([o[:, g * tq:(g + 1) * tq] for g in range(gq)], axis=0)
    o_ref[0] = ot.T.astype(BF16)


def _attention(qt, k4, vx, tq):
    b_, wq, l_ = qt.shape
    _, kvh, nkv, vr, tk = vx.shape
    kw = k4.shape[3]
    gw = wq // kvh
    return pl.pallas_call(
        functools.partial(_attn_kernel, nkv),
        out_shape=jax.ShapeDtypeStruct((b_, l_, wq), BF16),
        grid=(b_, kvh, l_ // tq),
        in_specs=[pl.BlockSpec((1, gw, tq), lambda b, h, i: (b, h, i)),
                  pl.BlockSpec((1, nkv, tk, kw), lambda b, h, i: (b, 0, 0, 0)),
                  pl.BlockSpec((1, 1, nkv, vr, tk), lambda b, h, i: (b, h, 0, 0, 0))],
        out_specs=pl.BlockSpec((1, tq, gw), lambda b, h, i: (b, i, h)),
        scratch_shapes=[pltpu.VMEM((tk, (gw // HEAD_DIM) * tq), F32)] * 2,
        compiler_params=_cparams(("parallel", "parallel", "parallel")),
        name="attn",
    )(qt, k4, vx)


def _merge_kernel(dm, ne, ys_ref, oa_ref, gl_ref, x_ref, mod_ref, wglu_ref, bglu_ref, wso_ref, wao_ref,
                  wout_ref, n2g_ref, rw_ref, rb_ref, ltri_ref, xmix_ref, h2_ref, idx_ref, gate_ref, rank_ref,
                  cnt_out_ref, cnt_ref):
    s = jax.nn.gelu(ys_ref[0].astype(F32))
    s = s * jax.nn.sigmoid(_dot(s.astype(BF16), wglu_ref[...]) + bglu_ref[...])
    a = _dot(s.astype(BF16), wso_ref[...])
    bq = _dot(oa_ref[0], wao_ref[...])
    gts = jax.nn.sigmoid(gl_ref[0].astype(F32))
    merged = gts[:, 0:dm] * a + gts[:, dm:2 * dm] * bq
    out = _dot(merged.astype(BF16), wout_ref[...])
    g1 = mod_ref[0, :, 2 * dm:3 * dm]
    sh2 = mod_ref[0, :, 3 * dm:4 * dm]
    sc2 = mod_ref[0, :, 4 * dm:5 * dm]
    xm = x_ref[0] + g1 * out
    xmix_ref[0] = xm
    ms = jnp.mean(xm * xm, axis=-1, keepdims=True)
    h2 = xm * lax.rsqrt(ms + EPS) * n2g_ref[...] * (1.0 + sc2) + sh2
    h2_ref[0] = _pack2(h2[:, 0:dm // 2], h2[:, dm // 2:dm])
    logits = _dot3(h2, rw_ref[...]) + rb_ref[...]
    lane = lax.broadcasted_iota(jnp.int32, logits.shape, 1).astype(F32)
    neg = jnp.float32(-jnp.inf)
    vals, idxs = [], []
    l = logits
    for _ in range(TOP_K):
        mx = jnp.max(l, axis=-1, keepdims=True)
        ix = jnp.min(jnp.where(l == mx, lane, float(LANES)), axis=-1, keepdims=True)
        vals.append(mx)
        idxs.append(ix)
        l = jnp.where(lane == ix, neg, l)
    es = [jnp.exp(v - vals[0]) for v in vals]
    den = es[0] + es[1] + es[2] + es[3]
    @pl.when((pl.program_id(0) == 0) & (pl.program_id(1) == 0))
    def _():
        cnt_ref[...] = jnp.zeros(cnt_ref.shape, F32)

    onehot = jnp.zeros(logits.shape, F32)
    for k in range(TOP_K):
        onehot = onehot + jnp.where(lane == idxs[k], 1.0, 0.0)
    prior = _dot(ltri_ref[...], onehot.astype(BF16)) + cnt_ref[...]
    cnt_ref[...] = cnt_ref[...] + jnp.sum(onehot, axis=0, keepdims=True)
    cnt_out_ref[...] = cnt_ref[...]
    idx_out = jnp.zeros(logits.shape, F32)
    gate_out = jnp.zeros(logits.shape, F32)
    rank_out = jnp.zeros(logits.shape, F32)
    for k in range(TOP_K):
        rk = jnp.sum(jnp.where(lane == idxs[k], prior, 0.0), axis=-1, keepdims=True)
        idx_out = jnp.where(lane == float(k), idxs[k], idx_out)
        gate_out = jnp.where(lane == float(k), es[k] / den, gate_out)
        rank_out = jnp.where(lane == float(k), rk, rank_out)
    idx_ref[0] = idx_out.astype(jnp.int32)
    gate_ref[0] = gate_out
    rank_ref[0] = rank_out.astype(jnp.int32)


def _merge(y_ssm, o_attn, glog, x, mod3, w_glu, b_glu, w_ssm_out, w_attn_out, w_out, norm2_g,
           rw_pad, rb_pad, ne, tm):
    b_, l_, dm = x.shape
    wu = dm // 2
    tok = lambda w: pl.BlockSpec((1, tm, w), lambda b, i: (b, i, 0))
    full = lambda r, c: pl.BlockSpec((r, c), lambda b, i: (0, 0))
    ltri = (jnp.arange(tm)[:, None] > jnp.arange(tm)[None, :]).astype(BF16)
    return pl.pallas_call(
        functools.partial(_merge_kernel, dm, ne),
        out_shape=[jax.ShapeDtypeStruct((b_, l_, dm), F32),
                   jax.ShapeDtypeStruct((b_, l_, dm // 2), jnp.uint32),
                   jax.ShapeDtypeStruct((b_, l_, LANES), jnp.int32),
                   jax.ShapeDtypeStruct((b_, l_, LANES), F32),
                   jax.ShapeDtypeStruct((b_, l_, LANES), jnp.int32),
                   jax.ShapeDtypeStruct((1, LANES), F32)],
        grid=(b_, l_ // tm),
        in_specs=[tok(wu), tok(wu), tok(2 * dm), tok(dm),
                  pl.BlockSpec((1, 1, N_MOD * dm), lambda b, i: (b, 0, 0)),
                  full(wu, wu), full(1, wu), full(wu, dm), full(wu, dm), full(dm, dm),
                  full(1, dm), full(dm, LANES), full(1, LANES), full(tm, tm)],
        out_specs=[tok(dm), tok(dm // 2), tok(LANES), tok(LANES), tok(LANES), full(1, LANES)],
        scratch_shapes=[pltpu.VMEM((1, LANES), F32)],
        compiler_params=_cparams(("arbitrary", "arbitrary")),
        name="merge",
    )(y_ssm, o_attn, glog, x, mod3, w_glu, b_glu, w_ssm_out, w_attn_out, w_out, norm2_g, rw_pad, rb_pad, ltri)


def _expert_kernel(de, be_ref, nv_ref, x_ref, wgu_ref, bgu_ref, wd_ref, bd_ref, y_ref, wgu_b, wd_b):
    i = pl.program_id(0)
    n_valid = nv_ref[i]
    used = n_valid > 0
    new_expert = (i == 0) | (be_ref[i] != be_ref[jnp.maximum(i - 1, 0)])

    @pl.when(used & new_expert)
    def _():
        wgu_b[...] = wgu_ref[0].astype(BF16)
        wd_b[...] = wd_ref[0].astype(BF16)

    @pl.when(used)
    def _():
        row = lax.broadcasted_iota(jnp.int32, (x_ref.shape[0], 1), 0)
        x_lo, x_hi = _unpack2(jnp.where(row < n_valid, x_ref[...], jnp.uint32(0)))
        xb = jnp.concatenate([x_lo, x_hi], axis=1).astype(BF16)
        gu = _dot(xb, wgu_b[...]) + bgu_ref[0]
        gate = jnp.minimum(gu[:, 0:de], SWIGLU_LIMIT)
        up = jnp.clip(gu[:, de:2 * de], -SWIGLU_LIMIT, SWIGLU_LIMIT)
        act = (up + 1.0) * (gate * jax.nn.sigmoid(SWIGLU_ALPHA * gate))
        y = _dot(act.astype(BF16), wd_b[...]) + bd_ref[0]
        half = y.shape[1] // 2
        y_ref[...] = _pack2(y[:, 0:half], y[:, half:])

    @pl.when(jnp.logical_not(used))
    def _():
        y_ref[...] = jnp.zeros(y_ref.shape, y_ref.dtype)


def _experts(xb, block_expert, block_valid, w_gate_up, b_gate_up, w_down, b_down):
    n_slots, dmh = xb.shape
    dm = 2 * dmh
    ne, _, de2 = w_gate_up.shape
    de = de2 // 2
    nb = n_slots // EXPERT_BLOCK
    return pl.pallas_call(
        functools.partial(_expert_kernel, de),
        out_shape=jax.ShapeDtypeStruct((n_slots, dmh), jnp.uint32),
        grid_spec=pltpu.PrefetchScalarGridSpec(
            num_scalar_prefetch=2,
            grid=(nb,),
            in_specs=[pl.BlockSpec((EXPERT_BLOCK, dmh), lambda i, be, nu: (i, 0)),
                      pl.BlockSpec((1, dm, de2), lambda i, be, nu: (be[i], 0, 0)),
                      pl.BlockSpec((1, 1, de2), lambda i, be, nu: (be[i], 0, 0)),
                      pl.BlockSpec((1, de, dm), lambda i, be, nu: (be[i], 0, 0)),
                      pl.BlockSpec((1, 1, dm), lambda i, be, nu: (be[i], 0, 0))],
            out_specs=pl.BlockSpec((EXPERT_BLOCK, dmh), lambda i, be, nu: (i, 0)),
            scratch_shapes=[pltpu.VMEM((dm, de2), BF16), pltpu.VMEM((de, dm), BF16)],
        ),
        compiler_params=_cparams(("arbitrary",)),
        name="expert",
    )(block_expert, block_valid, xb, w_gate_up, b_gate_up.reshape(ne, 1, de2), w_down, b_down.reshape(ne, 1, dm))


def _sc_gather_rows(table, idx):
    n = idx.shape[0]
    w = table.shape[1]
    mesh = plsc.VectorSubcoreMesh(core_axis_name="c", subcore_axis_name="s")
    nc, nw = mesh.num_cores, mesh.num_cores * mesh.num_subcores
    win = SC_GATHER_WINDOW
    assert n % (win * nw) == 0
    per_worker = n // nw

    @functools.partial(
        pl.kernel, out_type=jax.ShapeDtypeStruct((n, w), table.dtype), mesh=mesh,
        scratch_types=[pltpu.VMEM((win,), jnp.int32), pltpu.VMEM((win, w), table.dtype),
                       pltpu.SemaphoreType.DMA])
    def gather(x_hbm, i_hbm, o_hbm, idx_v, rows_v, sem):
        base = (lax.axis_index("s") * nc + lax.axis_index("c")) * per_worker

        @pl.loop(0, per_worker // win)
        def _(j):
            off = base + j * win
            pltpu.sync_copy(i_hbm.at[pl.ds(off, win)], idx_v)
            pltpu.async_copy(x_hbm.at[idx_v], rows_v, sem).wait()
            pltpu.sync_copy(rows_v, o_hbm.at[pl.ds(off, win)])

    return gather(table, idx)


def _sc_scatter_rows(rows, dest, n_out):
    t_, w = rows.shape
    kk = dest.shape[0] // t_
    mesh = plsc.VectorSubcoreMesh(core_axis_name="c", subcore_axis_name="s")
    nc, nw = mesh.num_cores, mesh.num_cores * mesh.num_subcores
    win = SC_GATHER_WINDOW
    assert t_ % (win * nw) == 0
    per_worker = t_ // nw

    @functools.partial(
        pl.kernel, out_type=jax.ShapeDtypeStruct((n_out, w), rows.dtype), mesh=mesh,
        scratch_types=[pltpu.VMEM((win,), jnp.int32)] * kk + [pltpu.VMEM((win, w), rows.dtype),
                                                              pltpu.SemaphoreType.DMA])
    def scatter(x_hbm, d_hbm, o_hbm, *scratch):
        idx_vs, rows_v, sem = scratch[:kk], scratch[kk], scratch[kk + 1]
        base = (lax.axis_index("s") * nc + lax.axis_index("c")) * per_worker

        @pl.loop(0, per_worker // win)
        def _(j):
            off = base + j * win
            pltpu.sync_copy(x_hbm.at[pl.ds(off, win)], rows_v)
            for k in range(kk):
                pltpu.sync_copy(d_hbm.at[pl.ds(k * t_ + off, win)], idx_vs[k])
            copies = [pltpu.async_copy(rows_v, o_hbm.at[idx_vs[k]], sem) for k in range(kk)]
            for cp in copies:
                cp.wait()

    return scatter(rows, dest)


def _final_kernel(dm, yg_ref, gate_ref, xm_ref, mod_ref, fg_ref, o_ref):
    gates = gate_ref[0]
    f_lo = f_hi = None
    for k in range(TOP_K):
        y_lo, y_hi = _unpack2(yg_ref[k, 0])
        gk = gates[:, k:k + 1]
        f_lo = gk * y_lo if f_lo is None else f_lo + gk * y_lo
        f_hi = gk * y_hi if f_hi is None else f_hi + gk * y_hi
    f = jnp.concatenate([f_lo, f_hi], axis=1)
    g2 = mod_ref[0, :, 5 * dm:6 * dm]
    xo = xm_ref[0] + g2 * f
    ms = jnp.mean(xo * xo, axis=-1, keepdims=True)
    o_ref[0] = xo * lax.rsqrt(ms + EPS) * fg_ref[...]


def _final(yg, gates, x_mix, mod3, final_g, tm):
    b_, l_, dm = x_mix.shape
    return pl.pallas_call(
        functools.partial(_final_kernel, dm),
        out_shape=jax.ShapeDtypeStruct((b_, l_, dm), F32),
        grid=(b_, l_ // tm),
        in_specs=[pl.BlockSpec((TOP_K, 1, tm, dm // 2), lambda b, i: (0, b, i, 0)),
                  pl.BlockSpec((1, tm, LANES), lambda b, i: (b, i, 0)),
                  pl.BlockSpec((1, tm, dm), lambda b, i: (b, i, 0)),
                  pl.BlockSpec((1, 1, N_MOD * dm), lambda b, i: (b, 0, 0)),
                  pl.BlockSpec((1, dm), lambda b, i: (0, 0))],
        out_specs=pl.BlockSpec((1, tm, dm), lambda b, i: (b, i, 0)),
        compiler_params=_cparams(("parallel", "parallel")),
        name="final",
    )(yg, gates, x_mix, mod3, final_g.reshape(1, dm))


def _rope_tables(l_):
    half = HEAD_DIM // 2
    inv_freq = ROPE_THETA ** (-jnp.arange(0, half, 2, dtype=F32) / half)
    t = jnp.arange(l_, dtype=jnp.int32)
    row_id, col_id = t // GRID_W, t % GRID_W
    ang = jnp.concatenate([row_id.astype(F32)[:, None] * inv_freq,
                           col_id.astype(F32)[:, None] * inv_freq], axis=-1)
    cos = jnp.repeat(jnp.cos(ang), 2, axis=-1)
    sin = jnp.repeat(jnp.sin(ang), 2, axis=-1)
    sign = jnp.tile(jnp.array([-1.0, 1.0], F32), HEAD_DIM // 2)
    reps = LANES // HEAD_DIM
    return jnp.tile(cos, (1, reps)), jnp.tile(sin * sign, (1, reps))


def _routing(idx, rank, counts, t_, ne):
    m_ = t_ * TOP_K
    blk = EXPERT_BLOCK
    nb = (m_ + ne * (blk - 1)) // blk
    nb = -(-nb // SLOT_BLOCK_MULTIPLE) * SLOT_BLOCK_MULTIPLE
    blk_counts = (counts + blk - 1) // blk
    blk_ends = jnp.cumsum(blk_counts)
    first_slot = (blk_ends - blk_counts) * blk
    onehot = idx[:, :, None] == jnp.arange(ne, dtype=jnp.int32)[None, None, :]
    dest = rank + jnp.sum(jnp.where(onehot, first_slot[None, None, :], 0), axis=2)
    blocks = jnp.arange(nb, dtype=jnp.int32)
    block_expert = jnp.minimum(jnp.sum(blocks[:, None] >= blk_ends[None, :], axis=1), ne - 1)
    in_expert = blocks * blk - first_slot[block_expert]
    block_valid = jnp.clip(counts[block_expert] - in_expert, 0, blk)
    block_valid = jnp.where(blocks < blk_ends[-1], block_valid, 0)
    return dest.astype(jnp.int32), block_expert.astype(jnp.int32), block_valid.astype(jnp.int32), nb * blk


def kernel(x, c, ctx, c_ctx, w_mod, b_mod, norm1_g, norm2_g, w_in, s5_lam_re, s5_lam_im, s5_log_step,
           s5_b_re, s5_b_im, s5_c_re, s5_c_im, s5_d, w_glu, b_glu, w_ssm_out, q_norm_g, k_norm_g,
           w_attn_out, w_out, router_w, router_b, w_gate_up, b_gate_up, w_down, b_down, final_norm_g):
    b_, l_, dm = x.shape
    lc = ctx.shape[1]
    depth = w_mod.shape[0]
    assert depth == 1, "single-layer block"
    assert b_ <= 7 and l_ % 512 == 0 and lc % S5_CHUNK == 0
    wu = dm // 2
    wkv = wu // GQA_GROUP
    kvh = wkv // HEAD_DIM
    ne = router_w.shape[-1]
    i = 0

    c8 = jnp.zeros((8, dm), F32).at[:b_].set(c).at[b_].set(c_ctx)
    mod3 = _modulation(c8, w_mod[i], b_mod[i]).reshape(8, 1, N_MOD * dm)

    w_in_b = w_in[i].astype(BF16)
    cos, sin = _rope_tables(l_)
    hsum = jnp.kron(jnp.eye(wu // HEAD_DIM, dtype=F32), jnp.ones((HEAD_DIM, HEAD_DIM), F32)).astype(BF16)
    qg = jnp.tile(q_norm_g[i], wu // HEAD_DIM).reshape(1, wu)
    kg = jnp.tile(k_norm_g[i], wkv // HEAD_DIM).reshape(1, wkv)
    u, k, vt, qt, glog = _inproj(x, mod3, None, norm1_g[i], w_in_b, cos, sin, qg, kg, hsum, True, 512)
    ones_t = jnp.ones((lc, LANES), F32)
    u_c, k_c, vt_c = _inproj(ctx, mod3, b_, norm1_g[i], w_in_b[:, :wu + 2 * wkv], ones_t, 0.0 * ones_t,
                             qg, kg, hsum, False, lc)

    ops = _s5_operators(s5_lam_re[i], s5_lam_im[i], s5_log_step[i], s5_b_re[i], s5_b_im[i],
                        s5_c_re[i], s5_c_im[i], s5_d[i])
    y_ssm = _s5_mixer(u, u_c, ops)

    s_all = lc + l_
    tk = max(t for t in range(LANES, ATTN_KEY_CHUNK + 1, LANES) if s_all % t == 0)
    nkv = s_all // tk
    k4 = jnp.concatenate([k_c, k], axis=1).reshape(b_, nkv, tk, wkv)
    vt_all = jnp.concatenate([vt_c, vt], axis=2).reshape(b_, kvh, HEAD_DIM, s_all)
    vx = jnp.concatenate([vt_all, jnp.ones((b_, kvh, 1, s_all), BF16),
                          jnp.zeros((b_, kvh, ATTN_V_ROWS - HEAD_DIM - 1, s_all), BF16)], axis=2)
    vx = vx.reshape(b_, kvh, ATTN_V_ROWS, nkv, tk).transpose(0, 1, 3, 2, 4)
    o_attn = _attention(qt, k4, vx, ATTN_Q_BLOCK)

    rw_pad = jnp.zeros((dm, LANES), F32).at[:, :ne].set(router_w[i])
    rb_pad = jnp.full((1, LANES), -jnp.inf, F32).at[0, :ne].set(router_b[i])
    x_mix, h2, idx, gates, rank, counts = _merge(
        y_ssm, o_attn, glog, x, mod3, w_glu[i].astype(BF16), b_glu[i].reshape(1, wu),
        w_ssm_out[i].astype(BF16), w_attn_out[i].astype(BF16), w_out[i].astype(BF16),
        norm2_g[i].reshape(1, dm), rw_pad, rb_pad, ne, 512)

    t_ = b_ * l_
    dest, block_expert, block_valid, n_slots = _routing(
        idx[..., :TOP_K].reshape(t_, TOP_K), rank[..., :TOP_K].reshape(t_, TOP_K),
        counts[0, :ne].astype(jnp.int32), t_, ne)
    dest_kmajor = dest.T.reshape(-1)
    xb = _sc_scatter_rows(h2.reshape(t_, dm // 2), dest_kmajor, n_slots)
    yb = _experts(xb, block_expert, block_valid, w_gate_up[i], b_gate_up[i], w_down[i], b_down[i])
    yg = _sc_gather_rows(yb, dest_kmajor).reshape(TOP_K, b_, l_, dm // 2)
    return _final(yg, gates, x_mix, mod3, final_norm_g, 512)
```

```python
import functools
import math

import jax
import jax.numpy as jnp
from jax import lax
from jax.experimental import pallas as pl
from jax.experimental.pallas import tpu as pltpu
from jax.experimental.pallas import tpu_sc as plsc

F32 = jnp.float32
BF16 = jnp.bfloat16

EPS = 1e-6
GRID_W = 64
N_MOD = 6
SSM_GROUP = 16
SSM_STATE = 64
HEAD_DIM = 64
GQA_GROUP = 4
ROPE_THETA = 10000.0
TOP_K = 4
SWIGLU_ALPHA = 1.702
SWIGLU_LIMIT = 7.0
LOG2E = 1.4426950408889634

LANES = 128
S5_CHUNK = 32
S5_POW_ROWS = 56
EXPERT_BLOCK = 512
ATTN_V_ROWS = 80
ATTN_KEY_CHUNK = 384
ATTN_Q_BLOCK = 256
ATTN_STEPS_PER_TRIP = 10
SC_GATHER_WINDOW = 128
SC_WORKERS = 32
SLOT_BLOCK_MULTIPLE = SC_GATHER_WINDOW * SC_WORKERS // EXPERT_BLOCK
VMEM_LIMIT = 56 * 1024 * 1024


def _cparams(sem, flags=None):
    return pltpu.CompilerParams(dimension_semantics=sem, vmem_limit_bytes=VMEM_LIMIT, flags=flags)


def _dot(a, b):
    return jnp.dot(a, b, preferred_element_type=F32)


def _split(a):
    hi = a.astype(BF16)
    lo = (a - hi.astype(F32)).astype(BF16)
    return hi, lo


def _pack2(lo, hi):
    lo_w = lax.bitcast_convert_type(lo.astype(BF16).astype(F32), jnp.uint32) >> 16
    hi_w = lax.bitcast_convert_type(hi.astype(BF16).astype(F32), jnp.uint32) & jnp.uint32(0xFFFF0000)
    return lo_w | hi_w


def _unpack2(words):
    lo = lax.bitcast_convert_type(words << 16, F32)
    hi = lax.bitcast_convert_type(words & jnp.uint32(0xFFFF0000), F32)
    return lo, hi


def _dot3(a, b):
    ah, al = _split(a)
    bh, bl = _split(b)
    return _dot(ah, bh) + _dot(ah, bl) + _dot(al, bh)


def _mod_kernel(c_ref, w_ref, b_ref, o_ref):
    c = c_ref[...]
    s = c * jax.nn.sigmoid(c)
    o_ref[...] = _dot3(s, w_ref[...]) + b_ref[...]


def _modulation(c8, w_mod, b_mod):
    d = c8.shape[1]
    n = w_mod.shape[1]
    tn = 1024
    return pl.pallas_call(
        _mod_kernel,
        out_shape=jax.ShapeDtypeStruct((8, n), F32),
        grid=(n // tn,),
        in_specs=[pl.BlockSpec((8, d), lambda j: (0, 0)),
                  pl.BlockSpec((d, tn), lambda j: (0, j)),
                  pl.BlockSpec((1, tn), lambda j: (0, j))],
        out_specs=pl.BlockSpec((8, tn), lambda j: (0, j)),
        compiler_params=_cparams(("parallel",)),
        name="mod",
    )(c8, w_mod, b_mod.reshape(1, n))


def _head_norm_rope(z, gain, hsum, cos, sin, scale):
    ss = _dot((z * z).astype(BF16), hsum)
    zn = z * lax.rsqrt(ss * (1.0 / HEAD_DIM) + EPS) * gain
    lane = lax.broadcasted_iota(jnp.int32, (1, LANES), 1)
    even = (lane & 1) == 0
    outs = []
    for ci in range(z.shape[1] // LANES):
        ch = zn[:, ci * LANES:(ci + 1) * LANES]
        nxt = pltpu.roll(ch, LANES - 1, axis=1)
        prv = pltpu.roll(ch, 1, axis=1)
        sw = jnp.where(even, nxt, prv)
        outs.append((ch * cos + sw * sin) * scale)
    return outs


def _inproj_kernel(with_q, dm, x_ref, mod_ref, g_ref, w_ref, cos_ref, sin_ref, qg_ref, kg_ref,
                   hsum_ref, *out_refs):
    x = x_ref[0]
    ms = jnp.mean(x * x, axis=-1, keepdims=True)
    y = x * lax.rsqrt(ms + EPS) * g_ref[...]
    sh = mod_ref[0, :, 0:dm]
    sc = mod_ref[0, :, dm:2 * dm]
    hb = (y * (1.0 + sc) + sh).astype(BF16)
    cos = cos_ref[...]
    sin = sin_ref[...]
    wu = dm // 2
    wkv = wu // GQA_GROUP
    c_k, c_v, c_q, c_g = wu, wu + wkv, wu + 2 * wkv, 2 * wu + 2 * wkv
    u_ref, k_ref, vt_ref = out_refs[:3]
    u_ref[0] = _dot(hb, w_ref[:, 0:c_k])
    kz = _dot(hb, w_ref[:, c_k:c_v])
    (kr,) = _head_norm_rope(kz, kg_ref[...], hsum_ref[0:wkv, 0:wkv], cos, sin, 1.0)
    k_ref[0] = kr.astype(BF16)
    vt_ref[0] = _dot(hb, w_ref[:, c_v:c_q]).T.astype(BF16)
    if with_q:
        qt_ref, gl_ref = out_refs[3:]
        qz = _dot(hb, w_ref[:, c_q:c_g])
        qs = _head_norm_rope(qz, qg_ref[...], hsum_ref[...], cos, sin, HEAD_DIM ** -0.5 * LOG2E)
        for ci, qc in enumerate(qs):
            qt_ref[0, ci * LANES:(ci + 1) * LANES, :] = qc.T.astype(BF16)
        gl_ref[0] = _dot(hb, w_ref[:, c_g:]).astype(BF16)


def _inproj(x, mod3, mod_row0, norm_g, w_in_b, cos, sin, qg, kg, hsum, with_q, tm):
    b_, l_, dm = x.shape
    wu = dm // 2
    wkv = wu // GQA_GROUP
    n_in = w_in_b.shape[1]
    outs = [jax.ShapeDtypeStruct((b_, l_, wu), F32),
            jax.ShapeDtypeStruct((b_, l_, wkv), BF16),
            jax.ShapeDtypeStruct((b_, wkv, l_), BF16)]
    ospecs = [pl.BlockSpec((1, tm, wu), lambda b, i: (b, i, 0)),
              pl.BlockSpec((1, tm, wkv), lambda b, i: (b, i, 0)),
              pl.BlockSpec((1, wkv, tm), lambda b, i: (b, 0, i))]
    if with_q:
        outs += [jax.ShapeDtypeStruct((b_, wu, l_), BF16),
                 jax.ShapeDtypeStruct((b_, l_, 2 * dm), BF16)]
        ospecs += [pl.BlockSpec((1, wu, tm), lambda b, i: (b, 0, i)),
                   pl.BlockSpec((1, tm, 2 * dm), lambda b, i: (b, i, 0))]
    if mod_row0 is None:
        mod_map = lambda b, i: (b, 0, 0)
    else:
        mod_map = lambda b, i: (mod_row0, 0, 0)
    return pl.pallas_call(
        functools.partial(_inproj_kernel, with_q, dm),
        out_shape=outs,
        grid=(b_, l_ // tm),
        in_specs=[pl.BlockSpec((1, tm, dm), lambda b, i: (b, i, 0)),
                  pl.BlockSpec((1, 1, N_MOD * dm), mod_map),
                  pl.BlockSpec((1, dm), lambda b, i: (0, 0)),
                  pl.BlockSpec((dm, n_in), lambda b, i: (0, 0)),
                  pl.BlockSpec((tm, LANES), lambda b, i: (i, 0)),
                  pl.BlockSpec((tm, LANES), lambda b, i: (i, 0)),
                  pl.BlockSpec((1, wu), lambda b, i: (0, 0)),
                  pl.BlockSpec((1, wkv), lambda b, i: (0, 0)),
                  pl.BlockSpec((wu, wu), lambda b, i: (0, 0))],
        out_specs=ospecs,
        compiler_params=_cparams(("parallel", "parallel")),
        name="inproj_x" if with_q else "inproj_ctx",
    )(x, mod3, norm_g.reshape(1, dm), w_in_b, cos, sin, qg, kg, hsum)


def _cexp_pow(e, lr, li):
    mag = jnp.exp(e * lr)
    ang = e * li
    return mag * jnp.cos(ang), mag * jnp.sin(ang)


def _cmul(ar, ai, br, bi):
    return ar * br - ai * bi, ar * bi + ai * br


def _s5pre_kernel(lamr_r, lami_r, lst_r, lamr_c, lami_c, lst_c, btr_ref, bti_ref, ctr_ref, cti_ref,
                  d_ref, win_ref, wout_ref, m_ref, dec_ref):
    tc = S5_CHUNK
    half = tc // 2
    p = SSM_GROUP
    rows = tc * p
    gsh = SSM_GROUP.bit_length() - 1
    nsh = SSM_STATE.bit_length() - 1
    row_i = lax.broadcasted_iota(jnp.int32, (rows, rows), 0)
    col_i = lax.broadcasted_iota(jnp.int32, (rows, rows), 1)
    lane_g = lax.broadcasted_iota(jnp.int32, (1, LANES), 1) >> nsh
    subl_g = lax.broadcasted_iota(jnp.int32, (LANES, 1), 0) >> nsh
    e_rows = lax.broadcasted_iota(jnp.int32, (S5_POW_ROWS, 1), 0) - half
    e_lanes = lax.broadcasted_iota(jnp.int32, (1, LANES), 1) - half
    sel_e = lax.broadcasted_iota(jnp.int32, (LANES, rows), 0) - half
    sel_t = lax.broadcasted_iota(jnp.int32, (LANES, rows), 1) >> gsh
    sel_p = (lax.broadcasted_iota(jnp.int32, (LANES, rows), 0)
             == (lax.broadcasted_iota(jnp.int32, (LANES, rows), 1) & (p - 1))).astype(BF16)

    def spread_lanes(table, sel):
        hi, lo = _split(table)
        return _dot(hi, sel) + _dot(lo, sel)

    def spread_rows(table, e_of_s):
        return jnp.concatenate([jnp.broadcast_to(table[e_of_s(s) + half:e_of_s(s) + half + 1, :], (p, LANES))
                                for s in range(tc)], axis=0)

    m_acc = [jnp.where(row_i == col_i, d_ref[0, gl], 0.0) for gl in range(2)]
    for dr in range(2):
        step = jnp.exp(lst_r[dr, 0])
        lam_r, lam_i = lamr_r[dr, 0], lami_r[dr, 0]
        lr, li = lam_r * step, lam_i * step
        pw_r, pw_i = _cexp_pow(e_rows.astype(F32), lr, li)
        lb_r, lb_i = pw_r[half + 1:half + 2, :], pw_i[half + 1:half + 2, :]
        den = lam_r * lam_r + lam_i * lam_i
        nr, ni = lb_r - 1.0, lb_i
        cf_r = (nr * lam_r + ni * lam_i) / den
        cf_i = (ni * lam_r - nr * lam_i) / den
        bt_r = jnp.concatenate([btr_ref[dr, 0]] * tc, axis=0)
        bt_i = jnp.concatenate([bti_ref[dr, 0]] * tc, axis=0)
        bb_r, bb_i = _cmul(cf_r, cf_i, bt_r, bt_i)
        e_in = (lambda s: tc - 1 - s) if dr == 0 else (lambda s: s)
        wi_r, wi_i = _cmul(bb_r, bb_i, spread_rows(pw_r, e_in), spread_rows(pw_i, e_in))
        e_a = (lambda s: half - s) if dr == 0 else (lambda s: s - half)
        a_r, a_i = _cmul(bb_r, bb_i, spread_rows(pw_r, e_a), spread_rows(pw_i, e_a))
        dec_ref[0, dr, 0] = pw_r[tc + half:tc + half + 1, :]
        dec_ref[0, dr, 1] = pw_i[tc + half:tc + half + 1, :]
        stepc = jnp.exp(lst_c[dr, 0])
        lrc, lic = lamr_c[dr, 0] * stepc, lami_c[dr, 0] * stepc
        qw_r, qw_i = _cexp_pow(e_lanes.astype(F32), lrc, lic)
        qw_r = jnp.where(e_lanes <= tc, qw_r, 0.0)
        qw_i = jnp.where(e_lanes <= tc, qw_i, 0.0)
        c_r, c_i = spread_lanes(ctr_ref[dr, 0], sel_p), spread_lanes(cti_ref[dr, 0], sel_p)
        sel_b = (sel_e == (sel_t - half if dr == 0 else half - sel_t)).astype(BF16)
        bm_r, bm_i = _cmul(c_r, c_i, spread_lanes(qw_r, sel_b), spread_lanes(qw_i, sel_b))
        sel_o = (sel_e == (sel_t + 1 if dr == 0 else tc - sel_t)).astype(BF16)
        wo_r, wo_i = _cmul(c_r, c_i, spread_lanes(qw_r, sel_o), spread_lanes(qw_i, sel_o))
        if dr == 0:
            mask = (col_i >> gsh) >= (row_i >> gsh)
        else:
            mask = (row_i >> gsh) >= (col_i >> gsh)
        for gl in range(2):
            lsel = lane_g == gl
            ssel = subl_g == gl
            win_ref[0, gl, 2 * dr] = jnp.where(lsel, wi_r, 0.0).astype(BF16)
            win_ref[0, gl, 2 * dr + 1] = jnp.where(lsel, wi_i, 0.0).astype(BF16)
            wout_ref[0, 2 * dr, :, gl * rows:(gl + 1) * rows] = jnp.where(ssel, wo_r, 0.0).astype(BF16)
            wout_ref[0, 2 * dr + 1, :, gl * rows:(gl + 1) * rows] = jnp.where(ssel, -wo_i, 0.0).astype(BF16)
            kmat = (_dot3(jnp.where(lsel, a_r, 0.0), bm_r) - _dot3(jnp.where(lsel, a_i, 0.0), bm_i))
            m_acc[gl] = m_acc[gl] + jnp.where(mask, kmat, 0.0)
    for gl in range(2):
        m_ref[0, gl] = m_acc[gl].astype(BF16)


def _s5_operators(lam_re, lam_im, log_step, b_re, b_im, c_re, c_im, d_skip):
    g = lam_re.shape[1]
    n, p = SSM_STATE, SSM_GROUP
    npair = g // 2
    tc = S5_CHUNK
    rows = tc * p

    def row_form(a):
        return a.reshape(2, npair, 1, 2 * n)

    def col_form(a):
        return a.reshape(2, npair, 2 * n, 1)

    lst = jnp.broadcast_to(log_step[:, :, None], (2, g, n))
    bt = lambda b: b.reshape(2, npair, 2, n, p).transpose(0, 1, 4, 2, 3).reshape(2, npair, p, 2 * n)
    ct = lambda c: jnp.pad(c.reshape(2, npair, 2, p, n).transpose(0, 1, 2, 4, 3).reshape(2, npair, 2 * n, p),
                           ((0, 0), (0, 0), (0, 0), (0, LANES - p)))
    d2 = jnp.broadcast_to(d_skip.reshape(npair, 2, 1, 1, p), (npair, 2, 1, tc, p)).reshape(npair, 2, 1, rows)

    def spec4(shape):
        return pl.BlockSpec((2, 1) + shape, lambda i: (0, i, 0, 0))

    return pl.pallas_call(
        _s5pre_kernel,
        out_shape=[jax.ShapeDtypeStruct((npair, 2, 4, rows, LANES), BF16),
                   jax.ShapeDtypeStruct((npair, 4, LANES, 2 * rows), BF16),
                   jax.ShapeDtypeStruct((npair, 2, rows, rows), BF16),
                   jax.ShapeDtypeStruct((npair, 2, 2, 1, LANES), F32)],
        grid=(npair,),
        in_specs=[spec4((1, LANES)), spec4((1, LANES)), spec4((1, LANES)),
                  spec4((LANES, 1)), spec4((LANES, 1)), spec4((LANES, 1)),
                  spec4((p, LANES)), spec4((p, LANES)),
                  spec4((LANES, LANES)), spec4((LANES, LANES)),
                  pl.BlockSpec((1, 2, 1, rows), lambda i: (i, 0, 0, 0))],
        out_specs=[pl.BlockSpec((1, 2, 4, rows, LANES), lambda i: (i, 0, 0, 0, 0)),
                   pl.BlockSpec((1, 4, LANES, 2 * rows), lambda i: (i, 0, 0, 0)),
                   pl.BlockSpec((1, 2, rows, rows), lambda i: (i, 0, 0, 0)),
                   pl.BlockSpec((1, 2, 2, 1, LANES), lambda i: (i, 0, 0, 0, 0))],
        compiler_params=_cparams(("parallel",)),
        name="s5pre",
    )(row_form(lam_re), row_form(lam_im), row_form(lst),
      col_form(lam_re), col_form(lam_im), col_form(lst),
      bt(b_re), bt(b_im), ct(c_re), ct(c_im), d2)


def _s5state_kernel(u_ref, win_ref, *s_refs):
    u0 = u_ref[0, 0]
    u1 = u_ref[1, 0]
    for kind in range(4):
        s_refs[kind][...] = _dot(u0, win_ref[0, 0, kind]) + _dot(u1, win_ref[0, 1, kind])


def _s5_states(uf, win):
    g, b_, j, rows = uf.shape
    npair = g // 2
    out = jax.ShapeDtypeStruct((j, b_ * npair * LANES), F32)
    ospec = pl.BlockSpec((j, LANES), lambda pr, b: (0, b * npair + pr))
    return pl.pallas_call(
        _s5state_kernel,
        out_shape=[out] * 4,
        grid=(npair, b_),
        in_specs=[pl.BlockSpec((2, 1, j, rows), lambda pr, b: (pr, b, 0, 0)),
                  pl.BlockSpec((1, 2, 4, rows, LANES), lambda pr, b: (pr, 0, 0, 0, 0))],
        out_specs=[ospec] * 4,
        compiler_params=_cparams(("parallel", "parallel")),
        name="s5state",
    )(uf, win)


def _s5scan_kernel(cfr, cfi, cbr, cbi, sfr, sfi, sbr, sbi, dec_ref, hfr, hfi, hbr, hbi):
    jc, jl, w = cfr.shape[0], sfr.shape[0], sfr.shape[1]
    zero = jnp.zeros((1, w), F32)

    def run(sr, si, outs, ar, ai, n, reverse, carry):
        def body(i, hc):
            row = (n - 1 - i) if reverse else i
            hr, hi = hc
            if outs is not None:
                outs[0][pl.ds(row, 1), :] = hr
                outs[1][pl.ds(row, 1), :] = hi
            nr = ar * hr - ai * hi + sr[pl.ds(row, 1), :]
            ni = ar * hi + ai * hr + si[pl.ds(row, 1), :]
            return nr, ni
        return lax.fori_loop(0, n, body, carry)

    afr, afi = dec_ref[0, 0], dec_ref[0, 1]
    abr, abi = dec_ref[1, 0], dec_ref[1, 1]
    c = run(cfr, cfi, None, afr, afi, jc, False, (zero, zero))
    run(sfr, sfi, (hfr, hfi), afr, afi, jl, False, c)
    c = run(cbr, cbi, None, abr, abi, jc, True, (zero, zero))
    run(sbr, sbi, (hbr, hbi), abr, abi, jl, True, c)


def _s5_scan(ctx_states, states, dec_cols):
    jl, wtot = states[0].shape
    jc = ctx_states[0].shape[0]
    tw = 1024
    spec = pl.BlockSpec((jl, tw), lambda i: (0, i))
    cspec = pl.BlockSpec((jc, tw), lambda i: (0, i))
    return pl.pallas_call(
        _s5scan_kernel,
        out_shape=[jax.ShapeDtypeStruct((jl, wtot), F32)] * 4,
        grid=(wtot // tw,),
        in_specs=[cspec] * 4 + [spec] * 4 + [pl.BlockSpec((2, 2, 1, tw), lambda i: (0, 0, 0, i))],
        out_specs=[spec] * 4,
        compiler_params=_cparams(("parallel",)),
        name="s5scan",
    )(*ctx_states, *states, dec_cols)


def _s5out_kernel(u_ref, hfr, hfi, hbr, hbi, wout_ref, m_ref, y_ref):
    rows = u_ref.shape[3]
    y2 = None
    for kind, h in enumerate((hfr, hfi, hbr, hbi)):
        t = _dot(h[...].astype(BF16), wout_ref[0, kind])
        y2 = t if y2 is None else y2 + t
    for gl in range(2):
        y = y2[:, gl * rows:(gl + 1) * rows] + _dot(u_ref[gl, 0], m_ref[0, gl])
        y_ref[gl, 0] = y.astype(BF16)


def _s5_outputs(uf, hprev, wout, m):
    g, b_, j, rows = uf.shape
    npair = g // 2
    hspec = pl.BlockSpec((j, LANES), lambda pr, b: (0, b * npair + pr))
    return pl.pallas_call(
        _s5out_kernel,
        out_shape=jax.ShapeDtypeStruct((g, b_, j, rows), BF16),
        grid=(npair, b_),
        in_specs=[pl.BlockSpec((2, 1, j, rows), lambda pr, b: (pr, b, 0, 0))] + [hspec] * 4 +
                 [pl.BlockSpec((1, 4, LANES, 2 * rows), lambda pr, b: (pr, 0, 0, 0)),
                  pl.BlockSpec((1, 2, rows, rows), lambda pr, b: (pr, 0, 0, 0))],
        out_specs=pl.BlockSpec((2, 1, j, rows), lambda pr, b: (pr, b, 0, 0)),
        compiler_params=_cparams(("parallel", "parallel")),
        name="s5out",
    )(uf, *hprev, wout, m)


def _chunks_per_step(j):
    return max(n for n in (64, 32, 16, 8) if j % n == 0)


def _s5_regroup_kernel(nj, u_ref, o_ref):
    tc, p = S5_CHUNK, SSM_GROUP
    for s in range(tc):
        xs = u_ref[0, pl.ds(s, nj, stride=tc), :]
        for gi in range(LANES // p):
            o_ref[gi, 0, :, s * p:(s + 1) * p] = xs[:, gi * p:(gi + 1) * p].astype(o_ref.dtype)


def _s5_regroup(u):
    b_, l_, w = u.shape
    tc, p = S5_CHUNK, SSM_GROUP
    gpt = LANES // p
    j = l_ // tc
    nj = _chunks_per_step(j)
    return pl.pallas_call(
        functools.partial(_s5_regroup_kernel, nj),
        out_shape=jax.ShapeDtypeStruct((w // p, b_, j, tc * p), BF16),
        grid=(b_, j // nj, w // LANES),
        in_specs=[pl.BlockSpec((1, tc * nj, LANES), lambda b, i, c: (b, i, c))],
        out_specs=pl.BlockSpec((gpt, 1, nj, tc * p), lambda b, i, c: (c, b, i, 0)),
        compiler_params=_cparams(("parallel", "parallel", "parallel")),
        name="s5regroup",
    )(u)


def _s5_ungroup_kernel(nj, y_ref, o_ref):
    tc, p = S5_CHUNK, SSM_GROUP
    for t in range(tc):
        row = jnp.concatenate([y_ref[gi, 0, :, t * p:(t + 1) * p].astype(F32) for gi in range(LANES // p)], axis=1)
        o_ref[0, pl.ds(t, nj, stride=tc), :] = row


def _s5_ungroup(yf):
    g, b_, j, rows = yf.shape
    tc, p = S5_CHUNK, SSM_GROUP
    gpt = LANES // p
    nj = _chunks_per_step(j)
    return pl.pallas_call(
        functools.partial(_s5_ungroup_kernel, nj),
        out_shape=jax.ShapeDtypeStruct((b_, j * tc, g * p), F32),
        grid=(b_, j // nj, g // gpt),
        in_specs=[pl.BlockSpec((gpt, 1, nj, rows), lambda b, i, c: (c, b, i, 0))],
        out_specs=pl.BlockSpec((1, tc * nj, LANES), lambda b, i, c: (b, i, c)),
        compiler_params=_cparams(("parallel", "parallel", "parallel")),
        name="s5ungroup",
    )(yf)


def _s5_mixer(u, u_c, ops):
    win, wout, m, dec = ops
    b_, l_, w = u.shape
    npair = w // SSM_GROUP // 2
    uf = _s5_regroup(u)
    states = _s5_states(uf, win)
    ctx_states = _s5_states(_s5_regroup(u_c), win)
    dec_cols = jnp.broadcast_to(dec.transpose(1, 2, 3, 0, 4)[:, :, :, None], (2, 2, 1, b_, npair, LANES))
    dec_cols = dec_cols.reshape(2, 2, 1, b_ * npair * LANES)
    hprev = _s5_scan(ctx_states, states, dec_cols)
    return _s5_ungroup(_s5_outputs(uf, hprev, wout, m))


def _attn_kernel(nkv, qt_ref, k_ref, vx_ref, o_ref, s0_ref, s1_ref):
    hd = HEAD_DIM
    kvh = pl.program_id(1)
    tq = qt_ref.shape[2]
    gq = qt_ref.shape[1] // hd
    kw = k_ref.shape[3]
    vr = vx_ref.shape[3]
    own = (lax.broadcasted_iota(jnp.int32, (kw, tq), 0) >> (hd.bit_length() - 1)) == kvh
    qp = []
    for g in range(gq):
        qg = qt_ref[0, g * hd:(g + 1) * hd, :].astype(F32)
        q2 = jnp.concatenate([qg] * (kw // hd), axis=0)
        qp.append(jnp.where(own, q2, 0.0).astype(BF16))
    qp = jnp.concatenate(qp, axis=1)
    m0 = jnp.full((1, gq * tq), -jnp.inf, F32)
    acc0 = jnp.zeros((vr, gq * tq), F32)

    def scores(jb, s_ref):
        s = _dot(k_ref[0, jb], qp)
        s_ref[...] = s
        return jnp.max(s, axis=0, keepdims=True)

    def consume(jb, s_ref, cmax, m_prev, acc):
        m_new = jnp.maximum(m_prev, cmax)
        alpha = jnp.exp2(m_prev - m_new)
        p = jnp.exp2(s_ref[...] - m_new).astype(BF16)
        return m_new, alpha * acc + _dot(vx_ref[0, 0, jb], p)

    bufs = (s0_ref, s1_ref)

    def steps(j0, count, m, acc, cmax):
        for u in range(count):
            cnext = scores(j0 + u + 1, bufs[(u + 1) % 2])
            m, acc = consume(j0 + u, bufs[u % 2], cmax, m, acc)
            cmax = cnext
        return m, acc, cmax

    unroll = ATTN_STEPS_PER_TRIP
    trips = (nkv - 1) // unroll
    carry = lax.fori_loop(0, trips, lambda i, c: steps(unroll * i, unroll, *c), (m0, acc0, scores(0, s0_ref)))
    m, acc, cmax = steps(trips * unroll, (nkv - 1) - trips * unroll, *carry)
    m, acc = consume(nkv - 1, bufs[(nkv - 1) % 2], cmax, m, acc)
    o = acc[0:hd] / acc[hd:hd + 1]
    ot = jnp.concatenate([o[:, g * tq:(g + 1) * tq] for g in range(gq)], axis=0)
    o_ref[0] = ot.T.astype(BF16)


def _attention(qt, k4, vx, tq):
    b_, wq, l_ = qt.shape
    _, kvh, nkv, vr, tk = vx.shape
    kw = k4.shape[3]
    gw = wq // kvh
    return pl.pallas_call(
        functools.partial(_attn_kernel, nkv),
        out_shape=jax.ShapeDtypeStruct((b_, l_, wq), BF16),
        grid=(b_, kvh, l_ // tq),
        in_specs=[pl.BlockSpec((1, gw, tq), lambda b, h, i: (b, h, i)),
                  pl.BlockSpec((1, nkv, tk, kw), lambda b, h, i: (b, 0, 0, 0)),
                  pl.BlockSpec((1, 1, nkv, vr, tk), lambda b, h, i: (b, h, 0, 0, 0))],
        out_specs=pl.BlockSpec((1, tq, gw), lambda b, h, i: (b, i, h)),
        scratch_shapes=[pltpu.VMEM((tk, (gw // HEAD_DIM) * tq), F32)] * 2,
        compiler_params=_cparams(("parallel", "parallel", "parallel")),
        name="attn",
    )(qt, k4, vx)


def _merge_kernel(dm, ne, ys_ref, oa_ref, gl_ref, x_ref, mod_ref, wglu_ref, bglu_ref, wso_ref, wao_ref,
                  wout_ref, n2g_ref, rw_ref, rb_ref, ltri_ref, xmix_ref, h2_ref, idx_ref, gate_ref, rank_ref,
                  cnt_out_ref, cnt_ref):
    s = jax.nn.gelu(ys_ref[0].astype(F32))
    s = s * jax.nn.sigmoid(_dot(s.astype(BF16), wglu_ref[...]) + bglu_ref[...])
    a = _dot(s.astype(BF16), wso_ref[...])
    bq = _dot(oa_ref[0], wao_ref[...])
    gts = jax.nn.sigmoid(gl_ref[0].astype(F32))
    merged = gts[:, 0:dm] * a + gts[:, dm:2 * dm] * bq
    out = _dot(merged.astype(BF16), wout_ref[...])
    g1 = mod_ref[0, :, 2 * dm:3 * dm]
    sh2 = mod_ref[0, :, 3 * dm:4 * dm]
    sc2 = mod_ref[0, :, 4 * dm:5 * dm]
    xm = x_ref[0] + g1 * out
    xmix_ref[0] = xm
    ms = jnp.mean(xm * xm, axis=-1, keepdims=True)
    h2 = xm * lax.rsqrt(ms + EPS) * n2g_ref[...] * (1.0 + sc2) + sh2
    h2_ref[0] = _pack2(h2[:, 0:dm // 2], h2[:, dm // 2:dm])
    logits = _dot3(h2, rw_ref[...]) + rb_ref[...]
    lane = lax.broadcasted_iota(jnp.int32, logits.shape, 1).astype(F32)
    neg = jnp.float32(-jnp.inf)
    vals, idxs = [], []
    l = logits
    for _ in range(TOP_K):
        mx = jnp.max(l, axis=-1, keepdims=True)
        ix = jnp.min(jnp.where(l == mx, lane, float(LANES)), axis=-1, keepdims=True)
        vals.append(mx)
        idxs.append(ix)
        l = jnp.where(lane == ix, neg, l)
    es = [jnp.exp(v - vals[0]) for v in vals]
    den = es[0] + es[1] + es[2] + es[3]
    @pl.when((pl.program_id(0) == 0) & (pl.program_id(1) == 0))
    def _():
        cnt_ref[...] = jnp.zeros(cnt_ref.shape, F32)

    onehot = jnp.zeros(logits.shape, F32)
    for k in range(TOP_K):
        onehot = onehot + jnp.where(lane == idxs[k], 1.0, 0.0)
    prior = _dot(ltri_ref[...], onehot.astype(BF16)) + cnt_ref[...]
    cnt_ref[...] = cnt_ref[...] + jnp.sum(onehot, axis=0, keepdims=True)
    cnt_out_ref[...] = cnt_ref[...]
    idx_out = jnp.zeros(logits.shape, F32)
    gate_out = jnp.zeros(logits.shape, F32)
    rank_out = jnp.zeros(logits.shape, F32)
    for k in range(TOP_K):
        rk = jnp.sum(jnp.where(lane == idxs[k], prior, 0.0), axis=-1, keepdims=True)
        idx_out = jnp.where(lane == float(k), idxs[k], idx_out)
        gate_out = jnp.where(lane == float(k), es[k] / den, gate_out)
        rank_out = jnp.where(lane == float(k), rk, rank_out)
    idx_ref[0] = idx_out.astype(jnp.int32)
    gate_ref[0] = gate_out
    rank_ref[0] = rank_out.astype(jnp.int32)


def _merge(y_ssm, o_attn, glog, x, mod3, w_glu, b_glu, w_ssm_out, w_attn_out, w_out, norm2_g,
           rw_pad, rb_pad, ne, tm):
    b_, l_, dm = x.shape
    wu = dm // 2
    tok = lambda w: pl.BlockSpec((1, tm, w), lambda b, i: (b, i, 0))
    full = lambda r, c: pl.BlockSpec((r, c), lambda b, i: (0, 0))
    ltri = (jnp.arange(tm)[:, None] > jnp.arange(tm)[None, :]).astype(BF16)
    return pl.pallas_call(
        functools.partial(_merge_kernel, dm, ne),
        out_shape=[jax.ShapeDtypeStruct((b_, l_, dm), F32),
                   jax.ShapeDtypeStruct((b_, l_, dm // 2), jnp.uint32),
                   jax.ShapeDtypeStruct((b_, l_, LANES), jnp.int32),
                   jax.ShapeDtypeStruct((b_, l_, LANES), F32),
                   jax.ShapeDtypeStruct((b_, l_, LANES), jnp.int32),
                   jax.ShapeDtypeStruct((1, LANES), F32)],
        grid=(b_, l_ // tm),
        in_specs=[tok(wu), tok(wu), tok(2 * dm), tok(dm),
                  pl.BlockSpec((1, 1, N_MOD * dm), lambda b, i: (b, 0, 0)),
                  full(wu, wu), full(1, wu), full(wu, dm), full(wu, dm), full(dm, dm),
                  full(1, dm), full(dm, LANES), full(1, LANES), full(tm, tm)],
        out_specs=[tok(dm), tok(dm // 2), tok(LANES), tok(LANES), tok(LANES), full(1, LANES)],
        scratch_shapes=[pltpu.VMEM((1, LANES), F32)],
        compiler_params=_cparams(("arbitrary", "arbitrary")),
        name="merge",
    )(y_ssm, o_attn, glog, x, mod3, w_glu, b_glu, w_ssm_out, w_attn_out, w_out, norm2_g, rw_pad, rb_pad, ltri)


def _expert_kernel(de, be_ref, nv_ref, x_ref, wgu_ref, bgu_ref, wd_ref, bd_ref, y_ref, wgu_b, wd_b):
    i = pl.program_id(0)
    n_valid = nv_ref[i]
    used = n_valid > 0
    new_expert = (i == 0) | (be_ref[i] != be_ref[jnp.maximum(i - 1, 0)])

    @pl.when(used & new_expert)
    def _():
        wgu_b[...] = wgu_ref[0].astype(BF16)
        wd_b[...] = wd_ref[0].astype(BF16)

    @pl.when(used)
    def _():
        row = lax.broadcasted_iota(jnp.int32, (x_ref.shape[0], 1), 0)
        x_lo, x_hi = _unpack2(jnp.where(row < n_valid, x_ref[...], jnp.uint32(0)))
        xb = jnp.concatenate([x_lo, x_hi], axis=1).astype(BF16)
        gu = _dot(xb, wgu_b[...]) + bgu_ref[0]
        gate = jnp.minimum(gu[:, 0:de], SWIGLU_LIMIT)
        up = jnp.clip(gu[:, de:2 * de], -SWIGLU_LIMIT, SWIGLU_LIMIT)
        act = (up + 1.0) * (gate * jax.nn.sigmoid(SWIGLU_ALPHA * gate))
        y = _dot(act.astype(BF16), wd_b[...]) + bd_ref[0]
        half = y.shape[1] // 2
        y_ref[...] = _pack2(y[:, 0:half], y[:, half:])

    @pl.when(jnp.logical_not(used))
    def _():
        y_ref[...] = jnp.zeros(y_ref.shape, y_ref.dtype)


def _experts(xb, block_expert, block_valid, w_gate_up, b_gate_up, w_down, b_down):
    n_slots, dmh = xb.shape
    dm = 2 * dmh
    ne, _, de2 = w_gate_up.shape
    de = de2 // 2
    nb = n_slots // EXPERT_BLOCK
    return pl.pallas_call(
        functools.partial(_expert_kernel, de),
        out_shape=jax.ShapeDtypeStruct((n_slots, dmh), jnp.uint32),
        grid_spec=pltpu.PrefetchScalarGridSpec(
            num_scalar_prefetch=2,
            grid=(nb,),
            in_specs=[pl.BlockSpec((EXPERT_BLOCK, dmh), lambda i, be, nu: (i, 0)),
                      pl.BlockSpec((1, dm, de2), lambda i, be, nu: (be[i], 0, 0)),
                      pl.BlockSpec((1, 1, de2), lambda i, be, nu: (be[i], 0, 0)),
                      pl.BlockSpec((1, de, dm), lambda i, be, nu: (be[i], 0, 0)),
                      pl.BlockSpec((1, 1, dm), lambda i, be, nu: (be[i], 0, 0))],
            out_specs=pl.BlockSpec((EXPERT_BLOCK, dmh), lambda i, be, nu: (i, 0)),
            scratch_shapes=[pltpu.VMEM((dm, de2), BF16), pltpu.VMEM((de, dm), BF16)],
        ),
        compiler_params=_cparams(("arbitrary",)),
        name="expert",
    )(block_expert, block_valid, xb, w_gate_up, b_gate_up.reshape(ne, 1, de2), w_down, b_down.reshape(ne, 1, dm))


def _sc_gather_rows(table, idx):
    n = idx.shape[0]
    w = table.shape[1]
    mesh = plsc.VectorSubcoreMesh(core_axis_name="c", subcore_axis_name="s")
    nc, nw = mesh.num_cores, mesh.num_cores * mesh.num_subcores
    win = SC_GATHER_WINDOW
    assert n % (win * nw) == 0
    per_worker = n // nw

    @functools.partial(
        pl.kernel, out_type=jax.ShapeDtypeStruct((n, w), table.dtype), mesh=mesh,
        scratch_types=[pltpu.VMEM((win,), jnp.int32), pltpu.VMEM((win, w), table.dtype),
                       pltpu.SemaphoreType.DMA])
    def gather(x_hbm, i_hbm, o_hbm, idx_v, rows_v, sem):
        base = (lax.axis_index("s") * nc + lax.axis_index("c")) * per_worker

        @pl.loop(0, per_worker // win)
        def _(j):
            off = base + j * win
            pltpu.sync_copy(i_hbm.at[pl.ds(off, win)], idx_v)
            pltpu.async_copy(x_hbm.at[idx_v], rows_v, sem).wait()
            pltpu.sync_copy(rows_v, o_hbm.at[pl.ds(off, win)])

    return gather(table, idx)


def _sc_scatter_rows(rows, dest, n_out):
    t_, w = rows.shape
    kk = dest.shape[0] // t_
    mesh = plsc.VectorSubcoreMesh(core_axis_name="c", subcore_axis_name="s")
    nc, nw = mesh.num_cores, mesh.num_cores * mesh.num_subcores
    win = SC_GATHER_WINDOW
    assert t_ % (win * nw) == 0
    per_worker = t_ // nw

    @functools.partial(
        pl.kernel, out_type=jax.ShapeDtypeStruct((n_out, w), rows.dtype), mesh=mesh,
        scratch_types=[pltpu.VMEM((win,), jnp.int32)] * kk + [pltpu.VMEM((win, w), rows.dtype),
                                                              pltpu.SemaphoreType.DMA])
    def scatter(x_hbm, d_hbm, o_hbm, *scratch):
        idx_vs, rows_v, sem = scratch[:kk], scratch[kk], scratch[kk + 1]
        base = (lax.axis_index("s") * nc + lax.axis_index("c")) * per_worker

        @pl.loop(0, per_worker // win)
        def _(j):
            off = base + j * win
            pltpu.sync_copy(x_hbm.at[pl.ds(off, win)], rows_v)
            for k in range(kk):
                pltpu.sync_copy(d_hbm.at[pl.ds(k * t_ + off, win)], idx_vs[k])
            copies = [pltpu.async_copy(rows_v, o_hbm.at[idx_vs[k]], sem) for k in range(kk)]
            for cp in copies:
                cp.wait()

    return scatter(rows, dest)


def _final_kernel(dm, yg_ref, gate_ref, xm_ref, mod_ref, fg_ref, o_ref):
    gates = gate_ref[0]
    f_lo = f_hi = None
    for k in range(TOP_K):
        y_lo, y_hi = _unpack2(yg_ref[k, 0])
        gk = gates[:, k:k + 1]
        f_lo = gk * y_lo if f_lo is None else f_lo + gk * y_lo
        f_hi = gk * y_hi if f_hi is None else f_hi + gk * y_hi
    f = jnp.concatenate([f_lo, f_hi], axis=1)
    g2 = mod_ref[0, :, 5 * dm:6 * dm]
    xo = xm_ref[0] + g2 * f
    ms = jnp.mean(xo * xo, axis=-1, keepdims=True)
    o_ref[0] = xo * lax.rsqrt(ms + EPS) * fg_ref[...]


def _final(yg, gates, x_mix, mod3, final_g, tm):
    b_, l_, dm = x_mix.shape
    return pl.pallas_call(
        functools.partial(_final_kernel, dm),
        out_shape=jax.ShapeDtypeStruct((b_, l_, dm), F32),
        grid=(b_, l_ // tm),
        in_specs=[pl.BlockSpec((TOP_K, 1, tm, dm // 2), lambda b, i: (0, b, i, 0)),
                  pl.BlockSpec((1, tm, LANES), lambda b, i: (b, i, 0)),
                  pl.BlockSpec((1, tm, dm), lambda b, i: (b, i, 0)),
                  pl.BlockSpec((1, 1, N_MOD * dm), lambda b, i: (b, 0, 0)),
                  pl.BlockSpec((1, dm), lambda b, i: (0, 0))],
        out_specs=pl.BlockSpec((1, tm, dm), lambda b, i: (b, i, 0)),
        compiler_params=_cparams(("parallel", "parallel")),
        name="final",
    )(yg, gates, x_mix, mod3, final_g.reshape(1, dm))


def _rope_tables(l_):
    half = HEAD_DIM // 2
    inv_freq = ROPE_THETA ** (-jnp.arange(0, half, 2, dtype=F32) / half)
    t = jnp.arange(l_, dtype=jnp.int32)
    row_id, col_id = t // GRID_W, t % GRID_W
    ang = jnp.concatenate([row_id.astype(F32)[:, None] * inv_freq,
                           col_id.astype(F32)[:, None] * inv_freq], axis=-1)
    cos = jnp.repeat(jnp.cos(ang), 2, axis=-1)
    sin = jnp.repeat(jnp.sin(ang), 2, axis=-1)
    sign = jnp.tile(jnp.array([-1.0, 1.0], F32), HEAD_DIM // 2)
    reps = LANES // HEAD_DIM
    return jnp.tile(cos, (1, reps)), jnp.tile(sin * sign, (1, reps))


def _routing(idx, rank, counts, t_, ne):
    m_ = t_ * TOP_K
    blk = EXPERT_BLOCK
    nb = (m_ + ne * (blk - 1)) // blk
    nb = -(-nb // SLOT_BLOCK_MULTIPLE) * SLOT_BLOCK_MULTIPLE
    blk_counts = (counts + blk - 1) // blk
    blk_ends = jnp.cumsum(blk_counts)
    first_slot = (blk_ends - blk_counts) * blk
    onehot = idx[:, :, None] == jnp.arange(ne, dtype=jnp.int32)[None, None, :]
    dest = rank + jnp.sum(jnp.where(onehot, first_slot[None, None, :], 0), axis=2)
    blocks = jnp.arange(nb, dtype=jnp.int32)
    block_expert = jnp.minimum(jnp.sum(blocks[:, None] >= blk_ends[None, :], axis=1), ne - 1)
    in_expert = blocks * blk - first_slot[block_expert]
    block_valid = jnp.clip(counts[block_expert] - in_expert, 0, blk)
    block_valid = jnp.where(blocks < blk_ends[-1], block_valid, 0)
    return dest.astype(jnp.int32), block_expert.astype(jnp.int32), block_valid.astype(jnp.int32), nb * blk


def kernel(x, c, ctx, c_ctx, w_mod, b_mod, norm1_g, norm2_g, w_in, s5_lam_re, s5_lam_im, s5_log_step,
           s5_b_re, s5_b_im, s5_c_re, s5_c_im, s5_d, w_glu, b_glu, w_ssm_out, q_norm_g, k_norm_g,
           w_attn_out, w_out, router_w, router_b, w_gate_up, b_gate_up, w_down, b_down, final_norm_g):
    b_, l_, dm = x.shape
    lc = ctx.shape[1]
    depth = w_mod.shape[0]
    assert depth == 1, "single-layer block"
    assert b_ <= 7 and l_ % 512 == 0 and lc % S5_CHUNK == 0
    wu = dm // 2
    wkv = wu // GQA_GROUP
    kvh = wkv // HEAD_DIM
    ne = router_w.shape[-1]
    i = 0

    c8 = jnp.zeros((8, dm), F32).at[:b_].set(c).at[b_].set(c_ctx)
    mod3 = _modulation(c8, w_mod[i], b_mod[i]).reshape(8, 1, N_MOD * dm)

    w_in_b = w_in[i].astype(BF16)
    cos, sin = _rope_tables(l_)
    hsum = jnp.kron(jnp.eye(wu // HEAD_DIM, dtype=F32), jnp.ones((HEAD_DIM, HEAD_DIM), F32)).astype(BF16)
    qg = jnp.tile(q_norm_g[i], wu // HEAD_DIM).reshape(1, wu)
    kg = jnp.tile(k_norm_g[i], wkv // HEAD_DIM).reshape(1, wkv)
    u, k, vt, qt, glog = _inproj(x, mod3, None, norm1_g[i], w_in_b, cos, sin, qg, kg, hsum, True, 512)
    ones_t = jnp.ones((lc, LANES), F32)
    u_c, k_c, vt_c = _inproj(ctx, mod3, b_, norm1_g[i], w_in_b[:, :wu + 2 * wkv], ones_t, 0.0 * ones_t,
                             qg, kg, hsum, False, lc)

    ops = _s5_operators(s5_lam_re[i], s5_lam_im[i], s5_log_step[i], s5_b_re[i], s5_b_im[i],
                        s5_c_re[i], s5_c_im[i], s5_d[i])
    y_ssm = _s5_mixer(u, u_c, ops)

    s_all = lc + l_
    tk = max(t for t in range(LANES, ATTN_KEY_CHUNK + 1, LANES) if s_all % t == 0)
    nkv = s_all // tk
    k4 = jnp.concatenate([k_c, k], axis=1).reshape(b_, nkv, tk, wkv)
    vt_all = jnp.concatenate([vt_c, vt], axis=2).reshape(b_, kvh, HEAD_DIM, s_all)
    vx = jnp.concatenate([vt_all, jnp.ones((b_, kvh, 1, s_all), BF16),
                          jnp.zeros((b_, kvh, ATTN_V_ROWS - HEAD_DIM - 1, s_all), BF16)], axis=2)
    vx = vx.reshape(b_, kvh, ATTN_V_ROWS, nkv, tk).transpose(0, 1, 3, 2, 4)
    o_attn = _attention(qt, k4, vx, ATTN_Q_BLOCK)

    rw_pad = jnp.zeros((dm, LANES), F32).at[:, :ne].set(router_w[i])
    rb_pad = jnp.full((1, LANES), -jnp.inf, F32).at[0, :ne].set(router_b[i])
    x_mix, h2, idx, gates, rank, counts = _merge(
        y_ssm, o_attn, glog, x, mod3, w_glu[i].astype(BF16), b_glu[i].reshape(1, wu),
        w_ssm_out[i].astype(BF16), w_attn_out[i].astype(BF16), w_out[i].astype(BF16),
        norm2_g[i].reshape(1, dm), rw_pad, rb_pad, ne, 512)

    t_ = b_ * l_
    dest, block_expert, block_valid, n_slots = _routing(
        idx[..., :TOP_K].reshape(t_, TOP_K), rank[..., :TOP_K].reshape(t_, TOP_K),
        counts[0, :ne].astype(jnp.int32), t_, ne)
    dest_kmajor = dest.T.reshape(-1)
    xb = _sc_scatter_rows(h2.reshape(t_, dm // 2), dest_kmajor, n_slots)
    yb = _experts(xb, block_expert, block_valid, w_gate_up[i], b_gate_up[i], w_down[i], b_down[i])
    yg = _sc_gather_rows(yb, dest_kmajor).reshape(TOP_K, b_, l_, dm // 2)
    return _final(yg, gates, x_mix, mod3, final_norm_g, 512)
```

```python
import functools
import math

import jax
import jax.numpy as jnp
from jax import lax
from jax.experimental import pallas as pl
from jax.experimental.pallas import tpu as pltpu
from jax.experimental.pallas import tpu_sc as plsc

F32 = jnp.float32
BF16 = jnp.bfloat16

EPS = 1e-6
GRID_W = 64
N_MOD = 6
SSM_GROUP = 16
SSM_STATE = 64
HEAD_DIM = 64
GQA_GROUP = 4
ROPE_THETA = 10000.0
TOP_K = 4
SWIGLU_ALPHA = 1.702
SWIGLU_LIMIT = 7.0
LOG2E = 1.4426950408889634

LANES = 128
S5_CHUNK = 32
S5_POW_ROWS = 56
EXPERT_BLOCK = 512
ATTN_V_ROWS = 80
ATTN_KEY_CHUNK = 384
ATTN_Q_BLOCK = 256
ATTN_STEPS_PER_TRIP = 10
MERGE_ROW_PARTS = 2
SC_GATHER_WINDOW = 128
SC_WORKERS = 32
SLOT_BLOCK_MULTIPLE = SC_GATHER_WINDOW * SC_WORKERS // EXPERT_BLOCK
VMEM_LIMIT = 56 * 1024 * 1024


def _cparams(sem, flags=None):
    return pltpu.CompilerParams(dimension_semantics=sem, vmem_limit_bytes=VMEM_LIMIT, flags=flags)


def _dot(a, b):
    return jnp.dot(a, b, preferred_element_type=F32)


def _split(a):
    hi = a.astype(BF16)
    lo = (a - hi.astype(F32)).astype(BF16)
    return hi, lo


def _pack2(lo, hi):
    lo_w = lax.bitcast_convert_type(lo.astype(BF16).astype(F32), jnp.uint32) >> 16
    hi_w = lax.bitcast_convert_type(hi.astype(BF16).astype(F32), jnp.uint32) & jnp.uint32(0xFFFF0000)
    return lo_w | hi_w


def _unpack2(words):
    lo = lax.bitcast_convert_type(words << 16, F32)
    hi = lax.bitcast_convert_type(words & jnp.uint32(0xFFFF0000), F32)
    return lo, hi


def _dot3(a, b):
    ah, al = _split(a)
    bh, bl = _split(b)
    return _dot(ah, bh) + _dot(ah, bl) + _dot(al, bh)


def _mod_kernel(c_ref, w_ref, b_ref, o_ref):
    c = c_ref[...]
    s = c * jax.nn.sigmoid(c)
    o_ref[...] = _dot3(s, w_ref[...]) + b_ref[...]


def _modulation(c8, w_mod, b_mod):
    d = c8.shape[1]
    n = w_mod.shape[1]
    tn = 1024
    return pl.pallas_call(
        _mod_kernel,
        out_shape=jax.ShapeDtypeStruct((8, n), F32),
        grid=(n // tn,),
        in_specs=[pl.BlockSpec((8, d), lambda j: (0, 0)),
                  pl.BlockSpec((d, tn), lambda j: (0, j)),
                  pl.BlockSpec((1, tn), lambda j: (0, j))],
        out_specs=pl.BlockSpec((8, tn), lambda j: (0, j)),
        compiler_params=_cparams(("parallel",)),
        name="mod",
    )(c8, w_mod, b_mod.reshape(1, n))


def _head_norm_rope(z, gain, hsum, cos, sin, scale):
    ss = _dot((z * z).astype(BF16), hsum)
    zn = z * lax.rsqrt(ss * (1.0 / HEAD_DIM) + EPS) * gain
    lane = lax.broadcasted_iota(jnp.int32, (1, LANES), 1)
    even = (lane & 1) == 0
    outs = []
    for ci in range(z.shape[1] // LANES):
        ch = zn[:, ci * LANES:(ci + 1) * LANES]
        nxt = pltpu.roll(ch, LANES - 1, axis=1)
        prv = pltpu.roll(ch, 1, axis=1)
        sw = jnp.where(even, nxt, prv)
        outs.append((ch * cos + sw * sin) * scale)
    return outs


def _inproj_kernel(with_q, dm, x_ref, mod_ref, g_ref, w_ref, cos_ref, sin_ref, qg_ref, kg_ref,
                   hsum_ref, *out_refs):
    x = x_ref[0]
    ms = jnp.mean(x * x, axis=-1, keepdims=True)
    y = x * lax.rsqrt(ms + EPS) * g_ref[...]
    sh = mod_ref[0, :, 0:dm]
    sc = mod_ref[0, :, dm:2 * dm]
    hb = (y * (1.0 + sc) + sh).astype(BF16)
    cos = cos_ref[...]
    sin = sin_ref[...]
    wu = dm // 2
    wkv = wu // GQA_GROUP
    c_k, c_v, c_q, c_g = wu, wu + wkv, wu + 2 * wkv, 2 * wu + 2 * wkv
    u_ref, k_ref, vt_ref = out_refs[:3]
    u_ref[0] = _dot(hb, w_ref[:, 0:c_k])
    kz = _dot(hb, w_ref[:, c_k:c_v])
    (kr,) = _head_norm_rope(kz, kg_ref[...], hsum_ref[0:wkv, 0:wkv], cos, sin, 1.0)
    k_ref[0] = kr.astype(BF16)
    vt_ref[0] = _dot(hb, w_ref[:, c_v:c_q]).T.astype(BF16)
    if with_q:
        qt_ref, gl_ref = out_refs[3:]
        qz = _dot(hb, w_ref[:, c_q:c_g])
        qs = _head_norm_rope(qz, qg_ref[...], hsum_ref[...], cos, sin, HEAD_DIM ** -0.5 * LOG2E)
        for ci, qc in enumerate(qs):
            qt_ref[0, ci * LANES:(ci + 1) * LANES, :] = qc.T.astype(BF16)
        gl_ref[0] = _dot(hb, w_ref[:, c_g:]).astype(BF16)


def _inproj(x, mod3, mod_row0, norm_g, w_in_b, cos, sin, qg, kg, hsum, with_q, tm):
    b_, l_, dm = x.shape
    wu = dm // 2
    wkv = wu // GQA_GROUP
    n_in = w_in_b.shape[1]
    outs = [jax.ShapeDtypeStruct((b_, l_, wu), F32),
            jax.ShapeDtypeStruct((b_, l_, wkv), BF16),
            jax.ShapeDtypeStruct((b_, wkv, l_), BF16)]
    ospecs = [pl.BlockSpec((1, tm, wu), lambda b, i: (b, i, 0)),
              pl.BlockSpec((1, tm, wkv), lambda b, i: (b, i, 0)),
              pl.BlockSpec((1, wkv, tm), lambda b, i: (b, 0, i))]
    if with_q:
        outs += [jax.ShapeDtypeStruct((b_, wu, l_), BF16),
                 jax.ShapeDtypeStruct((b_, l_, 2 * dm), BF16)]
        ospecs += [pl.BlockSpec((1, wu, tm), lambda b, i: (b, 0, i)),
                   pl.BlockSpec((1, tm, 2 * dm), lambda b, i: (b, i, 0))]
    if mod_row0 is None:
        mod_map = lambda b, i: (b, 0, 0)
    else:
        mod_map = lambda b, i: (mod_row0, 0, 0)
    return pl.pallas_call(
        functools.partial(_inproj_kernel, with_q, dm),
        out_shape=outs,
        grid=(b_, l_ // tm),
        in_specs=[pl.BlockSpec((1, tm, dm), lambda b, i: (b, i, 0)),
                  pl.BlockSpec((1, 1, N_MOD * dm), mod_map),
                  pl.BlockSpec((1, dm), lambda b, i: (0, 0)),
                  pl.BlockSpec((dm, n_in), lambda b, i: (0, 0)),
                  pl.BlockSpec((tm, LANES), lambda b, i: (i, 0)),
                  pl.BlockSpec((tm, LANES), lambda b, i: (i, 0)),
                  pl.BlockSpec((1, wu), lambda b, i: (0, 0)),
                  pl.BlockSpec((1, wkv), lambda b, i: (0, 0)),
                  pl.BlockSpec((wu, wu), lambda b, i: (0, 0))],
        out_specs=ospecs,
        compiler_params=_cparams(("parallel", "parallel")),
        name="inproj_x" if with_q else "inproj_ctx",
    )(x, mod3, norm_g.reshape(1, dm), w_in_b, cos, sin, qg, kg, hsum)


def _cexp_pow(e, lr, li):
    mag = jnp.exp(e * lr)
    ang = e * li
    return mag * jnp.cos(ang), mag * jnp.sin(ang)


def _cmul(ar, ai, br, bi):
    return ar * br - ai * bi, ar * bi + ai * br


def _s5pre_kernel(lamr_r, lami_r, lst_r, lamr_c, lami_c, lst_c, btr_ref, bti_ref, ctr_ref, cti_ref,
                  d_ref, win_ref, wout_ref, m_ref, dec_ref):
    tc = S5_CHUNK
    half = tc // 2
    p = SSM_GROUP
    rows = tc * p
    gsh = SSM_GROUP.bit_length() - 1
    nsh = SSM_STATE.bit_length() - 1
    row_i = lax.broadcasted_iota(jnp.int32, (rows, rows), 0)
    col_i = lax.broadcasted_iota(jnp.int32, (rows, rows), 1)
    lane_g = lax.broadcasted_iota(jnp.int32, (1, LANES), 1) >> nsh
    subl_g = lax.broadcasted_iota(jnp.int32, (LANES, 1), 0) >> nsh
    e_rows = lax.broadcasted_iota(jnp.int32, (S5_POW_ROWS, 1), 0) - half
    e_lanes = lax.broadcasted_iota(jnp.int32, (1, LANES), 1) - half
    sel_e = lax.broadcasted_iota(jnp.int32, (LANES, rows), 0) - half
    sel_t = lax.broadcasted_iota(jnp.int32, (LANES, rows), 1) >> gsh
    sel_p = (lax.broadcasted_iota(jnp.int32, (LANES, rows), 0)
             == (lax.broadcasted_iota(jnp.int32, (LANES, rows), 1) & (p - 1))).astype(BF16)

    def spread_lanes(table, sel):
        hi, lo = _split(table)
        return _dot(hi, sel) + _dot(lo, sel)

    def spread_rows(table, e_of_s):
        return jnp.concatenate([jnp.broadcast_to(table[e_of_s(s) + half:e_of_s(s) + half + 1, :], (p, LANES))
                                for s in range(tc)], axis=0)

    m_acc = [jnp.where(row_i == col_i, d_ref[0, gl], 0.0) for gl in range(2)]
    for dr in range(2):
        step = jnp.exp(lst_r[dr, 0])
        lam_r, lam_i = lamr_r[dr, 0], lami_r[dr, 0]
        lr, li = lam_r * step, lam_i * step
        pw_r, pw_i = _cexp_pow(e_rows.astype(F32), lr, li)
        lb_r, lb_i = pw_r[half + 1:half + 2, :], pw_i[half + 1:half + 2, :]
        den = lam_r * lam_r + lam_i * lam_i
        nr, ni = lb_r - 1.0, lb_i
        cf_r = (nr * lam_r + ni * lam_i) / den
        cf_i = (ni * lam_r - nr * lam_i) / den
        bt_r = jnp.concatenate([btr_ref[dr, 0]] * tc, axis=0)
        bt_i = jnp.concatenate([bti_ref[dr, 0]] * tc, axis=0)
        bb_r, bb_i = _cmul(cf_r, cf_i, bt_r, bt_i)
        e_in = (lambda s: tc - 1 - s) if dr == 0 else (lambda s: s)
        wi_r, wi_i = _cmul(bb_r, bb_i, spread_rows(pw_r, e_in), spread_rows(pw_i, e_in))
        e_a = (lambda s: half - s) if dr == 0 else (lambda s: s - half)
        a_r, a_i = _cmul(bb_r, bb_i, spread_rows(pw_r, e_a), spread_rows(pw_i, e_a))
        dec_ref[0, dr, 0] = pw_r[tc + half:tc + half + 1, :]
        dec_ref[0, dr, 1] = pw_i[tc + half:tc + half + 1, :]
        stepc = jnp.exp(lst_c[dr, 0])
        lrc, lic = lamr_c[dr, 0] * stepc, lami_c[dr, 0] * stepc
        qw_r, qw_i = _cexp_pow(e_lanes.astype(F32), lrc, lic)
        qw_r = jnp.where(e_lanes <= tc, qw_r, 0.0)
        qw_i = jnp.where(e_lanes <= tc, qw_i, 0.0)
        c_r, c_i = spread_lanes(ctr_ref[dr, 0], sel_p), spread_lanes(cti_ref[dr, 0], sel_p)
        sel_b = (sel_e == (sel_t - half if dr == 0 else half - sel_t)).astype(BF16)
        bm_r, bm_i = _cmul(c_r, c_i, spread_lanes(qw_r, sel_b), spread_lanes(qw_i, sel_b))
        sel_o = (sel_e == (sel_t + 1 if dr == 0 else tc - sel_t)).astype(BF16)
        wo_r, wo_i = _cmul(c_r, c_i, spread_lanes(qw_r, sel_o), spread_lanes(qw_i, sel_o))
        if dr == 0:
            mask = (col_i >> gsh) >= (row_i >> gsh)
        else:
            mask = (row_i >> gsh) >= (col_i >> gsh)
        for gl in range(2):
            lsel = lane_g == gl
            ssel = subl_g == gl
            win_ref[0, gl, 2 * dr] = jnp.where(lsel, wi_r, 0.0).astype(BF16)
            win_ref[0, gl, 2 * dr + 1] = jnp.where(lsel, wi_i, 0.0).astype(BF16)
            wout_ref[0, 2 * dr, :, gl * rows:(gl + 1) * rows] = jnp.where(ssel, wo_r, 0.0).astype(BF16)
            wout_ref[0, 2 * dr + 1, :, gl * rows:(gl + 1) * rows] = jnp.where(ssel, -wo_i, 0.0).astype(BF16)
            kmat = (_dot3(jnp.where(lsel, a_r, 0.0), bm_r) - _dot3(jnp.where(lsel, a_i, 0.0), bm_i))
            m_acc[gl] = m_acc[gl] + jnp.where(mask, kmat, 0.0)
    for gl in range(2):
        m_ref[0, gl] = m_acc[gl].astype(BF16)


def _s5_operators(lam_re, lam_im, log_step, b_re, b_im, c_re, c_im, d_skip):
    g = lam_re.shape[1]
    n, p = SSM_STATE, SSM_GROUP
    npair = g // 2
    tc = S5_CHUNK
    rows = tc * p

    def row_form(a):
        return a.reshape(2, npair, 1, 2 * n)

    def col_form(a):
        return a.reshape(2, npair, 2 * n, 1)

    lst = jnp.broadcast_to(log_step[:, :, None], (2, g, n))
    bt = lambda b: b.reshape(2, npair, 2, n, p).transpose(0, 1, 4, 2, 3).reshape(2, npair, p, 2 * n)
    ct = lambda c: jnp.pad(c.reshape(2, npair, 2, p, n).transpose(0, 1, 2, 4, 3).reshape(2, npair, 2 * n, p),
                           ((0, 0), (0, 0), (0, 0), (0, LANES - p)))
    d2 = jnp.broadcast_to(d_skip.reshape(npair, 2, 1, 1, p), (npair, 2, 1, tc, p)).reshape(npair, 2, 1, rows)

    def spec4(shape):
        return pl.BlockSpec((2, 1) + shape, lambda i: (0, i, 0, 0))

    return pl.pallas_call(
        _s5pre_kernel,
        out_shape=[jax.ShapeDtypeStruct((npair, 2, 4, rows, LANES), BF16),
                   jax.ShapeDtypeStruct((npair, 4, LANES, 2 * rows), BF16),
                   jax.ShapeDtypeStruct((npair, 2, rows, rows), BF16),
                   jax.ShapeDtypeStruct((npair, 2, 2, 1, LANES), F32)],
        grid=(npair,),
        in_specs=[spec4((1, LANES)), spec4((1, LANES)), spec4((1, LANES)),
                  spec4((LANES, 1)), spec4((LANES, 1)), spec4((LANES, 1)),
                  spec4((p, LANES)), spec4((p, LANES)),
                  spec4((LANES, LANES)), spec4((LANES, LANES)),
                  pl.BlockSpec((1, 2, 1, rows), lambda i: (i, 0, 0, 0))],
        out_specs=[pl.BlockSpec((1, 2, 4, rows, LANES), lambda i: (i, 0, 0, 0, 0)),
                   pl.BlockSpec((1, 4, LANES, 2 * rows), lambda i: (i, 0, 0, 0)),
                   pl.BlockSpec((1, 2, rows, rows), lambda i: (i, 0, 0, 0)),
                   pl.BlockSpec((1, 2, 2, 1, LANES), lambda i: (i, 0, 0, 0, 0))],
        compiler_params=_cparams(("parallel",)),
        name="s5pre",
    )(row_form(lam_re), row_form(lam_im), row_form(lst),
      col_form(lam_re), col_form(lam_im), col_form(lst),
      bt(b_re), bt(b_im), ct(c_re), ct(c_im), d2)


def _s5state_kernel(u_ref, win_ref, *s_refs):
    for b in range(u_ref.shape[1]):
        for kind in range(4):
            s_refs[kind][:, b * LANES:(b + 1) * LANES] = (_dot(u_ref[0, b], win_ref[0, 0, kind])
                                                          + _dot(u_ref[1, b], win_ref[0, 1, kind]))


def _s5_states(uf, win):
    g, b_, j, rows = uf.shape
    npair = g // 2
    out = jax.ShapeDtypeStruct((j, npair * b_ * LANES), F32)
    ospec = pl.BlockSpec((j, b_ * LANES), lambda pr: (0, pr))
    return pl.pallas_call(
        _s5state_kernel,
        out_shape=[out] * 4,
        grid=(npair,),
        in_specs=[pl.BlockSpec((2, b_, j, rows), lambda pr: (pr, 0, 0, 0)),
                  pl.BlockSpec((1, 2, 4, rows, LANES), lambda pr: (pr, 0, 0, 0, 0))],
        out_specs=[ospec] * 4,
        compiler_params=_cparams(("parallel",)),
        name="s5state",
    )(uf, win)


def _s5scan_kernel(cfr, cfi, cbr, cbi, sfr, sfi, sbr, sbi, dec_ref, hfr, hfi, hbr, hbi):
    jc, jl, w = cfr.shape[0], sfr.shape[0], sfr.shape[1]
    zero = jnp.zeros((1, w), F32)

    def run(sr, si, outs, ar, ai, n, reverse, carry):
        def body(i, hc):
            row = (n - 1 - i) if reverse else i
            hr, hi = hc
            if outs is not None:
                outs[0][pl.ds(row, 1), :] = hr
                outs[1][pl.ds(row, 1), :] = hi
            nr = ar * hr - ai * hi + sr[pl.ds(row, 1), :]
            ni = ar * hi + ai * hr + si[pl.ds(row, 1), :]
            return nr, ni
        return lax.fori_loop(0, n, body, carry)

    afr, afi = dec_ref[0, 0], dec_ref[0, 1]
    abr, abi = dec_ref[1, 0], dec_ref[1, 1]
    c = run(cfr, cfi, None, afr, afi, jc, False, (zero, zero))
    run(sfr, sfi, (hfr, hfi), afr, afi, jl, False, c)
    c = run(cbr, cbi, None, abr, abi, jc, True, (zero, zero))
    run(sbr, sbi, (hbr, hbi), abr, abi, jl, True, c)


def _s5_scan(ctx_states, states, dec_cols):
    jl, wtot = states[0].shape
    jc = ctx_states[0].shape[0]
    tw = 1024
    spec = pl.BlockSpec((jl, tw), lambda i: (0, i))
    cspec = pl.BlockSpec((jc, tw), lambda i: (0, i))
    return pl.pallas_call(
        _s5scan_kernel,
        out_shape=[jax.ShapeDtypeStruct((jl, wtot), F32)] * 4,
        grid=(wtot // tw,),
        in_specs=[cspec] * 4 + [spec] * 4 + [pl.BlockSpec((2, 2, 1, tw), lambda i: (0, 0, 0, i))],
        out_specs=[spec] * 4,
        compiler_params=_cparams(("parallel",)),
        name="s5scan",
    )(*ctx_states, *states, dec_cols)


def _s5out_kernel(u_ref, hfr, hfi, hbr, hbi, wout_ref, m_ref, y_ref):
    rows = u_ref.shape[3]
    for b in range(u_ref.shape[1]):
        y2 = None
        for kind, h in enumerate((hfr, hfi, hbr, hbi)):
            t = _dot(h[:, b * LANES:(b + 1) * LANES].astype(BF16), wout_ref[0, kind])
            y2 = t if y2 is None else y2 + t
        for gl in range(2):
            y = y2[:, gl * rows:(gl + 1) * rows] + _dot(u_ref[gl, b], m_ref[0, gl])
            y_ref[gl, b] = y.astype(BF16)


def _s5_outputs(uf, hprev, wout, m):
    g, b_, j, rows = uf.shape
    npair = g // 2
    hspec = pl.BlockSpec((j, b_ * LANES), lambda pr: (0, pr))
    return pl.pallas_call(
        _s5out_kernel,
        out_shape=jax.ShapeDtypeStruct((g, b_, j, rows), BF16),
        grid=(npair,),
        in_specs=[pl.BlockSpec((2, b_, j, rows), lambda pr: (pr, 0, 0, 0))] + [hspec] * 4 +
                 [pl.BlockSpec((1, 4, LANES, 2 * rows), lambda pr: (pr, 0, 0, 0)),
                  pl.BlockSpec((1, 2, rows, rows), lambda pr: (pr, 0, 0, 0))],
        out_specs=pl.BlockSpec((2, b_, j, rows), lambda pr: (pr, 0, 0, 0)),
        compiler_params=_cparams(("parallel",)),
        name="s5out",
    )(uf, *hprev, wout, m)


def _chunks_per_step(j):
    return max(n for n in (64, 32, 16, 8) if j % n == 0)


def _s5_regroup_kernel(nj, u_ref, o_ref):
    tc, p = S5_CHUNK, SSM_GROUP
    for s in range(tc):
        xs = u_ref[0, pl.ds(s, nj, stride=tc), :].astype(o_ref.dtype)
        for gi in range(LANES // p):
            o_ref[gi, 0, :, s * p:(s + 1) * p] = xs[:, gi * p:(gi + 1) * p]


def _s5_regroup(u):
    b_, l_, w = u.shape
    tc, p = S5_CHUNK, SSM_GROUP
    gpt = LANES // p
    j = l_ // tc
    nj = _chunks_per_step(j)
    return pl.pallas_call(
        functools.partial(_s5_regroup_kernel, nj),
        out_shape=jax.ShapeDtypeStruct((w // p, b_, j, tc * p), BF16),
        grid=(b_, j // nj, w // LANES),
        in_specs=[pl.BlockSpec((1, tc * nj, LANES), lambda b, i, c: (b, i, c))],
        out_specs=pl.BlockSpec((gpt, 1, nj, tc * p), lambda b, i, c: (c, b, i, 0)),
        compiler_params=_cparams(("parallel", "parallel", "parallel")),
        name="s5regroup",
    )(u)


def _s5_ungroup_kernel(nj, y_ref, o_ref):
    tc, p = S5_CHUNK, SSM_GROUP
    for t in range(tc):
        row = jnp.concatenate([y_ref[gi, 0, :, t * p:(t + 1) * p] for gi in range(LANES // p)], axis=1)
        o_ref[0, pl.ds(t, nj, stride=tc), :] = row.astype(F32)


def _s5_ungroup(yf):
    g, b_, j, rows = yf.shape
    tc, p = S5_CHUNK, SSM_GROUP
    gpt = LANES // p
    nj = _chunks_per_step(j)
    return pl.pallas_call(
        functools.partial(_s5_ungroup_kernel, nj),
        out_shape=jax.ShapeDtypeStruct((b_, j * tc, g * p), F32),
        grid=(b_, j // nj, g // gpt),
        in_specs=[pl.BlockSpec((gpt, 1, nj, rows), lambda b, i, c: (c, b, i, 0))],
        out_specs=pl.BlockSpec((1, tc * nj, LANES), lambda b, i, c: (b, i, c)),
        compiler_params=_cparams(("parallel", "parallel", "parallel")),
        name="s5ungroup",
    )(yf)


def _s5_mixer(u, u_c, ops):
    win, wout, m, dec = ops
    b_, l_, w = u.shape
    npair = w // SSM_GROUP // 2
    uf = _s5_regroup(u)
    states = _s5_states(uf, win)
    ctx_states = _s5_states(_s5_regroup(u_c), win)
    dec_cols = jnp.broadcast_to(dec.transpose(1, 2, 3, 0, 4)[:, :, :, :, None], (2, 2, 1, npair, b_, LANES))
    dec_cols = dec_cols.reshape(2, 2, 1, npair * b_ * LANES)
    hprev = _s5_scan(ctx_states, states, dec_cols)
    return _s5_ungroup(_s5_outputs(uf, hprev, wout, m))


def _attn_kernel(nkv, qt_ref, k_ref, vx_ref, o_ref, s0_ref, s1_ref):
    hd = HEAD_DIM
    kvh = pl.program_id(1)
    tq = qt_ref.shape[2]
    gq = qt_ref.shape[1] // hd
    kw = k_ref.shape[3]
    vr = vx_ref.shape[3]
    own = (lax.broadcasted_iota(jnp.int32, (kw, tq), 0) >> (hd.bit_length() - 1)) == kvh
    qp = []
    for g in range(gq):
        qg = qt_ref[0, g * hd:(g + 1) * hd, :].astype(F32)
        q2 = jnp.concatenate([qg] * (kw // hd), axis=0)
        qp.append(jnp.where(own, q2, 0.0).astype(BF16))
    qp = jnp.concatenate(qp, axis=1)
    m0 = jnp.full((1, gq * tq), -jnp.inf, F32)
    acc0 = jnp.zeros((vr, gq * tq), F32)

    def scores(jb, s_ref):
        s = _dot(k_ref[0, jb], qp)
        s_ref[...] = s
        return jnp.max(s, axis=0, keepdims=True)

    def consume(jb, s_ref, cmax, m_prev, acc):
        m_new = jnp.maximum(m_prev, cmax)
        alpha = jnp.exp2(m_prev - m_new)
        p = jnp.exp2(s_ref[...] - m_new).astype(BF16)
        return m_new, alpha * acc + _dot(vx_ref[0, 0, jb], p)

    bufs = (s0_ref, s1_ref)

    def steps(j0, count, m, acc, cmax):
        for u in range(count):
            cnext = scores(j0 + u + 1, bufs[(u + 1) % 2])
            m, acc = consume(j0 + u, bufs[u % 2], cmax, m, acc)
            cmax = cnext
        return m, acc, cmax

    unroll = ATTN_STEPS_PER_TRIP
    trips = (nkv - 1) // unroll
    carry = lax.fori_loop(0, trips, lambda i, c: steps(unroll * i, unroll, *c), (m0, acc0, scores(0, s0_ref)))
    m, acc, cmax = steps(trips * unroll, (nkv - 1) - trips * unroll, *carry)
    m, acc = consume(nkv - 1, bufs[(nkv - 1) % 2], cmax, m, acc)
    o = acc[0:hd] / acc[hd:hd + 1]
    ot = jnp.concatenate([o[:, g * tq:(g + 1) * tq] for g in range(gq)], axis=0)
    o_ref[0] = ot.T.astype(BF16)


def _attention(qt, k4, vx, tq):
    b_, wq, l_ = qt.shape
    _, kvh, nkv, vr, tk = vx.shape
    kw = k4.shape[3]
    gw = wq // kvh
    return pl.pallas_call(
        functools.partial(_attn_kernel, nkv),
        out_shape=jax.ShapeDtypeStruct((b_, l_, wq), BF16),
        grid=(b_, kvh, l_ // tq),
        in_specs=[pl.BlockSpec((1, gw, tq), lambda b, h, i: (b, h, i)),
                  pl.BlockSpec((1, nkv, tk, kw), lambda b, h, i: (b, 0, 0, 0)),
                  pl.BlockSpec((1, 1, nkv, vr, tk), lambda b, h, i: (b, h, 0, 0, 0))],
        out_specs=pl.BlockSpec((1, tq, gw), lambda b, h, i: (b, i, h)),
        scratch_shapes=[pltpu.VMEM((tk, (gw // HEAD_DIM) * tq), F32)] * 2,
        compiler_params=_cparams(("parallel", "parallel", "parallel")),
        name="attn",
    )(qt, k4, vx)


def _merge_kernel(dm, ne, ys_ref, oa_ref, gl_ref, x_ref, mod_ref, wglu_ref, bglu_ref, wso_ref, wao_ref,
                  wout_ref, n2g_ref, rw_ref, rb_ref, utri_ref, xmix_ref, h2_ref, idx_ref, gate_ref, rank_ref,
                  cnt_out_ref, cnt_ref):
    tm = x_ref.shape[1]
    parts = [pl.ds(r * (tm // MERGE_ROW_PARTS), tm // MERGE_ROW_PARTS) for r in range(MERGE_ROW_PARTS)]
    g1 = mod_ref[0, :, 2 * dm:3 * dm]
    sh2 = mod_ref[0, :, 3 * dm:4 * dm]
    sc2 = mod_ref[0, :, 4 * dm:5 * dm]
    s = [jax.nn.gelu(ys_ref[0, r, :].astype(F32)) for r in parts]
    z = [_dot(sr.astype(BF16), wglu_ref[...]) + bglu_ref[...] for sr in s]
    s = [sr * jax.nn.sigmoid(zr) for sr, zr in zip(s, z)]
    a = [_dot(sr.astype(BF16), wso_ref[...]) for sr in s]
    bq = [_dot(oa_ref[0, r, :], wao_ref[...]) for r in parts]
    merged = []
    for r, ar, br in zip(parts, a, bq):
        gts = jax.nn.sigmoid(gl_ref[0, r, :].astype(F32))
        merged.append((gts[:, 0:dm] * ar + gts[:, dm:2 * dm] * br).astype(BF16))
    out = [_dot(mr, wout_ref[...]) for mr in merged]
    h2_parts = []
    for r, outr in zip(parts, out):
        xm = x_ref[0, r, :] + g1 * outr
        xmix_ref[0, r, :] = xm
        ms = jnp.mean(xm * xm, axis=-1, keepdims=True)
        h2r = xm * lax.rsqrt(ms + EPS) * n2g_ref[...] * (1.0 + sc2) + sh2
        h2_ref[0, r, :] = _pack2(h2r[:, 0:dm // 2], h2r[:, dm // 2:dm])
        h2_parts.append(h2r)
    logits = jnp.concatenate([_dot3(h2r, rw_ref[...]) for h2r in h2_parts], axis=0) + rb_ref[...]
    nep = cnt_ref.shape[0]
    l = logits.T[0:nep, :]
    sub = lax.broadcasted_iota(jnp.int32, l.shape, 0).astype(F32)
    neg = jnp.float32(-jnp.inf)
    vals, idxs = [], []
    for _ in range(TOP_K):
        mx = jnp.max(l, axis=0, keepdims=True)
        ix = jnp.min(jnp.where(l == mx, sub, float(nep)), axis=0, keepdims=True)
        vals.append(mx)
        idxs.append(ix)
        l = jnp.where(sub == ix, neg, l)
    es = [jnp.exp(v - vals[0]) for v in vals]
    den = es[0] + es[1] + es[2] + es[3]
    @pl.when((pl.program_id(0) == 0) & (pl.program_id(1) == 0))
    def _():
        cnt_ref[...] = jnp.zeros(cnt_ref.shape, F32)

    onehot = jnp.zeros(l.shape, F32)
    for k in range(TOP_K):
        onehot = onehot + jnp.where(sub == idxs[k], 1.0, 0.0)
    prior = _dot(onehot.astype(BF16), utri_ref[...]) + cnt_ref[...]
    cnt_ref[...] = cnt_ref[...] + jnp.sum(onehot, axis=1, keepdims=True)
    cnt_out_ref[...] = cnt_ref[...]
    krow = lax.broadcasted_iota(jnp.int32, (8, l.shape[1]), 0)
    idx_out = jnp.zeros(krow.shape, F32)
    gate_out = jnp.zeros(krow.shape, F32)
    rank_out = jnp.zeros(krow.shape, F32)
    for k in range(TOP_K):
        rk = jnp.sum(jnp.where(sub == idxs[k], prior, 0.0), axis=0, keepdims=True)
        idx_out = jnp.where(krow == k, idxs[k], idx_out)
        gate_out = jnp.where(krow == k, es[k] / den, gate_out)
        rank_out = jnp.where(krow == k, rk, rank_out)
    idx_ref[0] = idx_out.astype(jnp.int32)
    gate_ref[0] = gate_out
    rank_ref[0] = rank_out.astype(jnp.int32)


def _merge(y_ssm, o_attn, glog, x, mod3, w_glu, b_glu, w_ssm_out, w_attn_out, w_out, norm2_g,
           rw_pad, rb_pad, ne, tm):
    b_, l_, dm = x.shape
    wu = dm // 2
    tok = lambda w: pl.BlockSpec((1, tm, w), lambda b, i: (b, i, 0))
    full = lambda r, c: pl.BlockSpec((r, c), lambda b, i: (0, 0))
    utri = (jnp.arange(tm)[:, None] < jnp.arange(tm)[None, :]).astype(BF16)
    nep = -(-ne // 8) * 8
    kmaj = pl.BlockSpec((1, 8, tm), lambda b, i: (b, 0, i))
    return pl.pallas_call(
        functools.partial(_merge_kernel, dm, ne),
        out_shape=[jax.ShapeDtypeStruct((b_, l_, dm), F32),
                   jax.ShapeDtypeStruct((b_, l_, dm // 2), jnp.uint32),
                   jax.ShapeDtypeStruct((b_, 8, l_), jnp.int32),
                   jax.ShapeDtypeStruct((b_, 8, l_), F32),
                   jax.ShapeDtypeStruct((b_, 8, l_), jnp.int32),
                   jax.ShapeDtypeStruct((nep, 1), F32)],
        grid=(b_, l_ // tm),
        in_specs=[tok(wu), tok(wu), tok(2 * dm), tok(dm),
                  pl.BlockSpec((1, 1, N_MOD * dm), lambda b, i: (b, 0, 0)),
                  full(wu, wu), full(1, wu), full(wu, dm), full(wu, dm), full(dm, dm),
                  full(1, dm), full(dm, LANES), full(1, LANES), full(tm, tm)],
        out_specs=[tok(dm), tok(dm // 2), kmaj, kmaj, kmaj, full(nep, 1)],
        scratch_shapes=[pltpu.VMEM((nep, 1), F32)],
        compiler_params=_cparams(("arbitrary", "arbitrary")),
        name="merge",
    )(y_ssm, o_attn, glog, x, mod3, w_glu, b_glu, w_ssm_out, w_attn_out, w_out, norm2_g, rw_pad, rb_pad, utri)


def _expert_kernel(de, be_ref, nv_ref, x_ref, wgu_ref, bgu_ref, wd_ref, bd_ref, y_ref, wgu_b, wd_b):
    i = pl.program_id(0)
    n_valid = nv_ref[i]
    used = n_valid > 0
    new_expert = (i == 0) | (be_ref[i] != be_ref[jnp.maximum(i - 1, 0)])

    @pl.when(used & new_expert)
    def _():
        wgu_b[...] = wgu_ref[0].astype(BF16)
        wd_b[...] = wd_ref[0].astype(BF16)

    @pl.when(used)
    def _():
        row = lax.broadcasted_iota(jnp.int32, (x_ref.shape[0], 1), 0)
        x_lo, x_hi = _unpack2(jnp.where(row < n_valid, x_ref[...], jnp.uint32(0)))
        xb = jnp.concatenate([x_lo, x_hi], axis=1).astype(BF16)
        gu = _dot(xb, wgu_b[...]) + bgu_ref[0]
        gate = jnp.minimum(gu[:, 0:de], SWIGLU_LIMIT)
        up = jnp.clip(gu[:, de:2 * de], -SWIGLU_LIMIT, SWIGLU_LIMIT)
        act = (up + 1.0) * (gate * jax.nn.sigmoid(SWIGLU_ALPHA * gate))
        y = _dot(act.astype(BF16), wd_b[...]) + bd_ref[0]
        half = y.shape[1] // 2
        y_ref[...] = _pack2(y[:, 0:half], y[:, half:])

    @pl.when(jnp.logical_not(used))
    def _():
        y_ref[...] = jnp.zeros(y_ref.shape, y_ref.dtype)


def _experts(xb, block_expert, block_valid, w_gate_up, b_gate_up, w_down, b_down):
    n_slots, dmh = xb.shape
    dm = 2 * dmh
    ne, _, de2 = w_gate_up.shape
    de = de2 // 2
    nb = n_slots // EXPERT_BLOCK
    return pl.pallas_call(
        functools.partial(_expert_kernel, de),
        out_shape=jax.ShapeDtypeStruct((n_slots, dmh), jnp.uint32),
        grid_spec=pltpu.PrefetchScalarGridSpec(
            num_scalar_prefetch=2,
            grid=(nb,),
            in_specs=[pl.BlockSpec((EXPERT_BLOCK, dmh), lambda i, be, nu: (i, 0)),
                      pl.BlockSpec((1, dm, de2), lambda i, be, nu: (be[i], 0, 0)),
                      pl.BlockSpec((1, 1, de2), lambda i, be, nu: (be[i], 0, 0)),
                      pl.BlockSpec((1, de, dm), lambda i, be, nu: (be[i], 0, 0)),
                      pl.BlockSpec((1, 1, dm), lambda i, be, nu: (be[i], 0, 0))],
            out_specs=pl.BlockSpec((EXPERT_BLOCK, dmh), lambda i, be, nu: (i, 0)),
            scratch_shapes=[pltpu.VMEM((dm, de2), BF16), pltpu.VMEM((de, dm), BF16)],
        ),
        compiler_params=_cparams(("arbitrary",)),
        name="expert",
    )(block_expert, block_valid, xb, w_gate_up, b_gate_up.reshape(ne, 1, de2), w_down, b_down.reshape(ne, 1, dm))


def _sc_gather_rows(table, idx):
    n = idx.shape[0]
    w = table.shape[1]
    mesh = plsc.VectorSubcoreMesh(core_axis_name="c", subcore_axis_name="s")
    nc, nw = mesh.num_cores, mesh.num_cores * mesh.num_subcores
    win = SC_GATHER_WINDOW
    assert n % (win * nw) == 0
    per_worker = n // nw

    @functools.partial(
        pl.kernel, out_type=jax.ShapeDtypeStruct((n, w), table.dtype), mesh=mesh,
        scratch_types=[pltpu.VMEM((win,), jnp.int32), pltpu.VMEM((win, w), table.dtype),
                       pltpu.SemaphoreType.DMA])
    def gather(x_hbm, i_hbm, o_hbm, idx_v, rows_v, sem):
        base = (lax.axis_index("s") * nc + lax.axis_index("c")) * per_worker

        @pl.loop(0, per_worker // win)
        def _(j):
            off = base + j * win
            pltpu.sync_copy(i_hbm.at[pl.ds(off, win)], idx_v)
            pltpu.async_copy(x_hbm.at[idx_v], rows_v, sem).wait()
            pltpu.sync_copy(rows_v, o_hbm.at[pl.ds(off, win)])

    return gather(table, idx)


def _sc_scatter_rows(rows, dest, n_out):
    t_, w = rows.shape
    kk = dest.shape[0] // t_
    mesh = plsc.VectorSubcoreMesh(core_axis_name="c", subcore_axis_name="s")
    nc, nw = mesh.num_cores, mesh.num_cores * mesh.num_subcores
    win = SC_GATHER_WINDOW
    assert t_ % (win * nw) == 0
    per_worker = t_ // nw

    @functools.partial(
        pl.kernel, out_type=jax.ShapeDtypeStruct((n_out, w), rows.dtype), mesh=mesh,
        scratch_types=[pltpu.VMEM((win,), jnp.int32)] * kk + [pltpu.VMEM((win, w), rows.dtype),
                                                              pltpu.SemaphoreType.DMA])
    def scatter(x_hbm, d_hbm, o_hbm, *scratch):
        idx_vs, rows_v, sem = scratch[:kk], scratch[kk], scratch[kk + 1]
        base = (lax.axis_index("s") * nc + lax.axis_index("c")) * per_worker

        @pl.loop(0, per_worker // win)
        def _(j):
            off = base + j * win
            pltpu.sync_copy(x_hbm.at[pl.ds(off, win)], rows_v)
            for k in range(kk):
                pltpu.sync_copy(d_hbm.at[pl.ds(k * t_ + off, win)], idx_vs[k])
            copies = [pltpu.async_copy(rows_v, o_hbm.at[idx_vs[k]], sem) for k in range(kk)]
            for cp in copies:
                cp.wait()

    return scatter(rows, dest)


def _final_kernel(dm, yg_ref, gate_ref, xm_ref, mod_ref, fg_ref, o_ref):
    gates = gate_ref[0]
    f_lo = f_hi = None
    for k in range(TOP_K):
        y_lo, y_hi = _unpack2(yg_ref[k, 0])
        gk = gates[:, k:k + 1]
        f_lo = gk * y_lo if f_lo is None else f_lo + gk * y_lo
        f_hi = gk * y_hi if f_hi is None else f_hi + gk * y_hi
    f = jnp.concatenate([f_lo, f_hi], axis=1)
    g2 = mod_ref[0, :, 5 * dm:6 * dm]
    xo = xm_ref[0] + g2 * f
    ms = jnp.mean(xo * xo, axis=-1, keepdims=True)
    o_ref[0] = xo * lax.rsqrt(ms + EPS) * fg_ref[...]


def _final(yg, gates, x_mix, mod3, final_g, tm):
    b_, l_, dm = x_mix.shape
    return pl.pallas_call(
        functools.partial(_final_kernel, dm),
        out_shape=jax.ShapeDtypeStruct((b_, l_, dm), F32),
        grid=(b_, l_ // tm),
        in_specs=[pl.BlockSpec((TOP_K, 1, tm, dm // 2), lambda b, i: (0, b, i, 0)),
                  pl.BlockSpec((1, tm, LANES), lambda b, i: (b, i, 0)),
                  pl.BlockSpec((1, tm, dm), lambda b, i: (b, i, 0)),
                  pl.BlockSpec((1, 1, N_MOD * dm), lambda b, i: (b, 0, 0)),
                  pl.BlockSpec((1, dm), lambda b, i: (0, 0))],
        out_specs=pl.BlockSpec((1, tm, dm), lambda b, i: (b, i, 0)),
        compiler_params=_cparams(("parallel", "parallel")),
        name="final",
    )(yg, gates, x_mix, mod3, final_g.reshape(1, dm))


def _rope_tables(l_):
    half = HEAD_DIM // 2
    inv_freq = ROPE_THETA ** (-jnp.arange(0, half, 2, dtype=F32) / half)
    t = jnp.arange(l_, dtype=jnp.int32)
    row_id, col_id = t // GRID_W, t % GRID_W
    ang = jnp.concatenate([row_id.astype(F32)[:, None] * inv_freq,
                           col_id.astype(F32)[:, None] * inv_freq], axis=-1)
    cos = jnp.repeat(jnp.cos(ang), 2, axis=-1)
    sin = jnp.repeat(jnp.sin(ang), 2, axis=-1)
    sign = jnp.tile(jnp.array([-1.0, 1.0], F32), HEAD_DIM // 2)
    reps = LANES // HEAD_DIM
    return jnp.tile(cos, (1, reps)), jnp.tile(sin * sign, (1, reps))


def _routing(idx, rank, counts, t_, ne):
    m_ = t_ * TOP_K
    blk = EXPERT_BLOCK
    nb = (m_ + ne * (blk - 1)) // blk
    nb = -(-nb // SLOT_BLOCK_MULTIPLE) * SLOT_BLOCK_MULTIPLE
    blk_counts = (counts + blk - 1) // blk
    blk_ends = jnp.cumsum(blk_counts)
    first_slot = (blk_ends - blk_counts) * blk
    onehot = idx[:, :, None] == jnp.arange(ne, dtype=jnp.int32)[None, None, :]
    dest = rank + jnp.sum(jnp.where(onehot, first_slot[None, None, :], 0), axis=2)
    blocks = jnp.arange(nb, dtype=jnp.int32)
    block_expert = jnp.minimum(jnp.sum(blocks[:, None] >= blk_ends[None, :], axis=1), ne - 1)
    in_expert = blocks * blk - first_slot[block_expert]
    block_valid = jnp.clip(counts[block_expert] - in_expert, 0, blk)
    block_valid = jnp.where(blocks < blk_ends[-1], block_valid, 0)
    return dest.astype(jnp.int32), block_expert.astype(jnp.int32), block_valid.astype(jnp.int32), nb * blk


def kernel(x, c, ctx, c_ctx, w_mod, b_mod, norm1_g, norm2_g, w_in, s5_lam_re, s5_lam_im, s5_log_step,
           s5_b_re, s5_b_im, s5_c_re, s5_c_im, s5_d, w_glu, b_glu, w_ssm_out, q_norm_g, k_norm_g,
           w_attn_out, w_out, router_w, router_b, w_gate_up, b_gate_up, w_down, b_down, final_norm_g):
    b_, l_, dm = x.shape
    lc = ctx.shape[1]
    depth = w_mod.shape[0]
    assert depth == 1, "single-layer block"
    assert b_ <= 7 and l_ % 512 == 0 and lc % S5_CHUNK == 0
    wu = dm // 2
    wkv = wu // GQA_GROUP
    kvh = wkv // HEAD_DIM
    ne = router_w.shape[-1]
    i = 0

    c8 = jnp.zeros((8, dm), F32).at[:b_].set(c).at[b_].set(c_ctx)
    mod3 = _modulation(c8, w_mod[i], b_mod[i]).reshape(8, 1, N_MOD * dm)

    w_in_b = w_in[i].astype(BF16)
    cos, sin = _rope_tables(l_)
    hsum = jnp.kron(jnp.eye(wu // HEAD_DIM, dtype=F32), jnp.ones((HEAD_DIM, HEAD_DIM), F32)).astype(BF16)
    qg = jnp.tile(q_norm_g[i], wu // HEAD_DIM).reshape(1, wu)
    kg = jnp.tile(k_norm_g[i], wkv // HEAD_DIM).reshape(1, wkv)
    u, k, vt, qt, glog = _inproj(x, mod3, None, norm1_g[i], w_in_b, cos, sin, qg, kg, hsum, True, 512)
    ones_t = jnp.ones((lc, LANES), F32)
    u_c, k_c, vt_c = _inproj(ctx, mod3, b_, norm1_g[i], w_in_b[:, :wu + 2 * wkv], ones_t, 0.0 * ones_t,
                             qg, kg, hsum, False, lc)

    ops = _s5_operators(s5_lam_re[i], s5_lam_im[i], s5_log_step[i], s5_b_re[i], s5_b_im[i],
                        s5_c_re[i], s5_c_im[i], s5_d[i])
    y_ssm = _s5_mixer(u, u_c, ops)

    s_all = lc + l_
    tk = max(t for t in range(LANES, ATTN_KEY_CHUNK + 1, LANES) if s_all % t == 0)
    nkv = s_all // tk
    k4 = jnp.concatenate([k_c, k], axis=1).reshape(b_, nkv, tk, wkv)
    vt_all = jnp.concatenate([vt_c, vt], axis=2).reshape(b_, kvh, HEAD_DIM, s_all)
    vx = jnp.concatenate([vt_all, jnp.ones((b_, kvh, 1, s_all), BF16),
                          jnp.zeros((b_, kvh, ATTN_V_ROWS - HEAD_DIM - 1, s_all), BF16)], axis=2)
    vx = vx.reshape(b_, kvh, ATTN_V_ROWS, nkv, tk).transpose(0, 1, 3, 2, 4)
    o_attn = _attention(qt, k4, vx, ATTN_Q_BLOCK)

    rw_pad = jnp.zeros((dm, LANES), F32).at[:, :ne].set(router_w[i])
    rb_pad = jnp.full((1, LANES), -jnp.inf, F32).at[0, :ne].set(router_b[i])
    x_mix, h2, idx, gates, rank, counts = _merge(
        y_ssm, o_attn, glog, x, mod3, w_glu[i].astype(BF16), b_glu[i].reshape(1, wu),
        w_ssm_out[i].astype(BF16), w_attn_out[i].astype(BF16), w_out[i].astype(BF16),
        norm2_g[i].reshape(1, dm), rw_pad, rb_pad, ne, 512)

    t_ = b_ * l_
    kmajor = lambda a: a[:, :TOP_K, :].transpose(1, 0, 2).reshape(TOP_K, t_)
    dest_kmajor, block_expert, block_valid, n_slots = _routing(
        kmajor(idx), kmajor(rank), counts[:ne, 0].astype(jnp.int32), t_, ne)
    dest_kmajor = dest_kmajor.reshape(-1)
    xb = _sc_scatter_rows(h2.reshape(t_, dm // 2), dest_kmajor, n_slots)
    yb = _experts(xb, block_expert, block_valid, w_gate_up[i], b_gate_up[i], w_down[i], b_down[i])
    yg = _sc_gather_rows(yb, dest_kmajor).reshape(TOP_K, b_, l_, dm // 2)
    gates_tok = jnp.pad(gates.transpose(0, 2, 1), ((0, 0), (0, 0), (0, LANES - 8)))
    return _final(yg, gates_tok, x_mix, mod3, final_norm_g, 512)
```

```python
import functools
import math

import jax
import jax.numpy as jnp
from jax import lax
from jax.experimental import pallas as pl
from jax.experimental.pallas import tpu as pltpu
from jax.experimental.pallas import tpu_sc as plsc

F32 = jnp.float32
BF16 = jnp.bfloat16

EPS = 1e-6
GRID_W = 64
N_MOD = 6
SSM_GROUP = 16
SSM_STATE = 64
HEAD_DIM = 64
GQA_GROUP = 4
ROPE_THETA = 10000.0
TOP_K = 4
SWIGLU_ALPHA = 1.702
SWIGLU_LIMIT = 7.0
LOG2E = 1.4426950408889634

LANES = 128
S5_CHUNK = 32
S5_POW_ROWS = 56
EXPERT_BLOCK = 512
ATTN_V_ROWS = 80
ATTN_KEY_CHUNK = 384
ATTN_Q_BLOCK = 256
ATTN_STEPS_PER_TRIP = 10
MERGE_ROW_PARTS = 2
SC_GATHER_WINDOW = 128
SC_WORKERS = 32
SLOT_BLOCK_MULTIPLE = SC_GATHER_WINDOW * SC_WORKERS // EXPERT_BLOCK
VMEM_LIMIT = 56 * 1024 * 1024


def _cparams(sem, flags=None):
    return pltpu.CompilerParams(dimension_semantics=sem, vmem_limit_bytes=VMEM_LIMIT, flags=flags)


def _dot(a, b):
    return jnp.dot(a, b, preferred_element_type=F32)


def _split(a):
    hi = a.astype(BF16)
    lo = (a - hi.astype(F32)).astype(BF16)
    return hi, lo


def _pack2(lo, hi):
    lo_w = lax.bitcast_convert_type(lo.astype(BF16).astype(F32), jnp.uint32) >> 16
    hi_w = lax.bitcast_convert_type(hi.astype(BF16).astype(F32), jnp.uint32) & jnp.uint32(0xFFFF0000)
    return lo_w | hi_w


def _unpack2(words):
    lo = lax.bitcast_convert_type(words << 16, F32)
    hi = lax.bitcast_convert_type(words & jnp.uint32(0xFFFF0000), F32)
    return lo, hi


def _dot3(a, b):
    ah, al = _split(a)
    bh, bl = _split(b)
    return _dot(ah, bh) + _dot(ah, bl) + _dot(al, bh)


def _mod_kernel(c_ref, w_ref, b_ref, o_ref):
    c = c_ref[...]
    s = c * jax.nn.sigmoid(c)
    o_ref[...] = _dot3(s, w_ref[...]) + b_ref[...]


def _modulation(c8, w_mod, b_mod):
    d = c8.shape[1]
    n = w_mod.shape[1]
    tn = 1024
    return pl.pallas_call(
        _mod_kernel,
        out_shape=jax.ShapeDtypeStruct((8, n), F32),
        grid=(n // tn,),
        in_specs=[pl.BlockSpec((8, d), lambda j: (0, 0)),
                  pl.BlockSpec((d, tn), lambda j: (0, j)),
                  pl.BlockSpec((1, tn), lambda j: (0, j))],
        out_specs=pl.BlockSpec((8, tn), lambda j: (0, j)),
        compiler_params=_cparams(("parallel",)),
        name="mod",
    )(c8, w_mod, b_mod.reshape(1, n))


def _head_norm_rope(z, gain, hsum, cos, sin, scale):
    ss = _dot((z * z).astype(BF16), hsum)
    zn = z * lax.rsqrt(ss * (1.0 / HEAD_DIM) + EPS) * gain
    lane = lax.broadcasted_iota(jnp.int32, (1, LANES), 1)
    even = (lane & 1) == 0
    outs = []
    for ci in range(z.shape[1] // LANES):
        ch = zn[:, ci * LANES:(ci + 1) * LANES]
        nxt = pltpu.roll(ch, LANES - 1, axis=1)
        prv = pltpu.roll(ch, 1, axis=1)
        sw = jnp.where(even, nxt, prv)
        outs.append((ch * cos + sw * sin) * scale)
    return outs


def _inproj_kernel(with_q, dm, x_ref, mod_ref, g_ref, w_ref, cos_ref, sin_ref, qg_ref, kg_ref,
                   hsum_ref, *out_refs):
    x = x_ref[0]
    ms = jnp.mean(x * x, axis=-1, keepdims=True)
    y = x * lax.rsqrt(ms + EPS) * g_ref[...]
    sh = mod_ref[0, :, 0:dm]
    sc = mod_ref[0, :, dm:2 * dm]
    hb = (y * (1.0 + sc) + sh).astype(BF16)
    cos = cos_ref[...]
    sin = sin_ref[...]
    wu = dm // 2
    wkv = wu // GQA_GROUP
    c_k, c_v, c_q, c_g = wu, wu + wkv, wu + 2 * wkv, 2 * wu + 2 * wkv
    u_ref, k_ref, vt_ref = out_refs[:3]
    u = _dot(hb, w_ref[:, 0:c_k])
    if with_q:
        ubuf = out_refs[-1]
        p, gpt, nj = SSM_GROUP, LANES // SSM_GROUP, x.shape[0] // S5_CHUNK
        for ci in range(wu // LANES):
            ubuf[ci] = u[:, ci * LANES:(ci + 1) * LANES]
        for s in range(S5_CHUNK):
            for ci in range(wu // LANES):
                xs = ubuf[ci, pl.ds(s, nj, stride=S5_CHUNK), :].astype(BF16)
                for gi in range(gpt):
                    u_ref[ci * gpt + gi, 0, :, s * p:(s + 1) * p] = xs[:, gi * p:(gi + 1) * p]
    else:
        u_ref[0] = u
    kz = _dot(hb, w_ref[:, c_k:c_v])
    (kr,) = _head_norm_rope(kz, kg_ref[...], hsum_ref[0:wkv, 0:wkv], cos, sin, 1.0)
    k_ref[0] = kr.astype(BF16)
    vt_ref[0] = _dot(hb, w_ref[:, c_v:c_q]).T.astype(BF16)
    if with_q:
        qt_ref, gl_ref = out_refs[3:5]
        qz = _dot(hb, w_ref[:, c_q:c_g])
        qs = _head_norm_rope(qz, qg_ref[...], hsum_ref[...], cos, sin, HEAD_DIM ** -0.5 * LOG2E)
        for ci, qc in enumerate(qs):
            qt_ref[0, ci * LANES:(ci + 1) * LANES, :] = qc.T.astype(BF16)
        gl_ref[0] = _dot(hb, w_ref[:, c_g:]).astype(BF16)


def _inproj(x, mod3, mod_row0, norm_g, w_in_b, cos, sin, qg, kg, hsum, with_q, tm):
    b_, l_, dm = x.shape
    wu = dm // 2
    wkv = wu // GQA_GROUP
    n_in = w_in_b.shape[1]
    outs = [jax.ShapeDtypeStruct((b_, l_, wu), F32),
            jax.ShapeDtypeStruct((b_, l_, wkv), BF16),
            jax.ShapeDtypeStruct((b_, wkv, l_), BF16)]
    ospecs = [pl.BlockSpec((1, tm, wu), lambda b, i: (b, i, 0)),
              pl.BlockSpec((1, tm, wkv), lambda b, i: (b, i, 0)),
              pl.BlockSpec((1, wkv, tm), lambda b, i: (b, 0, i))]
    scratch = []
    if with_q:
        g, rows = wu // SSM_GROUP, S5_CHUNK * SSM_GROUP
        outs[0] = jax.ShapeDtypeStruct((g, b_, l_ // S5_CHUNK, rows), BF16)
        ospecs[0] = pl.BlockSpec((g, 1, tm // S5_CHUNK, rows), lambda b, i: (0, b, i, 0))
        outs += [jax.ShapeDtypeStruct((b_, wu, l_), BF16),
                 jax.ShapeDtypeStruct((b_, l_, 2 * dm), BF16)]
        ospecs += [pl.BlockSpec((1, wu, tm), lambda b, i: (b, 0, i)),
                   pl.BlockSpec((1, tm, 2 * dm), lambda b, i: (b, i, 0))]
        scratch = [pltpu.VMEM((wu // LANES, tm, LANES), F32)]
    if mod_row0 is None:
        mod_map = lambda b, i: (b, 0, 0)
    else:
        mod_map = lambda b, i: (mod_row0, 0, 0)
    return pl.pallas_call(
        functools.partial(_inproj_kernel, with_q, dm),
        out_shape=outs,
        grid=(b_, l_ // tm),
        in_specs=[pl.BlockSpec((1, tm, dm), lambda b, i: (b, i, 0)),
                  pl.BlockSpec((1, 1, N_MOD * dm), mod_map),
                  pl.BlockSpec((1, dm), lambda b, i: (0, 0)),
                  pl.BlockSpec((dm, n_in), lambda b, i: (0, 0)),
                  pl.BlockSpec((tm, LANES), lambda b, i: (i, 0)),
                  pl.BlockSpec((tm, LANES), lambda b, i: (i, 0)),
                  pl.BlockSpec((1, wu), lambda b, i: (0, 0)),
                  pl.BlockSpec((1, wkv), lambda b, i: (0, 0)),
                  pl.BlockSpec((wu, wu), lambda b, i: (0, 0))],
        out_specs=ospecs,
        scratch_shapes=scratch,
        compiler_params=_cparams(("parallel", "parallel")),
        name="inproj_x" if with_q else "inproj_ctx",
    )(x, mod3, norm_g.reshape(1, dm), w_in_b, cos, sin, qg, kg, hsum)


def _cexp_pow(e, lr, li):
    mag = jnp.exp(e * lr)
    ang = e * li
    return mag * jnp.cos(ang), mag * jnp.sin(ang)


def _cmul(ar, ai, br, bi):
    return ar * br - ai * bi, ar * bi + ai * br


def _s5pre_kernel(lamr_r, lami_r, lst_r, lamr_c, lami_c, lst_c, btr_ref, bti_ref, ctr_ref, cti_ref,
                  d_ref, win_ref, wout_ref, m_ref, dec_ref):
    tc = S5_CHUNK
    half = tc // 2
    p = SSM_GROUP
    rows = tc * p
    gsh = SSM_GROUP.bit_length() - 1
    nsh = SSM_STATE.bit_length() - 1
    row_i = lax.broadcasted_iota(jnp.int32, (rows, rows), 0)
    col_i = lax.broadcasted_iota(jnp.int32, (rows, rows), 1)
    lane_g = lax.broadcasted_iota(jnp.int32, (1, LANES), 1) >> nsh
    subl_g = lax.broadcasted_iota(jnp.int32, (LANES, 1), 0) >> nsh
    e_rows = lax.broadcasted_iota(jnp.int32, (S5_POW_ROWS, 1), 0) - half
    e_lanes = lax.broadcasted_iota(jnp.int32, (1, LANES), 1) - half
    sel_e = lax.broadcasted_iota(jnp.int32, (LANES, rows), 0) - half
    sel_t = lax.broadcasted_iota(jnp.int32, (LANES, rows), 1) >> gsh
    sel_p = (lax.broadcasted_iota(jnp.int32, (LANES, rows), 0)
             == (lax.broadcasted_iota(jnp.int32, (LANES, rows), 1) & (p - 1))).astype(BF16)

    def spread_lanes(table, sel):
        hi, lo = _split(table)
        return _dot(hi, sel) + _dot(lo, sel)

    def spread_rows(table, e_of_s):
        return jnp.concatenate([jnp.broadcast_to(table[e_of_s(s) + half:e_of_s(s) + half + 1, :], (p, LANES))
                                for s in range(tc)], axis=0)

    m_acc = [jnp.where(row_i == col_i, d_ref[0, gl], 0.0) for gl in range(2)]
    for dr in range(2):
        step = jnp.exp(lst_r[dr, 0])
        lam_r, lam_i = lamr_r[dr, 0], lami_r[dr, 0]
        lr, li = lam_r * step, lam_i * step
        pw_r, pw_i = _cexp_pow(e_rows.astype(F32), lr, li)
        lb_r, lb_i = pw_r[half + 1:half + 2, :], pw_i[half + 1:half + 2, :]
        den = lam_r * lam_r + lam_i * lam_i
        nr, ni = lb_r - 1.0, lb_i
        cf_r = (nr * lam_r + ni * lam_i) / den
        cf_i = (ni * lam_r - nr * lam_i) / den
        bt_r = jnp.concatenate([btr_ref[dr, 0]] * tc, axis=0)
        bt_i = jnp.concatenate([bti_ref[dr, 0]] * tc, axis=0)
        bb_r, bb_i = _cmul(cf_r, cf_i, bt_r, bt_i)
        e_in = (lambda s: tc - 1 - s) if dr == 0 else (lambda s: s)
        wi_r, wi_i = _cmul(bb_r, bb_i, spread_rows(pw_r, e_in), spread_rows(pw_i, e_in))
        e_a = (lambda s: half - s) if dr == 0 else (lambda s: s - half)
        a_r, a_i = _cmul(bb_r, bb_i, spread_rows(pw_r, e_a), spread_rows(pw_i, e_a))
        dec_ref[0, dr, 0] = pw_r[tc + half:tc + half + 1, :]
        dec_ref[0, dr, 1] = pw_i[tc + half:tc + half + 1, :]
        stepc = jnp.exp(lst_c[dr, 0])
        lrc, lic = lamr_c[dr, 0] * stepc, lami_c[dr, 0] * stepc
        qw_r, qw_i = _cexp_pow(e_lanes.astype(F32), lrc, lic)
        qw_r = jnp.where(e_lanes <= tc, qw_r, 0.0)
        qw_i = jnp.where(e_lanes <= tc, qw_i, 0.0)
        c_r, c_i = spread_lanes(ctr_ref[dr, 0], sel_p), spread_lanes(cti_ref[dr, 0], sel_p)
        sel_b = (sel_e == (sel_t - half if dr == 0 else half - sel_t)).astype(BF16)
        bm_r, bm_i = _cmul(c_r, c_i, spread_lanes(qw_r, sel_b), spread_lanes(qw_i, sel_b))
        sel_o = (sel_e == (sel_t + 1 if dr == 0 else tc - sel_t)).astype(BF16)
        wo_r, wo_i = _cmul(c_r, c_i, spread_lanes(qw_r, sel_o), spread_lanes(qw_i, sel_o))
        if dr == 0:
            mask = (col_i >> gsh) >= (row_i >> gsh)
        else:
            mask = (row_i >> gsh) >= (col_i >> gsh)
        for gl in range(2):
            lsel = lane_g == gl
            ssel = subl_g == gl
            win_ref[0, gl, 2 * dr] = jnp.where(lsel, wi_r, 0.0).astype(BF16)
            win_ref[0, gl, 2 * dr + 1] = jnp.where(lsel, wi_i, 0.0).astype(BF16)
            wout_ref[0, 2 * dr, :, gl * rows:(gl + 1) * rows] = jnp.where(ssel, wo_r, 0.0).astype(BF16)
            wout_ref[0, 2 * dr + 1, :, gl * rows:(gl + 1) * rows] = jnp.where(ssel, -wo_i, 0.0).astype(BF16)
            kmat = (_dot3(jnp.where(lsel, a_r, 0.0), bm_r) - _dot3(jnp.where(lsel, a_i, 0.0), bm_i))
            m_acc[gl] = m_acc[gl] + jnp.where(mask, kmat, 0.0)
    for gl in range(2):
        m_ref[0, gl] = m_acc[gl].astype(BF16)


def _s5_operators(lam_re, lam_im, log_step, b_re, b_im, c_re, c_im, d_skip):
    g = lam_re.shape[1]
    n, p = SSM_STATE, SSM_GROUP
    npair = g // 2
    tc = S5_CHUNK
    rows = tc * p

    def row_form(a):
        return a.reshape(2, npair, 1, 2 * n)

    def col_form(a):
        return a.reshape(2, npair, 2 * n, 1)

    lst = jnp.broadcast_to(log_step[:, :, None], (2, g, n))
    bt = lambda b: b.reshape(2, npair, 2, n, p).transpose(0, 1, 4, 2, 3).reshape(2, npair, p, 2 * n)
    ct = lambda c: jnp.pad(c.reshape(2, npair, 2, p, n).transpose(0, 1, 2, 4, 3).reshape(2, npair, 2 * n, p),
                           ((0, 0), (0, 0), (0, 0), (0, LANES - p)))
    d2 = jnp.broadcast_to(d_skip.reshape(npair, 2, 1, 1, p), (npair, 2, 1, tc, p)).reshape(npair, 2, 1, rows)

    def spec4(shape):
        return pl.BlockSpec((2, 1) + shape, lambda i: (0, i, 0, 0))

    return pl.pallas_call(
        _s5pre_kernel,
        out_shape=[jax.ShapeDtypeStruct((npair, 2, 4, rows, LANES), BF16),
                   jax.ShapeDtypeStruct((npair, 4, LANES, 2 * rows), BF16),
                   jax.ShapeDtypeStruct((npair, 2, rows, rows), BF16),
                   jax.ShapeDtypeStruct((npair, 2, 2, 1, LANES), F32)],
        grid=(npair,),
        in_specs=[spec4((1, LANES)), spec4((1, LANES)), spec4((1, LANES)),
                  spec4((LANES, 1)), spec4((LANES, 1)), spec4((LANES, 1)),
                  spec4((p, LANES)), spec4((p, LANES)),
                  spec4((LANES, LANES)), spec4((LANES, LANES)),
                  pl.BlockSpec((1, 2, 1, rows), lambda i: (i, 0, 0, 0))],
        out_specs=[pl.BlockSpec((1, 2, 4, rows, LANES), lambda i: (i, 0, 0, 0, 0)),
                   pl.BlockSpec((1, 4, LANES, 2 * rows), lambda i: (i, 0, 0, 0)),
                   pl.BlockSpec((1, 2, rows, rows), lambda i: (i, 0, 0, 0)),
                   pl.BlockSpec((1, 2, 2, 1, LANES), lambda i: (i, 0, 0, 0, 0))],
        compiler_params=_cparams(("parallel",)),
        name="s5pre",
    )(row_form(lam_re), row_form(lam_im), row_form(lst),
      col_form(lam_re), col_form(lam_im), col_form(lst),
      bt(b_re), bt(b_im), ct(c_re), ct(c_im), d2)


def _s5state_kernel(u_ref, win_ref, *s_refs):
    for b in range(u_ref.shape[1]):
        for kind in range(4):
            s_refs[kind][:, b * LANES:(b + 1) * LANES] = (_dot(u_ref[0, b], win_ref[0, 0, kind])
                                                          + _dot(u_ref[1, b], win_ref[0, 1, kind]))


def _s5_states(uf, win):
    g, b_, j, rows = uf.shape
    npair = g // 2
    out = jax.ShapeDtypeStruct((j, npair * b_ * LANES), F32)
    ospec = pl.BlockSpec((j, b_ * LANES), lambda pr: (0, pr))
    return pl.pallas_call(
        _s5state_kernel,
        out_shape=[out] * 4,
        grid=(npair,),
        in_specs=[pl.BlockSpec((2, b_, j, rows), lambda pr: (pr, 0, 0, 0)),
                  pl.BlockSpec((1, 2, 4, rows, LANES), lambda pr: (pr, 0, 0, 0, 0))],
        out_specs=[ospec] * 4,
        compiler_params=_cparams(("parallel",)),
        name="s5state",
    )(uf, win)


def _s5scan_kernel(cfr, cfi, cbr, cbi, sfr, sfi, sbr, sbi, dec_ref, hfr, hfi, hbr, hbi):
    jc, jl, w = cfr.shape[0], sfr.shape[0], sfr.shape[1]
    zero = jnp.zeros((1, w), F32)

    def run(sr, si, outs, ar, ai, n, reverse, carry):
        def body(i, hc):
            row = (n - 1 - i) if reverse else i
            hr, hi = hc
            if outs is not None:
                outs[0][pl.ds(row, 1), :] = hr
                outs[1][pl.ds(row, 1), :] = hi
            nr = ar * hr - ai * hi + sr[pl.ds(row, 1), :]
            ni = ar * hi + ai * hr + si[pl.ds(row, 1), :]
            return nr, ni
        return lax.fori_loop(0, n, body, carry)

    afr, afi = dec_ref[0, 0], dec_ref[0, 1]
    abr, abi = dec_ref[1, 0], dec_ref[1, 1]
    c = run(cfr, cfi, None, afr, afi, jc, False, (zero, zero))
    run(sfr, sfi, (hfr, hfi), afr, afi, jl, False, c)
    c = run(cbr, cbi, None, abr, abi, jc, True, (zero, zero))
    run(sbr, sbi, (hbr, hbi), abr, abi, jl, True, c)


def _s5_scan(ctx_states, states, dec_cols):
    jl, wtot = states[0].shape
    jc = ctx_states[0].shape[0]
    tw = 1024
    spec = pl.BlockSpec((jl, tw), lambda i: (0, i))
    cspec = pl.BlockSpec((jc, tw), lambda i: (0, i))
    return pl.pallas_call(
        _s5scan_kernel,
        out_shape=[jax.ShapeDtypeStruct((jl, wtot), F32)] * 4,
        grid=(wtot // tw,),
        in_specs=[cspec] * 4 + [spec] * 4 + [pl.BlockSpec((2, 2, 1, tw), lambda i: (0, 0, 0, i))],
        out_specs=[spec] * 4,
        compiler_params=_cparams(("parallel",)),
        name="s5scan",
    )(*ctx_states, *states, dec_cols)


def _s5out_kernel(u_ref, hfr, hfi, hbr, hbi, wout_ref, m_ref, y_ref):
    rows = u_ref.shape[3]
    for b in range(u_ref.shape[1]):
        y2 = None
        for kind, h in enumerate((hfr, hfi, hbr, hbi)):
            t = _dot(h[:, b * LANES:(b + 1) * LANES].astype(BF16), wout_ref[0, kind])
            y2 = t if y2 is None else y2 + t
        for gl in range(2):
            y = y2[:, gl * rows:(gl + 1) * rows] + _dot(u_ref[gl, b], m_ref[0, gl])
            y_ref[gl, b] = y.astype(BF16)


def _s5_outputs(uf, hprev, wout, m):
    g, b_, j, rows = uf.shape
    npair = g // 2
    hspec = pl.BlockSpec((j, b_ * LANES), lambda pr: (0, pr))
    return pl.pallas_call(
        _s5out_kernel,
        out_shape=jax.ShapeDtypeStruct((g, b_, j, rows), BF16),
        grid=(npair,),
        in_specs=[pl.BlockSpec((2, b_, j, rows), lambda pr: (pr, 0, 0, 0))] + [hspec] * 4 +
                 [pl.BlockSpec((1, 4, LANES, 2 * rows), lambda pr: (pr, 0, 0, 0)),
                  pl.BlockSpec((1, 2, rows, rows), lambda pr: (pr, 0, 0, 0))],
        out_specs=pl.BlockSpec((2, b_, j, rows), lambda pr: (pr, 0, 0, 0)),
        compiler_params=_cparams(("parallel",)),
        name="s5out",
    )(uf, *hprev, wout, m)


def _chunks_per_step(j):
    return max(n for n in (64, 32, 16, 8) if j % n == 0)


def _s5_regroup_kernel(nj, u_ref, o_ref):
    tc, p = S5_CHUNK, SSM_GROUP
    for s in range(tc):
        xs = u_ref[0, pl.ds(s, nj, stride=tc), :].astype(o_ref.dtype)
        for gi in range(LANES // p):
            o_ref[gi, 0, :, s * p:(s + 1) * p] = xs[:, gi * p:(gi + 1) * p]


def _s5_regroup(u):
    b_, l_, w = u.shape
    tc, p = S5_CHUNK, SSM_GROUP
    gpt = LANES // p
    j = l_ // tc
    nj = _chunks_per_step(j)
    return pl.pallas_call(
        functools.partial(_s5_regroup_kernel, nj),
        out_shape=jax.ShapeDtypeStruct((w // p, b_, j, tc * p), BF16),
        grid=(b_, j // nj, w // LANES),
        in_specs=[pl.BlockSpec((1, tc * nj, LANES), lambda b, i, c: (b, i, c))],
        out_specs=pl.BlockSpec((gpt, 1, nj, tc * p), lambda b, i, c: (c, b, i, 0)),
        compiler_params=_cparams(("parallel", "parallel", "parallel")),
        name="s5regroup",
    )(u)


def _s5_ungroup_kernel(nj, y_ref, o_ref):
    tc, p = S5_CHUNK, SSM_GROUP
    for t in range(tc):
        row = jnp.concatenate([y_ref[gi, 0, :, t * p:(t + 1) * p] for gi in range(LANES // p)], axis=1)
        o_ref[0, pl.ds(t, nj, stride=tc), :] = row.astype(F32)


def _s5_ungroup(yf):
    g, b_, j, rows = yf.shape
    tc, p = S5_CHUNK, SSM_GROUP
    gpt = LANES // p
    nj = _chunks_per_step(j)
    return pl.pallas_call(
        functools.partial(_s5_ungroup_kernel, nj),
        out_shape=jax.ShapeDtypeStruct((b_, j * tc, g * p), F32),
        grid=(b_, j // nj, g // gpt),
        in_specs=[pl.BlockSpec((gpt, 1, nj, rows), lambda b, i, c: (c, b, i, 0))],
        out_specs=pl.BlockSpec((1, tc * nj, LANES), lambda b, i, c: (b, i, c)),
        compiler_params=_cparams(("parallel", "parallel", "parallel")),
        name="s5ungroup",
    )(yf)


def _s5_mixer(uf, u_c, ops):
    win, wout, m, dec = ops
    g, b_ = uf.shape[0], uf.shape[1]
    npair = g // 2
    states = _s5_states(uf, win)
    ctx_states = _s5_states(_s5_regroup(u_c), win)
    dec_cols = jnp.broadcast_to(dec.transpose(1, 2, 3, 0, 4)[:, :, :, :, None], (2, 2, 1, npair, b_, LANES))
    dec_cols = dec_cols.reshape(2, 2, 1, npair * b_ * LANES)
    hprev = _s5_scan(ctx_states, states, dec_cols)
    return _s5_ungroup(_s5_outputs(uf, hprev, wout, m))


def _attn_kernel(nkv, qt_ref, k_ref, vx_ref, o_ref, s0_ref, s1_ref):
    hd = HEAD_DIM
    kvh = pl.program_id(1)
    tq = qt_ref.shape[2]
    gq = qt_ref.shape[1] // hd
    kw = k_ref.shape[3]
    vr = vx_ref.shape[3]
    own = (lax.broadcasted_iota(jnp.int32, (kw, tq), 0) >> (hd.bit_length() - 1)) == kvh
    qp = []
    for g in range(gq):
        qg = qt_ref[0, g * hd:(g + 1) * hd, :].astype(F32)
        q2 = jnp.concatenate([qg] * (kw // hd), axis=0)
        qp.append(jnp.where(own, q2, 0.0).astype(BF16))
    qp = jnp.concatenate(qp, axis=1)
    m0 = jnp.full((1, gq * tq), -jnp.inf, F32)
    acc0 = jnp.zeros((vr, gq * tq), F32)

    def scores(jb, s_ref):
        s = _dot(k_ref[0, jb], qp)
        s_ref[...] = s
        return jnp.max(s, axis=0, keepdims=True)

    def consume(jb, s_ref, cmax, m_prev, acc):
        m_new = jnp.maximum(m_prev, cmax)
        alpha = jnp.exp2(m_prev - m_new)
        p = jnp.exp2(s_ref[...] - m_new).astype(BF16)
        return m_new, alpha * acc + _dot(vx_ref[0, 0, jb], p)

    bufs = (s0_ref, s1_ref)

    def steps(j0, count, m, acc, cmax):
        for u in range(count):
            cnext = scores(j0 + u + 1, bufs[(u + 1) % 2])
            m, acc = consume(j0 + u, bufs[u % 2], cmax, m, acc)
            cmax = cnext
        return m, acc, cmax

    unroll = ATTN_STEPS_PER_TRIP
    trips = (nkv - 1) // unroll
    carry = lax.fori_loop(0, trips, lambda i, c: steps(unroll * i, unroll, *c), (m0, acc0, scores(0, s0_ref)))
    m, acc, cmax = steps(trips * unroll, (nkv - 1) - trips * unroll, *carry)
    m, acc = consume(nkv - 1, bufs[(nkv - 1) % 2], cmax, m, acc)
    o = acc[0:hd] / acc[hd:hd + 1]
    ot = jnp.concatenate([o[:, g * tq:(g + 1) * tq] for g in range(gq)], axis=0)
    o_ref[0] = ot.T.astype(BF16)


def _attention(qt, k4, vx, tq):
    b_, wq, l_ = qt.shape
    _, kvh, nkv, vr, tk = vx.shape
    kw = k4.shape[3]
    gw = wq // kvh
    return pl.pallas_call(
        functools.partial(_attn_kernel, nkv),
        out_shape=jax.ShapeDtypeStruct((b_, l_, wq), BF16),
        grid=(b_, kvh, l_ // tq),
        in_specs=[pl.BlockSpec((1, gw, tq), lambda b, h, i: (b, h, i)),
                  pl.BlockSpec((1, nkv, tk, kw), lambda b, h, i: (b, 0, 0, 0)),
                  pl.BlockSpec((1, 1, nkv, vr, tk), lambda b, h, i: (b, h, 0, 0, 0))],
        out_specs=pl.BlockSpec((1, tq, gw), lambda b, h, i: (b, i, h)),
        scratch_shapes=[pltpu.VMEM((tk, (gw // HEAD_DIM) * tq), F32)] * 2,
        compiler_params=_cparams(("parallel", "parallel", "parallel")),
        name="attn",
    )(qt, k4, vx)


def _merge_kernel(dm, ne, ys_ref, oa_ref, gl_ref, x_ref, mod_ref, wglu_ref, bglu_ref, wso_ref, wao_ref,
                  wout_ref, n2g_ref, rw_ref, rb_ref, utri_ref, xmix_ref, h2_ref, idx_ref, gate_ref, rank_ref,
                  cnt_out_ref, cnt_ref):
    tm = x_ref.shape[1]
    parts = [pl.ds(r * (tm // MERGE_ROW_PARTS), tm // MERGE_ROW_PARTS) for r in range(MERGE_ROW_PARTS)]
    g1 = mod_ref[0, :, 2 * dm:3 * dm]
    sh2 = mod_ref[0, :, 3 * dm:4 * dm]
    sc2 = mod_ref[0, :, 4 * dm:5 * dm]
    s = [jax.nn.gelu(ys_ref[0, r, :]) for r in parts]
    z = [_dot(sr.astype(BF16), wglu_ref[...]) + bglu_ref[...] for sr in s]
    s = [sr * jax.nn.sigmoid(zr) for sr, zr in zip(s, z)]
    a = [_dot(sr.astype(BF16), wso_ref[...]) for sr in s]
    bq = [_dot(oa_ref[0, r, :], wao_ref[...]) for r in parts]
    merged = []
    for r, ar, br in zip(parts, a, bq):
        gts = jax.nn.sigmoid(gl_ref[0, r, :].astype(F32))
        merged.append((gts[:, 0:dm] * ar + gts[:, dm:2 * dm] * br).astype(BF16))
    out = [_dot(mr, wout_ref[...]) for mr in merged]
    h2_parts = []
    for r, outr in zip(parts, out):
        xm = x_ref[0, r, :] + g1 * outr
        xmix_ref[0, r, :] = xm
        ms = jnp.mean(xm * xm, axis=-1, keepdims=True)
        h2r = xm * lax.rsqrt(ms + EPS) * n2g_ref[...] * (1.0 + sc2) + sh2
        h2_ref[0, r, :] = _pack2(h2r[:, 0:dm // 2], h2r[:, dm // 2:dm])
        h2_parts.append(h2r)
    logits = jnp.concatenate([_dot3(h2r, rw_ref[...]) for h2r in h2_parts], axis=0) + rb_ref[...]
    nep = cnt_ref.shape[0]
    l = logits.T[0:nep, :]
    sub = lax.broadcasted_iota(jnp.int32, l.shape, 0).astype(F32)
    neg = jnp.float32(-jnp.inf)
    vals, idxs = [], []
    for _ in range(TOP_K):
        mx = jnp.max(l, axis=0, keepdims=True)
        ix = jnp.min(jnp.where(l == mx, sub, float(nep)), axis=0, keepdims=True)
        vals.append(mx)
        idxs.append(ix)
        l = jnp.where(sub == ix, neg, l)
    es = [jnp.exp(v - vals[0]) for v in vals]
    den = es[0] + es[1] + es[2] + es[3]
    @pl.when((pl.program_id(0) == 0) & (pl.program_id(1) == 0))
    def _():
        cnt_ref[...] = jnp.zeros(cnt_ref.shape, F32)

    onehot = jnp.zeros(l.shape, F32)
    for k in range(TOP_K):
        onehot = onehot + jnp.where(sub == idxs[k], 1.0, 0.0)
    prior = _dot(onehot.astype(BF16), utri_ref[...]) + cnt_ref[...]
    cnt_ref[...] = cnt_ref[...] + jnp.sum(onehot, axis=1, keepdims=True)
    cnt_out_ref[...] = cnt_ref[...]
    krow = lax.broadcasted_iota(jnp.int32, (8, l.shape[1]), 0)
    idx_out = jnp.zeros(krow.shape, F32)
    gate_out = jnp.zeros(krow.shape, F32)
    rank_out = jnp.zeros(krow.shape, F32)
    for k in range(TOP_K):
        rk = jnp.sum(jnp.where(sub == idxs[k], prior, 0.0), axis=0, keepdims=True)
        idx_out = jnp.where(krow == k, idxs[k], idx_out)
        gate_out = jnp.where(krow == k, es[k] / den, gate_out)
        rank_out = jnp.where(krow == k, rk, rank_out)
    idx_ref[0] = idx_out.astype(jnp.int32)
    gate_ref[0] = gate_out
    rank_ref[0] = rank_out.astype(jnp.int32)


def _merge(y_ssm, o_attn, glog, x, mod3, w_glu, b_glu, w_ssm_out, w_attn_out, w_out, norm2_g,
           rw_pad, rb_pad, ne, tm):
    b_, l_, dm = x.shape
    wu = dm // 2
    tok = lambda w: pl.BlockSpec((1, tm, w), lambda b, i: (b, i, 0))
    full = lambda r, c: pl.BlockSpec((r, c), lambda b, i: (0, 0))
    utri = (jnp.arange(tm)[:, None] < jnp.arange(tm)[None, :]).astype(BF16)
    nep = -(-ne // 8) * 8
    kmaj = pl.BlockSpec((1, 8, tm), lambda b, i: (b, 0, i))
    return pl.pallas_call(
        functools.partial(_merge_kernel, dm, ne),
        out_shape=[jax.ShapeDtypeStruct((b_, l_, dm), F32),
                   jax.ShapeDtypeStruct((b_, l_, dm // 2), jnp.uint32),
                   jax.ShapeDtypeStruct((b_, 8, l_), jnp.int32),
                   jax.ShapeDtypeStruct((b_, 8, l_), F32),
                   jax.ShapeDtypeStruct((b_, 8, l_), jnp.int32),
                   jax.ShapeDtypeStruct((nep, 1), F32)],
        grid=(b_, l_ // tm),
        in_specs=[tok(wu), tok(wu), tok(2 * dm), tok(dm),
                  pl.BlockSpec((1, 1, N_MOD * dm), lambda b, i: (b, 0, 0)),
                  full(wu, wu), full(1, wu), full(wu, dm), full(wu, dm), full(dm, dm),
                  full(1, dm), full(dm, LANES), full(1, LANES), full(tm, tm)],
        out_specs=[tok(dm), tok(dm // 2), kmaj, kmaj, kmaj, full(nep, 1)],
        scratch_shapes=[pltpu.VMEM((nep, 1), F32)],
        compiler_params=_cparams(("arbitrary", "arbitrary")),
        name="merge",
    )(y_ssm, o_attn, glog, x, mod3, w_glu, b_glu, w_ssm_out, w_attn_out, w_out, norm2_g, rw_pad, rb_pad, utri)


def _expert_kernel(de, be_ref, nv_ref, x_ref, wgu_ref, bgu_ref, wd_ref, bd_ref, y_ref, wgu_b, wd_b):
    i = pl.program_id(0)
    n_valid = nv_ref[i]
    used = n_valid > 0
    new_expert = (i == 0) | (be_ref[i] != be_ref[jnp.maximum(i - 1, 0)])

    @pl.when(used & new_expert)
    def _():
        wgu_b[...] = wgu_ref[0].astype(BF16)
        wd_b[...] = wd_ref[0].astype(BF16)

    @pl.when(used)
    def _():
        row = lax.broadcasted_iota(jnp.int32, (x_ref.shape[0], 1), 0)
        x_lo, x_hi = _unpack2(jnp.where(row < n_valid, x_ref[...], jnp.uint32(0)))
        xb = jnp.concatenate([x_lo, x_hi], axis=1).astype(BF16)
        gu = _dot(xb, wgu_b[...]) + bgu_ref[0]
        gate = jnp.minimum(gu[:, 0:de], SWIGLU_LIMIT)
        up = jnp.clip(gu[:, de:2 * de], -SWIGLU_LIMIT, SWIGLU_LIMIT)
        act = (up + 1.0) * (gate * jax.nn.sigmoid(SWIGLU_ALPHA * gate))
        y = _dot(act.astype(BF16), wd_b[...]) + bd_ref[0]
        half = y.shape[1] // 2
        y_ref[...] = _pack2(y[:, 0:half], y[:, half:])

    @pl.when(jnp.logical_not(used))
    def _():
        y_ref[...] = jnp.zeros(y_ref.shape, y_ref.dtype)


def _experts(xb, block_expert, block_valid, w_gate_up, b_gate_up, w_down, b_down):
    n_slots, dmh = xb.shape
    dm = 2 * dmh
    ne, _, de2 = w_gate_up.shape
    de = de2 // 2
    nb = n_slots // EXPERT_BLOCK
    return pl.pallas_call(
        functools.partial(_expert_kernel, de),
        out_shape=jax.ShapeDtypeStruct((n_slots, dmh), jnp.uint32),
        grid_spec=pltpu.PrefetchScalarGridSpec(
            num_scalar_prefetch=2,
            grid=(nb,),
            in_specs=[pl.BlockSpec((EXPERT_BLOCK, dmh), lambda i, be, nu: (i, 0)),
                      pl.BlockSpec((1, dm, de2), lambda i, be, nu: (be[i], 0, 0)),
                      pl.BlockSpec((1, 1, de2), lambda i, be, nu: (be[i], 0, 0)),
                      pl.BlockSpec((1, de, dm), lambda i, be, nu: (be[i], 0, 0)),
                      pl.BlockSpec((1, 1, dm), lambda i, be, nu: (be[i], 0, 0))],
            out_specs=pl.BlockSpec((EXPERT_BLOCK, dmh), lambda i, be, nu: (i, 0)),
            scratch_shapes=[pltpu.VMEM((dm, de2), BF16), pltpu.VMEM((de, dm), BF16)],
        ),
        compiler_params=_cparams(("arbitrary",)),
        name="expert",
    )(block_expert, block_valid, xb, w_gate_up, b_gate_up.reshape(ne, 1, de2), w_down, b_down.reshape(ne, 1, dm))


def _sc_gather_rows(table, idx):
    n = idx.shape[0]
    w = table.shape[1]
    mesh = plsc.VectorSubcoreMesh(core_axis_name="c", subcore_axis_name="s")
    nc, nw = mesh.num_cores, mesh.num_cores * mesh.num_subcores
    win = SC_GATHER_WINDOW
    assert n % (win * nw) == 0
    per_worker = n // nw

    @functools.partial(
        pl.kernel, out_type=jax.ShapeDtypeStruct((n, w), table.dtype), mesh=mesh,
        scratch_types=[pltpu.VMEM((win,), jnp.int32), pltpu.VMEM((win, w), table.dtype),
                       pltpu.SemaphoreType.DMA])
    def gather(x_hbm, i_hbm, o_hbm, idx_v, rows_v, sem):
        base = (lax.axis_index("s") * nc + lax.axis_index("c")) * per_worker

        @pl.loop(0, per_worker // win)
        def _(j):
            off = base + j * win
            pltpu.sync_copy(i_hbm.at[pl.ds(off, win)], idx_v)
            pltpu.async_copy(x_hbm.at[idx_v], rows_v, sem).wait()
            pltpu.sync_copy(rows_v, o_hbm.at[pl.ds(off, win)])

    return gather(table, idx)


def _sc_scatter_rows(rows, dest, n_out):
    t_, w = rows.shape
    kk = dest.shape[0] // t_
    mesh = plsc.VectorSubcoreMesh(core_axis_name="c", subcore_axis_name="s")
    nc, nw = mesh.num_cores, mesh.num_cores * mesh.num_subcores
    win = SC_GATHER_WINDOW
    assert t_ % (win * nw) == 0
    per_worker = t_ // nw

    @functools.partial(
        pl.kernel, out_type=jax.ShapeDtypeStruct((n_out, w), rows.dtype), mesh=mesh,
        scratch_types=[pltpu.VMEM((win,), jnp.int32)] * kk + [pltpu.VMEM((win, w), rows.dtype),
                                                              pltpu.SemaphoreType.DMA])
    def scatter(x_hbm, d_hbm, o_hbm, *scratch):
        idx_vs, rows_v, sem = scratch[:kk], scratch[kk], scratch[kk + 1]
        base = (lax.axis_index("s") * nc + lax.axis_index("c")) * per_worker

        @pl.loop(0, per_worker // win)
        def _(j):
            off = base + j * win
            pltpu.sync_copy(x_hbm.at[pl.ds(off, win)], rows_v)
            for k in range(kk):
                pltpu.sync_copy(d_hbm.at[pl.ds(k * t_ + off, win)], idx_vs[k])
            copies = [pltpu.async_copy(rows_v, o_hbm.at[idx_vs[k]], sem) for k in range(kk)]
            for cp in copies:
                cp.wait()

    return scatter(rows, dest)


def _final_kernel(dm, yg_ref, gate_ref, xm_ref, mod_ref, fg_ref, o_ref):
    gates = gate_ref[0]
    f_lo = f_hi = None
    for k in range(TOP_K):
        y_lo, y_hi = _unpack2(yg_ref[k, 0])
        gk = gates[:, k:k + 1]
        f_lo = gk * y_lo if f_lo is None else f_lo + gk * y_lo
        f_hi = gk * y_hi if f_hi is None else f_hi + gk * y_hi
    f = jnp.concatenate([f_lo, f_hi], axis=1)
    g2 = mod_ref[0, :, 5 * dm:6 * dm]
    xo = xm_ref[0] + g2 * f
    ms = jnp.mean(xo * xo, axis=-1, keepdims=True)
    o_ref[0] = xo * lax.rsqrt(ms + EPS) * fg_ref[...]


def _final(yg, gates, x_mix, mod3, final_g, tm):
    b_, l_, dm = x_mix.shape
    return pl.pallas_call(
        functools.partial(_final_kernel, dm),
        out_shape=jax.ShapeDtypeStruct((b_, l_, dm), F32),
        grid=(b_, l_ // tm),
        in_specs=[pl.BlockSpec((TOP_K, 1, tm, dm // 2), lambda b, i: (0, b, i, 0)),
                  pl.BlockSpec((1, tm, LANES), lambda b, i: (b, i, 0)),
                  pl.BlockSpec((1, tm, dm), lambda b, i: (b, i, 0)),
                  pl.BlockSpec((1, 1, N_MOD * dm), lambda b, i: (b, 0, 0)),
                  pl.BlockSpec((1, dm), lambda b, i: (0, 0))],
        out_specs=pl.BlockSpec((1, tm, dm), lambda b, i: (b, i, 0)),
        compiler_params=_cparams(("parallel", "parallel")),
        name="final",
    )(yg, gates, x_mix, mod3, final_g.reshape(1, dm))


def _rope_tables(l_):
    half = HEAD_DIM // 2
    inv_freq = ROPE_THETA ** (-jnp.arange(0, half, 2, dtype=F32) / half)
    t = jnp.arange(l_, dtype=jnp.int32)
    row_id, col_id = t // GRID_W, t % GRID_W
    ang = jnp.concatenate([row_id.astype(F32)[:, None] * inv_freq,
                           col_id.astype(F32)[:, None] * inv_freq], axis=-1)
    cos = jnp.repeat(jnp.cos(ang), 2, axis=-1)
    sin = jnp.repeat(jnp.sin(ang), 2, axis=-1)
    sign = jnp.tile(jnp.array([-1.0, 1.0], F32), HEAD_DIM // 2)
    reps = LANES // HEAD_DIM
    return jnp.tile(cos, (1, reps)), jnp.tile(sin * sign, (1, reps))


def _routing(idx, rank, counts, t_, ne):
    m_ = t_ * TOP_K
    blk = EXPERT_BLOCK
    nb = (m_ + ne * (blk - 1)) // blk
    nb = -(-nb // SLOT_BLOCK_MULTIPLE) * SLOT_BLOCK_MULTIPLE
    blk_counts = (counts + blk - 1) // blk
    blk_ends = jnp.cumsum(blk_counts)
    first_slot = (blk_ends - blk_counts) * blk
    onehot = idx[:, :, None] == jnp.arange(ne, dtype=jnp.int32)[None, None, :]
    dest = rank + jnp.sum(jnp.where(onehot, first_slot[None, None, :], 0), axis=2)
    blocks = jnp.arange(nb, dtype=jnp.int32)
    block_expert = jnp.minimum(jnp.sum(blocks[:, None] >= blk_ends[None, :], axis=1), ne - 1)
    in_expert = blocks * blk - first_slot[block_expert]
    block_valid = jnp.clip(counts[block_expert] - in_expert, 0, blk)
    block_valid = jnp.where(blocks < blk_ends[-1], block_valid, 0)
    return dest.astype(jnp.int32), block_expert.astype(jnp.int32), block_valid.astype(jnp.int32), nb * blk


def kernel(x, c, ctx, c_ctx, w_mod, b_mod, norm1_g, norm2_g, w_in, s5_lam_re, s5_lam_im, s5_log_step,
           s5_b_re, s5_b_im, s5_c_re, s5_c_im, s5_d, w_glu, b_glu, w_ssm_out, q_norm_g, k_norm_g,
           w_attn_out, w_out, router_w, router_b, w_gate_up, b_gate_up, w_down, b_down, final_norm_g):
    b_, l_, dm = x.shape
    lc = ctx.shape[1]
    depth = w_mod.shape[0]
    assert depth == 1, "single-layer block"
    assert b_ <= 7 and l_ % 512 == 0 and lc % S5_CHUNK == 0
    wu = dm // 2
    wkv = wu // GQA_GROUP
    kvh = wkv // HEAD_DIM
    ne = router_w.shape[-1]
    i = 0

    c8 = jnp.zeros((8, dm), F32).at[:b_].set(c).at[b_].set(c_ctx)
    mod3 = _modulation(c8, w_mod[i], b_mod[i]).reshape(8, 1, N_MOD * dm)

    w_in_b = w_in[i].astype(BF16)
    cos, sin = _rope_tables(l_)
    hsum = jnp.kron(jnp.eye(wu // HEAD_DIM, dtype=F32), jnp.ones((HEAD_DIM, HEAD_DIM), F32)).astype(BF16)
    qg = jnp.tile(q_norm_g[i], wu // HEAD_DIM).reshape(1, wu)
    kg = jnp.tile(k_norm_g[i], wkv // HEAD_DIM).reshape(1, wkv)
    uf, k, vt, qt, glog = _inproj(x, mod3, None, norm1_g[i], w_in_b, cos, sin, qg, kg, hsum, True, 512)
    ones_t = jnp.ones((lc, LANES), F32)
    u_c, k_c, vt_c = _inproj(ctx, mod3, b_, norm1_g[i], w_in_b[:, :wu + 2 * wkv], ones_t, 0.0 * ones_t,
                             qg, kg, hsum, False, lc)

    ops = _s5_operators(s5_lam_re[i], s5_lam_im[i], s5_log_step[i], s5_b_re[i], s5_b_im[i],
                        s5_c_re[i], s5_c_im[i], s5_d[i])
    y_ssm = _s5_mixer(uf, u_c, ops)

    s_all = lc + l_
    tk = max(t for t in range(LANES, ATTN_KEY_CHUNK + 1, LANES) if s_all % t == 0)
    nkv = s_all // tk
    k4 = jnp.concatenate([k_c, k], axis=1).reshape(b_, nkv, tk, wkv)
    vt_all = jnp.concatenate([vt_c, vt], axis=2).reshape(b_, kvh, HEAD_DIM, s_all)
    vx = jnp.concatenate([vt_all, jnp.ones((b_, kvh, 1, s_all), BF16),
                          jnp.zeros((b_, kvh, ATTN_V_ROWS - HEAD_DIM - 1, s_all), BF16)], axis=2)
    vx = vx.reshape(b_, kvh, ATTN_V_ROWS, nkv, tk).transpose(0, 1, 3, 2, 4)
    o_attn = _attention(qt, k4, vx, ATTN_Q_BLOCK)

    rw_pad = jnp.zeros((dm, LANES), F32).at[:, :ne].set(router_w[i])
    rb_pad = jnp.full((1, LANES), -jnp.inf, F32).at[0, :ne].set(router_b[i])
    x_mix, h2, idx, gates, rank, counts = _merge(
        y_ssm, o_attn, glog, x, mod3, w_glu[i].astype(BF16), b_glu[i].reshape(1, wu),
        w_ssm_out[i].astype(BF16), w_attn_out[i].astype(BF16), w_out[i].astype(BF16),
        norm2_g[i].reshape(1, dm), rw_pad, rb_pad, ne, 512)

    t_ = b_ * l_
    kmajor = lambda a: a[:, :TOP_K, :].transpose(1, 0, 2).reshape(TOP_K, t_)
    dest_kmajor, block_expert, block_valid, n_slots = _routing(
        kmajor(idx), kmajor(rank), counts[:ne, 0].astype(jnp.int32), t_, ne)
    dest_kmajor = dest_kmajor.reshape(-1)
    xb = _sc_scatter_rows(h2.reshape(t_, dm // 2), dest_kmajor, n_slots)
    yb = _experts(xb, block_expert, block_valid, w_gate_up[i], b_gate_up[i], w_down[i], b_down[i])
    yg = _sc_gather_rows(yb, dest_kmajor).reshape(TOP_K, b_, l_, dm // 2)
    gates_tok = jnp.pad(gates.transpose(0, 2, 1), ((0, 0), (0, 0), (0, LANES - 8)))
    return _final(yg, gates_tok, x_mix, mod3, final_norm_g, 512)
```

```python
import functools
import math

import jax
import jax.numpy as jnp
from jax import lax
from jax.experimental import pallas as pl
from jax.experimental.pallas import tpu as pltpu
from jax.experimental.pallas import tpu_sc as plsc

F32 = jnp.float32
BF16 = jnp.bfloat16

EPS = 1e-6
GRID_W = 64
N_MOD = 6
SSM_GROUP = 16
SSM_STATE = 64
HEAD_DIM = 64
GQA_GROUP = 4
ROPE_THETA = 10000.0
TOP_K = 4
SWIGLU_ALPHA = 1.702
SWIGLU_LIMIT = 7.0
LOG2E = 1.4426950408889634

LANES = 128
S5_CHUNK = 32
S5_POW_ROWS = 56
EXPERT_BLOCK = 512
ATTN_V_ROWS = 80
ATTN_KEY_CHUNK = 384
ATTN_Q_BLOCK = 256
ATTN_STEPS_PER_TRIP = 10
MERGE_ROW_PARTS = 2
SC_GATHER_WINDOW = 128
SC_WORKERS = 32
SLOT_BLOCK_MULTIPLE = SC_GATHER_WINDOW * SC_WORKERS // EXPERT_BLOCK
VMEM_LIMIT = 56 * 1024 * 1024


def _cparams(sem, flags=None):
    return pltpu.CompilerParams(dimension_semantics=sem, vmem_limit_bytes=VMEM_LIMIT, flags=flags)


def _dot(a, b):
    return jnp.dot(a, b, preferred_element_type=F32)


def _split(a):
    hi = a.astype(BF16)
    lo = (a - hi.astype(F32)).astype(BF16)
    return hi, lo


def _pack2(lo, hi):
    lo_w = lax.bitcast_convert_type(lo.astype(BF16).astype(F32), jnp.uint32) >> 16
    hi_w = lax.bitcast_convert_type(hi.astype(BF16).astype(F32), jnp.uint32) & jnp.uint32(0xFFFF0000)
    return lo_w | hi_w


def _unpack2(words):
    lo = lax.bitcast_convert_type(words << 16, F32)
    hi = lax.bitcast_convert_type(words & jnp.uint32(0xFFFF0000), F32)
    return lo, hi


def _dot3(a, b):
    ah, al = _split(a)
    bh, bl = _split(b)
    return _dot(ah, bh) + _dot(ah, bl) + _dot(al, bh)


def _mod_kernel(c_ref, w_ref, b_ref, o_ref):
    c = c_ref[...]
    s = c * jax.nn.sigmoid(c)
    o_ref[...] = _dot3(s, w_ref[...]) + b_ref[...]


def _modulation(c8, w_mod, b_mod):
    d = c8.shape[1]
    n = w_mod.shape[1]
    tn = 1024
    return pl.pallas_call(
        _mod_kernel,
        out_shape=jax.ShapeDtypeStruct((8, n), F32),
        grid=(n // tn,),
        in_specs=[pl.BlockSpec((8, d), lambda j: (0, 0)),
                  pl.BlockSpec((d, tn), lambda j: (0, j)),
                  pl.BlockSpec((1, tn), lambda j: (0, j))],
        out_specs=pl.BlockSpec((8, tn), lambda j: (0, j)),
        compiler_params=_cparams(("parallel",)),
        name="mod",
    )(c8, w_mod, b_mod.reshape(1, n))


def _head_norm_rope(z, gain, hsum, cos, sin, scale):
    ss = _dot((z * z).astype(BF16), hsum)
    zn = z * lax.rsqrt(ss * (1.0 / HEAD_DIM) + EPS) * gain
    lane = lax.broadcasted_iota(jnp.int32, (1, LANES), 1)
    even = (lane & 1) == 0
    outs = []
    for ci in range(z.shape[1] // LANES):
        ch = zn[:, ci * LANES:(ci + 1) * LANES]
        nxt = pltpu.roll(ch, LANES - 1, axis=1)
        prv = pltpu.roll(ch, 1, axis=1)
        sw = jnp.where(even, nxt, prv)
        outs.append((ch * cos + sw * sin) * scale)
    return outs


def _inproj_kernel(with_q, dm, x_ref, mod_ref, g_ref, w_ref, cos_ref, sin_ref, qg_ref, kg_ref,
                   hsum_ref, *out_refs):
    x = x_ref[0]
    ms = jnp.mean(x * x, axis=-1, keepdims=True)
    y = x * lax.rsqrt(ms + EPS) * g_ref[...]
    sh = mod_ref[0, :, 0:dm]
    sc = mod_ref[0, :, dm:2 * dm]
    hb = (y * (1.0 + sc) + sh).astype(BF16)
    cos = cos_ref[...]
    sin = sin_ref[...]
    wu = dm // 2
    wkv = wu // GQA_GROUP
    c_k, c_v, c_q, c_g = wu, wu + wkv, wu + 2 * wkv, 2 * wu + 2 * wkv
    u_ref, k_ref, vt_ref = out_refs[:3]
    u = _dot(hb, w_ref[:, 0:c_k])
    if with_q:
        ubuf = out_refs[-1]
        p, gpt, nj = SSM_GROUP, LANES // SSM_GROUP, x.shape[0] // S5_CHUNK
        for ci in range(wu // LANES):
            ubuf[ci] = u[:, ci * LANES:(ci + 1) * LANES]
        for s in range(S5_CHUNK):
            for ci in range(wu // LANES):
                xs = ubuf[ci, pl.ds(s, nj, stride=S5_CHUNK), :].astype(BF16)
                for gi in range(gpt):
                    u_ref[ci * gpt + gi, 0, :, s * p:(s + 1) * p] = xs[:, gi * p:(gi + 1) * p]
    else:
        u_ref[0] = u
    kz = _dot(hb, w_ref[:, c_k:c_v])
    (kr,) = _head_norm_rope(kz, kg_ref[...], hsum_ref[0:wkv, 0:wkv], cos, sin, 1.0)
    k_ref[0] = kr.astype(BF16)
    vt_ref[0] = _dot(hb, w_ref[:, c_v:c_q]).T.astype(BF16)
    if with_q:
        qt_ref, gl_ref = out_refs[3:5]
        qz = _dot(hb, w_ref[:, c_q:c_g])
        qs = _head_norm_rope(qz, qg_ref[...], hsum_ref[...], cos, sin, HEAD_DIM ** -0.5 * LOG2E)
        for ci, qc in enumerate(qs):
            qt_ref[0, ci * LANES:(ci + 1) * LANES, :] = qc.T.astype(BF16)
        gl_ref[0] = _dot(hb, w_ref[:, c_g:]).astype(BF16)


def _inproj(x, mod3, mod_row0, norm_g, w_in_b, cos, sin, qg, kg, hsum, with_q, tm):
    b_, l_, dm = x.shape
    wu = dm // 2
    wkv = wu // GQA_GROUP
    n_in = w_in_b.shape[1]
    outs = [jax.ShapeDtypeStruct((b_, l_, wu), F32),
            jax.ShapeDtypeStruct((b_, l_, wkv), BF16),
            jax.ShapeDtypeStruct((b_, wkv, l_), BF16)]
    ospecs = [pl.BlockSpec((1, tm, wu), lambda b, i: (b, i, 0)),
              pl.BlockSpec((1, tm, wkv), lambda b, i: (b, i, 0)),
              pl.BlockSpec((1, wkv, tm), lambda b, i: (b, 0, i))]
    scratch = []
    if with_q:
        g, rows = wu // SSM_GROUP, S5_CHUNK * SSM_GROUP
        outs[0] = jax.ShapeDtypeStruct((g, b_, l_ // S5_CHUNK, rows), BF16)
        ospecs[0] = pl.BlockSpec((g, 1, tm // S5_CHUNK, rows), lambda b, i: (0, b, i, 0))
        outs += [jax.ShapeDtypeStruct((b_, wu, l_), BF16),
                 jax.ShapeDtypeStruct((b_, l_, 2 * dm), BF16)]
        ospecs += [pl.BlockSpec((1, wu, tm), lambda b, i: (b, 0, i)),
                   pl.BlockSpec((1, tm, 2 * dm), lambda b, i: (b, i, 0))]
        scratch = [pltpu.VMEM((wu // LANES, tm, LANES), F32)]
    if mod_row0 is None:
        mod_map = lambda b, i: (b, 0, 0)
    else:
        mod_map = lambda b, i: (mod_row0, 0, 0)
    return pl.pallas_call(
        functools.partial(_inproj_kernel, with_q, dm),
        out_shape=outs,
        grid=(b_, l_ // tm),
        in_specs=[pl.BlockSpec((1, tm, dm), lambda b, i: (b, i, 0)),
                  pl.BlockSpec((1, 1, N_MOD * dm), mod_map),
                  pl.BlockSpec((1, dm), lambda b, i: (0, 0)),
                  pl.BlockSpec((dm, n_in), lambda b, i: (0, 0)),
                  pl.BlockSpec((tm, LANES), lambda b, i: (i, 0)),
                  pl.BlockSpec((tm, LANES), lambda b, i: (i, 0)),
                  pl.BlockSpec((1, wu), lambda b, i: (0, 0)),
                  pl.BlockSpec((1, wkv), lambda b, i: (0, 0)),
                  pl.BlockSpec((wu, wu), lambda b, i: (0, 0))],
        out_specs=ospecs,
        scratch_shapes=scratch,
        compiler_params=_cparams(("parallel", "parallel")),
        name="inproj_x" if with_q else "inproj_ctx",
    )(x, mod3, norm_g.reshape(1, dm), w_in_b, cos, sin, qg, kg, hsum)


def _cexp_pow(e, lr, li):
    mag = jnp.exp(e * lr)
    ang = e * li
    return mag * jnp.cos(ang), mag * jnp.sin(ang)


def _cmul(ar, ai, br, bi):
    return ar * br - ai * bi, ar * bi + ai * br


def _s5pre_kernel(lamr_r, lami_r, lst_r, lamr_c, lami_c, lst_c, btr_ref, bti_ref, ctr_ref, cti_ref,
                  d_ref, win_ref, wout_ref, m_ref, dec_ref):
    tc = S5_CHUNK
    half = tc // 2
    p = SSM_GROUP
    rows = tc * p
    gsh = SSM_GROUP.bit_length() - 1
    nsh = SSM_STATE.bit_length() - 1
    row_i = lax.broadcasted_iota(jnp.int32, (rows, rows), 0)
    col_i = lax.broadcasted_iota(jnp.int32, (rows, rows), 1)
    lane_g = lax.broadcasted_iota(jnp.int32, (1, LANES), 1) >> nsh
    subl_g = lax.broadcasted_iota(jnp.int32, (LANES, 1), 0) >> nsh
    e_rows = lax.broadcasted_iota(jnp.int32, (S5_POW_ROWS, 1), 0) - half
    e_lanes = lax.broadcasted_iota(jnp.int32, (1, LANES), 1) - half
    sel_e = lax.broadcasted_iota(jnp.int32, (LANES, rows), 0) - half
    sel_t = lax.broadcasted_iota(jnp.int32, (LANES, rows), 1) >> gsh
    sel_p = (lax.broadcasted_iota(jnp.int32, (LANES, rows), 0)
             == (lax.broadcasted_iota(jnp.int32, (LANES, rows), 1) & (p - 1))).astype(BF16)

    def spread_lanes(table, sel):
        hi, lo = _split(table)
        return _dot(hi, sel) + _dot(lo, sel)

    def spread_rows(table, e_of_s):
        return jnp.concatenate([jnp.broadcast_to(table[e_of_s(s) + half:e_of_s(s) + half + 1, :], (p, LANES))
                                for s in range(tc)], axis=0)

    m_acc = [jnp.where(row_i == col_i, d_ref[0, gl], 0.0) for gl in range(2)]
    for dr in range(2):
        step = jnp.exp(lst_r[dr, 0])
        lam_r, lam_i = lamr_r[dr, 0], lami_r[dr, 0]
        lr, li = lam_r * step, lam_i * step
        pw_r, pw_i = _cexp_pow(e_rows.astype(F32), lr, li)
        lb_r, lb_i = pw_r[half + 1:half + 2, :], pw_i[half + 1:half + 2, :]
        den = lam_r * lam_r + lam_i * lam_i
        nr, ni = lb_r - 1.0, lb_i
        cf_r = (nr * lam_r + ni * lam_i) / den
        cf_i = (ni * lam_r - nr * lam_i) / den
        bt_r = jnp.concatenate([btr_ref[dr, 0]] * tc, axis=0)
        bt_i = jnp.concatenate([bti_ref[dr, 0]] * tc, axis=0)
        bb_r, bb_i = _cmul(cf_r, cf_i, bt_r, bt_i)
        e_in = (lambda s: tc - 1 - s) if dr == 0 else (lambda s: s)
        wi_r, wi_i = _cmul(bb_r, bb_i, spread_rows(pw_r, e_in), spread_rows(pw_i, e_in))
        e_a = (lambda s: half - s) if dr == 0 else (lambda s: s - half)
        a_r, a_i = _cmul(bb_r, bb_i, spread_rows(pw_r, e_a), spread_rows(pw_i, e_a))
        dec_ref[0, dr, 0] = pw_r[tc + half:tc + half + 1, :]
        dec_ref[0, dr, 1] = pw_i[tc + half:tc + half + 1, :]
        stepc = jnp.exp(lst_c[dr, 0])
        lrc, lic = lamr_c[dr, 0] * stepc, lami_c[dr, 0] * stepc
        qw_r, qw_i = _cexp_pow(e_lanes.astype(F32), lrc, lic)
        qw_r = jnp.where(e_lanes <= tc, qw_r, 0.0)
        qw_i = jnp.where(e_lanes <= tc, qw_i, 0.0)
        c_r, c_i = spread_lanes(ctr_ref[dr, 0], sel_p), spread_lanes(cti_ref[dr, 0], sel_p)
        sel_b = (sel_e == (sel_t - half if dr == 0 else half - sel_t)).astype(BF16)
        bm_r, bm_i = _cmul(c_r, c_i, spread_lanes(qw_r, sel_b), spread_lanes(qw_i, sel_b))
        sel_o = (sel_e == (sel_t + 1 if dr == 0 else tc - sel_t)).astype(BF16)
        wo_r, wo_i = _cmul(c_r, c_i, spread_lanes(qw_r, sel_o), spread_lanes(qw_i, sel_o))
        if dr == 0:
            mask = (col_i >> gsh) >= (row_i >> gsh)
        else:
            mask = (row_i >> gsh) >= (col_i >> gsh)
        for gl in range(2):
            lsel = lane_g == gl
            ssel = subl_g == gl
            win_ref[0, gl, 2 * dr] = jnp.where(lsel, wi_r, 0.0).astype(BF16)
            win_ref[0, gl, 2 * dr + 1] = jnp.where(lsel, wi_i, 0.0).astype(BF16)
            wout_ref[0, 2 * dr, :, gl * rows:(gl + 1) * rows] = jnp.where(ssel, wo_r, 0.0).astype(BF16)
            wout_ref[0, 2 * dr + 1, :, gl * rows:(gl + 1) * rows] = jnp.where(ssel, -wo_i, 0.0).astype(BF16)
            kmat = (_dot3(jnp.where(lsel, a_r, 0.0), bm_r) - _dot3(jnp.where(lsel, a_i, 0.0), bm_i))
            m_acc[gl] = m_acc[gl] + jnp.where(mask, kmat, 0.0)
    for gl in range(2):
        m_ref[0, gl] = m_acc[gl].astype(BF16)


def _s5_operators(lam_re, lam_im, log_step, b_re, b_im, c_re, c_im, d_skip):
    g = lam_re.shape[1]
    n, p = SSM_STATE, SSM_GROUP
    npair = g // 2
    tc = S5_CHUNK
    rows = tc * p

    def row_form(a):
        return a.reshape(2, npair, 1, 2 * n)

    def col_form(a):
        return a.reshape(2, npair, 2 * n, 1)

    lst = jnp.broadcast_to(log_step[:, :, None], (2, g, n))
    bt = lambda b: b.reshape(2, npair, 2, n, p).transpose(0, 1, 4, 2, 3).reshape(2, npair, p, 2 * n)
    ct = lambda c: jnp.pad(c.reshape(2, npair, 2, p, n).transpose(0, 1, 2, 4, 3).reshape(2, npair, 2 * n, p),
                           ((0, 0), (0, 0), (0, 0), (0, LANES - p)))
    d2 = jnp.broadcast_to(d_skip.reshape(npair, 2, 1, 1, p), (npair, 2, 1, tc, p)).reshape(npair, 2, 1, rows)

    def spec4(shape):
        return pl.BlockSpec((2, 1) + shape, lambda i: (0, i, 0, 0))

    return pl.pallas_call(
        _s5pre_kernel,
        out_shape=[jax.ShapeDtypeStruct((npair, 2, 4, rows, LANES), BF16),
                   jax.ShapeDtypeStruct((npair, 4, LANES, 2 * rows), BF16),
                   jax.ShapeDtypeStruct((npair, 2, rows, rows), BF16),
                   jax.ShapeDtypeStruct((npair, 2, 2, 1, LANES), F32)],
        grid=(npair,),
        in_specs=[spec4((1, LANES)), spec4((1, LANES)), spec4((1, LANES)),
                  spec4((LANES, 1)), spec4((LANES, 1)), spec4((LANES, 1)),
                  spec4((p, LANES)), spec4((p, LANES)),
                  spec4((LANES, LANES)), spec4((LANES, LANES)),
                  pl.BlockSpec((1, 2, 1, rows), lambda i: (i, 0, 0, 0))],
        out_specs=[pl.BlockSpec((1, 2, 4, rows, LANES), lambda i: (i, 0, 0, 0, 0)),
                   pl.BlockSpec((1, 4, LANES, 2 * rows), lambda i: (i, 0, 0, 0)),
                   pl.BlockSpec((1, 2, rows, rows), lambda i: (i, 0, 0, 0)),
                   pl.BlockSpec((1, 2, 2, 1, LANES), lambda i: (i, 0, 0, 0, 0))],
        compiler_params=_cparams(("parallel",)),
        name="s5pre",
    )(row_form(lam_re), row_form(lam_im), row_form(lst),
      col_form(lam_re), col_form(lam_im), col_form(lst),
      bt(b_re), bt(b_im), ct(c_re), ct(c_im), d2)


def _s5state_kernel(u_ref, win_ref, *s_refs):
    for b in range(u_ref.shape[1]):
        for kind in range(4):
            s_refs[kind][:, b * LANES:(b + 1) * LANES] = (_dot(u_ref[0, b], win_ref[0, 0, kind])
                                                          + _dot(u_ref[1, b], win_ref[0, 1, kind]))


def _s5_states(uf, win):
    g, b_, j, rows = uf.shape
    npair = g // 2
    out = jax.ShapeDtypeStruct((j, npair * b_ * LANES), F32)
    ospec = pl.BlockSpec((j, b_ * LANES), lambda pr: (0, pr))
    return pl.pallas_call(
        _s5state_kernel,
        out_shape=[out] * 4,
        grid=(npair,),
        in_specs=[pl.BlockSpec((2, b_, j, rows), lambda pr: (pr, 0, 0, 0)),
                  pl.BlockSpec((1, 2, 4, rows, LANES), lambda pr: (pr, 0, 0, 0, 0))],
        out_specs=[ospec] * 4,
        compiler_params=_cparams(("parallel",)),
        name="s5state",
    )(uf, win)


def _s5scan_kernel(cfr, cfi, cbr, cbi, sfr, sfi, sbr, sbi, dec_ref, hfr, hfi, hbr, hbi):
    jc, jl, w = cfr.shape[0], sfr.shape[0], sfr.shape[1]
    zero = jnp.zeros((1, w), F32)

    def run(sr, si, outs, ar, ai, n, reverse, carry):
        def body(i, hc):
            row = (n - 1 - i) if reverse else i
            hr, hi = hc
            if outs is not None:
                outs[0][pl.ds(row, 1), :] = hr
                outs[1][pl.ds(row, 1), :] = hi
            nr = ar * hr - ai * hi + sr[pl.ds(row, 1), :]
            ni = ar * hi + ai * hr + si[pl.ds(row, 1), :]
            return nr, ni
        return lax.fori_loop(0, n, body, carry)

    afr, afi = dec_ref[0, 0], dec_ref[0, 1]
    abr, abi = dec_ref[1, 0], dec_ref[1, 1]
    c = run(cfr, cfi, None, afr, afi, jc, False, (zero, zero))
    run(sfr, sfi, (hfr, hfi), afr, afi, jl, False, c)
    c = run(cbr, cbi, None, abr, abi, jc, True, (zero, zero))
    run(sbr, sbi, (hbr, hbi), abr, abi, jl, True, c)


def _s5_scan(ctx_states, states, dec_cols):
    jl, wtot = states[0].shape
    jc = ctx_states[0].shape[0]
    tw = 1024
    spec = pl.BlockSpec((jl, tw), lambda i: (0, i))
    cspec = pl.BlockSpec((jc, tw), lambda i: (0, i))
    return pl.pallas_call(
        _s5scan_kernel,
        out_shape=[jax.ShapeDtypeStruct((jl, wtot), F32)] * 4,
        grid=(wtot // tw,),
        in_specs=[cspec] * 4 + [spec] * 4 + [pl.BlockSpec((2, 2, 1, tw), lambda i: (0, 0, 0, i))],
        out_specs=[spec] * 4,
        compiler_params=_cparams(("parallel",)),
        name="s5scan",
    )(*ctx_states, *states, dec_cols)


def _s5out_kernel(u_ref, hfr, hfi, hbr, hbi, wout_ref, m_ref, y_ref):
    rows = u_ref.shape[3]
    for b in range(u_ref.shape[1]):
        y2 = None
        for kind, h in enumerate((hfr, hfi, hbr, hbi)):
            t = _dot(h[:, b * LANES:(b + 1) * LANES].astype(BF16), wout_ref[0, kind])
            y2 = t if y2 is None else y2 + t
        for gl in range(2):
            y = y2[:, gl * rows:(gl + 1) * rows] + _dot(u_ref[gl, b], m_ref[0, gl])
            y_ref[gl, b] = y.astype(BF16)


def _s5_outputs(uf, hprev, wout, m):
    g, b_, j, rows = uf.shape
    npair = g // 2
    hspec = pl.BlockSpec((j, b_ * LANES), lambda pr: (0, pr))
    return pl.pallas_call(
        _s5out_kernel,
        out_shape=jax.ShapeDtypeStruct((g, b_, j, rows), BF16),
        grid=(npair,),
        in_specs=[pl.BlockSpec((2, b_, j, rows), lambda pr: (pr, 0, 0, 0))] + [hspec] * 4 +
                 [pl.BlockSpec((1, 4, LANES, 2 * rows), lambda pr: (pr, 0, 0, 0)),
                  pl.BlockSpec((1, 2, rows, rows), lambda pr: (pr, 0, 0, 0))],
        out_specs=pl.BlockSpec((2, b_, j, rows), lambda pr: (pr, 0, 0, 0)),
        compiler_params=_cparams(("parallel",)),
        name="s5out",
    )(uf, *hprev, wout, m)


def _chunks_per_step(j):
    return max(n for n in (64, 32, 16, 8) if j % n == 0)


def _s5_regroup_kernel(nj, u_ref, o_ref):
    tc, p = S5_CHUNK, SSM_GROUP
    for s in range(tc):
        xs = u_ref[0, pl.ds(s, nj, stride=tc), :].astype(o_ref.dtype)
        for gi in range(LANES // p):
            o_ref[gi, 0, :, s * p:(s + 1) * p] = xs[:, gi * p:(gi + 1) * p]


def _s5_regroup(u):
    b_, l_, w = u.shape
    tc, p = S5_CHUNK, SSM_GROUP
    gpt = LANES // p
    j = l_ // tc
    nj = _chunks_per_step(j)
    return pl.pallas_call(
        functools.partial(_s5_regroup_kernel, nj),
        out_shape=jax.ShapeDtypeStruct((w // p, b_, j, tc * p), BF16),
        grid=(b_, j // nj, w // LANES),
        in_specs=[pl.BlockSpec((1, tc * nj, LANES), lambda b, i, c: (b, i, c))],
        out_specs=pl.BlockSpec((gpt, 1, nj, tc * p), lambda b, i, c: (c, b, i, 0)),
        compiler_params=_cparams(("parallel", "parallel", "parallel")),
        name="s5regroup",
    )(u)


def _s5_ungroup_kernel(nj, y_ref, o_ref):
    tc, p = S5_CHUNK, SSM_GROUP
    for t in range(tc):
        row = jnp.concatenate([y_ref[gi, 0, :, t * p:(t + 1) * p] for gi in range(LANES // p)], axis=1)
        o_ref[0, pl.ds(t, nj, stride=tc), :] = row.astype(F32)


def _s5_ungroup(yf):
    g, b_, j, rows = yf.shape
    tc, p = S5_CHUNK, SSM_GROUP
    gpt = LANES // p
    nj = _chunks_per_step(j)
    return pl.pallas_call(
        functools.partial(_s5_ungroup_kernel, nj),
        out_shape=jax.ShapeDtypeStruct((b_, j * tc, g * p), F32),
        grid=(b_, j // nj, g // gpt),
        in_specs=[pl.BlockSpec((gpt, 1, nj, rows), lambda b, i, c: (c, b, i, 0))],
        out_specs=pl.BlockSpec((1, tc * nj, LANES), lambda b, i, c: (b, i, c)),
        compiler_params=_cparams(("parallel", "parallel", "parallel")),
        name="s5ungroup",
    )(yf)


def _s5_mixer(uf, u_c, ops):
    win, wout, m, dec = ops
    g, b_ = uf.shape[0], uf.shape[1]
    npair = g // 2
    states = _s5_states(uf, win)
    ctx_states = _s5_states(_s5_regroup(u_c), win)
    dec_cols = jnp.broadcast_to(dec.transpose(1, 2, 3, 0, 4)[:, :, :, :, None], (2, 2, 1, npair, b_, LANES))
    dec_cols = dec_cols.reshape(2, 2, 1, npair * b_ * LANES)
    hprev = _s5_scan(ctx_states, states, dec_cols)
    return _s5_ungroup(_s5_outputs(uf, hprev, wout, m))


def _attn_kernel(nkv, qt_ref, k_ref, vx_ref, o_ref, s0_ref, s1_ref):
    hd = HEAD_DIM
    kvh = pl.program_id(1)
    tq = qt_ref.shape[2]
    gq = qt_ref.shape[1] // hd
    kw = k_ref.shape[3]
    vr = vx_ref.shape[3]
    own = (lax.broadcasted_iota(jnp.int32, (kw, tq), 0) >> (hd.bit_length() - 1)) == kvh
    qp = []
    for g in range(gq):
        qg = qt_ref[0, g * hd:(g + 1) * hd, :].astype(F32)
        q2 = jnp.concatenate([qg] * (kw // hd), axis=0)
        qp.append(jnp.where(own, q2, 0.0).astype(BF16))
    qp = jnp.concatenate(qp, axis=1)
    m0 = jnp.full((1, gq * tq), -jnp.inf, F32)
    acc0 = jnp.zeros((vr, gq * tq), F32)

    def scores(jb, s_ref):
        s = _dot(k_ref[0, jb], qp)
        s_ref[...] = s
        return jnp.max(s, axis=0, keepdims=True)

    def consume(jb, s_ref, cmax, m_prev, acc):
        m_new = jnp.maximum(m_prev, cmax)
        alpha = jnp.exp2(m_prev - m_new)
        p = jnp.exp2(s_ref[...] - m_new).astype(BF16)
        return m_new, alpha * acc + _dot(vx_ref[0, 0, jb], p)

    bufs = (s0_ref, s1_ref)

    def steps(j0, count, m, acc, cmax):
        for u in range(count):
            cnext = scores(j0 + u + 1, bufs[(u + 1) % 2])
            m, acc = consume(j0 + u, bufs[u % 2], cmax, m, acc)
            cmax = cnext
        return m, acc, cmax

    unroll = ATTN_STEPS_PER_TRIP
    trips = (nkv - 1) // unroll
    carry = lax.fori_loop(0, trips, lambda i, c: steps(unroll * i, unroll, *c), (m0, acc0, scores(0, s0_ref)))
    m, acc, cmax = steps(trips * unroll, (nkv - 1) - trips * unroll, *carry)
    m, acc = consume(nkv - 1, bufs[(nkv - 1) % 2], cmax, m, acc)
    o = acc[0:hd] / acc[hd:hd + 1]
    ot = jnp.concatenate([o[:, g * tq:(g + 1) * tq] for g in range(gq)], axis=0)
    o_ref[0] = ot.T.astype(BF16)


def _attention(qt, k4, vx, tq):
    b_, wq, l_ = qt.shape
    _, kvh, nkv, vr, tk = vx.shape
    kw = k4.shape[3]
    gw = wq // kvh
    return pl.pallas_call(
        functools.partial(_attn_kernel, nkv),
        out_shape=jax.ShapeDtypeStruct((b_, l_, wq), BF16),
        grid=(b_, kvh, l_ // tq),
        in_specs=[pl.BlockSpec((1, gw, tq), lambda b, h, i: (b, h, i)),
                  pl.BlockSpec((1, nkv, tk, kw), lambda b, h, i: (b, 0, 0, 0)),
                  pl.BlockSpec((1, 1, nkv, vr, tk), lambda b, h, i: (b, h, 0, 0, 0))],
        out_specs=pl.BlockSpec((1, tq, gw), lambda b, h, i: (b, i, h)),
        scratch_shapes=[pltpu.VMEM((tk, (gw // HEAD_DIM) * tq), F32)] * 2,
        compiler_params=_cparams(("parallel", "parallel", "parallel")),
        name="attn",
    )(qt, k4, vx)


def _merge_kernel(dm, ne, ys_ref, oa_ref, gl_ref, x_ref, mod_ref, wglu_ref, bglu_ref, wso_ref, wao_ref,
                  wout_ref, n2g_ref, rw_ref, rb_ref, utri_ref, xmix_ref, h2_ref, idx_ref, gate_ref, rank_ref,
                  cnt_out_ref, cnt_ref):
    tm = x_ref.shape[1]
    parts = [pl.ds(r * (tm // MERGE_ROW_PARTS), tm // MERGE_ROW_PARTS) for r in range(MERGE_ROW_PARTS)]
    g1 = mod_ref[0, :, 2 * dm:3 * dm]
    sh2 = mod_ref[0, :, 3 * dm:4 * dm]
    sc2 = mod_ref[0, :, 4 * dm:5 * dm]
    s = [jax.nn.gelu(ys_ref[0, r, :]) for r in parts]
    z = [_dot(sr.astype(BF16), wglu_ref[...]) + bglu_ref[...] for sr in s]
    s = [sr * jax.nn.sigmoid(zr) for sr, zr in zip(s, z)]
    a = [_dot(sr.astype(BF16), wso_ref[...]) for sr in s]
    bq = [_dot(oa_ref[0, r, :], wao_ref[...]) for r in parts]
    merged = []
    for r, ar, br in zip(parts, a, bq):
        gts = jax.nn.sigmoid(gl_ref[0, r, :].astype(F32))
        merged.append((gts[:, 0:dm] * ar + gts[:, dm:2 * dm] * br).astype(BF16))
    out = [_dot(mr, wout_ref[...]) for mr in merged]
    h2_parts = []
    for r, outr in zip(parts, out):
        xm = x_ref[0, r, :] + g1 * outr
        xmix_ref[0, r, :] = xm
        ms = jnp.mean(xm * xm, axis=-1, keepdims=True)
        h2r = xm * lax.rsqrt(ms + EPS) * n2g_ref[...] * (1.0 + sc2) + sh2
        h2_ref[0, r, :] = _pack2(h2r[:, 0:dm // 2], h2r[:, dm // 2:dm])
        h2_parts.append(h2r)
    logits = jnp.concatenate([_dot3(h2r, rw_ref[...]) for h2r in h2_parts], axis=0) + rb_ref[...]
    nep = cnt_ref.shape[0]
    l = logits.T[0:nep, :]
    sub = lax.broadcasted_iota(jnp.int32, l.shape, 0).astype(F32)
    neg = jnp.float32(-jnp.inf)
    vals, idxs = [], []
    for _ in range(TOP_K):
        mx = jnp.max(l, axis=0, keepdims=True)
        ix = jnp.min(jnp.where(l == mx, sub, float(nep)), axis=0, keepdims=True)
        vals.append(mx)
        idxs.append(ix)
        l = jnp.where(sub == ix, neg, l)
    es = [jnp.exp(v - vals[0]) for v in vals]
    den = es[0] + es[1] + es[2] + es[3]
    @pl.when((pl.program_id(0) == 0) & (pl.program_id(1) == 0))
    def _():
        cnt_ref[...] = jnp.zeros(cnt_ref.shape, F32)

    onehot = jnp.zeros(l.shape, F32)
    for k in range(TOP_K):
        onehot = onehot + jnp.where(sub == idxs[k], 1.0, 0.0)
    prior = _dot(onehot.astype(BF16), utri_ref[...]) + cnt_ref[...]
    cnt_ref[...] = cnt_ref[...] + jnp.sum(onehot, axis=1, keepdims=True)
    cnt_out_ref[...] = cnt_ref[...]
    krow = lax.broadcasted_iota(jnp.int32, (8, l.shape[1]), 0)
    idx_out = jnp.zeros(krow.shape, F32)
    gate_out = jnp.zeros(krow.shape, F32)
    rank_out = jnp.zeros(krow.shape, F32)
    for k in range(TOP_K):
        rk = jnp.sum(jnp.where(sub == idxs[k], prior, 0.0), axis=0, keepdims=True)
        idx_out = jnp.where(krow == k, idxs[k], idx_out)
        gate_out = jnp.where(krow == k, es[k] / den, gate_out)
        rank_out = jnp.where(krow == k, rk, rank_out)
    idx_ref[0] = idx_out.astype(jnp.int32)
    gate_ref[0] = gate_out
    rank_ref[0] = rank_out.astype(jnp.int32)


def _merge(y_ssm, o_attn, glog, x, mod3, w_glu, b_glu, w_ssm_out, w_attn_out, w_out, norm2_g,
           rw_pad, rb_pad, ne, tm):
    b_, l_, dm = x.shape
    wu = dm // 2
    tok = lambda w: pl.BlockSpec((1, tm, w), lambda b, i: (b, i, 0))
    full = lambda r, c: pl.BlockSpec((r, c), lambda b, i: (0, 0))
    utri = (jnp.arange(tm)[:, None] < jnp.arange(tm)[None, :]).astype(BF16)
    nep = -(-ne // 8) * 8
    kmaj = pl.BlockSpec((1, 8, tm), lambda b, i: (b, 0, i))
    return pl.pallas_call(
        functools.partial(_merge_kernel, dm, ne),
        out_shape=[jax.ShapeDtypeStruct((b_, l_, dm), F32),
                   jax.ShapeDtypeStruct((b_, l_, dm // 2), jnp.uint32),
                   jax.ShapeDtypeStruct((b_, 8, l_), jnp.int32),
                   jax.ShapeDtypeStruct((b_, 8, l_), F32),
                   jax.ShapeDtypeStruct((b_, 8, l_), jnp.int32),
                   jax.ShapeDtypeStruct((nep, 1), F32)],
        grid=(b_, l_ // tm),
        in_specs=[tok(wu), tok(wu), tok(2 * dm), tok(dm),
                  pl.BlockSpec((1, 1, N_MOD * dm), lambda b, i: (b, 0, 0)),
                  full(wu, wu), full(1, wu), full(wu, dm), full(wu, dm), full(dm, dm),
                  full(1, dm), full(dm, LANES), full(1, LANES), full(tm, tm)],
        out_specs=[tok(dm), tok(dm // 2), kmaj, kmaj, kmaj, full(nep, 1)],
        scratch_shapes=[pltpu.VMEM((nep, 1), F32)],
        compiler_params=_cparams(("arbitrary", "arbitrary")),
        name="merge",
    )(y_ssm, o_attn, glog, x, mod3, w_glu, b_glu, w_ssm_out, w_attn_out, w_out, norm2_g, rw_pad, rb_pad, utri)


def _expert_kernel(de, be_ref, nv_ref, x_ref, wgu_ref, bgu_ref, wd_ref, bd_ref, y_ref, wgu_b, wd_b):
    i = pl.program_id(0)
    n_valid = nv_ref[i]
    used = n_valid > 0
    new_expert = (i == 0) | (be_ref[i] != be_ref[jnp.maximum(i - 1, 0)])

    @pl.when(used & new_expert)
    def _():
        wgu_b[...] = wgu_ref[0].astype(BF16)
        wd_b[...] = wd_ref[0].astype(BF16)

    @pl.when(used)
    def _():
        row = lax.broadcasted_iota(jnp.int32, (x_ref.shape[0], 1), 0)
        x_lo, x_hi = _unpack2(jnp.where(row < n_valid, x_ref[...], jnp.uint32(0)))
        xb = jnp.concatenate([x_lo, x_hi], axis=1).astype(BF16)
        gu = _dot(xb, wgu_b[...]) + bgu_ref[0]
        gate = jnp.minimum(gu[:, 0:de], SWIGLU_LIMIT)
        up = jnp.clip(gu[:, de:2 * de], -SWIGLU_LIMIT, SWIGLU_LIMIT)
        act = (up + 1.0) * (gate * jax.nn.sigmoid(SWIGLU_ALPHA * gate))
        y = _dot(act.astype(BF16), wd_b[...]) + bd_ref[0]
        half = y.shape[1] // 2
        y_ref[...] = _pack2(y[:, 0:half], y[:, half:])

    @pl.when(jnp.logical_not(used))
    def _():
        y_ref[...] = jnp.zeros(y_ref.shape, y_ref.dtype)


def _experts(xb, block_expert, block_valid, w_gate_up, b_gate_up, w_down, b_down):
    n_slots, dmh = xb.shape
    dm = 2 * dmh
    ne, _, de2 = w_gate_up.shape
    de = de2 // 2
    nb = n_slots // EXPERT_BLOCK
    return pl.pallas_call(
        functools.partial(_expert_kernel, de),
        out_shape=jax.ShapeDtypeStruct((n_slots, dmh), jnp.uint32),
        grid_spec=pltpu.PrefetchScalarGridSpec(
            num_scalar_prefetch=2,
            grid=(nb,),
            in_specs=[pl.BlockSpec((EXPERT_BLOCK, dmh), lambda i, be, nu: (i, 0)),
                      pl.BlockSpec((1, dm, de2), lambda i, be, nu: (be[i], 0, 0)),
                      pl.BlockSpec((1, 1, de2), lambda i, be, nu: (be[i], 0, 0)),
                      pl.BlockSpec((1, de, dm), lambda i, be, nu: (be[i], 0, 0)),
                      pl.BlockSpec((1, 1, dm), lambda i, be, nu: (be[i], 0, 0))],
            out_specs=pl.BlockSpec((EXPERT_BLOCK, dmh), lambda i, be, nu: (i, 0)),
            scratch_shapes=[pltpu.VMEM((dm, de2), BF16), pltpu.VMEM((de, dm), BF16)],
        ),
        compiler_params=_cparams(("arbitrary",)),
        name="expert",
    )(block_expert, block_valid, xb, w_gate_up, b_gate_up.reshape(ne, 1, de2), w_down, b_down.reshape(ne, 1, dm))


def _sc_gather_rows(table, idx):
    n = idx.shape[0]
    w = table.shape[1]
    mesh = plsc.VectorSubcoreMesh(core_axis_name="c", subcore_axis_name="s")
    nc, nw = mesh.num_cores, mesh.num_cores * mesh.num_subcores
    win = SC_GATHER_WINDOW
    assert n % (win * nw) == 0
    per_worker = n // nw

    @functools.partial(
        pl.kernel, out_type=jax.ShapeDtypeStruct((n, w), table.dtype), mesh=mesh,
        scratch_types=[pltpu.VMEM((win,), jnp.int32), pltpu.VMEM((win, w), table.dtype),
                       pltpu.SemaphoreType.DMA])
    def gather(x_hbm, i_hbm, o_hbm, idx_v, rows_v, sem):
        base = (lax.axis_index("s") * nc + lax.axis_index("c")) * per_worker

        @pl.loop(0, per_worker // win)
        def _(j):
            off = base + j * win
            pltpu.sync_copy(i_hbm.at[pl.ds(off, win)], idx_v)
            pltpu.async_copy(x_hbm.at[idx_v], rows_v, sem).wait()
            pltpu.sync_copy(rows_v, o_hbm.at[pl.ds(off, win)])

    return gather(table, idx)


def _sc_scatter_rows(rows, dest, n_out):
    t_, w = rows.shape
    kk = dest.shape[0] // t_
    mesh = plsc.VectorSubcoreMesh(core_axis_name="c", subcore_axis_name="s")
    nc, nw = mesh.num_cores, mesh.num_cores * mesh.num_subcores
    win = SC_GATHER_WINDOW
    assert t_ % (win * nw) == 0
    per_worker = t_ // nw

    @functools.partial(
        pl.kernel, out_type=jax.ShapeDtypeStruct((n_out, w), rows.dtype), mesh=mesh,
        scratch_types=[pltpu.VMEM((win,), jnp.int32)] * kk + [pltpu.VMEM((win, w), rows.dtype),
                                                              pltpu.SemaphoreType.DMA])
    def scatter(x_hbm, d_hbm, o_hbm, *scratch):
        idx_vs, rows_v, sem = scratch[:kk], scratch[kk], scratch[kk + 1]
        base = (lax.axis_index("s") * nc + lax.axis_index("c")) * per_worker

        @pl.loop(0, per_worker // win)
        def _(j):
            off = base + j * win
            pltpu.sync_copy(x_hbm.at[pl.ds(off, win)], rows_v)
            for k in range(kk):
                pltpu.sync_copy(d_hbm.at[pl.ds(k * t_ + off, win)], idx_vs[k])
            copies = [pltpu.async_copy(rows_v, o_hbm.at[idx_vs[k]], sem) for k in range(kk)]
            for cp in copies:
                cp.wait()

    return scatter(rows, dest)


def _final_kernel(dm, yg_ref, gate_ref, xm_ref, mod_ref, fg_ref, *rest):
    o_ref = rest[-1]
    gates = gate_ref[0]
    f_lo = f_hi = None
    for k in range(TOP_K):
        y_lo, y_hi = _unpack2(yg_ref[k, 0])
        gk = gates[:, k:k + 1]
        f_lo = gk * y_lo if f_lo is None else f_lo + gk * y_lo
        f_hi = gk * y_hi if f_hi is None else f_hi + gk * y_hi
    f = jnp.concatenate([f_lo, f_hi], axis=1)
    g2 = mod_ref[0, :, 5 * dm:6 * dm]
    xo = xm_ref[0] + g2 * f
    ms = jnp.mean(xo * xo, axis=-1, keepdims=True)
    o_ref[0] = xo * lax.rsqrt(ms + EPS) * fg_ref[...]


def _final(yg, gates, x_mix, mod3, final_g, tm, sample, prev):
    b_, l_, dm = x_mix.shape
    in_specs = [pl.BlockSpec((TOP_K, 1, tm, dm // 2), lambda i: (0, 0, i, 0)),
                pl.BlockSpec((1, tm, LANES), lambda i: (sample, i, 0)),
                pl.BlockSpec((1, tm, dm), lambda i: (sample, i, 0)),
                pl.BlockSpec((1, 1, N_MOD * dm), lambda i: (sample, 0, 0)),
                pl.BlockSpec((1, dm), lambda i: (0, 0))]
    args = [yg, gates, x_mix, mod3, final_g.reshape(1, dm)]
    aliases = {}
    if prev is not None:
        in_specs.append(pl.BlockSpec(memory_space=pl.ANY))
        args.append(prev)
        aliases = {len(args) - 1: 0}
    return pl.pallas_call(
        functools.partial(_final_kernel, dm),
        out_shape=jax.ShapeDtypeStruct((b_, l_, dm), F32),
        grid=(l_ // tm,),
        in_specs=in_specs,
        out_specs=pl.BlockSpec((1, tm, dm), lambda i: (sample, i, 0)),
        input_output_aliases=aliases,
        compiler_params=_cparams(("parallel",)),
        name="final",
    )(*args)


def _rope_tables(l_):
    half = HEAD_DIM // 2
    inv_freq = ROPE_THETA ** (-jnp.arange(0, half, 2, dtype=F32) / half)
    t = jnp.arange(l_, dtype=jnp.int32)
    row_id, col_id = t // GRID_W, t % GRID_W
    ang = jnp.concatenate([row_id.astype(F32)[:, None] * inv_freq,
                           col_id.astype(F32)[:, None] * inv_freq], axis=-1)
    cos = jnp.repeat(jnp.cos(ang), 2, axis=-1)
    sin = jnp.repeat(jnp.sin(ang), 2, axis=-1)
    sign = jnp.tile(jnp.array([-1.0, 1.0], F32), HEAD_DIM // 2)
    reps = LANES // HEAD_DIM
    return jnp.tile(cos, (1, reps)), jnp.tile(sin * sign, (1, reps))


def _routing(idx, rank, counts, t_, ne):
    m_ = t_ * TOP_K
    blk = EXPERT_BLOCK
    nb = (m_ + ne * (blk - 1)) // blk
    nb = -(-nb // SLOT_BLOCK_MULTIPLE) * SLOT_BLOCK_MULTIPLE
    blk_counts = (counts + blk - 1) // blk
    blk_ends = jnp.cumsum(blk_counts)
    first_slot = (blk_ends - blk_counts) * blk
    onehot = idx[:, :, None] == jnp.arange(ne, dtype=jnp.int32)[None, None, :]
    dest = rank + jnp.sum(jnp.where(onehot, first_slot[None, None, :], 0), axis=2)
    blocks = jnp.arange(nb, dtype=jnp.int32)
    block_expert = jnp.minimum(jnp.sum(blocks[:, None] >= blk_ends[None, :], axis=1), ne - 1)
    in_expert = blocks * blk - first_slot[block_expert]
    block_valid = jnp.clip(counts[block_expert] - in_expert, 0, blk)
    block_valid = jnp.where(blocks < blk_ends[-1], block_valid, 0)
    return dest.astype(jnp.int32), block_expert.astype(jnp.int32), block_valid.astype(jnp.int32), nb * blk


def kernel(x, c, ctx, c_ctx, w_mod, b_mod, norm1_g, norm2_g, w_in, s5_lam_re, s5_lam_im, s5_log_step,
           s5_b_re, s5_b_im, s5_c_re, s5_c_im, s5_d, w_glu, b_glu, w_ssm_out, q_norm_g, k_norm_g,
           w_attn_out, w_out, router_w, router_b, w_gate_up, b_gate_up, w_down, b_down, final_norm_g):
    b_, l_, dm = x.shape
    lc = ctx.shape[1]
    depth = w_mod.shape[0]
    assert depth == 1, "single-layer block"
    assert b_ <= 7 and l_ % 512 == 0 and lc % S5_CHUNK == 0
    wu = dm // 2
    wkv = wu // GQA_GROUP
    kvh = wkv // HEAD_DIM
    ne = router_w.shape[-1]
    i = 0

    c8 = jnp.zeros((8, dm), F32).at[:b_].set(c).at[b_].set(c_ctx)
    mod3 = _modulation(c8, w_mod[i], b_mod[i]).reshape(8, 1, N_MOD * dm)

    w_in_b = w_in[i].astype(BF16)
    cos, sin = _rope_tables(l_)
    hsum = jnp.kron(jnp.eye(wu // HEAD_DIM, dtype=F32), jnp.ones((HEAD_DIM, HEAD_DIM), F32)).astype(BF16)
    qg = jnp.tile(q_norm_g[i], wu // HEAD_DIM).reshape(1, wu)
    kg = jnp.tile(k_norm_g[i], wkv // HEAD_DIM).reshape(1, wkv)
    uf, k, vt, qt, glog = _inproj(x, mod3, None, norm1_g[i], w_in_b, cos, sin, qg, kg, hsum, True, 512)
    ones_t = jnp.ones((lc, LANES), F32)
    u_c, k_c, vt_c = _inproj(ctx, mod3, b_, norm1_g[i], w_in_b[:, :wu + 2 * wkv], ones_t, 0.0 * ones_t,
                             qg, kg, hsum, False, lc)

    ops = _s5_operators(s5_lam_re[i], s5_lam_im[i], s5_log_step[i], s5_b_re[i], s5_b_im[i],
                        s5_c_re[i], s5_c_im[i], s5_d[i])
    y_ssm = _s5_mixer(uf, u_c, ops)

    s_all = lc + l_
    tk = max(t for t in range(LANES, ATTN_KEY_CHUNK + 1, LANES) if s_all % t == 0)
    nkv = s_all // tk
    k4 = jnp.concatenate([k_c, k], axis=1).reshape(b_, nkv, tk, wkv)
    vt_all = jnp.concatenate([vt_c, vt], axis=2).reshape(b_, kvh, HEAD_DIM, s_all)
    vx = jnp.concatenate([vt_all, jnp.ones((b_, kvh, 1, s_all), BF16),
                          jnp.zeros((b_, kvh, ATTN_V_ROWS - HEAD_DIM - 1, s_all), BF16)], axis=2)
    vx = vx.reshape(b_, kvh, ATTN_V_ROWS, nkv, tk).transpose(0, 1, 3, 2, 4)
    o_attn = _attention(qt, k4, vx, ATTN_Q_BLOCK)

    rw_pad = jnp.zeros((dm, LANES), F32).at[:, :ne].set(router_w[i])
    rb_pad = jnp.full((1, LANES), -jnp.inf, F32).at[0, :ne].set(router_b[i])
    x_mix, h2, idx, gates, rank, counts = _merge(
        y_ssm, o_attn, glog, x, mod3, w_glu[i].astype(BF16), b_glu[i].reshape(1, wu),
        w_ssm_out[i].astype(BF16), w_attn_out[i].astype(BF16), w_out[i].astype(BF16),
        norm2_g[i].reshape(1, dm), rw_pad, rb_pad, ne, 512)

    t_ = b_ * l_
    kmajor = lambda a: a[:, :TOP_K, :].transpose(1, 0, 2).reshape(TOP_K, t_)
    dest_kmajor, block_expert, block_valid, n_slots = _routing(
        kmajor(idx), kmajor(rank), counts[:ne, 0].astype(jnp.int32), t_, ne)
    xb = _sc_scatter_rows(h2.reshape(t_, dm // 2), dest_kmajor.reshape(-1), n_slots)
    yb = _experts(xb, block_expert, block_valid, w_gate_up[i], b_gate_up[i], w_down[i], b_down[i])
    gates_tok = jnp.pad(gates.transpose(0, 2, 1), ((0, 0), (0, 0), (0, LANES - 8)))
    out = None
    for b in range(b_):
        yg = _sc_gather_rows(yb, dest_kmajor[:, b * l_:(b + 1) * l_].reshape(-1)).reshape(TOP_K, 1, l_, dm // 2)
        out = _final(yg, gates_tok, x_mix, mod3, final_norm_g, 512, b, out)
    return out
```

```python
import functools

import jax
import jax.numpy as jnp
from jax import lax
from jax.experimental import pallas as pl
from jax.experimental.pallas import tpu as pltpu
from jax.experimental.pallas import tpu_sc as plsc

F32 = jnp.float32
BF16 = jnp.bfloat16

EPS = 1e-6
GRID_W = 64
N_MOD = 6
SSM_GROUP = 16
SSM_STATE = 64
HEAD_DIM = 64
GQA_GROUP = 4
ROPE_THETA = 10000.0
TOP_K = 4
SWIGLU_ALPHA = 1.702
SWIGLU_LIMIT = 7.0
LOG2E = 1.4426950408889634

LANES = 128
S5_CHUNK = 32
S5_POW_ROWS = 56
TOKEN_BLOCK = 512
WIDE_BLOCK = 1024
EXPERT_BLOCK = 512
ATTN_V_ROWS = 80
ATTN_KEY_CHUNK = 384
ATTN_Q_BLOCK = 256
ATTN_STEPS_PER_TRIP = 10
MERGE_ROW_PARTS = 2
SC_GATHER_WINDOW = 128
SC_WORKERS = 32
SLOT_BLOCK_MULTIPLE = SC_GATHER_WINDOW * SC_WORKERS // EXPERT_BLOCK
VMEM_LIMIT = 56 * 1024 * 1024


def _cparams(sem):
    return pltpu.CompilerParams(dimension_semantics=sem, vmem_limit_bytes=VMEM_LIMIT)


def _dot(a, b):
    return jnp.dot(a, b, preferred_element_type=F32)


def _split(a):
    hi = a.astype(BF16)
    lo = (a - hi.astype(F32)).astype(BF16)
    return hi, lo


def _pack2(lo, hi):
    lo_w = lax.bitcast_convert_type(lo.astype(BF16).astype(F32), jnp.uint32) >> 16
    hi_w = lax.bitcast_convert_type(hi.astype(BF16).astype(F32), jnp.uint32) & jnp.uint32(0xFFFF0000)
    return lo_w | hi_w


def _unpack2(words):
    lo = lax.bitcast_convert_type(words << 16, F32)
    hi = lax.bitcast_convert_type(words & jnp.uint32(0xFFFF0000), F32)
    return lo, hi


def _dot3(a, b):
    ah, al = _split(a)
    bh, bl = _split(b)
    return _dot(ah, bh) + _dot(ah, bl) + _dot(al, bh)


def _mod_kernel(c_ref, w_ref, b_ref, o_ref):
    c = c_ref[...]
    s = c * jax.nn.sigmoid(c)
    o_ref[...] = _dot3(s, w_ref[...]) + b_ref[...]


def _modulation(c8, w_mod, b_mod):
    d = c8.shape[1]
    n = w_mod.shape[1]
    tn = WIDE_BLOCK
    return pl.pallas_call(
        _mod_kernel,
        out_shape=jax.ShapeDtypeStruct((8, n), F32),
        grid=(n // tn,),
        in_specs=[pl.BlockSpec((8, d), lambda j: (0, 0)),
                  pl.BlockSpec((d, tn), lambda j: (0, j)),
                  pl.BlockSpec((1, tn), lambda j: (0, j))],
        out_specs=pl.BlockSpec((8, tn), lambda j: (0, j)),
        compiler_params=_cparams(("parallel",)),
        name="mod",
    )(c8, w_mod, b_mod.reshape(1, n))


def _head_norm_rope(z, gain, hsum, cos, sin, scale):
    ss = _dot((z * z).astype(BF16), hsum)
    zn = z * lax.rsqrt(ss * (1.0 / HEAD_DIM) + EPS) * gain
    lane = lax.broadcasted_iota(jnp.int32, (1, LANES), 1)
    even = (lane & 1) == 0
    outs = []
    for ci in range(z.shape[1] // LANES):
        ch = zn[:, ci * LANES:(ci + 1) * LANES]
        nxt = pltpu.roll(ch, LANES - 1, axis=1)
        prv = pltpu.roll(ch, 1, axis=1)
        sw = jnp.where(even, nxt, prv)
        outs.append((ch * cos + sw * sin) * scale)
    return outs


def _inproj_kernel(with_q, dm, x_ref, mod_ref, g_ref, w_ref, cos_ref, sin_ref, qg_ref, kg_ref,
                   hsum_ref, *out_refs):
    x = x_ref[0]
    ms = jnp.mean(x * x, axis=-1, keepdims=True)
    y = x * lax.rsqrt(ms + EPS) * g_ref[...]
    sh = mod_ref[0, :, 0:dm]
    sc = mod_ref[0, :, dm:2 * dm]
    hb = (y * (1.0 + sc) + sh).astype(BF16)
    cos = cos_ref[...]
    sin = sin_ref[...]
    wu = dm // 2
    wkv = wu // GQA_GROUP
    c_k, c_v, c_q, c_g = wu, wu + wkv, wu + 2 * wkv, 2 * wu + 2 * wkv
    u_ref, k_ref, vt_ref = out_refs[:3]
    u = _dot(hb, w_ref[:, 0:c_k])
    if with_q:
        ubuf = out_refs[-1]
        p, gpt, nj = SSM_GROUP, LANES // SSM_GROUP, x.shape[0] // S5_CHUNK
        for ci in range(wu // LANES):
            ubuf[ci] = u[:, ci * LANES:(ci + 1) * LANES]
        for s in range(S5_CHUNK):
            for ci in range(wu // LANES):
                xs = ubuf[ci, pl.ds(s, nj, stride=S5_CHUNK), :].astype(BF16)
                for gi in range(gpt):
                    u_ref[ci * gpt + gi, 0, :, s * p:(s + 1) * p] = xs[:, gi * p:(gi + 1) * p]
    else:
        u_ref[0] = u
    kz = _dot(hb, w_ref[:, c_k:c_v])
    (kr,) = _head_norm_rope(kz, kg_ref[...], hsum_ref[0:wkv, 0:wkv], cos, sin, 1.0)
    k_ref[0] = kr.astype(BF16)
    vt_ref[0] = _dot(hb, w_ref[:, c_v:c_q]).T.astype(BF16)
    if with_q:
        qt_ref, gl_ref = out_refs[3:5]
        qz = _dot(hb, w_ref[:, c_q:c_g])
        qs = _head_norm_rope(qz, qg_ref[...], hsum_ref[...], cos, sin, HEAD_DIM ** -0.5 * LOG2E)
        for ci, qc in enumerate(qs):
            qt_ref[0, ci * LANES:(ci + 1) * LANES, :] = qc.T.astype(BF16)
        gl_ref[0] = _dot(hb, w_ref[:, c_g:]).astype(BF16)


def _inproj(x, mod3, mod_row0, norm_g, w_in_b, cos, sin, qg, kg, hsum, with_q, tm):
    b_, l_, dm = x.shape
    wu = dm // 2
    wkv = wu // GQA_GROUP
    n_in = w_in_b.shape[1]
    outs = [jax.ShapeDtypeStruct((b_, l_, wu), F32),
            jax.ShapeDtypeStruct((b_, l_, wkv), BF16),
            jax.ShapeDtypeStruct((b_, wkv, l_), BF16)]
    ospecs = [pl.BlockSpec((1, tm, wu), lambda b, i: (b, i, 0)),
              pl.BlockSpec((1, tm, wkv), lambda b, i: (b, i, 0)),
              pl.BlockSpec((1, wkv, tm), lambda b, i: (b, 0, i))]
    scratch = []
    if with_q:
        g, rows = wu // SSM_GROUP, S5_CHUNK * SSM_GROUP
        outs[0] = jax.ShapeDtypeStruct((g, b_, l_ // S5_CHUNK, rows), BF16)
        ospecs[0] = pl.BlockSpec((g, 1, tm // S5_CHUNK, rows), lambda b, i: (0, b, i, 0))
        outs += [jax.ShapeDtypeStruct((b_, wu, l_), BF16),
                 jax.ShapeDtypeStruct((b_, l_, 2 * dm), BF16)]
        ospecs += [pl.BlockSpec((1, wu, tm), lambda b, i: (b, 0, i)),
                   pl.BlockSpec((1, tm, 2 * dm), lambda b, i: (b, i, 0))]
        scratch = [pltpu.VMEM((wu // LANES, tm, LANES), F32)]
    if mod_row0 is None:
        mod_map = lambda b, i: (b, 0, 0)
    else:
        mod_map = lambda b, i: (mod_row0, 0, 0)
    return pl.pallas_call(
        functools.partial(_inproj_kernel, with_q, dm),
        out_shape=outs,
        grid=(b_, l_ // tm),
        in_specs=[pl.BlockSpec((1, tm, dm), lambda b, i: (b, i, 0)),
                  pl.BlockSpec((1, 1, N_MOD * dm), mod_map),
                  pl.BlockSpec((1, dm), lambda b, i: (0, 0)),
                  pl.BlockSpec((dm, n_in), lambda b, i: (0, 0)),
                  pl.BlockSpec((tm, LANES), lambda b, i: (i, 0)),
                  pl.BlockSpec((tm, LANES), lambda b, i: (i, 0)),
                  pl.BlockSpec((1, wu), lambda b, i: (0, 0)),
                  pl.BlockSpec((1, wkv), lambda b, i: (0, 0)),
                  pl.BlockSpec((wu, wu), lambda b, i: (0, 0))],
        out_specs=ospecs,
        scratch_shapes=scratch,
        compiler_params=_cparams(("parallel", "parallel")),
        name="inproj_x" if with_q else "inproj_ctx",
    )(x, mod3, norm_g.reshape(1, dm), w_in_b, cos, sin, qg, kg, hsum)


def _cexp_pow(e, lr, li):
    mag = jnp.exp(e * lr)
    ang = e * li
    return mag * jnp.cos(ang), mag * jnp.sin(ang)


def _cmul(ar, ai, br, bi):
    return ar * br - ai * bi, ar * bi + ai * br


def _s5pre_kernel(lamr_r, lami_r, lst_r, lamr_c, lami_c, lst_c, btr_ref, bti_ref, ctr_ref, cti_ref,
                  d_ref, win_ref, wout_ref, m_ref, dec_ref):
    tc = S5_CHUNK
    half = tc // 2
    p = SSM_GROUP
    rows = tc * p
    gsh = SSM_GROUP.bit_length() - 1
    nsh = SSM_STATE.bit_length() - 1
    row_i = lax.broadcasted_iota(jnp.int32, (rows, rows), 0)
    col_i = lax.broadcasted_iota(jnp.int32, (rows, rows), 1)
    lane_g = lax.broadcasted_iota(jnp.int32, (1, LANES), 1) >> nsh
    subl_g = lax.broadcasted_iota(jnp.int32, (LANES, 1), 0) >> nsh
    e_rows = lax.broadcasted_iota(jnp.int32, (S5_POW_ROWS, 1), 0) - half
    e_lanes = lax.broadcasted_iota(jnp.int32, (1, LANES), 1) - half
    sel_e = lax.broadcasted_iota(jnp.int32, (LANES, rows), 0) - half
    sel_t = lax.broadcasted_iota(jnp.int32, (LANES, rows), 1) >> gsh
    sel_p = (lax.broadcasted_iota(jnp.int32, (LANES, rows), 0)
             == (lax.broadcasted_iota(jnp.int32, (LANES, rows), 1) & (p - 1))).astype(BF16)

    def spread_lanes(table, sel):
        hi, lo = _split(table)
        return _dot(hi, sel) + _dot(lo, sel)

    def spread_rows(table, e_of_s):
        return jnp.concatenate([jnp.broadcast_to(table[e_of_s(s) + half:e_of_s(s) + half + 1, :], (p, LANES))
                                for s in range(tc)], axis=0)

    m_acc = [jnp.where(row_i == col_i, d_ref[0, gl], 0.0) for gl in range(2)]
    for dr in range(2):
        step = jnp.exp(lst_r[dr, 0])
        lam_r, lam_i = lamr_r[dr, 0], lami_r[dr, 0]
        lr, li = lam_r * step, lam_i * step
        pw_r, pw_i = _cexp_pow(e_rows.astype(F32), lr, li)
        lb_r, lb_i = pw_r[half + 1:half + 2, :], pw_i[half + 1:half + 2, :]
        den = lam_r * lam_r + lam_i * lam_i
        nr, ni = lb_r - 1.0, lb_i
        cf_r = (nr * lam_r + ni * lam_i) / den
        cf_i = (ni * lam_r - nr * lam_i) / den
        bt_r = jnp.concatenate([btr_ref[dr, 0]] * tc, axis=0)
        bt_i = jnp.concatenate([bti_ref[dr, 0]] * tc, axis=0)
        bb_r, bb_i = _cmul(cf_r, cf_i, bt_r, bt_i)
        e_in = (lambda s: tc - 1 - s) if dr == 0 else (lambda s: s)
        wi_r, wi_i = _cmul(bb_r, bb_i, spread_rows(pw_r, e_in), spread_rows(pw_i, e_in))
        e_a = (lambda s: half - s) if dr == 0 else (lambda s: s - half)
        a_r, a_i = _cmul(bb_r, bb_i, spread_rows(pw_r, e_a), spread_rows(pw_i, e_a))
        dec_ref[0, dr, 0] = pw_r[tc + half:tc + half + 1, :]
        dec_ref[0, dr, 1] = pw_i[tc + half:tc + half + 1, :]
        stepc = jnp.exp(lst_c[dr, 0])
        lrc, lic = lamr_c[dr, 0] * stepc, lami_c[dr, 0] * stepc
        qw_r, qw_i = _cexp_pow(e_lanes.astype(F32), lrc, lic)
        qw_r = jnp.where(e_lanes <= tc, qw_r, 0.0)
        qw_i = jnp.where(e_lanes <= tc, qw_i, 0.0)
        c_r, c_i = spread_lanes(ctr_ref[dr, 0], sel_p), spread_lanes(cti_ref[dr, 0], sel_p)
        sel_b = (sel_e == (sel_t - half if dr == 0 else half - sel_t)).astype(BF16)
        bm_r, bm_i = _cmul(c_r, c_i, spread_lanes(qw_r, sel_b), spread_lanes(qw_i, sel_b))
        sel_o = (sel_e == (sel_t + 1 if dr == 0 else tc - sel_t)).astype(BF16)
        wo_r, wo_i = _cmul(c_r, c_i, spread_lanes(qw_r, sel_o), spread_lanes(qw_i, sel_o))
        if dr == 0:
            mask = (col_i >> gsh) >= (row_i >> gsh)
        else:
            mask = (row_i >> gsh) >= (col_i >> gsh)
        for gl in range(2):
            lsel = lane_g == gl
            ssel = subl_g == gl
            win_ref[0, gl, 2 * dr] = jnp.where(lsel, wi_r, 0.0).astype(BF16)
            win_ref[0, gl, 2 * dr + 1] = jnp.where(lsel, wi_i, 0.0).astype(BF16)
            wout_ref[0, 2 * dr, :, gl * rows:(gl + 1) * rows] = jnp.where(ssel, wo_r, 0.0).astype(BF16)
            wout_ref[0, 2 * dr + 1, :, gl * rows:(gl + 1) * rows] = jnp.where(ssel, -wo_i, 0.0).astype(BF16)
            kmat = (_dot3(jnp.where(lsel, a_r, 0.0), bm_r) - _dot3(jnp.where(lsel, a_i, 0.0), bm_i))
            m_acc[gl] = m_acc[gl] + jnp.where(mask, kmat, 0.0)
    for gl in range(2):
        m_ref[0, gl] = m_acc[gl].astype(BF16)


def _s5_operators(lam_re, lam_im, log_step, b_re, b_im, c_re, c_im, d_skip):
    g = lam_re.shape[1]
    n, p = SSM_STATE, SSM_GROUP
    npair = g // 2
    tc = S5_CHUNK
    rows = tc * p

    def row_form(a):
        return a.reshape(2, npair, 1, 2 * n)

    def col_form(a):
        return a.reshape(2, npair, 2 * n, 1)

    lst = jnp.broadcast_to(log_step[:, :, None], (2, g, n))
    bt = lambda b: b.reshape(2, npair, 2, n, p).transpose(0, 1, 4, 2, 3).reshape(2, npair, p, 2 * n)
    ct = lambda c: jnp.pad(c.reshape(2, npair, 2, p, n).transpose(0, 1, 2, 4, 3).reshape(2, npair, 2 * n, p),
                           ((0, 0), (0, 0), (0, 0), (0, LANES - p)))
    d2 = jnp.broadcast_to(d_skip.reshape(npair, 2, 1, 1, p), (npair, 2, 1, tc, p)).reshape(npair, 2, 1, rows)

    def spec4(shape):
        return pl.BlockSpec((2, 1) + shape, lambda i: (0, i, 0, 0))

    return pl.pallas_call(
        _s5pre_kernel,
        out_shape=[jax.ShapeDtypeStruct((npair, 2, 4, rows, LANES), BF16),
                   jax.ShapeDtypeStruct((npair, 4, LANES, 2 * rows), BF16),
                   jax.ShapeDtypeStruct((npair, 2, rows, rows), BF16),
                   jax.ShapeDtypeStruct((npair, 2, 2, 1, LANES), F32)],
        grid=(npair,),
        in_specs=[spec4((1, LANES)), spec4((1, LANES)), spec4((1, LANES)),
                  spec4((LANES, 1)), spec4((LANES, 1)), spec4((LANES, 1)),
                  spec4((p, LANES)), spec4((p, LANES)),
                  spec4((LANES, LANES)), spec4((LANES, LANES)),
                  pl.BlockSpec((1, 2, 1, rows), lambda i: (i, 0, 0, 0))],
        out_specs=[pl.BlockSpec((1, 2, 4, rows, LANES), lambda i: (i, 0, 0, 0, 0)),
                   pl.BlockSpec((1, 4, LANES, 2 * rows), lambda i: (i, 0, 0, 0)),
                   pl.BlockSpec((1, 2, rows, rows), lambda i: (i, 0, 0, 0)),
                   pl.BlockSpec((1, 2, 2, 1, LANES), lambda i: (i, 0, 0, 0, 0))],
        compiler_params=_cparams(("parallel",)),
        name="s5pre",
    )(row_form(lam_re), row_form(lam_im), row_form(lst),
      col_form(lam_re), col_form(lam_im), col_form(lst),
      bt(b_re), bt(b_im), ct(c_re), ct(c_im), d2)


def _s5state_kernel(u_ref, win_ref, *s_refs):
    for b in range(u_ref.shape[1]):
        for kind in range(4):
            s_refs[kind][:, b * LANES:(b + 1) * LANES] = (_dot(u_ref[0, b], win_ref[0, 0, kind])
                                                          + _dot(u_ref[1, b], win_ref[0, 1, kind]))


def _s5_states(uf, win):
    g, b_, j, rows = uf.shape
    npair = g // 2
    out = jax.ShapeDtypeStruct((j, npair * b_ * LANES), F32)
    ospec = pl.BlockSpec((j, b_ * LANES), lambda pr: (0, pr))
    return pl.pallas_call(
        _s5state_kernel,
        out_shape=[out] * 4,
        grid=(npair,),
        in_specs=[pl.BlockSpec((2, b_, j, rows), lambda pr: (pr, 0, 0, 0)),
                  pl.BlockSpec((1, 2, 4, rows, LANES), lambda pr: (pr, 0, 0, 0, 0))],
        out_specs=[ospec] * 4,
        compiler_params=_cparams(("parallel",)),
        name="s5state",
    )(uf, win)


def _s5scan_kernel(cfr, cfi, cbr, cbi, sfr, sfi, sbr, sbi, dec_ref, hfr, hfi, hbr, hbi):
    jc, jl, w = cfr.shape[0], sfr.shape[0], sfr.shape[1]
    zero = jnp.zeros((1, w), F32)

    def run(sr, si, outs, ar, ai, n, reverse, carry):
        def body(i, hc):
            row = (n - 1 - i) if reverse else i
            hr, hi = hc
            if outs is not None:
                outs[0][pl.ds(row, 1), :] = hr
                outs[1][pl.ds(row, 1), :] = hi
            nr = ar * hr - ai * hi + sr[pl.ds(row, 1), :]
            ni = ar * hi + ai * hr + si[pl.ds(row, 1), :]
            return nr, ni
        return lax.fori_loop(0, n, body, carry)

    afr, afi = dec_ref[0, 0], dec_ref[0, 1]
    abr, abi = dec_ref[1, 0], dec_ref[1, 1]
    c = run(cfr, cfi, None, afr, afi, jc, False, (zero, zero))
    run(sfr, sfi, (hfr, hfi), afr, afi, jl, False, c)
    c = run(cbr, cbi, None, abr, abi, jc, True, (zero, zero))
    run(sbr, sbi, (hbr, hbi), abr, abi, jl, True, c)


def _s5_scan(ctx_states, states, dec_cols):
    jl, wtot = states[0].shape
    jc = ctx_states[0].shape[0]
    tw = WIDE_BLOCK
    spec = pl.BlockSpec((jl, tw), lambda i: (0, i))
    cspec = pl.BlockSpec((jc, tw), lambda i: (0, i))
    return pl.pallas_call(
        _s5scan_kernel,
        out_shape=[jax.ShapeDtypeStruct((jl, wtot), F32)] * 4,
        grid=(wtot // tw,),
        in_specs=[cspec] * 4 + [spec] * 4 + [pl.BlockSpec((2, 2, 1, tw), lambda i: (0, 0, 0, i))],
        out_specs=[spec] * 4,
        compiler_params=_cparams(("parallel",)),
        name="s5scan",
    )(*ctx_states, *states, dec_cols)


def _s5out_kernel(u_ref, hfr, hfi, hbr, hbi, wout_ref, m_ref, y_ref):
    rows = u_ref.shape[3]
    for b in range(u_ref.shape[1]):
        y2 = None
        for kind, h in enumerate((hfr, hfi, hbr, hbi)):
            t = _dot(h[:, b * LANES:(b + 1) * LANES].astype(BF16), wout_ref[0, kind])
            y2 = t if y2 is None else y2 + t
        for gl in range(2):
            y = y2[:, gl * rows:(gl + 1) * rows] + _dot(u_ref[gl, b], m_ref[0, gl])
            y_ref[gl, b] = y.astype(BF16)


def _s5_outputs(uf, hprev, wout, m):
    g, b_, j, rows = uf.shape
    npair = g // 2
    hspec = pl.BlockSpec((j, b_ * LANES), lambda pr: (0, pr))
    return pl.pallas_call(
        _s5out_kernel,
        out_shape=jax.ShapeDtypeStruct((g, b_, j, rows), BF16),
        grid=(npair,),
        in_specs=[pl.BlockSpec((2, b_, j, rows), lambda pr: (pr, 0, 0, 0))] + [hspec] * 4 +
                 [pl.BlockSpec((1, 4, LANES, 2 * rows), lambda pr: (pr, 0, 0, 0)),
                  pl.BlockSpec((1, 2, rows, rows), lambda pr: (pr, 0, 0, 0))],
        out_specs=pl.BlockSpec((2, b_, j, rows), lambda pr: (pr, 0, 0, 0)),
        compiler_params=_cparams(("parallel",)),
        name="s5out",
    )(uf, *hprev, wout, m)


def _chunks_per_step(j):
    return max(n for n in (64, 32, 16, 8) if j % n == 0)


def _s5_regroup_kernel(nj, u_ref, o_ref):
    tc, p = S5_CHUNK, SSM_GROUP
    for s in range(tc):
        xs = u_ref[0, pl.ds(s, nj, stride=tc), :].astype(o_ref.dtype)
        for gi in range(LANES // p):
            o_ref[gi, 0, :, s * p:(s + 1) * p] = xs[:, gi * p:(gi + 1) * p]


def _s5_regroup(u):
    b_, l_, w = u.shape
    tc, p = S5_CHUNK, SSM_GROUP
    gpt = LANES // p
    j = l_ // tc
    nj = _chunks_per_step(j)
    return pl.pallas_call(
        functools.partial(_s5_regroup_kernel, nj),
        out_shape=jax.ShapeDtypeStruct((w // p, b_, j, tc * p), BF16),
        grid=(b_, j // nj, w // LANES),
        in_specs=[pl.BlockSpec((1, tc * nj, LANES), lambda b, i, c: (b, i, c))],
        out_specs=pl.BlockSpec((gpt, 1, nj, tc * p), lambda b, i, c: (c, b, i, 0)),
        compiler_params=_cparams(("parallel", "parallel", "parallel")),
        name="s5regroup",
    )(u)


def _s5_ungroup_kernel(nj, y_ref, o_ref):
    tc, p = S5_CHUNK, SSM_GROUP
    for t in range(tc):
        row = jnp.concatenate([y_ref[gi, 0, :, t * p:(t + 1) * p] for gi in range(LANES // p)], axis=1)
        o_ref[0, pl.ds(t, nj, stride=tc), :] = row.astype(F32)


def _s5_ungroup(yf):
    g, b_, j, rows = yf.shape
    tc, p = S5_CHUNK, SSM_GROUP
    gpt = LANES // p
    nj = _chunks_per_step(j)
    return pl.pallas_call(
        functools.partial(_s5_ungroup_kernel, nj),
        out_shape=jax.ShapeDtypeStruct((b_, j * tc, g * p), F32),
        grid=(b_, j // nj, g // gpt),
        in_specs=[pl.BlockSpec((gpt, 1, nj, rows), lambda b, i, c: (c, b, i, 0))],
        out_specs=pl.BlockSpec((1, tc * nj, LANES), lambda b, i, c: (b, i, c)),
        compiler_params=_cparams(("parallel", "parallel", "parallel")),
        name="s5ungroup",
    )(yf)


def _s5_mixer(uf, u_c, ops):
    win, wout, m, dec = ops
    g, b_ = uf.shape[0], uf.shape[1]
    npair = g // 2
    states = _s5_states(uf, win)
    ctx_states = _s5_states(_s5_regroup(u_c), win)
    dec_cols = jnp.broadcast_to(dec.transpose(1, 2, 3, 0, 4)[:, :, :, :, None], (2, 2, 1, npair, b_, LANES))
    dec_cols = dec_cols.reshape(2, 2, 1, npair * b_ * LANES)
    hprev = _s5_scan(ctx_states, states, dec_cols)
    return _s5_ungroup(_s5_outputs(uf, hprev, wout, m))


def _attn_kernel(nkv, qt_ref, k_ref, vx_ref, o_ref, s0_ref, s1_ref):
    hd = HEAD_DIM
    kvh = pl.program_id(1)
    tq = qt_ref.shape[2]
    gq = qt_ref.shape[1] // hd
    kw = k_ref.shape[3]
    vr = vx_ref.shape[3]
    own = (lax.broadcasted_iota(jnp.int32, (kw, tq), 0) >> (hd.bit_length() - 1)) == kvh
    qp = []
    for g in range(gq):
        qg = qt_ref[0, g * hd:(g + 1) * hd, :].astype(F32)
        q2 = jnp.concatenate([qg] * (kw // hd), axis=0)
        qp.append(jnp.where(own, q2, 0.0).astype(BF16))
    qp = jnp.concatenate(qp, axis=1)
    m0 = jnp.full((1, gq * tq), -jnp.inf, F32)
    acc0 = jnp.zeros((vr, gq * tq), F32)

    def scores(jb, s_ref):
        s = _dot(k_ref[0, jb], qp)
        s_ref[...] = s
        return jnp.max(s, axis=0, keepdims=True)

    def consume(jb, s_ref, cmax, m_prev, acc):
        m_new = jnp.maximum(m_prev, cmax)
        alpha = jnp.exp2(m_prev - m_new)
        p = jnp.exp2(s_ref[...] - m_new).astype(BF16)
        return m_new, alpha * acc + _dot(vx_ref[0, 0, jb], p)

    bufs = (s0_ref, s1_ref)

    def steps(j0, count, m, acc, cmax):
        for u in range(count):
            cnext = scores(j0 + u + 1, bufs[(u + 1) % 2])
            m, acc = consume(j0 + u, bufs[u % 2], cmax, m, acc)
            cmax = cnext
        return m, acc, cmax

    unroll = ATTN_STEPS_PER_TRIP
    trips = (nkv - 1) // unroll
    carry = lax.fori_loop(0, trips, lambda i, c: steps(unroll * i, unroll, *c), (m0, acc0, scores(0, s0_ref)))
    m, acc, cmax = steps(trips * unroll, (nkv - 1) - trips * unroll, *carry)
    m, acc = consume(nkv - 1, bufs[(nkv - 1) % 2], cmax, m, acc)
    o = acc[0:hd] / acc[hd:hd + 1]
    ot = jnp.concatenate([o[:, g * tq:(g + 1) * tq] for g in range(gq)], axis=0)
    o_ref[0] = ot.T.astype(BF16)


def _attention(qt, k4, vx, tq):
    b_, wq, l_ = qt.shape
    _, kvh, nkv, vr, tk = vx.shape
    kw = k4.shape[3]
    gw = wq // kvh
    return pl.pallas_call(
        functools.partial(_attn_kernel, nkv),
        out_shape=jax.ShapeDtypeStruct((b_, l_, wq), BF16),
        grid=(b_, kvh, l_ // tq),
        in_specs=[pl.BlockSpec((1, gw, tq), lambda b, h, i: (b, h, i)),
                  pl.BlockSpec((1, nkv, tk, kw), lambda b, h, i: (b, 0, 0, 0)),
                  pl.BlockSpec((1, 1, nkv, vr, tk), lambda b, h, i: (b, h, 0, 0, 0))],
        out_specs=pl.BlockSpec((1, tq, gw), lambda b, h, i: (b, i, h)),
        scratch_shapes=[pltpu.VMEM((tk, (gw // HEAD_DIM) * tq), F32)] * 2,
        compiler_params=_cparams(("parallel", "parallel", "parallel")),
        name="attn",
    )(qt, k4, vx)


def _merge_kernel(dm, ys_ref, oa_ref, gl_ref, x_ref, mod_ref, wglu_ref, bglu_ref, wso_ref, wao_ref,
                  wout_ref, n2g_ref, rw_ref, rb_ref, utri_ref, xmix_ref, h2_ref, idx_ref, gate_ref, rank_ref,
                  cnt_out_ref, cnt_ref):
    tm = x_ref.shape[1]
    parts = [pl.ds(r * (tm // MERGE_ROW_PARTS), tm // MERGE_ROW_PARTS) for r in range(MERGE_ROW_PARTS)]
    g1 = mod_ref[0, :, 2 * dm:3 * dm]
    sh2 = mod_ref[0, :, 3 * dm:4 * dm]
    sc2 = mod_ref[0, :, 4 * dm:5 * dm]
    s = [jax.nn.gelu(ys_ref[0, r, :]) for r in parts]
    z = [_dot(sr.astype(BF16), wglu_ref[...]) + bglu_ref[...] for sr in s]
    s = [sr * jax.nn.sigmoid(zr) for sr, zr in zip(s, z)]
    a = [_dot(sr.astype(BF16), wso_ref[...]) for sr in s]
    bq = [_dot(oa_ref[0, r, :], wao_ref[...]) for r in parts]
    merged = []
    for r, ar, br in zip(parts, a, bq):
        gts = jax.nn.sigmoid(gl_ref[0, r, :].astype(F32))
        merged.append((gts[:, 0:dm] * ar + gts[:, dm:2 * dm] * br).astype(BF16))
    out = [_dot(mr, wout_ref[...]) for mr in merged]
    h2_parts = []
    for r, outr in zip(parts, out):
        xm = x_ref[0, r, :] + g1 * outr
        xmix_ref[0, r, :] = xm
        ms = jnp.mean(xm * xm, axis=-1, keepdims=True)
        h2r = xm * lax.rsqrt(ms + EPS) * n2g_ref[...] * (1.0 + sc2) + sh2
        h2_ref[0, r, :] = _pack2(h2r[:, 0:dm // 2], h2r[:, dm // 2:dm])
        h2_parts.append(h2r)
    logits = jnp.concatenate([_dot3(h2r, rw_ref[...]) for h2r in h2_parts], axis=0) + rb_ref[...]
    nep = cnt_ref.shape[0]
    l = logits.T[0:nep, :]
    sub = lax.broadcasted_iota(jnp.int32, l.shape, 0).astype(F32)
    neg = jnp.float32(-jnp.inf)
    vals, idxs = [], []
    for _ in range(TOP_K):
        mx = jnp.max(l, axis=0, keepdims=True)
        ix = jnp.min(jnp.where(l == mx, sub, float(nep)), axis=0, keepdims=True)
        vals.append(mx)
        idxs.append(ix)
        l = jnp.where(sub == ix, neg, l)
    es = [jnp.exp(v - vals[0]) for v in vals]
    den = es[0] + es[1] + es[2] + es[3]
    @pl.when((pl.program_id(0) == 0) & (pl.program_id(1) == 0))
    def _():
        cnt_ref[...] = jnp.zeros(cnt_ref.shape, F32)

    onehot = jnp.zeros(l.shape, F32)
    for k in range(TOP_K):
        onehot = onehot + jnp.where(sub == idxs[k], 1.0, 0.0)
    prior = _dot(onehot.astype(BF16), utri_ref[...]) + cnt_ref[...]
    cnt_ref[...] = cnt_ref[...] + jnp.sum(onehot, axis=1, keepdims=True)
    cnt_out_ref[...] = cnt_ref[...]
    krow = lax.broadcasted_iota(jnp.int32, (8, l.shape[1]), 0)
    idx_out = jnp.zeros(krow.shape, F32)
    gate_out = jnp.zeros(krow.shape, F32)
    rank_out = jnp.zeros(krow.shape, F32)
    for k in range(TOP_K):
        rk = jnp.sum(jnp.where(sub == idxs[k], prior, 0.0), axis=0, keepdims=True)
        idx_out = jnp.where(krow == k, idxs[k], idx_out)
        gate_out = jnp.where(krow == k, es[k] / den, gate_out)
        rank_out = jnp.where(krow == k, rk, rank_out)
    idx_ref[0] = idx_out.astype(jnp.int32)
    gate_ref[0] = jnp.concatenate([gate_out, jnp.zeros((LANES - 8, gate_out.shape[1]), F32)], axis=0).T
    rank_ref[0] = rank_out.astype(jnp.int32)


def _merge(y_ssm, o_attn, glog, x, mod3, w_glu, b_glu, w_ssm_out, w_attn_out, w_out, norm2_g,
           rw_pad, rb_pad, ne, tm):
    b_, l_, dm = x.shape
    wu = dm // 2
    tok = lambda w: pl.BlockSpec((1, tm, w), lambda b, i: (b, i, 0))
    full = lambda r, c: pl.BlockSpec((r, c), lambda b, i: (0, 0))
    utri = (jnp.arange(tm)[:, None] < jnp.arange(tm)[None, :]).astype(BF16)
    nep = -(-ne // 8) * 8
    kmaj = pl.BlockSpec((1, 8, tm), lambda b, i: (b, 0, i))
    return pl.pallas_call(
        functools.partial(_merge_kernel, dm),
        out_shape=[jax.ShapeDtypeStruct((b_, l_, dm), F32),
                   jax.ShapeDtypeStruct((b_, l_, dm // 2), jnp.uint32),
                   jax.ShapeDtypeStruct((b_, 8, l_), jnp.int32),
                   jax.ShapeDtypeStruct((b_, l_, LANES), F32),
                   jax.ShapeDtypeStruct((b_, 8, l_), jnp.int32),
                   jax.ShapeDtypeStruct((nep, 1), F32)],
        grid=(b_, l_ // tm),
        in_specs=[tok(wu), tok(wu), tok(2 * dm), tok(dm),
                  pl.BlockSpec((1, 1, N_MOD * dm), lambda b, i: (b, 0, 0)),
                  full(wu, wu), full(1, wu), full(wu, dm), full(wu, dm), full(dm, dm),
                  full(1, dm), full(dm, LANES), full(1, LANES), full(tm, tm)],
        out_specs=[tok(dm), tok(dm // 2), kmaj, tok(LANES), kmaj, full(nep, 1)],
        scratch_shapes=[pltpu.VMEM((nep, 1), F32)],
        compiler_params=_cparams(("arbitrary", "arbitrary")),
        name="merge",
    )(y_ssm, o_attn, glog, x, mod3, w_glu, b_glu, w_ssm_out, w_attn_out, w_out, norm2_g, rw_pad, rb_pad, utri)


def _expert_kernel(de, be_ref, nv_ref, x_ref, wgu_ref, bgu_ref, wd_ref, bd_ref, y_ref, wgu_b, wd_b):
    i = pl.program_id(0)
    n_valid = nv_ref[i]
    used = n_valid > 0
    new_expert = (i == 0) | (be_ref[i] != be_ref[jnp.maximum(i - 1, 0)])

    @pl.when(used & new_expert)
    def _():
        wgu_b[...] = wgu_ref[0].astype(BF16)
        wd_b[...] = wd_ref[0].astype(BF16)

    @pl.when(used)
    def _():
        row = lax.broadcasted_iota(jnp.int32, (x_ref.shape[0], 1), 0)
        x_lo, x_hi = _unpack2(jnp.where(row < n_valid, x_ref[...], jnp.uint32(0)))
        xb = jnp.concatenate([x_lo, x_hi], axis=1).astype(BF16)
        gu = _dot(xb, wgu_b[...]) + bgu_ref[0]
        gate = jnp.minimum(gu[:, 0:de], SWIGLU_LIMIT)
        up = jnp.clip(gu[:, de:2 * de], -SWIGLU_LIMIT, SWIGLU_LIMIT)
        act = (up + 1.0) * (gate * jax.nn.sigmoid(SWIGLU_ALPHA * gate))
        y = _dot(act.astype(BF16), wd_b[...]) + bd_ref[0]
        half = y.shape[1] // 2
        y_ref[...] = _pack2(y[:, 0:half], y[:, half:])

    @pl.when(jnp.logical_not(used))
    def _():
        y_ref[...] = jnp.zeros(y_ref.shape, y_ref.dtype)


def _experts(xb, block_expert, block_valid, w_gate_up, b_gate_up, w_down, b_down):
    n_slots, dmh = xb.shape
    dm = 2 * dmh
    ne, _, de2 = w_gate_up.shape
    de = de2 // 2
    nb = n_slots // EXPERT_BLOCK
    return pl.pallas_call(
        functools.partial(_expert_kernel, de),
        out_shape=jax.ShapeDtypeStruct((n_slots, dmh), jnp.uint32),
        grid_spec=pltpu.PrefetchScalarGridSpec(
            num_scalar_prefetch=2,
            grid=(nb,),
            in_specs=[pl.BlockSpec((EXPERT_BLOCK, dmh), lambda i, be, nu: (i, 0)),
                      pl.BlockSpec((1, dm, de2), lambda i, be, nu: (be[i], 0, 0)),
                      pl.BlockSpec((1, 1, de2), lambda i, be, nu: (be[i], 0, 0)),
                      pl.BlockSpec((1, de, dm), lambda i, be, nu: (be[i], 0, 0)),
                      pl.BlockSpec((1, 1, dm), lambda i, be, nu: (be[i], 0, 0))],
            out_specs=pl.BlockSpec((EXPERT_BLOCK, dmh), lambda i, be, nu: (i, 0)),
            scratch_shapes=[pltpu.VMEM((dm, de2), BF16), pltpu.VMEM((de, dm), BF16)],
        ),
        compiler_params=_cparams(("arbitrary",)),
        name="expert",
    )(block_expert, block_valid, xb, w_gate_up, b_gate_up.reshape(ne, 1, de2), w_down, b_down.reshape(ne, 1, dm))


def _sc_gather_rows(table, idx):
    n = idx.shape[0]
    w = table.shape[1]
    mesh = plsc.VectorSubcoreMesh(core_axis_name="c", subcore_axis_name="s")
    nc, nw = mesh.num_cores, mesh.num_cores * mesh.num_subcores
    win = SC_GATHER_WINDOW
    assert n % (win * nw) == 0
    per_worker = n // nw

    @functools.partial(
        pl.kernel, out_type=jax.ShapeDtypeStruct((n, w), table.dtype), mesh=mesh,
        scratch_types=[pltpu.VMEM((win,), jnp.int32), pltpu.VMEM((win, w), table.dtype),
                       pltpu.SemaphoreType.DMA])
    def gather(x_hbm, i_hbm, o_hbm, idx_v, rows_v, sem):
        base = (lax.axis_index("s") * nc + lax.axis_index("c")) * per_worker

        @pl.loop(0, per_worker // win)
        def _(j):
            off = base + j * win
            pltpu.sync_copy(i_hbm.at[pl.ds(off, win)], idx_v)
            pltpu.async_copy(x_hbm.at[idx_v], rows_v, sem).wait()
            pltpu.sync_copy(rows_v, o_hbm.at[pl.ds(off, win)])

    return gather(table, idx)


def _sc_scatter_rows(rows, dest, n_out):
    t_, w = rows.shape
    kk = dest.shape[0] // t_
    mesh = plsc.VectorSubcoreMesh(core_axis_name="c", subcore_axis_name="s")
    nc, nw = mesh.num_cores, mesh.num_cores * mesh.num_subcores
    win = SC_GATHER_WINDOW
    assert t_ % (win * nw) == 0
    per_worker = t_ // nw

    @functools.partial(
        pl.kernel, out_type=jax.ShapeDtypeStruct((n_out, w), rows.dtype), mesh=mesh,
        scratch_types=[pltpu.VMEM((win,), jnp.int32)] * kk + [pltpu.VMEM((win, w), rows.dtype),
                                                              pltpu.SemaphoreType.DMA])
    def scatter(x_hbm, d_hbm, o_hbm, *scratch):
        idx_vs, rows_v, sem = scratch[:kk], scratch[kk], scratch[kk + 1]
        base = (lax.axis_index("s") * nc + lax.axis_index("c")) * per_worker

        @pl.loop(0, per_worker // win)
        def _(j):
            off = base + j * win
            pltpu.sync_copy(x_hbm.at[pl.ds(off, win)], rows_v)
            for k in range(kk):
                pltpu.sync_copy(d_hbm.at[pl.ds(k * t_ + off, win)], idx_vs[k])
            copies = [pltpu.async_copy(rows_v, o_hbm.at[idx_vs[k]], sem) for k in range(kk)]
            for cp in copies:
                cp.wait()

    return scatter(rows, dest)


def _final_kernel(dm, yg_ref, gate_ref, xm_ref, mod_ref, fg_ref, *rest):
    o_ref = rest[-1]
    gates = gate_ref[0]
    f_lo = f_hi = None
    for k in range(TOP_K):
        y_lo, y_hi = _unpack2(yg_ref[k, 0])
        gk = gates[:, k:k + 1]
        f_lo = gk * y_lo if f_lo is None else f_lo + gk * y_lo
        f_hi = gk * y_hi if f_hi is None else f_hi + gk * y_hi
    f = jnp.concatenate([f_lo, f_hi], axis=1)
    g2 = mod_ref[0, :, 5 * dm:6 * dm]
    xo = xm_ref[0] + g2 * f
    ms = jnp.mean(xo * xo, axis=-1, keepdims=True)
    o_ref[0] = xo * lax.rsqrt(ms + EPS) * fg_ref[...]


def _final(yg, gates, x_mix, mod3, final_g, tm, sample, prev):
    b_, l_, dm = x_mix.shape
    in_specs = [pl.BlockSpec((TOP_K, 1, tm, dm // 2), lambda i: (0, 0, i, 0)),
                pl.BlockSpec((1, tm, LANES), lambda i: (sample, i, 0)),
                pl.BlockSpec((1, tm, dm), lambda i: (sample, i, 0)),
                pl.BlockSpec((1, 1, N_MOD * dm), lambda i: (sample, 0, 0)),
                pl.BlockSpec((1, dm), lambda i: (0, 0))]
    args = [yg, gates, x_mix, mod3, final_g.reshape(1, dm)]
    aliases = {}
    if prev is not None:
        in_specs.append(pl.BlockSpec(memory_space=pl.ANY))
        args.append(prev)
        aliases = {len(args) - 1: 0}
    return pl.pallas_call(
        functools.partial(_final_kernel, dm),
        out_shape=jax.ShapeDtypeStruct((b_, l_, dm), F32),
        grid=(l_ // tm,),
        in_specs=in_specs,
        out_specs=pl.BlockSpec((1, tm, dm), lambda i: (sample, i, 0)),
        input_output_aliases=aliases,
        compiler_params=_cparams(("parallel",)),
        name="final",
    )(*args)


def _rope_tables(l_):
    half = HEAD_DIM // 2
    inv_freq = ROPE_THETA ** (-jnp.arange(0, half, 2, dtype=F32) / half)
    t = jnp.arange(l_, dtype=jnp.int32)
    row_id, col_id = t // GRID_W, t % GRID_W
    ang = jnp.concatenate([row_id.astype(F32)[:, None] * inv_freq,
                           col_id.astype(F32)[:, None] * inv_freq], axis=-1)
    cos = jnp.repeat(jnp.cos(ang), 2, axis=-1)
    sin = jnp.repeat(jnp.sin(ang), 2, axis=-1)
    sign = jnp.tile(jnp.array([-1.0, 1.0], F32), HEAD_DIM // 2)
    reps = LANES // HEAD_DIM
    return jnp.tile(cos, (1, reps)), jnp.tile(sin * sign, (1, reps))


def _routing(idx, rank, counts, t_, ne):
    m_ = t_ * TOP_K
    blk = EXPERT_BLOCK
    nb = (m_ + ne * (blk - 1)) // blk
    nb = -(-nb // SLOT_BLOCK_MULTIPLE) * SLOT_BLOCK_MULTIPLE
    blk_counts = (counts + blk - 1) // blk
    blk_ends = jnp.cumsum(blk_counts)
    first_slot = (blk_ends - blk_counts) * blk
    onehot = idx[:, :, None] == jnp.arange(ne, dtype=jnp.int32)[None, None, :]
    dest = rank + jnp.sum(jnp.where(onehot, first_slot[None, None, :], 0), axis=2)
    blocks = jnp.arange(nb, dtype=jnp.int32)
    block_expert = jnp.minimum(jnp.sum(blocks[:, None] >= blk_ends[None, :], axis=1), ne - 1)
    in_expert = blocks * blk - first_slot[block_expert]
    block_valid = jnp.clip(counts[block_expert] - in_expert, 0, blk)
    block_valid = jnp.where(blocks < blk_ends[-1], block_valid, 0)
    return dest.astype(jnp.int32), block_expert.astype(jnp.int32), block_valid.astype(jnp.int32), nb * blk


def kernel(x, c, ctx, c_ctx, w_mod, b_mod, norm1_g, norm2_g, w_in, s5_lam_re, s5_lam_im, s5_log_step,
           s5_b_re, s5_b_im, s5_c_re, s5_c_im, s5_d, w_glu, b_glu, w_ssm_out, q_norm_g, k_norm_g,
           w_attn_out, w_out, router_w, router_b, w_gate_up, b_gate_up, w_down, b_down, final_norm_g):
    b_, l_, dm = x.shape
    lc = ctx.shape[1]
    depth = w_mod.shape[0]
    assert depth == 1, "single-layer block"
    assert b_ <= 7 and l_ % TOKEN_BLOCK == 0 and lc % S5_CHUNK == 0
    wu = dm // 2
    wkv = wu // GQA_GROUP
    kvh = wkv // HEAD_DIM
    ne = router_w.shape[-1]
    i = 0

    c8 = jnp.zeros((8, dm), F32).at[:b_].set(c).at[b_].set(c_ctx)
    mod3 = _modulation(c8, w_mod[i], b_mod[i]).reshape(8, 1, N_MOD * dm)

    w_in_b = w_in[i].astype(BF16)
    cos, sin = _rope_tables(l_)
    hsum = jnp.kron(jnp.eye(wu // HEAD_DIM, dtype=F32), jnp.ones((HEAD_DIM, HEAD_DIM), F32)).astype(BF16)
    qg = jnp.tile(q_norm_g[i], wu // HEAD_DIM).reshape(1, wu)
    kg = jnp.tile(k_norm_g[i], wkv // HEAD_DIM).reshape(1, wkv)
    uf, k, vt, qt, glog = _inproj(x, mod3, None, norm1_g[i], w_in_b, cos, sin, qg, kg, hsum, True, TOKEN_BLOCK)
    ones_t = jnp.ones((lc, LANES), F32)
    u_c, k_c, vt_c = _inproj(ctx, mod3, b_, norm1_g[i], w_in_b[:, :wu + 2 * wkv], ones_t, 0.0 * ones_t,
                             qg, kg, hsum, False, lc)

    ops = _s5_operators(s5_lam_re[i], s5_lam_im[i], s5_log_step[i], s5_b_re[i], s5_b_im[i],
                        s5_c_re[i], s5_c_im[i], s5_d[i])
    y_ssm = _s5_mixer(uf, u_c, ops)

    s_all = lc + l_
    tk = max(t for t in range(LANES, ATTN_KEY_CHUNK + 1, LANES) if s_all % t == 0)
    nkv = s_all // tk
    k4 = jnp.concatenate([k_c, k], axis=1).reshape(b_, nkv, tk, wkv)
    vt_all = jnp.concatenate([vt_c, vt], axis=2).reshape(b_, kvh, HEAD_DIM, s_all)
    vx = jnp.concatenate([vt_all, jnp.ones((b_, kvh, 1, s_all), BF16),
                          jnp.zeros((b_, kvh, ATTN_V_ROWS - HEAD_DIM - 1, s_all), BF16)], axis=2)
    vx = vx.reshape(b_, kvh, ATTN_V_ROWS, nkv, tk).transpose(0, 1, 3, 2, 4)
    o_attn = _attention(qt, k4, vx, ATTN_Q_BLOCK)

    rw_pad = jnp.zeros((dm, LANES), F32).at[:, :ne].set(router_w[i])
    rb_pad = jnp.full((1, LANES), -jnp.inf, F32).at[0, :ne].set(router_b[i])
    x_mix, h2, idx, gates, rank, counts = _merge(
        y_ssm, o_attn, glog, x, mod3, w_glu[i].astype(BF16), b_glu[i].reshape(1, wu),
        w_ssm_out[i].astype(BF16), w_attn_out[i].astype(BF16), w_out[i].astype(BF16),
        norm2_g[i].reshape(1, dm), rw_pad, rb_pad, ne, TOKEN_BLOCK)

    t_ = b_ * l_
    kmajor = lambda a: a[:, :TOP_K, :].transpose(1, 0, 2).reshape(TOP_K, t_)
    dest_kmajor, block_expert, block_valid, n_slots = _routing(
        kmajor(idx), kmajor(rank), counts[:ne, 0].astype(jnp.int32), t_, ne)
    xb = _sc_scatter_rows(h2.reshape(t_, dm // 2), dest_kmajor.reshape(-1), n_slots)
    yb = _experts(xb, block_expert, block_valid, w_gate_up[i], b_gate_up[i], w_down[i], b_down[i])
    out = None
    for b in range(b_):
        yg = _sc_gather_rows(yb, dest_kmajor[:, b * l_:(b + 1) * l_].reshape(-1)).reshape(TOP_K, 1, l_, dm // 2)
        out = _final(yg, gates, x_mix, mod3, final_norm_g, TOKEN_BLOCK, b, out)
    return out
```

```python
import functools

import jax
import jax.numpy as jnp
from jax import lax
from jax.experimental import pallas as pl
from jax.experimental.pallas import tpu as pltpu
from jax.experimental.pallas import tpu_sc as plsc

F32 = jnp.float32
BF16 = jnp.bfloat16

EPS = 1e-6
GRID_W = 64
N_MOD = 6
SSM_GROUP = 16
SSM_STATE = 64
HEAD_DIM = 64
GQA_GROUP = 4
ROPE_THETA = 10000.0
TOP_K = 4
SWIGLU_ALPHA = 1.702
SWIGLU_LIMIT = 7.0
LOG2E = 1.4426950408889634

LANES = 128
S5_CHUNK = 32
S5_POW_ROWS = 56
TOKEN_BLOCK = 512
WIDE_BLOCK = 1024
EXPERT_BLOCK = 512
ATTN_V_ROWS = 80
ATTN_KEY_CHUNK = 384
ATTN_SUB_BLOCKS = 4
ATTN_Q_BLOCK = 256 * ATTN_SUB_BLOCKS
ATTN_STEPS_PER_TRIP = 10
MERGE_ROW_PARTS = 2
SC_GATHER_WINDOW = 128
SC_WORKERS = 32
SLOT_BLOCK_MULTIPLE = SC_GATHER_WINDOW * SC_WORKERS // EXPERT_BLOCK
VMEM_LIMIT = 56 * 1024 * 1024


def _cparams(sem):
    return pltpu.CompilerParams(dimension_semantics=sem, vmem_limit_bytes=VMEM_LIMIT)


def _dot(a, b):
    return jnp.dot(a, b, preferred_element_type=F32)


def _split(a):
    hi = a.astype(BF16)
    lo = (a - hi.astype(F32)).astype(BF16)
    return hi, lo


def _pack2(lo, hi):
    lo_w = lax.bitcast_convert_type(lo.astype(BF16).astype(F32), jnp.uint32) >> 16
    hi_w = lax.bitcast_convert_type(hi.astype(BF16).astype(F32), jnp.uint32) & jnp.uint32(0xFFFF0000)
    return lo_w | hi_w


def _unpack2(words):
    lo = lax.bitcast_convert_type(words << 16, F32)
    hi = lax.bitcast_convert_type(words & jnp.uint32(0xFFFF0000), F32)
    return lo, hi


def _dot3(a, b):
    ah, al = _split(a)
    bh, bl = _split(b)
    return _dot(ah, bh) + _dot(ah, bl) + _dot(al, bh)


def _mod_kernel(c_ref, w_ref, b_ref, o_ref):
    c = c_ref[...]
    s = c * jax.nn.sigmoid(c)
    o_ref[...] = _dot3(s, w_ref[...]) + b_ref[...]


def _modulation(c8, w_mod, b_mod):
    d = c8.shape[1]
    n = w_mod.shape[1]
    tn = WIDE_BLOCK
    return pl.pallas_call(
        _mod_kernel,
        out_shape=jax.ShapeDtypeStruct((8, n), F32),
        grid=(n // tn,),
        in_specs=[pl.BlockSpec((8, d), lambda j: (0, 0)),
                  pl.BlockSpec((d, tn), lambda j: (0, j)),
                  pl.BlockSpec((1, tn), lambda j: (0, j))],
        out_specs=pl.BlockSpec((8, tn), lambda j: (0, j)),
        compiler_params=_cparams(("parallel",)),
        name="mod",
    )(c8, w_mod, b_mod.reshape(1, n))


def _head_norm_rope(z, gain, hsum, cos, sin, scale):
    ss = _dot((z * z).astype(BF16), hsum)
    zn = z * lax.rsqrt(ss * (1.0 / HEAD_DIM) + EPS) * gain
    lane = lax.broadcasted_iota(jnp.int32, (1, LANES), 1)
    even = (lane & 1) == 0
    outs = []
    for ci in range(z.shape[1] // LANES):
        ch = zn[:, ci * LANES:(ci + 1) * LANES]
        nxt = pltpu.roll(ch, LANES - 1, axis=1)
        prv = pltpu.roll(ch, 1, axis=1)
        sw = jnp.where(even, nxt, prv)
        outs.append((ch * cos + sw * sin) * scale)
    return outs


def _inproj_kernel(with_q, dm, x_ref, mod_ref, g_ref, w_ref, cos_ref, sin_ref, qg_ref, kg_ref,
                   hsum_ref, *out_refs):
    x = x_ref[0]
    ms = jnp.mean(x * x, axis=-1, keepdims=True)
    y = x * lax.rsqrt(ms + EPS) * g_ref[...]
    sh = mod_ref[0, :, 0:dm]
    sc = mod_ref[0, :, dm:2 * dm]
    hb = (y * (1.0 + sc) + sh).astype(BF16)
    cos = cos_ref[...]
    sin = sin_ref[...]
    wu = dm // 2
    wkv = wu // GQA_GROUP
    c_k, c_v, c_q, c_g = wu, wu + wkv, wu + 2 * wkv, 2 * wu + 2 * wkv
    u_ref, k_ref, vt_ref = out_refs[:3]
    u = _dot(hb, w_ref[:, 0:c_k])
    if with_q:
        ubuf = out_refs[-1]
        p, gpt, nj = SSM_GROUP, LANES // SSM_GROUP, x.shape[0] // S5_CHUNK
        for ci in range(wu // LANES):
            ubuf[ci] = u[:, ci * LANES:(ci + 1) * LANES]
        for s in range(S5_CHUNK):
            for ci in range(wu // LANES):
                xs = ubuf[ci, pl.ds(s, nj, stride=S5_CHUNK), :].astype(BF16)
                for gi in range(gpt):
                    u_ref[ci * gpt + gi, 0, :, s * p:(s + 1) * p] = xs[:, gi * p:(gi + 1) * p]
    else:
        u_ref[0] = u
    kz = _dot(hb, w_ref[:, c_k:c_v])
    (kr,) = _head_norm_rope(kz, kg_ref[...], hsum_ref[0:wkv, 0:wkv], cos, sin, 1.0)
    k_ref[0] = kr.astype(BF16)
    vt_ref[0] = _dot(hb, w_ref[:, c_v:c_q]).T.astype(BF16)
    if with_q:
        qt_ref, gl_ref = out_refs[3:5]
        qz = _dot(hb, w_ref[:, c_q:c_g])
        qs = _head_norm_rope(qz, qg_ref[...], hsum_ref[...], cos, sin, HEAD_DIM ** -0.5 * LOG2E)
        for ci, qc in enumerate(qs):
            qt_ref[0, ci * LANES:(ci + 1) * LANES, :] = qc.T.astype(BF16)
        gl_ref[0] = _dot(hb, w_ref[:, c_g:]).astype(BF16)


def _inproj(x, mod3, mod_row0, norm_g, w_in_b, cos, sin, qg, kg, hsum, with_q, tm):
    b_, l_, dm = x.shape
    wu = dm // 2
    wkv = wu // GQA_GROUP
    n_in = w_in_b.shape[1]
    outs = [jax.ShapeDtypeStruct((b_, l_, wu), F32),
            jax.ShapeDtypeStruct((b_, l_, wkv), BF16),
            jax.ShapeDtypeStruct((b_, wkv, l_), BF16)]
    ospecs = [pl.BlockSpec((1, tm, wu), lambda b, i: (b, i, 0)),
              pl.BlockSpec((1, tm, wkv), lambda b, i: (b, i, 0)),
              pl.BlockSpec((1, wkv, tm), lambda b, i: (b, 0, i))]
    scratch = []
    if with_q:
        g, rows = wu // SSM_GROUP, S5_CHUNK * SSM_GROUP
        outs[0] = jax.ShapeDtypeStruct((g, b_, l_ // S5_CHUNK, rows), BF16)
        ospecs[0] = pl.BlockSpec((g, 1, tm // S5_CHUNK, rows), lambda b, i: (0, b, i, 0))
        outs += [jax.ShapeDtypeStruct((b_, wu, l_), BF16),
                 jax.ShapeDtypeStruct((b_, l_, 2 * dm), BF16)]
        ospecs += [pl.BlockSpec((1, wu, tm), lambda b, i: (b, 0, i)),
                   pl.BlockSpec((1, tm, 2 * dm), lambda b, i: (b, i, 0))]
        scratch = [pltpu.VMEM((wu // LANES, tm, LANES), F32)]
    if mod_row0 is None:
        mod_map = lambda b, i: (b, 0, 0)
    else:
        mod_map = lambda b, i: (mod_row0, 0, 0)
    return pl.pallas_call(
        functools.partial(_inproj_kernel, with_q, dm),
        out_shape=outs,
        grid=(b_, l_ // tm),
        in_specs=[pl.BlockSpec((1, tm, dm), lambda b, i: (b, i, 0)),
                  pl.BlockSpec((1, 1, N_MOD * dm), mod_map),
                  pl.BlockSpec((1, dm), lambda b, i: (0, 0)),
                  pl.BlockSpec((dm, n_in), lambda b, i: (0, 0)),
                  pl.BlockSpec((tm, LANES), lambda b, i: (i, 0)),
                  pl.BlockSpec((tm, LANES), lambda b, i: (i, 0)),
                  pl.BlockSpec((1, wu), lambda b, i: (0, 0)),
                  pl.BlockSpec((1, wkv), lambda b, i: (0, 0)),
                  pl.BlockSpec((wu, wu), lambda b, i: (0, 0))],
        out_specs=ospecs,
        scratch_shapes=scratch,
        compiler_params=_cparams(("parallel", "parallel")),
        name="inproj_x" if with_q else "inproj_ctx",
    )(x, mod3, norm_g.reshape(1, dm), w_in_b, cos, sin, qg, kg, hsum)


def _cexp_pow(e, lr, li):
    mag = jnp.exp(e * lr)
    ang = e * li
    return mag * jnp.cos(ang), mag * jnp.sin(ang)


def _cmul(ar, ai, br, bi):
    return ar * br - ai * bi, ar * bi + ai * br


def _s5pre_kernel(lamr_r, lami_r, lst_r, lamr_c, lami_c, lst_c, btr_ref, bti_ref, ctr_ref, cti_ref,
                  d_ref, win_ref, wout_ref, m_ref, dec_ref):
    tc = S5_CHUNK
    half = tc // 2
    p = SSM_GROUP
    rows = tc * p
    gsh = SSM_GROUP.bit_length() - 1
    nsh = SSM_STATE.bit_length() - 1
    row_i = lax.broadcasted_iota(jnp.int32, (rows, rows), 0)
    col_i = lax.broadcasted_iota(jnp.int32, (rows, rows), 1)
    lane_g = lax.broadcasted_iota(jnp.int32, (1, LANES), 1) >> nsh
    subl_g = lax.broadcasted_iota(jnp.int32, (LANES, 1), 0) >> nsh
    e_rows = lax.broadcasted_iota(jnp.int32, (S5_POW_ROWS, 1), 0) - half
    e_lanes = lax.broadcasted_iota(jnp.int32, (1, LANES), 1) - half
    sel_e = lax.broadcasted_iota(jnp.int32, (LANES, rows), 0) - half
    sel_t = lax.broadcasted_iota(jnp.int32, (LANES, rows), 1) >> gsh
    sel_p = (lax.broadcasted_iota(jnp.int32, (LANES, rows), 0)
             == (lax.broadcasted_iota(jnp.int32, (LANES, rows), 1) & (p - 1))).astype(BF16)

    def spread_lanes(table, sel):
        hi, lo = _split(table)
        return _dot(hi, sel) + _dot(lo, sel)

    def spread_rows(table, e_of_s):
        return jnp.concatenate([jnp.broadcast_to(table[e_of_s(s) + half:e_of_s(s) + half + 1, :], (p, LANES))
                                for s in range(tc)], axis=0)

    m_acc = [jnp.where(row_i == col_i, d_ref[0, gl], 0.0) for gl in range(2)]
    for dr in range(2):
        step = jnp.exp(lst_r[dr, 0])
        lam_r, lam_i = lamr_r[dr, 0], lami_r[dr, 0]
        lr, li = lam_r * step, lam_i * step
        pw_r, pw_i = _cexp_pow(e_rows.astype(F32), lr, li)
        lb_r, lb_i = pw_r[half + 1:half + 2, :], pw_i[half + 1:half + 2, :]
        den = lam_r * lam_r + lam_i * lam_i
        nr, ni = lb_r - 1.0, lb_i
        cf_r = (nr * lam_r + ni * lam_i) / den
        cf_i = (ni * lam_r - nr * lam_i) / den
        bt_r = jnp.concatenate([btr_ref[dr, 0]] * tc, axis=0)
        bt_i = jnp.concatenate([bti_ref[dr, 0]] * tc, axis=0)
        bb_r, bb_i = _cmul(cf_r, cf_i, bt_r, bt_i)
        e_in = (lambda s: tc - 1 - s) if dr == 0 else (lambda s: s)
        wi_r, wi_i = _cmul(bb_r, bb_i, spread_rows(pw_r, e_in), spread_rows(pw_i, e_in))
        e_a = (lambda s: half - s) if dr == 0 else (lambda s: s - half)
        a_r, a_i = _cmul(bb_r, bb_i, spread_rows(pw_r, e_a), spread_rows(pw_i, e_a))
        dec_ref[0, dr, 0] = pw_r[tc + half:tc + half + 1, :]
        dec_ref[0, dr, 1] = pw_i[tc + half:tc + half + 1, :]
        stepc = jnp.exp(lst_c[dr, 0])
        lrc, lic = lamr_c[dr, 0] * stepc, lami_c[dr, 0] * stepc
        qw_r, qw_i = _cexp_pow(e_lanes.astype(F32), lrc, lic)
        qw_r = jnp.where(e_lanes <= tc, qw_r, 0.0)
        qw_i = jnp.where(e_lanes <= tc, qw_i, 0.0)
        c_r, c_i = spread_lanes(ctr_ref[dr, 0], sel_p), spread_lanes(cti_ref[dr, 0], sel_p)
        sel_b = (sel_e == (sel_t - half if dr == 0 else half - sel_t)).astype(BF16)
        bm_r, bm_i = _cmul(c_r, c_i, spread_lanes(qw_r, sel_b), spread_lanes(qw_i, sel_b))
        sel_o = (sel_e == (sel_t + 1 if dr == 0 else tc - sel_t)).astype(BF16)
        wo_r, wo_i = _cmul(c_r, c_i, spread_lanes(qw_r, sel_o), spread_lanes(qw_i, sel_o))
        if dr == 0:
            mask = (col_i >> gsh) >= (row_i >> gsh)
        else:
            mask = (row_i >> gsh) >= (col_i >> gsh)
        for gl in range(2):
            lsel = lane_g == gl
            ssel = subl_g == gl
            win_ref[0, gl, 2 * dr] = jnp.where(lsel, wi_r, 0.0).astype(BF16)
            win_ref[0, gl, 2 * dr + 1] = jnp.where(lsel, wi_i, 0.0).astype(BF16)
            wout_ref[0, 2 * dr, :, gl * rows:(gl + 1) * rows] = jnp.where(ssel, wo_r, 0.0).astype(BF16)
            wout_ref[0, 2 * dr + 1, :, gl * rows:(gl + 1) * rows] = jnp.where(ssel, -wo_i, 0.0).astype(BF16)
            kmat = (_dot3(jnp.where(lsel, a_r, 0.0), bm_r) - _dot3(jnp.where(lsel, a_i, 0.0), bm_i))
            m_acc[gl] = m_acc[gl] + jnp.where(mask, kmat, 0.0)
    for gl in range(2):
        m_ref[0, gl] = m_acc[gl].astype(BF16)


def _s5_operators(lam_re, lam_im, log_step, b_re, b_im, c_re, c_im, d_skip):
    g = lam_re.shape[1]
    n, p = SSM_STATE, SSM_GROUP
    npair = g // 2
    tc = S5_CHUNK
    rows = tc * p

    def row_form(a):
        return a.reshape(2, npair, 1, 2 * n)

    def col_form(a):
        return a.reshape(2, npair, 2 * n, 1)

    lst = jnp.broadcast_to(log_step[:, :, None], (2, g, n))
    bt = lambda b: b.reshape(2, npair, 2, n, p).transpose(0, 1, 4, 2, 3).reshape(2, npair, p, 2 * n)
    ct = lambda c: jnp.pad(c.reshape(2, npair, 2, p, n).transpose(0, 1, 2, 4, 3).reshape(2, npair, 2 * n, p),
                           ((0, 0), (0, 0), (0, 0), (0, LANES - p)))
    d2 = jnp.broadcast_to(d_skip.reshape(npair, 2, 1, 1, p), (npair, 2, 1, tc, p)).reshape(npair, 2, 1, rows)

    def spec4(shape):
        return pl.BlockSpec((2, 1) + shape, lambda i: (0, i, 0, 0))

    return pl.pallas_call(
        _s5pre_kernel,
        out_shape=[jax.ShapeDtypeStruct((npair, 2, 4, rows, LANES), BF16),
                   jax.ShapeDtypeStruct((npair, 4, LANES, 2 * rows), BF16),
                   jax.ShapeDtypeStruct((npair, 2, rows, rows), BF16),
                   jax.ShapeDtypeStruct((npair, 2, 2, 1, LANES), F32)],
        grid=(npair,),
        in_specs=[spec4((1, LANES)), spec4((1, LANES)), spec4((1, LANES)),
                  spec4((LANES, 1)), spec4((LANES, 1)), spec4((LANES, 1)),
                  spec4((p, LANES)), spec4((p, LANES)),
                  spec4((LANES, LANES)), spec4((LANES, LANES)),
                  pl.BlockSpec((1, 2, 1, rows), lambda i: (i, 0, 0, 0))],
        out_specs=[pl.BlockSpec((1, 2, 4, rows, LANES), lambda i: (i, 0, 0, 0, 0)),
                   pl.BlockSpec((1, 4, LANES, 2 * rows), lambda i: (i, 0, 0, 0)),
                   pl.BlockSpec((1, 2, rows, rows), lambda i: (i, 0, 0, 0)),
                   pl.BlockSpec((1, 2, 2, 1, LANES), lambda i: (i, 0, 0, 0, 0))],
        compiler_params=_cparams(("parallel",)),
        name="s5pre",
    )(row_form(lam_re), row_form(lam_im), row_form(lst),
      col_form(lam_re), col_form(lam_im), col_form(lst),
      bt(b_re), bt(b_im), ct(c_re), ct(c_im), d2)


def _s5state_kernel(u_ref, win_ref, *s_refs):
    for b in range(u_ref.shape[1]):
        for kind in range(4):
            s_refs[kind][:, b * LANES:(b + 1) * LANES] = (_dot(u_ref[0, b], win_ref[0, 0, kind])
                                                          + _dot(u_ref[1, b], win_ref[0, 1, kind]))


def _s5_states(uf, win):
    g, b_, j, rows = uf.shape
    npair = g // 2
    out = jax.ShapeDtypeStruct((j, npair * b_ * LANES), F32)
    ospec = pl.BlockSpec((j, b_ * LANES), lambda pr: (0, pr))
    return pl.pallas_call(
        _s5state_kernel,
        out_shape=[out] * 4,
        grid=(npair,),
        in_specs=[pl.BlockSpec((2, b_, j, rows), lambda pr: (pr, 0, 0, 0)),
                  pl.BlockSpec((1, 2, 4, rows, LANES), lambda pr: (pr, 0, 0, 0, 0))],
        out_specs=[ospec] * 4,
        compiler_params=_cparams(("parallel",)),
        name="s5state",
    )(uf, win)


def _s5scan_kernel(cfr, cfi, cbr, cbi, sfr, sfi, sbr, sbi, dec_ref, hfr, hfi, hbr, hbi):
    jc, jl, w = cfr.shape[0], sfr.shape[0], sfr.shape[1]
    zero = jnp.zeros((1, w), F32)

    def run(sr, si, outs, ar, ai, n, reverse, carry):
        def body(i, hc):
            row = (n - 1 - i) if reverse else i
            hr, hi = hc
            if outs is not None:
                outs[0][pl.ds(row, 1), :] = hr
                outs[1][pl.ds(row, 1), :] = hi
            nr = ar * hr - ai * hi + sr[pl.ds(row, 1), :]
            ni = ar * hi + ai * hr + si[pl.ds(row, 1), :]
            return nr, ni
        return lax.fori_loop(0, n, body, carry)

    afr, afi = dec_ref[0, 0], dec_ref[0, 1]
    abr, abi = dec_ref[1, 0], dec_ref[1, 1]
    c = run(cfr, cfi, None, afr, afi, jc, False, (zero, zero))
    run(sfr, sfi, (hfr, hfi), afr, afi, jl, False, c)
    c = run(cbr, cbi, None, abr, abi, jc, True, (zero, zero))
    run(sbr, sbi, (hbr, hbi), abr, abi, jl, True, c)


def _s5_scan(ctx_states, states, dec_cols):
    jl, wtot = states[0].shape
    jc = ctx_states[0].shape[0]
    tw = WIDE_BLOCK
    spec = pl.BlockSpec((jl, tw), lambda i: (0, i))
    cspec = pl.BlockSpec((jc, tw), lambda i: (0, i))
    return pl.pallas_call(
        _s5scan_kernel,
        out_shape=[jax.ShapeDtypeStruct((jl, wtot), F32)] * 4,
        grid=(wtot // tw,),
        in_specs=[cspec] * 4 + [spec] * 4 + [pl.BlockSpec((2, 2, 1, tw), lambda i: (0, 0, 0, i))],
        out_specs=[spec] * 4,
        compiler_params=_cparams(("parallel",)),
        name="s5scan",
    )(*ctx_states, *states, dec_cols)


def _s5out_kernel(u_ref, hfr, hfi, hbr, hbi, wout_ref, m_ref, y_ref):
    rows = u_ref.shape[3]
    for b in range(u_ref.shape[1]):
        y2 = None
        for kind, h in enumerate((hfr, hfi, hbr, hbi)):
            t = _dot(h[:, b * LANES:(b + 1) * LANES].astype(BF16), wout_ref[0, kind])
            y2 = t if y2 is None else y2 + t
        for gl in range(2):
            y = y2[:, gl * rows:(gl + 1) * rows] + _dot(u_ref[gl, b], m_ref[0, gl])
            y_ref[gl, b] = y.astype(BF16)


def _s5_outputs(uf, hprev, wout, m):
    g, b_, j, rows = uf.shape
    npair = g // 2
    hspec = pl.BlockSpec((j, b_ * LANES), lambda pr: (0, pr))
    return pl.pallas_call(
        _s5out_kernel,
        out_shape=jax.ShapeDtypeStruct((g, b_, j, rows), BF16),
        grid=(npair,),
        in_specs=[pl.BlockSpec((2, b_, j, rows), lambda pr: (pr, 0, 0, 0))] + [hspec] * 4 +
                 [pl.BlockSpec((1, 4, LANES, 2 * rows), lambda pr: (pr, 0, 0, 0)),
                  pl.BlockSpec((1, 2, rows, rows), lambda pr: (pr, 0, 0, 0))],
        out_specs=pl.BlockSpec((2, b_, j, rows), lambda pr: (pr, 0, 0, 0)),
        compiler_params=_cparams(("parallel",)),
        name="s5out",
    )(uf, *hprev, wout, m)


def _chunks_per_step(j):
    return max(n for n in (64, 32, 16, 8) if j % n == 0)


def _s5_regroup_kernel(nj, u_ref, o_ref):
    tc, p = S5_CHUNK, SSM_GROUP
    for s in range(tc):
        xs = u_ref[0, pl.ds(s, nj, stride=tc), :].astype(o_ref.dtype)
        for gi in range(LANES // p):
            o_ref[gi, 0, :, s * p:(s + 1) * p] = xs[:, gi * p:(gi + 1) * p]


def _s5_regroup(u):
    b_, l_, w = u.shape
    tc, p = S5_CHUNK, SSM_GROUP
    gpt = LANES // p
    j = l_ // tc
    nj = _chunks_per_step(j)
    return pl.pallas_call(
        functools.partial(_s5_regroup_kernel, nj),
        out_shape=jax.ShapeDtypeStruct((w // p, b_, j, tc * p), BF16),
        grid=(b_, j // nj, w // LANES),
        in_specs=[pl.BlockSpec((1, tc * nj, LANES), lambda b, i, c: (b, i, c))],
        out_specs=pl.BlockSpec((gpt, 1, nj, tc * p), lambda b, i, c: (c, b, i, 0)),
        compiler_params=_cparams(("parallel", "parallel", "parallel")),
        name="s5regroup",
    )(u)


def _s5_ungroup_kernel(nj, y_ref, o_ref):
    tc, p = S5_CHUNK, SSM_GROUP
    for t in range(tc):
        row = jnp.concatenate([y_ref[gi, 0, :, t * p:(t + 1) * p] for gi in range(LANES // p)], axis=1)
        o_ref[0, pl.ds(t, nj, stride=tc), :] = row.astype(F32)


def _s5_ungroup(yf):
    g, b_, j, rows = yf.shape
    tc, p = S5_CHUNK, SSM_GROUP
    gpt = LANES // p
    nj = _chunks_per_step(j)
    return pl.pallas_call(
        functools.partial(_s5_ungroup_kernel, nj),
        out_shape=jax.ShapeDtypeStruct((b_, j * tc, g * p), F32),
        grid=(b_, j // nj, g // gpt),
        in_specs=[pl.BlockSpec((gpt, 1, nj, rows), lambda b, i, c: (c, b, i, 0))],
        out_specs=pl.BlockSpec((1, tc * nj, LANES), lambda b, i, c: (b, i, c)),
        compiler_params=_cparams(("parallel", "parallel", "parallel")),
        name="s5ungroup",
    )(yf)


def _s5_mixer(uf, u_c, ops):
    win, wout, m, dec = ops
    g, b_ = uf.shape[0], uf.shape[1]
    npair = g // 2
    states = _s5_states(uf, win)
    ctx_states = _s5_states(_s5_regroup(u_c), win)
    dec_cols = jnp.broadcast_to(dec.transpose(1, 2, 3, 0, 4)[:, :, :, :, None], (2, 2, 1, npair, b_, LANES))
    dec_cols = dec_cols.reshape(2, 2, 1, npair * b_ * LANES)
    hprev = _s5_scan(ctx_states, states, dec_cols)
    return _s5_ungroup(_s5_outputs(uf, hprev, wout, m))


def _attn_kernel(nkv, qt_ref, k_ref, vx_ref, o_ref, s0_ref, s1_ref):
    hd = HEAD_DIM
    kvh = pl.program_id(1)
    tq = qt_ref.shape[2] // ATTN_SUB_BLOCKS
    gq = qt_ref.shape[1] // hd
    kw = k_ref.shape[3]
    vr = vx_ref.shape[3]
    own = (lax.broadcasted_iota(jnp.int32, (kw, tq), 0) >> (hd.bit_length() - 1)) == kvh
    bufs = (s0_ref, s1_ref)
    for sb in range(ATTN_SUB_BLOCKS):
        qp = []
        for g in range(gq):
            qg = qt_ref[0, g * hd:(g + 1) * hd, sb * tq:(sb + 1) * tq].astype(F32)
            q2 = jnp.concatenate([qg] * (kw // hd), axis=0)
            qp.append(jnp.where(own, q2, 0.0).astype(BF16))
        qp = jnp.concatenate(qp, axis=1)
        m0 = jnp.full((1, gq * tq), -jnp.inf, F32)
        acc0 = jnp.zeros((vr, gq * tq), F32)

        def scores(jb, s_ref, qp=qp):
            s = _dot(k_ref[0, jb], qp)
            s_ref[...] = s
            return jnp.max(s, axis=0, keepdims=True)

        def consume(jb, s_ref, cmax, m_prev, acc):
            m_new = jnp.maximum(m_prev, cmax)
            alpha = jnp.exp2(m_prev - m_new)
            p = jnp.exp2(s_ref[...] - m_new).astype(BF16)
            return m_new, alpha * acc + _dot(vx_ref[0, 0, jb], p)

        def steps(j0, count, m, acc, cmax, scores=scores, consume=consume):
            for u in range(count):
                cnext = scores(j0 + u + 1, bufs[(u + 1) % 2])
                m, acc = consume(j0 + u, bufs[u % 2], cmax, m, acc)
                cmax = cnext
            return m, acc, cmax

        unroll = ATTN_STEPS_PER_TRIP
        trips = (nkv - 1) // unroll
        carry = lax.fori_loop(0, trips, lambda i, c, steps=steps: steps(unroll * i, unroll, *c),
                              (m0, acc0, scores(0, s0_ref)))
        m, acc, cmax = steps(trips * unroll, (nkv - 1) - trips * unroll, *carry)
        m, acc = consume(nkv - 1, bufs[(nkv - 1) % 2], cmax, m, acc)
        o = acc[0:hd] / acc[hd:hd + 1]
        ot = jnp.concatenate([o[:, g * tq:(g + 1) * tq] for g in range(gq)], axis=0)
        o_ref[0, sb * tq:(sb + 1) * tq, :] = ot.T.astype(BF16)


def _attention(qt, k4, vx, tq):
    b_, wq, l_ = qt.shape
    _, kvh, nkv, vr, tk = vx.shape
    kw = k4.shape[3]
    gw = wq // kvh
    return pl.pallas_call(
        functools.partial(_attn_kernel, nkv),
        out_shape=jax.ShapeDtypeStruct((b_, l_, wq), BF16),
        grid=(b_, kvh, l_ // tq),
        in_specs=[pl.BlockSpec((1, gw, tq), lambda b, h, i: (b, h, i)),
                  pl.BlockSpec((1, nkv, tk, kw), lambda b, h, i: (b, 0, 0, 0)),
                  pl.BlockSpec((1, 1, nkv, vr, tk), lambda b, h, i: (b, h, 0, 0, 0))],
        out_specs=pl.BlockSpec((1, tq, gw), lambda b, h, i: (b, i, h)),
        scratch_shapes=[pltpu.VMEM((tk, (gw // HEAD_DIM) * tq // ATTN_SUB_BLOCKS), F32)] * 2,
        compiler_params=_cparams(("parallel", "parallel", "parallel")),
        name="attn",
    )(qt, k4, vx)


def _merge_kernel(dm, ys_ref, oa_ref, gl_ref, x_ref, mod_ref, wglu_ref, bglu_ref, wso_ref, wao_ref,
                  wout_ref, n2g_ref, rw_ref, rb_ref, utri_ref, xmix_ref, h2_ref, idx_ref, gate_ref, rank_ref,
                  cnt_out_ref, cnt_ref):
    tm = x_ref.shape[1]
    parts = [pl.ds(r * (tm // MERGE_ROW_PARTS), tm // MERGE_ROW_PARTS) for r in range(MERGE_ROW_PARTS)]
    g1 = mod_ref[0, :, 2 * dm:3 * dm]
    sh2 = mod_ref[0, :, 3 * dm:4 * dm]
    sc2 = mod_ref[0, :, 4 * dm:5 * dm]
    s = [jax.nn.gelu(ys_ref[0, r, :]) for r in parts]
    z = [_dot(sr.astype(BF16), wglu_ref[...]) + bglu_ref[...] for sr in s]
    s = [sr * jax.nn.sigmoid(zr) for sr, zr in zip(s, z)]
    a = [_dot(sr.astype(BF16), wso_ref[...]) for sr in s]
    bq = [_dot(oa_ref[0, r, :], wao_ref[...]) for r in parts]
    merged = []
    for r, ar, br in zip(parts, a, bq):
        gts = jax.nn.sigmoid(gl_ref[0, r, :].astype(F32))
        merged.append((gts[:, 0:dm] * ar + gts[:, dm:2 * dm] * br).astype(BF16))
    out = [_dot(mr, wout_ref[...]) for mr in merged]
    h2_parts = []
    for r, outr in zip(parts, out):
        xm = x_ref[0, r, :] + g1 * outr
        xmix_ref[0, r, :] = xm
        ms = jnp.mean(xm * xm, axis=-1, keepdims=True)
        h2r = xm * lax.rsqrt(ms + EPS) * n2g_ref[...] * (1.0 + sc2) + sh2
        h2_ref[0, r, :] = _pack2(h2r[:, 0:dm // 2], h2r[:, dm // 2:dm])
        h2_parts.append(h2r)
    logits = jnp.concatenate([_dot3(h2r, rw_ref[...]) for h2r in h2_parts], axis=0) + rb_ref[...]
    nep = cnt_ref.shape[0]
    l = logits.T[0:nep, :]
    sub = lax.broadcasted_iota(jnp.int32, l.shape, 0).astype(F32)
    neg = jnp.float32(-jnp.inf)
    vals, idxs = [], []
    for _ in range(TOP_K):
        mx = jnp.max(l, axis=0, keepdims=True)
        ix = jnp.min(jnp.where(l == mx, sub, float(nep)), axis=0, keepdims=True)
        vals.append(mx)
        idxs.append(ix)
        l = jnp.where(sub == ix, neg, l)
    es = [jnp.exp(v - vals[0]) for v in vals]
    den = es[0] + es[1] + es[2] + es[3]
    @pl.when((pl.program_id(0) == 0) & (pl.program_id(1) == 0))
    def _():
        cnt_ref[...] = jnp.zeros(cnt_ref.shape, F32)

    onehot = jnp.zeros(l.shape, F32)
    for k in range(TOP_K):
        onehot = onehot + jnp.where(sub == idxs[k], 1.0, 0.0)
    prior = _dot(onehot.astype(BF16), utri_ref[...]) + cnt_ref[...]
    cnt_ref[...] = cnt_ref[...] + jnp.sum(onehot, axis=1, keepdims=True)
    cnt_out_ref[...] = cnt_ref[...]
    krow = lax.broadcasted_iota(jnp.int32, (8, l.shape[1]), 0)
    idx_out = jnp.zeros(krow.shape, F32)
    gate_out = jnp.zeros(krow.shape, F32)
    rank_out = jnp.zeros(krow.shape, F32)
    for k in range(TOP_K):
        rk = jnp.sum(jnp.where(sub == idxs[k], prior, 0.0), axis=0, keepdims=True)
        idx_out = jnp.where(krow == k, idxs[k], idx_out)
        gate_out = jnp.where(krow == k, es[k] / den, gate_out)
        rank_out = jnp.where(krow == k, rk, rank_out)
    idx_ref[0] = idx_out.astype(jnp.int32)
    gate_ref[0] = jnp.concatenate([gate_out, jnp.zeros((LANES - 8, gate_out.shape[1]), F32)], axis=0).T
    rank_ref[0] = rank_out.astype(jnp.int32)


def _merge(y_ssm, o_attn, glog, x, mod3, w_glu, b_glu, w_ssm_out, w_attn_out, w_out, norm2_g,
           rw_pad, rb_pad, ne, tm):
    b_, l_, dm = x.shape
    wu = dm // 2
    tok = lambda w: pl.BlockSpec((1, tm, w), lambda b, i: (b, i, 0))
    full = lambda r, c: pl.BlockSpec((r, c), lambda b, i: (0, 0))
    utri = (jnp.arange(tm)[:, None] < jnp.arange(tm)[None, :]).astype(BF16)
    nep = -(-ne // 8) * 8
    kmaj = pl.BlockSpec((1, 8, tm), lambda b, i: (b, 0, i))
    return pl.pallas_call(
        functools.partial(_merge_kernel, dm),
        out_shape=[jax.ShapeDtypeStruct((b_, l_, dm), F32),
                   jax.ShapeDtypeStruct((b_, l_, dm // 2), jnp.uint32),
                   jax.ShapeDtypeStruct((b_, 8, l_), jnp.int32),
                   jax.ShapeDtypeStruct((b_, l_, LANES), F32),
                   jax.ShapeDtypeStruct((b_, 8, l_), jnp.int32),
                   jax.ShapeDtypeStruct((nep, 1), F32)],
        grid=(b_, l_ // tm),
        in_specs=[tok(wu), tok(wu), tok(2 * dm), tok(dm),
                  pl.BlockSpec((1, 1, N_MOD * dm), lambda b, i: (b, 0, 0)),
                  full(wu, wu), full(1, wu), full(wu, dm), full(wu, dm), full(dm, dm),
                  full(1, dm), full(dm, LANES), full(1, LANES), full(tm, tm)],
        out_specs=[tok(dm), tok(dm // 2), kmaj, tok(LANES), kmaj, full(nep, 1)],
        scratch_shapes=[pltpu.VMEM((nep, 1), F32)],
        compiler_params=_cparams(("arbitrary", "arbitrary")),
        name="merge",
    )(y_ssm, o_attn, glog, x, mod3, w_glu, b_glu, w_ssm_out, w_attn_out, w_out, norm2_g, rw_pad, rb_pad, utri)


def _expert_kernel(de, be_ref, nv_ref, x_ref, wgu_ref, bgu_ref, wd_ref, bd_ref, y_ref, wgu_b, wd_b):
    i = pl.program_id(0)
    n_valid = nv_ref[i]
    used = n_valid > 0
    new_expert = (i == 0) | (be_ref[i] != be_ref[jnp.maximum(i - 1, 0)])

    @pl.when(used & new_expert)
    def _():
        wgu_b[...] = wgu_ref[0].astype(BF16)
        wd_b[...] = wd_ref[0].astype(BF16)

    @pl.when(used)
    def _():
        row = lax.broadcasted_iota(jnp.int32, (x_ref.shape[0], 1), 0)
        x_lo, x_hi = _unpack2(jnp.where(row < n_valid, x_ref[...], jnp.uint32(0)))
        xb = jnp.concatenate([x_lo, x_hi], axis=1).astype(BF16)
        gu = _dot(xb, wgu_b[...]) + bgu_ref[0]
        gate = jnp.minimum(gu[:, 0:de], SWIGLU_LIMIT)
        up = jnp.clip(gu[:, de:2 * de], -SWIGLU_LIMIT, SWIGLU_LIMIT)
        act = (up + 1.0) * (gate * jax.nn.sigmoid(SWIGLU_ALPHA * gate))
        y = _dot(act.astype(BF16), wd_b[...]) + bd_ref[0]
        half = y.shape[1] // 2
        y_ref[...] = _pack2(y[:, 0:half], y[:, half:])

    @pl.when(jnp.logical_not(used))
    def _():
        y_ref[...] = jnp.zeros(y_ref.shape, y_ref.dtype)


def _experts(xb, block_expert, block_valid, w_gate_up, b_gate_up, w_down, b_down):
    n_slots, dmh = xb.shape
    dm = 2 * dmh
    ne, _, de2 = w_gate_up.shape
    de = de2 // 2
    nb = n_slots // EXPERT_BLOCK
    return pl.pallas_call(
        functools.partial(_expert_kernel, de),
        out_shape=jax.ShapeDtypeStruct((n_slots, dmh), jnp.uint32),
        grid_spec=pltpu.PrefetchScalarGridSpec(
            num_scalar_prefetch=2,
            grid=(nb,),
            in_specs=[pl.BlockSpec((EXPERT_BLOCK, dmh), lambda i, be, nu: (i, 0)),
                      pl.BlockSpec((1, dm, de2), lambda i, be, nu: (be[i], 0, 0)),
                      pl.BlockSpec((1, 1, de2), lambda i, be, nu: (be[i], 0, 0)),
                      pl.BlockSpec((1, de, dm), lambda i, be, nu: (be[i], 0, 0)),
                      pl.BlockSpec((1, 1, dm), lambda i, be, nu: (be[i], 0, 0))],
            out_specs=pl.BlockSpec((EXPERT_BLOCK, dmh), lambda i, be, nu: (i, 0)),
            scratch_shapes=[pltpu.VMEM((dm, de2), BF16), pltpu.VMEM((de, dm), BF16)],
        ),
        compiler_params=_cparams(("arbitrary",)),
        name="expert",
    )(block_expert, block_valid, xb, w_gate_up, b_gate_up.reshape(ne, 1, de2), w_down, b_down.reshape(ne, 1, dm))


def _sc_gather_rows(table, idx):
    n = idx.shape[0]
    w = table.shape[1]
    mesh = plsc.VectorSubcoreMesh(core_axis_name="c", subcore_axis_name="s")
    nc, nw = mesh.num_cores, mesh.num_cores * mesh.num_subcores
    win = SC_GATHER_WINDOW
    assert n % (win * nw) == 0
    per_worker = n // nw

    @functools.partial(
        pl.kernel, out_type=jax.ShapeDtypeStruct((n, w), table.dtype), mesh=mesh,
        scratch_types=[pltpu.VMEM((win,), jnp.int32), pltpu.VMEM((win, w), table.dtype),
                       pltpu.SemaphoreType.DMA])
    def gather(x_hbm, i_hbm, o_hbm, idx_v, rows_v, sem):
        base = (lax.axis_index("s") * nc + lax.axis_index("c")) * per_worker

        @pl.loop(0, per_worker // win)
        def _(j):
            off = base + j * win
            pltpu.sync_copy(i_hbm.at[pl.ds(off, win)], idx_v)
            pltpu.async_copy(x_hbm.at[idx_v], rows_v, sem).wait()
            pltpu.sync_copy(rows_v, o_hbm.at[pl.ds(off, win)])

    return gather(table, idx)


def _sc_scatter_rows(rows, dest, n_out):
    t_, w = rows.shape
    kk = dest.shape[0] // t_
    mesh = plsc.VectorSubcoreMesh(core_axis_name="c", subcore_axis_name="s")
    nc, nw = mesh.num_cores, mesh.num_cores * mesh.num_subcores
    win = SC_GATHER_WINDOW
    assert t_ % (win * nw) == 0
    per_worker = t_ // nw

    @functools.partial(
        pl.kernel, out_type=jax.ShapeDtypeStruct((n_out, w), rows.dtype), mesh=mesh,
        scratch_types=[pltpu.VMEM((win,), jnp.int32)] * kk + [pltpu.VMEM((win, w), rows.dtype),
                                                              pltpu.SemaphoreType.DMA])
    def scatter(x_hbm, d_hbm, o_hbm, *scratch):
        idx_vs, rows_v, sem = scratch[:kk], scratch[kk], scratch[kk + 1]
        base = (lax.axis_index("s") * nc + lax.axis_index("c")) * per_worker

        @pl.loop(0, per_worker // win)
        def _(j):
            off = base + j * win
            pltpu.sync_copy(x_hbm.at[pl.ds(off, win)], rows_v)
            for k in range(kk):
                pltpu.sync_copy(d_hbm.at[pl.ds(k * t_ + off, win)], idx_vs[k])
            copies = [pltpu.async_copy(rows_v, o_hbm.at[idx_vs[k]], sem) for k in range(kk)]
            for cp in copies:
                cp.wait()

    return scatter(rows, dest)


def _final_kernel(dm, yg_ref, gate_ref, xm_ref, mod_ref, fg_ref, *rest):
    o_ref = rest[-1]
    gates = gate_ref[0]
    f_lo = f_hi = None
    for k in range(TOP_K):
        y_lo, y_hi = _unpack2(yg_ref[k, 0])
        gk = gates[:, k:k + 1]
        f_lo = gk * y_lo if f_lo is None else f_lo + gk * y_lo
        f_hi = gk * y_hi if f_hi is None else f_hi + gk * y_hi
    f = jnp.concatenate([f_lo, f_hi], axis=1)
    g2 = mod_ref[0, :, 5 * dm:6 * dm]
    xo = xm_ref[0] + g2 * f
    ms = jnp.mean(xo * xo, axis=-1, keepdims=True)
    o_ref[0] = xo * lax.rsqrt(ms + EPS) * fg_ref[...]


def _final(yg, gates, x_mix, mod3, final_g, tm, sample, prev):
    b_, l_, dm = x_mix.shape
    in_specs = [pl.BlockSpec((TOP_K, 1, tm, dm // 2), lambda i: (0, 0, i, 0)),
                pl.BlockSpec((1, tm, LANES), lambda i: (sample, i, 0)),
                pl.BlockSpec((1, tm, dm), lambda i: (sample, i, 0)),
                pl.BlockSpec((1, 1, N_MOD * dm), lambda i: (sample, 0, 0)),
                pl.BlockSpec((1, dm), lambda i: (0, 0))]
    args = [yg, gates, x_mix, mod3, final_g.reshape(1, dm)]
    aliases = {}
    if prev is not None:
        in_specs.append(pl.BlockSpec(memory_space=pl.ANY))
        args.append(prev)
        aliases = {len(args) - 1: 0}
    return pl.pallas_call(
        functools.partial(_final_kernel, dm),
        out_shape=jax.ShapeDtypeStruct((b_, l_, dm), F32),
        grid=(l_ // tm,),
        in_specs=in_specs,
        out_specs=pl.BlockSpec((1, tm, dm), lambda i: (sample, i, 0)),
        input_output_aliases=aliases,
        compiler_params=_cparams(("parallel",)),
        name="final",
    )(*args)


def _rope_tables(l_):
    half = HEAD_DIM // 2
    inv_freq = ROPE_THETA ** (-jnp.arange(0, half, 2, dtype=F32) / half)
    t = jnp.arange(l_, dtype=jnp.int32)
    row_id, col_id = t // GRID_W, t % GRID_W
    ang = jnp.concatenate([row_id.astype(F32)[:, None] * inv_freq,
                           col_id.astype(F32)[:, None] * inv_freq], axis=-1)
    cos = jnp.repeat(jnp.cos(ang), 2, axis=-1)
    sin = jnp.repeat(jnp.sin(ang), 2, axis=-1)
    sign = jnp.tile(jnp.array([-1.0, 1.0], F32), HEAD_DIM // 2)
    reps = LANES // HEAD_DIM
    return jnp.tile(cos, (1, reps)), jnp.tile(sin * sign, (1, reps))


def _routing(idx, rank, counts, t_, ne):
    m_ = t_ * TOP_K
    blk = EXPERT_BLOCK
    nb = (m_ + ne * (blk - 1)) // blk
    nb = -(-nb // SLOT_BLOCK_MULTIPLE) * SLOT_BLOCK_MULTIPLE
    blk_counts = (counts + blk - 1) // blk
    blk_ends = jnp.cumsum(blk_counts)
    first_slot = (blk_ends - blk_counts) * blk
    onehot = idx[:, :, None] == jnp.arange(ne, dtype=jnp.int32)[None, None, :]
    dest = rank + jnp.sum(jnp.where(onehot, first_slot[None, None, :], 0), axis=2)
    blocks = jnp.arange(nb, dtype=jnp.int32)
    block_expert = jnp.minimum(jnp.sum(blocks[:, None] >= blk_ends[None, :], axis=1), ne - 1)
    in_expert = blocks * blk - first_slot[block_expert]
    block_valid = jnp.clip(counts[block_expert] - in_expert, 0, blk)
    block_valid = jnp.where(blocks < blk_ends[-1], block_valid, 0)
    return dest.astype(jnp.int32), block_expert.astype(jnp.int32), block_valid.astype(jnp.int32), nb * blk


def kernel(x, c, ctx, c_ctx, w_mod, b_mod, norm1_g, norm2_g, w_in, s5_lam_re, s5_lam_im, s5_log_step,
           s5_b_re, s5_b_im, s5_c_re, s5_c_im, s5_d, w_glu, b_glu, w_ssm_out, q_norm_g, k_norm_g,
           w_attn_out, w_out, router_w, router_b, w_gate_up, b_gate_up, w_down, b_down, final_norm_g):
    b_, l_, dm = x.shape
    lc = ctx.shape[1]
    depth = w_mod.shape[0]
    assert depth == 1, "single-layer block"
    assert b_ <= 7 and l_ % TOKEN_BLOCK == 0 and l_ % ATTN_Q_BLOCK == 0 and lc % S5_CHUNK == 0
    wu = dm // 2
    wkv = wu // GQA_GROUP
    kvh = wkv // HEAD_DIM
    ne = router_w.shape[-1]
    i = 0

    c8 = jnp.zeros((8, dm), F32).at[:b_].set(c).at[b_].set(c_ctx)
    mod3 = _modulation(c8, w_mod[i], b_mod[i]).reshape(8, 1, N_MOD * dm)

    w_in_b = w_in[i].astype(BF16)
    cos, sin = _rope_tables(l_)
    hsum = jnp.kron(jnp.eye(wu // HEAD_DIM, dtype=F32), jnp.ones((HEAD_DIM, HEAD_DIM), F32)).astype(BF16)
    qg = jnp.tile(q_norm_g[i], wu // HEAD_DIM).reshape(1, wu)
    kg = jnp.tile(k_norm_g[i], wkv // HEAD_DIM).reshape(1, wkv)
    uf, k, vt, qt, glog = _inproj(x, mod3, None, norm1_g[i], w_in_b, cos, sin, qg, kg, hsum, True, TOKEN_BLOCK)
    ones_t = jnp.ones((lc, LANES), F32)
    u_c, k_c, vt_c = _inproj(ctx, mod3, b_, norm1_g[i], w_in_b[:, :wu + 2 * wkv], ones_t, 0.0 * ones_t,
                             qg, kg, hsum, False, lc)

    ops = _s5_operators(s5_lam_re[i], s5_lam_im[i], s5_log_step[i], s5_b_re[i], s5_b_im[i],
                        s5_c_re[i], s5_c_im[i], s5_d[i])
    y_ssm = _s5_mixer(uf, u_c, ops)

    s_all = lc + l_
    tk = max(t for t in range(LANES, ATTN_KEY_CHUNK + 1, LANES) if s_all % t == 0)
    nkv = s_all // tk
    k4 = jnp.concatenate([k_c, k], axis=1).reshape(b_, nkv, tk, wkv)
    vt_all = jnp.concatenate([vt_c, vt], axis=2).reshape(b_, kvh, HEAD_DIM, s_all)
    vx = jnp.concatenate([vt_all, jnp.ones((b_, kvh, 1, s_all), BF16),
                          jnp.zeros((b_, kvh, ATTN_V_ROWS - HEAD_DIM - 1, s_all), BF16)], axis=2)
    vx = vx.reshape(b_, kvh, ATTN_V_ROWS, nkv, tk).transpose(0, 1, 3, 2, 4)
    o_attn = _attention(qt, k4, vx, ATTN_Q_BLOCK)

    rw_pad = jnp.zeros((dm, LANES), F32).at[:, :ne].set(router_w[i])
    rb_pad = jnp.full((1, LANES), -jnp.inf, F32).at[0, :ne].set(router_b[i])
    x_mix, h2, idx, gates, rank, counts = _merge(
        y_ssm, o_attn, glog, x, mod3, w_glu[i].astype(BF16), b_glu[i].reshape(1, wu),
        w_ssm_out[i].astype(BF16), w_attn_out[i].astype(BF16), w_out[i].astype(BF16),
        norm2_g[i].reshape(1, dm), rw_pad, rb_pad, ne, TOKEN_BLOCK)

    t_ = b_ * l_
    kmajor = lambda a: a[:, :TOP_K, :].transpose(1, 0, 2).reshape(TOP_K, t_)
    dest_kmajor, block_expert, block_valid, n_slots = _routing(
        kmajor(idx), kmajor(rank), counts[:ne, 0].astype(jnp.int32), t_, ne)
    xb = _sc_scatter_rows(h2.reshape(t_, dm // 2), dest_kmajor.reshape(-1), n_slots)
    yb = _experts(xb, block_expert, block_valid, w_gate_up[i], b_gate_up[i], w_down[i], b_down[i])
    out = None
    for b in range(b_):
        yg = _sc_gather_rows(yb, dest_kmajor[:, b * l_:(b + 1) * l_].reshape(-1)).reshape(TOP_K, 1, l_, dm // 2)
        out = _final(yg, gates, x_mix, mod3, final_norm_g, TOKEN_BLOCK, b, out)
    return out
```

```python
import functools

import jax
import jax.numpy as jnp
from jax import lax
from jax.experimental import pallas as pl
from jax.experimental.pallas import tpu as pltpu
from jax.experimental.pallas import tpu_sc as plsc

F32 = jnp.float32
BF16 = jnp.bfloat16

EPS = 1e-6
GRID_W = 64
N_MOD = 6
SSM_GROUP = 16
SSM_STATE = 64
HEAD_DIM = 64
GQA_GROUP = 4
ROPE_THETA = 10000.0
TOP_K = 4
SWIGLU_ALPHA = 1.702
SWIGLU_LIMIT = 7.0
LOG2E = 1.4426950408889634

LANES = 128
S5_CHUNK = 32
S5_POW_ROWS = 56
TOKEN_BLOCK = 512
WIDE_BLOCK = 1024
EXPERT_BLOCK = 512
ATTN_V_ROWS = 80
ATTN_KEY_CHUNK = 384
ATTN_SUB_BLOCKS = 4
ATTN_Q_BLOCK = 256 * ATTN_SUB_BLOCKS
ATTN_STEPS_PER_TRIP = 10
MERGE_ROW_PARTS = 2
SC_GATHER_WINDOW = 128
SC_WORKERS = 32
SLOT_BLOCK_MULTIPLE = SC_GATHER_WINDOW * SC_WORKERS // EXPERT_BLOCK
VMEM_LIMIT = 56 * 1024 * 1024


def _cparams(sem):
    return pltpu.CompilerParams(dimension_semantics=sem, vmem_limit_bytes=VMEM_LIMIT)


def _dot(a, b):
    return jnp.dot(a, b, preferred_element_type=F32)


def _split(a):
    hi = a.astype(BF16)
    lo = (a - hi.astype(F32)).astype(BF16)
    return hi, lo


def _pack2(lo, hi):
    lo_w = lax.bitcast_convert_type(lo.astype(BF16).astype(F32), jnp.uint32) >> 16
    hi_w = lax.bitcast_convert_type(hi.astype(BF16).astype(F32), jnp.uint32) & jnp.uint32(0xFFFF0000)
    return lo_w | hi_w


def _unpack2(words):
    lo = lax.bitcast_convert_type(words << 16, F32)
    hi = lax.bitcast_convert_type(words & jnp.uint32(0xFFFF0000), F32)
    return lo, hi


def _dot3(a, b):
    ah, al = _split(a)
    bh, bl = _split(b)
    return _dot(ah, bh) + _dot(ah, bl) + _dot(al, bh)


def _mod_kernel(c_ref, w_ref, b_ref, o_ref):
    c = c_ref[...]
    s = c * jax.nn.sigmoid(c)
    o_ref[...] = _dot3(s, w_ref[...]) + b_ref[...]


def _modulation(c8, w_mod, b_mod):
    d = c8.shape[1]
    n = w_mod.shape[1]
    tn = WIDE_BLOCK
    return pl.pallas_call(
        _mod_kernel,
        out_shape=jax.ShapeDtypeStruct((8, n), F32),
        grid=(n // tn,),
        in_specs=[pl.BlockSpec((8, d), lambda j: (0, 0)),
                  pl.BlockSpec((d, tn), lambda j: (0, j)),
                  pl.BlockSpec((1, tn), lambda j: (0, j))],
        out_specs=pl.BlockSpec((8, tn), lambda j: (0, j)),
        compiler_params=_cparams(("parallel",)),
        name="mod",
    )(c8, w_mod, b_mod.reshape(1, n))


def _head_norm_rope(z, gain, hsum, cos, sin, scale):
    ss = _dot((z * z).astype(BF16), hsum)
    zn = z * lax.rsqrt(ss * (1.0 / HEAD_DIM) + EPS) * gain
    lane = lax.broadcasted_iota(jnp.int32, (1, LANES), 1)
    even = (lane & 1) == 0
    outs = []
    for ci in range(z.shape[1] // LANES):
        ch = zn[:, ci * LANES:(ci + 1) * LANES]
        nxt = pltpu.roll(ch, LANES - 1, axis=1)
        prv = pltpu.roll(ch, 1, axis=1)
        sw = jnp.where(even, nxt, prv)
        outs.append((ch * cos + sw * sin) * scale)
    return outs


def _inproj_kernel(with_q, dm, x_ref, mod_ref, g_ref, w_ref, cos_ref, sin_ref, qg_ref, kg_ref,
                   hsum_ref, *out_refs):
    x = x_ref[0]
    ms = jnp.mean(x * x, axis=-1, keepdims=True)
    y = x * lax.rsqrt(ms + EPS) * g_ref[...]
    sh = mod_ref[0, :, 0:dm]
    sc = mod_ref[0, :, dm:2 * dm]
    hb = (y * (1.0 + sc) + sh).astype(BF16)
    cos = cos_ref[...]
    sin = sin_ref[...]
    wu = dm // 2
    wkv = wu // GQA_GROUP
    c_k, c_v, c_q, c_g = wu, wu + wkv, wu + 2 * wkv, 2 * wu + 2 * wkv
    u_ref, k_ref, vt_ref = out_refs[:3]
    u = _dot(hb, w_ref[:, 0:c_k])
    if with_q:
        ubuf = out_refs[-1]
        p, gpt, nj = SSM_GROUP, LANES // SSM_GROUP, x.shape[0] // S5_CHUNK
        for ci in range(wu // LANES):
            ubuf[ci] = u[:, ci * LANES:(ci + 1) * LANES]
        for s in range(S5_CHUNK):
            for ci in range(wu // LANES):
                xs = ubuf[ci, pl.ds(s, nj, stride=S5_CHUNK), :].astype(BF16)
                for gi in range(gpt):
                    u_ref[ci * gpt + gi, 0, :, s * p:(s + 1) * p] = xs[:, gi * p:(gi + 1) * p]
    else:
        u_ref[0] = u
    kz = _dot(hb, w_ref[:, c_k:c_v])
    (kr,) = _head_norm_rope(kz, kg_ref[...], hsum_ref[0:wkv, 0:wkv], cos, sin, 1.0)
    k_ref[0] = kr.astype(BF16)
    vt_ref[0] = _dot(hb, w_ref[:, c_v:c_q]).T.astype(BF16)
    if with_q:
        qt_ref, gl_ref = out_refs[3:5]
        qz = _dot(hb, w_ref[:, c_q:c_g])
        qs = _head_norm_rope(qz, qg_ref[...], hsum_ref[...], cos, sin, HEAD_DIM ** -0.5 * LOG2E)
        for ci, qc in enumerate(qs):
            qt_ref[0, ci * LANES:(ci + 1) * LANES, :] = qc.T.astype(BF16)
        gl_ref[0] = _dot(hb, w_ref[:, c_g:]).astype(BF16)


def _inproj(x, mod3, mod_row0, norm_g, w_in_b, cos, sin, qg, kg, hsum, with_q, tm):
    b_, l_, dm = x.shape
    wu = dm // 2
    wkv = wu // GQA_GROUP
    n_in = w_in_b.shape[1]
    outs = [jax.ShapeDtypeStruct((b_, l_, wu), F32),
            jax.ShapeDtypeStruct((b_, l_, wkv), BF16),
            jax.ShapeDtypeStruct((b_, wkv, l_), BF16)]
    ospecs = [pl.BlockSpec((1, tm, wu), lambda b, i: (b, i, 0)),
              pl.BlockSpec((1, tm, wkv), lambda b, i: (b, i, 0)),
              pl.BlockSpec((1, wkv, tm), lambda b, i: (b, 0, i))]
    scratch = []
    if with_q:
        g, rows = wu // SSM_GROUP, S5_CHUNK * SSM_GROUP
        outs[0] = jax.ShapeDtypeStruct((g, b_, l_ // S5_CHUNK, rows), BF16)
        ospecs[0] = pl.BlockSpec((g, 1, tm // S5_CHUNK, rows), lambda b, i: (0, b, i, 0))
        outs += [jax.ShapeDtypeStruct((b_, wu, l_), BF16),
                 jax.ShapeDtypeStruct((b_, l_, 2 * dm), BF16)]
        ospecs += [pl.BlockSpec((1, wu, tm), lambda b, i: (b, 0, i)),
                   pl.BlockSpec((1, tm, 2 * dm), lambda b, i: (b, i, 0))]
        scratch = [pltpu.VMEM((wu // LANES, tm, LANES), F32)]
    if mod_row0 is None:
        mod_map = lambda b, i: (b, 0, 0)
    else:
        mod_map = lambda b, i: (mod_row0, 0, 0)
    return pl.pallas_call(
        functools.partial(_inproj_kernel, with_q, dm),
        out_shape=outs,
        grid=(b_, l_ // tm),
        in_specs=[pl.BlockSpec((1, tm, dm), lambda b, i: (b, i, 0)),
                  pl.BlockSpec((1, 1, N_MOD * dm), mod_map),
                  pl.BlockSpec((1, dm), lambda b, i: (0, 0)),
                  pl.BlockSpec((dm, n_in), lambda b, i: (0, 0)),
                  pl.BlockSpec((tm, LANES), lambda b, i: (i, 0)),
                  pl.BlockSpec((tm, LANES), lambda b, i: (i, 0)),
                  pl.BlockSpec((1, wu), lambda b, i: (0, 0)),
                  pl.BlockSpec((1, wkv), lambda b, i: (0, 0)),
                  pl.BlockSpec((wu, wu), lambda b, i: (0, 0))],
        out_specs=ospecs,
        scratch_shapes=scratch,
        compiler_params=_cparams(("parallel", "parallel")),
        name="inproj_x" if with_q else "inproj_ctx",
    )(x, mod3, norm_g.reshape(1, dm), w_in_b, cos, sin, qg, kg, hsum)


def _cexp_pow(e, lr, li):
    mag = jnp.exp(e * lr)
    ang = e * li
    return mag * jnp.cos(ang), mag * jnp.sin(ang)


def _cmul(ar, ai, br, bi):
    return ar * br - ai * bi, ar * bi + ai * br


def _s5pre_kernel(lamr_r, lami_r, lst_r, lamr_c, lami_c, lst_c, btr_ref, bti_ref, ctr_ref, cti_ref,
                  d_ref, win_ref, wout_ref, m_ref, dec_ref):
    tc = S5_CHUNK
    half = tc // 2
    p = SSM_GROUP
    rows = tc * p
    gsh = SSM_GROUP.bit_length() - 1
    nsh = SSM_STATE.bit_length() - 1
    row_i = lax.broadcasted_iota(jnp.int32, (rows, rows), 0)
    col_i = lax.broadcasted_iota(jnp.int32, (rows, rows), 1)
    lane_g = lax.broadcasted_iota(jnp.int32, (1, LANES), 1) >> nsh
    subl_g = lax.broadcasted_iota(jnp.int32, (LANES, 1), 0) >> nsh
    e_rows = lax.broadcasted_iota(jnp.int32, (S5_POW_ROWS, 1), 0) - half
    e_lanes = lax.broadcasted_iota(jnp.int32, (1, LANES), 1) - half
    sel_e = lax.broadcasted_iota(jnp.int32, (LANES, rows), 0) - half
    sel_t = lax.broadcasted_iota(jnp.int32, (LANES, rows), 1) >> gsh
    sel_p = (lax.broadcasted_iota(jnp.int32, (LANES, rows), 0)
             == (lax.broadcasted_iota(jnp.int32, (LANES, rows), 1) & (p - 1))).astype(BF16)

    def spread_lanes(table, sel):
        hi, lo = _split(table)
        return _dot(hi, sel) + _dot(lo, sel)

    def spread_rows(table, e_of_s):
        return jnp.concatenate([jnp.broadcast_to(table[e_of_s(s) + half:e_of_s(s) + half + 1, :], (p, LANES))
                                for s in range(tc)], axis=0)

    m_acc = [jnp.where(row_i == col_i, d_ref[0, gl], 0.0) for gl in range(2)]
    for dr in range(2):
        step = jnp.exp(lst_r[dr, 0])
        lam_r, lam_i = lamr_r[dr, 0], lami_r[dr, 0]
        lr, li = lam_r * step, lam_i * step
        pw_r, pw_i = _cexp_pow(e_rows.astype(F32), lr, li)
        lb_r, lb_i = pw_r[half + 1:half + 2, :], pw_i[half + 1:half + 2, :]
        den = lam_r * lam_r + lam_i * lam_i
        nr, ni = lb_r - 1.0, lb_i
        cf_r = (nr * lam_r + ni * lam_i) / den
        cf_i = (ni * lam_r - nr * lam_i) / den
        bt_r = jnp.concatenate([btr_ref[dr, 0]] * tc, axis=0)
        bt_i = jnp.concatenate([bti_ref[dr, 0]] * tc, axis=0)
        bb_r, bb_i = _cmul(cf_r, cf_i, bt_r, bt_i)
        e_in = (lambda s: tc - 1 - s) if dr == 0 else (lambda s: s)
        wi_r, wi_i = _cmul(bb_r, bb_i, spread_rows(pw_r, e_in), spread_rows(pw_i, e_in))
        e_a = (lambda s: half - s) if dr == 0 else (lambda s: s - half)
        a_r, a_i = _cmul(bb_r, bb_i, spread_rows(pw_r, e_a), spread_rows(pw_i, e_a))
        dec_ref[0, dr, 0] = pw_r[tc + half:tc + half + 1, :]
        dec_ref[0, dr, 1] = pw_i[tc + half:tc + half + 1, :]
        stepc = jnp.exp(lst_c[dr, 0])
        lrc, lic = lamr_c[dr, 0] * stepc, lami_c[dr, 0] * stepc
        qw_r, qw_i = _cexp_pow(e_lanes.astype(F32), lrc, lic)
        qw_r = jnp.where(e_lanes <= tc, qw_r, 0.0)
        qw_i = jnp.where(e_lanes <= tc, qw_i, 0.0)
        c_r, c_i = spread_lanes(ctr_ref[dr, 0], sel_p), spread_lanes(cti_ref[dr, 0], sel_p)
        sel_b = (sel_e == (sel_t - half if dr == 0 else half - sel_t)).astype(BF16)
        bm_r, bm_i = _cmul(c_r, c_i, spread_lanes(qw_r, sel_b), spread_lanes(qw_i, sel_b))
        sel_o = (sel_e == (sel_t + 1 if dr == 0 else tc - sel_t)).astype(BF16)
        wo_r, wo_i = _cmul(c_r, c_i, spread_lanes(qw_r, sel_o), spread_lanes(qw_i, sel_o))
        if dr == 0:
            mask = (col_i >> gsh) >= (row_i >> gsh)
        else:
            mask = (row_i >> gsh) >= (col_i >> gsh)
        for gl in range(2):
            lsel = lane_g == gl
            ssel = subl_g == gl
            win_ref[0, gl, 2 * dr] = jnp.where(lsel, wi_r, 0.0).astype(BF16)
            win_ref[0, gl, 2 * dr + 1] = jnp.where(lsel, wi_i, 0.0).astype(BF16)
            wout_ref[0, 2 * dr, :, gl * rows:(gl + 1) * rows] = jnp.where(ssel, wo_r, 0.0).astype(BF16)
            wout_ref[0, 2 * dr + 1, :, gl * rows:(gl + 1) * rows] = jnp.where(ssel, -wo_i, 0.0).astype(BF16)
            kmat = (_dot3(jnp.where(lsel, a_r, 0.0), bm_r) - _dot3(jnp.where(lsel, a_i, 0.0), bm_i))
            m_acc[gl] = m_acc[gl] + jnp.where(mask, kmat, 0.0)
    for gl in range(2):
        m_ref[0, gl] = m_acc[gl].astype(BF16)


def _s5_operators(lam_re, lam_im, log_step, b_re, b_im, c_re, c_im, d_skip):
    g = lam_re.shape[1]
    n, p = SSM_STATE, SSM_GROUP
    npair = g // 2
    tc = S5_CHUNK
    rows = tc * p

    def row_form(a):
        return a.reshape(2, npair, 1, 2 * n)

    def col_form(a):
        return a.reshape(2, npair, 2 * n, 1)

    lst = jnp.broadcast_to(log_step[:, :, None], (2, g, n))
    bt = lambda b: b.reshape(2, npair, 2, n, p).transpose(0, 1, 4, 2, 3).reshape(2, npair, p, 2 * n)
    ct = lambda c: jnp.pad(c.reshape(2, npair, 2, p, n).transpose(0, 1, 2, 4, 3).reshape(2, npair, 2 * n, p),
                           ((0, 0), (0, 0), (0, 0), (0, LANES - p)))
    d2 = jnp.broadcast_to(d_skip.reshape(npair, 2, 1, 1, p), (npair, 2, 1, tc, p)).reshape(npair, 2, 1, rows)

    def spec4(shape):
        return pl.BlockSpec((2, 1) + shape, lambda i: (0, i, 0, 0))

    return pl.pallas_call(
        _s5pre_kernel,
        out_shape=[jax.ShapeDtypeStruct((npair, 2, 4, rows, LANES), BF16),
                   jax.ShapeDtypeStruct((npair, 4, LANES, 2 * rows), BF16),
                   jax.ShapeDtypeStruct((npair, 2, rows, rows), BF16),
                   jax.ShapeDtypeStruct((npair, 2, 2, 1, LANES), F32)],
        grid=(npair,),
        in_specs=[spec4((1, LANES)), spec4((1, LANES)), spec4((1, LANES)),
                  spec4((LANES, 1)), spec4((LANES, 1)), spec4((LANES, 1)),
                  spec4((p, LANES)), spec4((p, LANES)),
                  spec4((LANES, LANES)), spec4((LANES, LANES)),
                  pl.BlockSpec((1, 2, 1, rows), lambda i: (i, 0, 0, 0))],
        out_specs=[pl.BlockSpec((1, 2, 4, rows, LANES), lambda i: (i, 0, 0, 0, 0)),
                   pl.BlockSpec((1, 4, LANES, 2 * rows), lambda i: (i, 0, 0, 0)),
                   pl.BlockSpec((1, 2, rows, rows), lambda i: (i, 0, 0, 0)),
                   pl.BlockSpec((1, 2, 2, 1, LANES), lambda i: (i, 0, 0, 0, 0))],
        compiler_params=_cparams(("parallel",)),
        name="s5pre",
    )(row_form(lam_re), row_form(lam_im), row_form(lst),
      col_form(lam_re), col_form(lam_im), col_form(lst),
      bt(b_re), bt(b_im), ct(c_re), ct(c_im), d2)


def _s5state_kernel(u_ref, win_ref, *s_refs):
    for b in range(u_ref.shape[1]):
        for kind in range(4):
            s_refs[kind][:, b * LANES:(b + 1) * LANES] = (_dot(u_ref[0, b], win_ref[0, 0, kind])
                                                          + _dot(u_ref[1, b], win_ref[0, 1, kind]))


def _s5_states(uf, win):
    g, b_, j, rows = uf.shape
    npair = g // 2
    out = jax.ShapeDtypeStruct((j, npair * b_ * LANES), F32)
    ospec = pl.BlockSpec((j, b_ * LANES), lambda pr: (0, pr))
    return pl.pallas_call(
        _s5state_kernel,
        out_shape=[out] * 4,
        grid=(npair,),
        in_specs=[pl.BlockSpec((2, b_, j, rows), lambda pr: (pr, 0, 0, 0)),
                  pl.BlockSpec((1, 2, 4, rows, LANES), lambda pr: (pr, 0, 0, 0, 0))],
        out_specs=[ospec] * 4,
        compiler_params=_cparams(("parallel",)),
        name="s5state",
    )(uf, win)


def _s5scan_kernel(cfr, cfi, cbr, cbi, sfr, sfi, sbr, sbi, dec_ref, hfr, hfi, hbr, hbi):
    jc, jl, w = cfr.shape[0], sfr.shape[0], sfr.shape[1]
    zero = jnp.zeros((1, w), F32)

    def run(sr, si, outs, ar, ai, n, reverse, carry):
        def body(i, hc):
            row = (n - 1 - i) if reverse else i
            hr, hi = hc
            if outs is not None:
                outs[0][pl.ds(row, 1), :] = hr
                outs[1][pl.ds(row, 1), :] = hi
            nr = ar * hr - ai * hi + sr[pl.ds(row, 1), :]
            ni = ar * hi + ai * hr + si[pl.ds(row, 1), :]
            return nr, ni
        return lax.fori_loop(0, n, body, carry)

    afr, afi = dec_ref[0, 0], dec_ref[0, 1]
    abr, abi = dec_ref[1, 0], dec_ref[1, 1]
    c = run(cfr, cfi, None, afr, afi, jc, False, (zero, zero))
    run(sfr, sfi, (hfr, hfi), afr, afi, jl, False, c)
    c = run(cbr, cbi, None, abr, abi, jc, True, (zero, zero))
    run(sbr, sbi, (hbr, hbi), abr, abi, jl, True, c)


def _s5_scan(ctx_states, states, dec_cols):
    jl, wtot = states[0].shape
    jc = ctx_states[0].shape[0]
    tw = WIDE_BLOCK
    spec = pl.BlockSpec((jl, tw), lambda i: (0, i))
    cspec = pl.BlockSpec((jc, tw), lambda i: (0, i))
    return pl.pallas_call(
        _s5scan_kernel,
        out_shape=[jax.ShapeDtypeStruct((jl, wtot), F32)] * 4,
        grid=(wtot // tw,),
        in_specs=[cspec] * 4 + [spec] * 4 + [pl.BlockSpec((2, 2, 1, tw), lambda i: (0, 0, 0, i))],
        out_specs=[spec] * 4,
        compiler_params=_cparams(("parallel",)),
        name="s5scan",
    )(*ctx_states, *states, dec_cols)


def _s5out_kernel(ppt, u_ref, *refs):
    h_refs, (wout_ref, m_ref, o_ref, yf) = refs[:4 * ppt], refs[4 * ppt:]
    tc, p = S5_CHUNK, SSM_GROUP
    rows = u_ref.shape[3]
    j = u_ref.shape[2]
    for pr in range(ppt):
        y2 = None
        for kind in range(4):
            t = _dot(h_refs[4 * pr + kind][...].astype(BF16), wout_ref[pr, kind])
            y2 = t if y2 is None else y2 + t
        for gl in range(2):
            y = y2[:, gl * rows:(gl + 1) * rows] + _dot(u_ref[2 * pr + gl, 0], m_ref[pr, gl])
            yf[2 * pr + gl] = y.astype(BF16)
    for t in range(tc):
        row = jnp.concatenate([yf[gi, :, t * p:(t + 1) * p] for gi in range(2 * ppt)], axis=1)
        o_ref[0, pl.ds(t, j, stride=tc), :] = row.astype(F32)


def _s5_outputs(uf, hprev, wout, m):
    g, b_, j, rows = uf.shape
    gpt = LANES // SSM_GROUP
    ppt = gpt // 2
    hspecs = [pl.BlockSpec((j, LANES), lambda c, b, pr=pr: (0, (c * ppt + pr) * b_ + b))
              for pr in range(ppt) for _ in range(4)]
    hargs = [hprev[kind] for _ in range(ppt) for kind in range(4)]
    return pl.pallas_call(
        functools.partial(_s5out_kernel, ppt),
        out_shape=jax.ShapeDtypeStruct((b_, j * S5_CHUNK, g * SSM_GROUP), F32),
        grid=(g // gpt, b_),
        in_specs=[pl.BlockSpec((gpt, 1, j, rows), lambda c, b: (c, b, 0, 0))] + hspecs +
                 [pl.BlockSpec((ppt, 4, LANES, 2 * rows), lambda c, b: (c, 0, 0, 0)),
                  pl.BlockSpec((ppt, 2, rows, rows), lambda c, b: (c, 0, 0, 0))],
        out_specs=pl.BlockSpec((1, j * S5_CHUNK, LANES), lambda c, b: (b, 0, c)),
        scratch_shapes=[pltpu.VMEM((gpt, j, rows), BF16)],
        compiler_params=_cparams(("parallel", "parallel")),
        name="s5out",
    )(uf, *hargs, wout, m)


def _chunks_per_step(j):
    return max(n for n in (64, 32, 16, 8) if j % n == 0)


def _s5_regroup_kernel(nj, u_ref, o_ref):
    tc, p = S5_CHUNK, SSM_GROUP
    for s in range(tc):
        xs = u_ref[0, pl.ds(s, nj, stride=tc), :].astype(o_ref.dtype)
        for gi in range(LANES // p):
            o_ref[gi, 0, :, s * p:(s + 1) * p] = xs[:, gi * p:(gi + 1) * p]


def _s5_regroup(u):
    b_, l_, w = u.shape
    tc, p = S5_CHUNK, SSM_GROUP
    gpt = LANES // p
    j = l_ // tc
    nj = _chunks_per_step(j)
    return pl.pallas_call(
        functools.partial(_s5_regroup_kernel, nj),
        out_shape=jax.ShapeDtypeStruct((w // p, b_, j, tc * p), BF16),
        grid=(b_, j // nj, w // LANES),
        in_specs=[pl.BlockSpec((1, tc * nj, LANES), lambda b, i, c: (b, i, c))],
        out_specs=pl.BlockSpec((gpt, 1, nj, tc * p), lambda b, i, c: (c, b, i, 0)),
        compiler_params=_cparams(("parallel", "parallel", "parallel")),
        name="s5regroup",
    )(u)


def _s5_mixer(uf, u_c, ops):
    win, wout, m, dec = ops
    g, b_ = uf.shape[0], uf.shape[1]
    npair = g // 2
    states = _s5_states(uf, win)
    ctx_states = _s5_states(_s5_regroup(u_c), win)
    dec_cols = jnp.broadcast_to(dec.transpose(1, 2, 3, 0, 4)[:, :, :, :, None], (2, 2, 1, npair, b_, LANES))
    dec_cols = dec_cols.reshape(2, 2, 1, npair * b_ * LANES)
    hprev = _s5_scan(ctx_states, states, dec_cols)
    return _s5_outputs(uf, hprev, wout, m)


def _attn_kernel(nkv, qt_ref, k_ref, vx_ref, o_ref, s0_ref, s1_ref):
    hd = HEAD_DIM
    kvh = pl.program_id(1)
    tq = qt_ref.shape[2] // ATTN_SUB_BLOCKS
    gq = qt_ref.shape[1] // hd
    kw = k_ref.shape[3]
    vr = vx_ref.shape[3]
    own = (lax.broadcasted_iota(jnp.int32, (kw, tq), 0) >> (hd.bit_length() - 1)) == kvh
    bufs = (s0_ref, s1_ref)
    for sb in range(ATTN_SUB_BLOCKS):
        qp = []
        for g in range(gq):
            qg = qt_ref[0, g * hd:(g + 1) * hd, sb * tq:(sb + 1) * tq].astype(F32)
            q2 = jnp.concatenate([qg] * (kw // hd), axis=0)
            qp.append(jnp.where(own, q2, 0.0).astype(BF16))
        qp = jnp.concatenate(qp, axis=1)
        m0 = jnp.full((1, gq * tq), -jnp.inf, F32)
        acc0 = jnp.zeros((vr, gq * tq), F32)

        def scores(jb, s_ref, qp=qp):
            s = _dot(k_ref[0, jb], qp)
            s_ref[...] = s
            return jnp.max(s, axis=0, keepdims=True)

        def consume(jb, s_ref, cmax, m_prev, acc):
            m_new = jnp.maximum(m_prev, cmax)
            alpha = jnp.exp2(m_prev - m_new)
            p = jnp.exp2(s_ref[...] - m_new).astype(BF16)
            return m_new, alpha * acc + _dot(vx_ref[0, 0, jb], p)

        def steps(j0, count, m, acc, cmax, scores=scores, consume=consume):
            for u in range(count):
                cnext = scores(j0 + u + 1, bufs[(u + 1) % 2])
                m, acc = consume(j0 + u, bufs[u % 2], cmax, m, acc)
                cmax = cnext
            return m, acc, cmax

        unroll = ATTN_STEPS_PER_TRIP
        trips = (nkv - 1) // unroll
        carry = lax.fori_loop(0, trips, lambda i, c, steps=steps: steps(unroll * i, unroll, *c),
                              (m0, acc0, scores(0, s0_ref)))
        m, acc, cmax = steps(trips * unroll, (nkv - 1) - trips * unroll, *carry)
        m, acc = consume(nkv - 1, bufs[(nkv - 1) % 2], cmax, m, acc)
        o = acc[0:hd] / acc[hd:hd + 1]
        ot = jnp.concatenate([o[:, g * tq:(g + 1) * tq] for g in range(gq)], axis=0)
        o_ref[0, sb * tq:(sb + 1) * tq, :] = ot.T.astype(BF16)


def _attention(qt, k4, vx, tq):
    b_, wq, l_ = qt.shape
    _, kvh, nkv, vr, tk = vx.shape
    kw = k4.shape[3]
    gw = wq // kvh
    return pl.pallas_call(
        functools.partial(_attn_kernel, nkv),
        out_shape=jax.ShapeDtypeStruct((b_, l_, wq), BF16),
        grid=(b_, kvh, l_ // tq),
        in_specs=[pl.BlockSpec((1, gw, tq), lambda b, h, i: (b, h, i)),
                  pl.BlockSpec((1, nkv, tk, kw), lambda b, h, i: (b, 0, 0, 0)),
                  pl.BlockSpec((1, 1, nkv, vr, tk), lambda b, h, i: (b, h, 0, 0, 0))],
        out_specs=pl.BlockSpec((1, tq, gw), lambda b, h, i: (b, i, h)),
        scratch_shapes=[pltpu.VMEM((tk, (gw // HEAD_DIM) * tq // ATTN_SUB_BLOCKS), F32)] * 2,
        compiler_params=_cparams(("parallel", "parallel", "parallel")),
        name="attn",
    )(qt, k4, vx)


def _merge_kernel(dm, ys_ref, oa_ref, gl_ref, x_ref, mod_ref, wglu_ref, bglu_ref, wso_ref, wao_ref,
                  wout_ref, n2g_ref, rw_ref, rb_ref, utri_ref, xmix_ref, h2_ref, idx_ref, gate_ref, rank_ref,
                  cnt_out_ref, cnt_ref):
    tm = x_ref.shape[1]
    parts = [pl.ds(r * (tm // MERGE_ROW_PARTS), tm // MERGE_ROW_PARTS) for r in range(MERGE_ROW_PARTS)]
    g1 = mod_ref[0, :, 2 * dm:3 * dm]
    sh2 = mod_ref[0, :, 3 * dm:4 * dm]
    sc2 = mod_ref[0, :, 4 * dm:5 * dm]
    s = [jax.nn.gelu(ys_ref[0, r, :]) for r in parts]
    z = [_dot(sr.astype(BF16), wglu_ref[...]) + bglu_ref[...] for sr in s]
    s = [sr * jax.nn.sigmoid(zr) for sr, zr in zip(s, z)]
    a = [_dot(sr.astype(BF16), wso_ref[...]) for sr in s]
    bq = [_dot(oa_ref[0, r, :], wao_ref[...]) for r in parts]
    merged = []
    for r, ar, br in zip(parts, a, bq):
        gts = jax.nn.sigmoid(gl_ref[0, r, :].astype(F32))
        merged.append((gts[:, 0:dm] * ar + gts[:, dm:2 * dm] * br).astype(BF16))
    out = [_dot(mr, wout_ref[...]) for mr in merged]
    h2_parts = []
    for r, outr in zip(parts, out):
        xm = x_ref[0, r, :] + g1 * outr
        xmix_ref[0, r, :] = xm
        ms = jnp.mean(xm * xm, axis=-1, keepdims=True)
        h2r = xm * lax.rsqrt(ms + EPS) * n2g_ref[...] * (1.0 + sc2) + sh2
        h2_ref[0, r, :] = _pack2(h2r[:, 0:dm // 2], h2r[:, dm // 2:dm])
        h2_parts.append(h2r)
    logits = jnp.concatenate([_dot3(h2r, rw_ref[...]) for h2r in h2_parts], axis=0) + rb_ref[...]
    nep = cnt_ref.shape[0]
    l = logits.T[0:nep, :]
    sub = lax.broadcasted_iota(jnp.int32, l.shape, 0).astype(F32)
    neg = jnp.float32(-jnp.inf)
    vals, idxs = [], []
    for _ in range(TOP_K):
        mx = jnp.max(l, axis=0, keepdims=True)
        ix = jnp.min(jnp.where(l == mx, sub, float(nep)), axis=0, keepdims=True)
        vals.append(mx)
        idxs.append(ix)
        l = jnp.where(sub == ix, neg, l)
    es = [jnp.exp(v - vals[0]) for v in vals]
    den = es[0] + es[1] + es[2] + es[3]
    @pl.when((pl.program_id(0) == 0) & (pl.program_id(1) == 0))
    def _():
        cnt_ref[...] = jnp.zeros(cnt_ref.shape, F32)

    onehot = jnp.zeros(l.shape, F32)
    for k in range(TOP_K):
        onehot = onehot + jnp.where(sub == idxs[k], 1.0, 0.0)
    prior = _dot(onehot.astype(BF16), utri_ref[...]) + cnt_ref[...]
    cnt_ref[...] = cnt_ref[...] + jnp.sum(onehot, axis=1, keepdims=True)
    cnt_out_ref[...] = cnt_ref[...]
    krow = lax.broadcasted_iota(jnp.int32, (8, l.shape[1]), 0)
    idx_out = jnp.zeros(krow.shape, F32)
    gate_out = jnp.zeros(krow.shape, F32)
    rank_out = jnp.zeros(krow.shape, F32)
    for k in range(TOP_K):
        rk = jnp.sum(jnp.where(sub == idxs[k], prior, 0.0), axis=0, keepdims=True)
        idx_out = jnp.where(krow == k, idxs[k], idx_out)
        gate_out = jnp.where(krow == k, es[k] / den, gate_out)
        rank_out = jnp.where(krow == k, rk, rank_out)
    idx_ref[0] = idx_out.astype(jnp.int32)
    gate_ref[0] = jnp.concatenate([gate_out, jnp.zeros((LANES - 8, gate_out.shape[1]), F32)], axis=0).T
    rank_ref[0] = rank_out.astype(jnp.int32)


def _merge(y_ssm, o_attn, glog, x, mod3, w_glu, b_glu, w_ssm_out, w_attn_out, w_out, norm2_g,
           rw_pad, rb_pad, ne, tm):
    b_, l_, dm = x.shape
    wu = dm // 2
    tok = lambda w: pl.BlockSpec((1, tm, w), lambda b, i: (b, i, 0))
    full = lambda r, c: pl.BlockSpec((r, c), lambda b, i: (0, 0))
    utri = (jnp.arange(tm)[:, None] < jnp.arange(tm)[None, :]).astype(BF16)
    nep = -(-ne // 8) * 8
    kmaj = pl.BlockSpec((1, 8, tm), lambda b, i: (b, 0, i))
    return pl.pallas_call(
        functools.partial(_merge_kernel, dm),
        out_shape=[jax.ShapeDtypeStruct((b_, l_, dm), F32),
                   jax.ShapeDtypeStruct((b_, l_, dm // 2), jnp.uint32),
                   jax.ShapeDtypeStruct((b_, 8, l_), jnp.int32),
                   jax.ShapeDtypeStruct((b_, l_, LANES), F32),
                   jax.ShapeDtypeStruct((b_, 8, l_), jnp.int32),
                   jax.ShapeDtypeStruct((nep, 1), F32)],
        grid=(b_, l_ // tm),
        in_specs=[tok(wu), tok(wu), tok(2 * dm), tok(dm),
                  pl.BlockSpec((1, 1, N_MOD * dm), lambda b, i: (b, 0, 0)),
                  full(wu, wu), full(1, wu), full(wu, dm), full(wu, dm), full(dm, dm),
                  full(1, dm), full(dm, LANES), full(1, LANES), full(tm, tm)],
        out_specs=[tok(dm), tok(dm // 2), kmaj, tok(LANES), kmaj, full(nep, 1)],
        scratch_shapes=[pltpu.VMEM((nep, 1), F32)],
        compiler_params=_cparams(("arbitrary", "arbitrary")),
        name="merge",
    )(y_ssm, o_attn, glog, x, mod3, w_glu, b_glu, w_ssm_out, w_attn_out, w_out, norm2_g, rw_pad, rb_pad, utri)


def _expert_kernel(de, be_ref, nv_ref, x_ref, wgu_ref, bgu_ref, wd_ref, bd_ref, y_ref, wgu_b, wd_b):
    i = pl.program_id(0)
    n_valid = nv_ref[i]
    used = n_valid > 0
    new_expert = (i == 0) | (be_ref[i] != be_ref[jnp.maximum(i - 1, 0)])

    @pl.when(used & new_expert)
    def _():
        wgu_b[...] = wgu_ref[0].astype(BF16)
        wd_b[...] = wd_ref[0].astype(BF16)

    @pl.when(used)
    def _():
        row = lax.broadcasted_iota(jnp.int32, (x_ref.shape[0], 1), 0)
        x_lo, x_hi = _unpack2(jnp.where(row < n_valid, x_ref[...], jnp.uint32(0)))
        xb = jnp.concatenate([x_lo, x_hi], axis=1).astype(BF16)
        gu = _dot(xb, wgu_b[...]) + bgu_ref[0]
        gate = jnp.minimum(gu[:, 0:de], SWIGLU_LIMIT)
        up = jnp.clip(gu[:, de:2 * de], -SWIGLU_LIMIT, SWIGLU_LIMIT)
        act = (up + 1.0) * (gate * jax.nn.sigmoid(SWIGLU_ALPHA * gate))
        y = _dot(act.astype(BF16), wd_b[...]) + bd_ref[0]
        half = y.shape[1] // 2
        y_ref[...] = _pack2(y[:, 0:half], y[:, half:])

    @pl.when(jnp.logical_not(used))
    def _():
        y_ref[...] = jnp.zeros(y_ref.shape, y_ref.dtype)


def _experts(xb, block_expert, block_valid, w_gate_up, b_gate_up, w_down, b_down):
    n_slots, dmh = xb.shape
    dm = 2 * dmh
    ne, _, de2 = w_gate_up.shape
    de = de2 // 2
    nb = n_slots // EXPERT_BLOCK
    return pl.pallas_call(
        functools.partial(_expert_kernel, de),
        out_shape=jax.ShapeDtypeStruct((n_slots, dmh), jnp.uint32),
        grid_spec=pltpu.PrefetchScalarGridSpec(
            num_scalar_prefetch=2,
            grid=(nb,),
            in_specs=[pl.BlockSpec((EXPERT_BLOCK, dmh), lambda i, be, nu: (i, 0)),
                      pl.BlockSpec((1, dm, de2), lambda i, be, nu: (be[i], 0, 0)),
                      pl.BlockSpec((1, 1, de2), lambda i, be, nu: (be[i], 0, 0)),
                      pl.BlockSpec((1, de, dm), lambda i, be, nu: (be[i], 0, 0)),
                      pl.BlockSpec((1, 1, dm), lambda i, be, nu: (be[i], 0, 0))],
            out_specs=pl.BlockSpec((EXPERT_BLOCK, dmh), lambda i, be, nu: (i, 0)),
            scratch_shapes=[pltpu.VMEM((dm, de2), BF16), pltpu.VMEM((de, dm), BF16)],
        ),
        compiler_params=_cparams(("arbitrary",)),
        name="expert",
    )(block_expert, block_valid, xb, w_gate_up, b_gate_up.reshape(ne, 1, de2), w_down, b_down.reshape(ne, 1, dm))


def _sc_gather_rows(table, idx):
    n = idx.shape[0]
    w = table.shape[1]
    mesh = plsc.VectorSubcoreMesh(core_axis_name="c", subcore_axis_name="s")
    nc, nw = mesh.num_cores, mesh.num_cores * mesh.num_subcores
    win = SC_GATHER_WINDOW
    assert n % (win * nw) == 0
    per_worker = n // nw

    @functools.partial(
        pl.kernel, out_type=jax.ShapeDtypeStruct((n, w), table.dtype), mesh=mesh,
        scratch_types=[pltpu.VMEM((win,), jnp.int32), pltpu.VMEM((win, w), table.dtype),
                       pltpu.SemaphoreType.DMA])
    def gather(x_hbm, i_hbm, o_hbm, idx_v, rows_v, sem):
        base = (lax.axis_index("s") * nc + lax.axis_index("c")) * per_worker

        @pl.loop(0, per_worker // win)
        def _(j):
            off = base + j * win
            pltpu.sync_copy(i_hbm.at[pl.ds(off, win)], idx_v)
            pltpu.async_copy(x_hbm.at[idx_v], rows_v, sem).wait()
            pltpu.sync_copy(rows_v, o_hbm.at[pl.ds(off, win)])

    return gather(table, idx)


def _sc_scatter_rows(rows, dest, n_out):
    t_, w = rows.shape
    kk = dest.shape[0] // t_
    mesh = plsc.VectorSubcoreMesh(core_axis_name="c", subcore_axis_name="s")
    nc, nw = mesh.num_cores, mesh.num_cores * mesh.num_subcores
    win = SC_GATHER_WINDOW
    assert t_ % (win * nw) == 0
    per_worker = t_ // nw

    @functools.partial(
        pl.kernel, out_type=jax.ShapeDtypeStruct((n_out, w), rows.dtype), mesh=mesh,
        scratch_types=[pltpu.VMEM((win,), jnp.int32)] * kk + [pltpu.VMEM((win, w), rows.dtype),
                                                              pltpu.SemaphoreType.DMA])
    def scatter(x_hbm, d_hbm, o_hbm, *scratch):
        idx_vs, rows_v, sem = scratch[:kk], scratch[kk], scratch[kk + 1]
        base = (lax.axis_index("s") * nc + lax.axis_index("c")) * per_worker

        @pl.loop(0, per_worker // win)
        def _(j):
            off = base + j * win
            pltpu.sync_copy(x_hbm.at[pl.ds(off, win)], rows_v)
            for k in range(kk):
                pltpu.sync_copy(d_hbm.at[pl.ds(k * t_ + off, win)], idx_vs[k])
            copies = [pltpu.async_copy(rows_v, o_hbm.at[idx_vs[k]], sem) for k in range(kk)]
            for cp in copies:
                cp.wait()

    return scatter(rows, dest)


def _final_kernel(dm, yg_ref, gate_ref, xm_ref, mod_ref, fg_ref, *rest):
    o_ref = rest[-1]
    gates = gate_ref[0]
    f_lo = f_hi = None
    for k in range(TOP_K):
        y_lo, y_hi = _unpack2(yg_ref[k, 0])
        gk = gates[:, k:k + 1]
        f_lo = gk * y_lo if f_lo is None else f_lo + gk * y_lo
        f_hi = gk * y_hi if f_hi is None else f_hi + gk * y_hi
    f = jnp.concatenate([f_lo, f_hi], axis=1)
    g2 = mod_ref[0, :, 5 * dm:6 * dm]
    xo = xm_ref[0] + g2 * f
    ms = jnp.mean(xo * xo, axis=-1, keepdims=True)
    o_ref[0] = xo * lax.rsqrt(ms + EPS) * fg_ref[...]


def _final(yg, gates, x_mix, mod3, final_g, tm, sample, prev):
    b_, l_, dm = x_mix.shape
    in_specs = [pl.BlockSpec((TOP_K, 1, tm, dm // 2), lambda i: (0, 0, i, 0)),
                pl.BlockSpec((1, tm, LANES), lambda i: (sample, i, 0)),
                pl.BlockSpec((1, tm, dm), lambda i: (sample, i, 0)),
                pl.BlockSpec((1, 1, N_MOD * dm), lambda i: (sample, 0, 0)),
                pl.BlockSpec((1, dm), lambda i: (0, 0))]
    args = [yg, gates, x_mix, mod3, final_g.reshape(1, dm)]
    aliases = {}
    if prev is not None:
        in_specs.append(pl.BlockSpec(memory_space=pl.ANY))
        args.append(prev)
        aliases = {len(args) - 1: 0}
    return pl.pallas_call(
        functools.partial(_final_kernel, dm),
        out_shape=jax.ShapeDtypeStruct((b_, l_, dm), F32),
        grid=(l_ // tm,),
        in_specs=in_specs,
        out_specs=pl.BlockSpec((1, tm, dm), lambda i: (sample, i, 0)),
        input_output_aliases=aliases,
        compiler_params=_cparams(("parallel",)),
        name="final",
    )(*args)


def _rope_tables(l_):
    half = HEAD_DIM // 2
    inv_freq = ROPE_THETA ** (-jnp.arange(0, half, 2, dtype=F32) / half)
    t = jnp.arange(l_, dtype=jnp.int32)
    row_id, col_id = t // GRID_W, t % GRID_W
    ang = jnp.concatenate([row_id.astype(F32)[:, None] * inv_freq,
                           col_id.astype(F32)[:, None] * inv_freq], axis=-1)
    cos = jnp.repeat(jnp.cos(ang), 2, axis=-1)
    sin = jnp.repeat(jnp.sin(ang), 2, axis=-1)
    sign = jnp.tile(jnp.array([-1.0, 1.0], F32), HEAD_DIM // 2)
    reps = LANES // HEAD_DIM
    return jnp.tile(cos, (1, reps)), jnp.tile(sin * sign, (1, reps))


def _routing(idx, rank, counts, t_, ne):
    m_ = t_ * TOP_K
    blk = EXPERT_BLOCK
    nb = (m_ + ne * (blk - 1)) // blk
    nb = -(-nb // SLOT_BLOCK_MULTIPLE) * SLOT_BLOCK_MULTIPLE
    blk_counts = (counts + blk - 1) // blk
    blk_ends = jnp.cumsum(blk_counts)
    first_slot = (blk_ends - blk_counts) * blk
    onehot = idx[:, :, None] == jnp.arange(ne, dtype=jnp.int32)[None, None, :]
    dest = rank + jnp.sum(jnp.where(onehot, first_slot[None, None, :], 0), axis=2)
    blocks = jnp.arange(nb, dtype=jnp.int32)
    block_expert = jnp.minimum(jnp.sum(blocks[:, None] >= blk_ends[None, :], axis=1), ne - 1)
    in_expert = blocks * blk - first_slot[block_expert]
    block_valid = jnp.clip(counts[block_expert] - in_expert, 0, blk)
    block_valid = jnp.where(blocks < blk_ends[-1], block_valid, 0)
    return dest.astype(jnp.int32), block_expert.astype(jnp.int32), block_valid.astype(jnp.int32), nb * blk


def kernel(x, c, ctx, c_ctx, w_mod, b_mod, norm1_g, norm2_g, w_in, s5_lam_re, s5_lam_im, s5_log_step,
           s5_b_re, s5_b_im, s5_c_re, s5_c_im, s5_d, w_glu, b_glu, w_ssm_out, q_norm_g, k_norm_g,
           w_attn_out, w_out, router_w, router_b, w_gate_up, b_gate_up, w_down, b_down, final_norm_g):
    b_, l_, dm = x.shape
    lc = ctx.shape[1]
    depth = w_mod.shape[0]
    assert depth == 1, "single-layer block"
    assert b_ <= 7 and l_ % TOKEN_BLOCK == 0 and l_ % ATTN_Q_BLOCK == 0 and lc % S5_CHUNK == 0
    wu = dm // 2
    wkv = wu // GQA_GROUP
    kvh = wkv // HEAD_DIM
    ne = router_w.shape[-1]
    i = 0

    c8 = jnp.zeros((8, dm), F32).at[:b_].set(c).at[b_].set(c_ctx)
    mod3 = _modulation(c8, w_mod[i], b_mod[i]).reshape(8, 1, N_MOD * dm)

    w_in_b = w_in[i].astype(BF16)
    cos, sin = _rope_tables(l_)
    hsum = jnp.kron(jnp.eye(wu // HEAD_DIM, dtype=F32), jnp.ones((HEAD_DIM, HEAD_DIM), F32)).astype(BF16)
    qg = jnp.tile(q_norm_g[i], wu // HEAD_DIM).reshape(1, wu)
    kg = jnp.tile(k_norm_g[i], wkv // HEAD_DIM).reshape(1, wkv)
    uf, k, vt, qt, glog = _inproj(x, mod3, None, norm1_g[i], w_in_b, cos, sin, qg, kg, hsum, True, TOKEN_BLOCK)
    ones_t = jnp.ones((lc, LANES), F32)
    u_c, k_c, vt_c = _inproj(ctx, mod3, b_, norm1_g[i], w_in_b[:, :wu + 2 * wkv], ones_t, 0.0 * ones_t,
                             qg, kg, hsum, False, lc)

    ops = _s5_operators(s5_lam_re[i], s5_lam_im[i], s5_log_step[i], s5_b_re[i], s5_b_im[i],
                        s5_c_re[i], s5_c_im[i], s5_d[i])
    y_ssm = _s5_mixer(uf, u_c, ops)

    s_all = lc + l_
    tk = max(t for t in range(LANES, ATTN_KEY_CHUNK + 1, LANES) if s_all % t == 0)
    nkv = s_all // tk
    k4 = jnp.concatenate([k_c, k], axis=1).reshape(b_, nkv, tk, wkv)
    vt_all = jnp.concatenate([vt_c, vt], axis=2).reshape(b_, kvh, HEAD_DIM, s_all)
    vx = jnp.concatenate([vt_all, jnp.ones((b_, kvh, 1, s_all), BF16),
                          jnp.zeros((b_, kvh, ATTN_V_ROWS - HEAD_DIM - 1, s_all), BF16)], axis=2)
    vx = vx.reshape(b_, kvh, ATTN_V_ROWS, nkv, tk).transpose(0, 1, 3, 2, 4)
    o_attn = _attention(qt, k4, vx, ATTN_Q_BLOCK)

    rw_pad = jnp.zeros((dm, LANES), F32).at[:, :ne].set(router_w[i])
    rb_pad = jnp.full((1, LANES), -jnp.inf, F32).at[0, :ne].set(router_b[i])
    x_mix, h2, idx, gates, rank, counts = _merge(
        y_ssm, o_attn, glog, x, mod3, w_glu[i].astype(BF16), b_glu[i].reshape(1, wu),
        w_ssm_out[i].astype(BF16), w_attn_out[i].astype(BF16), w_out[i].astype(BF16),
        norm2_g[i].reshape(1, dm), rw_pad, rb_pad, ne, TOKEN_BLOCK)

    t_ = b_ * l_
    kmajor = lambda a: a[:, :TOP_K, :].transpose(1, 0, 2).reshape(TOP_K, t_)
    dest_kmajor, block_expert, block_valid, n_slots = _routing(
        kmajor(idx), kmajor(rank), counts[:ne, 0].astype(jnp.int32), t_, ne)
    xb = _sc_scatter_rows(h2.reshape(t_, dm // 2), dest_kmajor.reshape(-1), n_slots)
    yb = _experts(xb, block_expert, block_valid, w_gate_up[i], b_gate_up[i], w_down[i], b_down[i])
    out = None
    for b in range(b_):
        yg = _sc_gather_rows(yb, dest_kmajor[:, b * l_:(b + 1) * l_].reshape(-1)).reshape(TOP_K, 1, l_, dm // 2)
        out = _final(yg, gates, x_mix, mod3, final_norm_g, TOKEN_BLOCK, b, out)
    return out
```

```python
import functools

import jax
import jax.numpy as jnp
from jax import lax
from jax.experimental import pallas as pl
from jax.experimental.pallas import tpu as pltpu
from jax.experimental.pallas import tpu_sc as plsc

F32 = jnp.float32
BF16 = jnp.bfloat16

EPS = 1e-6
GRID_W = 64
N_MOD = 6
SSM_GROUP = 16
SSM_STATE = 64
HEAD_DIM = 64
GQA_GROUP = 4
ROPE_THETA = 10000.0
TOP_K = 4
SWIGLU_ALPHA = 1.702
SWIGLU_LIMIT = 7.0
LOG2E = 1.4426950408889634

LANES = 128
S5_CHUNK = 32
S5_POW_ROWS = 56
TOKEN_BLOCK = 512
WIDE_BLOCK = 1024
EXPERT_BLOCK = 512
ATTN_V_ROWS = 80
ATTN_KEY_CHUNK = 384
ATTN_SUB_BLOCKS = 4
ATTN_Q_BLOCK = 256 * ATTN_SUB_BLOCKS
ATTN_STEPS_PER_TRIP = 10
MERGE_ROW_PARTS = 2
SC_GATHER_WINDOW = 128
SC_WORKERS = 32
SLOT_BLOCK_MULTIPLE = SC_GATHER_WINDOW * SC_WORKERS // EXPERT_BLOCK
VMEM_LIMIT = 56 * 1024 * 1024


def _cparams(sem):
    return pltpu.CompilerParams(dimension_semantics=sem, vmem_limit_bytes=VMEM_LIMIT)


def _dot(a, b):
    return jnp.dot(a, b, preferred_element_type=F32)


def _split(a):
    hi = a.astype(BF16)
    lo = (a - hi.astype(F32)).astype(BF16)
    return hi, lo


def _pack2(lo, hi):
    lo_w = lax.bitcast_convert_type(lo.astype(BF16).astype(F32), jnp.uint32) >> 16
    hi_w = lax.bitcast_convert_type(hi.astype(BF16).astype(F32), jnp.uint32) & jnp.uint32(0xFFFF0000)
    return lo_w | hi_w


def _unpack2(words):
    lo = lax.bitcast_convert_type(words << 16, F32)
    hi = lax.bitcast_convert_type(words & jnp.uint32(0xFFFF0000), F32)
    return lo, hi


def _dot3(a, b):
    ah, al = _split(a)
    bh, bl = _split(b)
    return _dot(ah, bh) + _dot(ah, bl) + _dot(al, bh)


def _mod_kernel(c_ref, w_ref, b_ref, o_ref):
    c = c_ref[...]
    s = c * jax.nn.sigmoid(c)
    o_ref[...] = _dot3(s, w_ref[...]) + b_ref[...]


def _modulation(c8, w_mod, b_mod):
    d = c8.shape[1]
    n = w_mod.shape[1]
    tn = WIDE_BLOCK
    return pl.pallas_call(
        _mod_kernel,
        out_shape=jax.ShapeDtypeStruct((8, n), F32),
        grid=(n // tn,),
        in_specs=[pl.BlockSpec((8, d), lambda j: (0, 0)),
                  pl.BlockSpec((d, tn), lambda j: (0, j)),
                  pl.BlockSpec((1, tn), lambda j: (0, j))],
        out_specs=pl.BlockSpec((8, tn), lambda j: (0, j)),
        compiler_params=_cparams(("parallel",)),
        name="mod",
    )(c8, w_mod, b_mod.reshape(1, n))


def _head_norm_rope(z, gain, hsum, cos, sin, scale):
    ss = _dot((z * z).astype(BF16), hsum)
    zn = z * lax.rsqrt(ss * (1.0 / HEAD_DIM) + EPS) * gain
    lane = lax.broadcasted_iota(jnp.int32, (1, LANES), 1)
    even = (lane & 1) == 0
    outs = []
    for ci in range(z.shape[1] // LANES):
        ch = zn[:, ci * LANES:(ci + 1) * LANES]
        nxt = pltpu.roll(ch, LANES - 1, axis=1)
        prv = pltpu.roll(ch, 1, axis=1)
        sw = jnp.where(even, nxt, prv)
        outs.append((ch * cos + sw * sin) * scale)
    return outs


def _inproj_kernel(with_q, dm, x_ref, mod_ref, g_ref, w_ref, cos_ref, sin_ref, qg_ref, kg_ref,
                   hsum_ref, *out_refs):
    x = x_ref[0]
    ms = jnp.mean(x * x, axis=-1, keepdims=True)
    y = x * lax.rsqrt(ms + EPS) * g_ref[...]
    sh = mod_ref[0, :, 0:dm]
    sc = mod_ref[0, :, dm:2 * dm]
    hb = (y * (1.0 + sc) + sh).astype(BF16)
    cos = cos_ref[...]
    sin = sin_ref[...]
    wu = dm // 2
    wkv = wu // GQA_GROUP
    c_k, c_v, c_q, c_g = wu, wu + wkv, wu + 2 * wkv, 2 * wu + 2 * wkv
    u_ref, k_ref, vt_ref = out_refs[:3]
    u = _dot(hb, w_ref[:, 0:c_k])
    if with_q:
        ubuf = out_refs[-1]
        p, gpt, nj = SSM_GROUP, LANES // SSM_GROUP, x.shape[0] // S5_CHUNK
        for ci in range(wu // LANES):
            ubuf[ci] = u[:, ci * LANES:(ci + 1) * LANES]
        for s in range(S5_CHUNK):
            for ci in range(wu // LANES):
                xs = ubuf[ci, pl.ds(s, nj, stride=S5_CHUNK), :].astype(BF16)
                for gi in range(gpt):
                    u_ref[ci * gpt + gi, 0, :, s * p:(s + 1) * p] = xs[:, gi * p:(gi + 1) * p]
    else:
        u_ref[0] = u
    kz = _dot(hb, w_ref[:, c_k:c_v])
    (kr,) = _head_norm_rope(kz, kg_ref[...], hsum_ref[0:wkv, 0:wkv], cos, sin, 1.0)
    k_ref[0] = kr.astype(BF16)
    vt_ref[0] = _dot(hb, w_ref[:, c_v:c_q]).T.astype(BF16)
    if with_q:
        qt_ref, gl_ref = out_refs[3:5]
        qz = _dot(hb, w_ref[:, c_q:c_g])
        qs = _head_norm_rope(qz, qg_ref[...], hsum_ref[...], cos, sin, HEAD_DIM ** -0.5 * LOG2E)
        for ci, qc in enumerate(qs):
            qt_ref[0, ci * LANES:(ci + 1) * LANES, :] = qc.T.astype(BF16)
        gl_ref[0] = _dot(hb, w_ref[:, c_g:]).astype(BF16)


def _inproj(x, mod3, mod_row0, norm_g, w_in_b, cos, sin, qg, kg, hsum, with_q, tm):
    b_, l_, dm = x.shape
    wu = dm // 2
    wkv = wu // GQA_GROUP
    n_in = w_in_b.shape[1]
    outs = [jax.ShapeDtypeStruct((b_, l_, wu), F32),
            jax.ShapeDtypeStruct((b_, l_, wkv), BF16),
            jax.ShapeDtypeStruct((b_, wkv, l_), BF16)]
    ospecs = [pl.BlockSpec((1, tm, wu), lambda b, i: (b, i, 0)),
              pl.BlockSpec((1, tm, wkv), lambda b, i: (b, i, 0)),
              pl.BlockSpec((1, wkv, tm), lambda b, i: (b, 0, i))]
    scratch = []
    if with_q:
        g, rows = wu // SSM_GROUP, S5_CHUNK * SSM_GROUP
        outs[0] = jax.ShapeDtypeStruct((g, b_, l_ // S5_CHUNK, rows), BF16)
        ospecs[0] = pl.BlockSpec((g, 1, tm // S5_CHUNK, rows), lambda b, i: (0, b, i, 0))
        outs += [jax.ShapeDtypeStruct((b_, wu, l_), BF16),
                 jax.ShapeDtypeStruct((b_, l_, 2 * dm), BF16)]
        ospecs += [pl.BlockSpec((1, wu, tm), lambda b, i: (b, 0, i)),
                   pl.BlockSpec((1, tm, 2 * dm), lambda b, i: (b, i, 0))]
        scratch = [pltpu.VMEM((wu // LANES, tm, LANES), F32)]
    if mod_row0 is None:
        mod_map = lambda b, i: (b, 0, 0)
    else:
        mod_map = lambda b, i: (mod_row0, 0, 0)
    return pl.pallas_call(
        functools.partial(_inproj_kernel, with_q, dm),
        out_shape=outs,
        grid=(b_, l_ // tm),
        in_specs=[pl.BlockSpec((1, tm, dm), lambda b, i: (b, i, 0)),
                  pl.BlockSpec((1, 1, N_MOD * dm), mod_map),
                  pl.BlockSpec((1, dm), lambda b, i: (0, 0)),
                  pl.BlockSpec((dm, n_in), lambda b, i: (0, 0)),
                  pl.BlockSpec((tm, LANES), lambda b, i: (i, 0)),
                  pl.BlockSpec((tm, LANES), lambda b, i: (i, 0)),
                  pl.BlockSpec((1, wu), lambda b, i: (0, 0)),
                  pl.BlockSpec((1, wkv), lambda b, i: (0, 0)),
                  pl.BlockSpec((wu, wu), lambda b, i: (0, 0))],
        out_specs=ospecs,
        scratch_shapes=scratch,
        compiler_params=_cparams(("parallel", "parallel")),
        name="inproj_x" if with_q else "inproj_ctx",
    )(x, mod3, norm_g.reshape(1, dm), w_in_b, cos, sin, qg, kg, hsum)


def _cexp_pow(e, lr, li):
    mag = jnp.exp(e * lr)
    ang = e * li
    return mag * jnp.cos(ang), mag * jnp.sin(ang)


def _cmul(ar, ai, br, bi):
    return ar * br - ai * bi, ar * bi + ai * br


def _s5pre_kernel(lamr_r, lami_r, lst_r, lamr_c, lami_c, lst_c, btr_ref, bti_ref, ctr_ref, cti_ref,
                  d_ref, win_ref, wout_ref, m_ref, dec_ref):
    tc = S5_CHUNK
    half = tc // 2
    p = SSM_GROUP
    rows = tc * p
    gsh = SSM_GROUP.bit_length() - 1
    nsh = SSM_STATE.bit_length() - 1
    row_i = lax.broadcasted_iota(jnp.int32, (rows, rows), 0)
    col_i = lax.broadcasted_iota(jnp.int32, (rows, rows), 1)
    lane_g = lax.broadcasted_iota(jnp.int32, (1, LANES), 1) >> nsh
    subl_g = lax.broadcasted_iota(jnp.int32, (LANES, 1), 0) >> nsh
    e_rows = lax.broadcasted_iota(jnp.int32, (S5_POW_ROWS, 1), 0) - half
    e_lanes = lax.broadcasted_iota(jnp.int32, (1, LANES), 1) - half
    sel_e = lax.broadcasted_iota(jnp.int32, (LANES, rows), 0) - half
    sel_t = lax.broadcasted_iota(jnp.int32, (LANES, rows), 1) >> gsh
    sel_p = (lax.broadcasted_iota(jnp.int32, (LANES, rows), 0)
             == (lax.broadcasted_iota(jnp.int32, (LANES, rows), 1) & (p - 1))).astype(BF16)

    def spread_lanes(table, sel):
        hi, lo = _split(table)
        return _dot(hi, sel) + _dot(lo, sel)

    def spread_rows(table, e_of_s):
        return jnp.concatenate([jnp.broadcast_to(table[e_of_s(s) + half:e_of_s(s) + half + 1, :], (p, LANES))
                                for s in range(tc)], axis=0)

    m_acc = [jnp.where(row_i == col_i, d_ref[0, gl], 0.0) for gl in range(2)]
    for dr in range(2):
        step = jnp.exp(lst_r[dr, 0])
        lam_r, lam_i = lamr_r[dr, 0], lami_r[dr, 0]
        lr, li = lam_r * step, lam_i * step
        pw_r, pw_i = _cexp_pow(e_rows.astype(F32), lr, li)
        lb_r, lb_i = pw_r[half + 1:half + 2, :], pw_i[half + 1:half + 2, :]
        den = lam_r * lam_r + lam_i * lam_i
        nr, ni = lb_r - 1.0, lb_i
        cf_r = (nr * lam_r + ni * lam_i) / den
        cf_i = (ni * lam_r - nr * lam_i) / den
        bt_r = jnp.concatenate([btr_ref[dr, 0]] * tc, axis=0)
        bt_i = jnp.concatenate([bti_ref[dr, 0]] * tc, axis=0)
        bb_r, bb_i = _cmul(cf_r, cf_i, bt_r, bt_i)
        e_in = (lambda s: tc - 1 - s) if dr == 0 else (lambda s: s)
        wi_r, wi_i = _cmul(bb_r, bb_i, spread_rows(pw_r, e_in), spread_rows(pw_i, e_in))
        e_a = (lambda s: half - s) if dr == 0 else (lambda s: s - half)
        a_r, a_i = _cmul(bb_r, bb_i, spread_rows(pw_r, e_a), spread_rows(pw_i, e_a))
        dec_ref[0, dr, 0] = pw_r[tc + half:tc + half + 1, :]
        dec_ref[0, dr, 1] = pw_i[tc + half:tc + half + 1, :]
        stepc = jnp.exp(lst_c[dr, 0])
        lrc, lic = lamr_c[dr, 0] * stepc, lami_c[dr, 0] * stepc
        qw_r, qw_i = _cexp_pow(e_lanes.astype(F32), lrc, lic)
        qw_r = jnp.where(e_lanes <= tc, qw_r, 0.0)
        qw_i = jnp.where(e_lanes <= tc, qw_i, 0.0)
        c_r, c_i = spread_lanes(ctr_ref[dr, 0], sel_p), spread_lanes(cti_ref[dr, 0], sel_p)
        sel_b = (sel_e == (sel_t - half if dr == 0 else half - sel_t)).astype(BF16)
        bm_r, bm_i = _cmul(c_r, c_i, spread_lanes(qw_r, sel_b), spread_lanes(qw_i, sel_b))
        sel_o = (sel_e == (sel_t + 1 if dr == 0 else tc - sel_t)).astype(BF16)
        wo_r, wo_i = _cmul(c_r, c_i, spread_lanes(qw_r, sel_o), spread_lanes(qw_i, sel_o))
        if dr == 0:
            mask = (col_i >> gsh) >= (row_i >> gsh)
        else:
            mask = (row_i >> gsh) >= (col_i >> gsh)
        for gl in range(2):
            lsel = lane_g == gl
            ssel = subl_g == gl
            win_ref[0, gl, 2 * dr] = jnp.where(lsel, wi_r, 0.0).astype(BF16)
            win_ref[0, gl, 2 * dr + 1] = jnp.where(lsel, wi_i, 0.0).astype(BF16)
            wout_ref[0, 2 * dr, :, gl * rows:(gl + 1) * rows] = jnp.where(ssel, wo_r, 0.0).astype(BF16)
            wout_ref[0, 2 * dr + 1, :, gl * rows:(gl + 1) * rows] = jnp.where(ssel, -wo_i, 0.0).astype(BF16)
            kmat = (_dot3(jnp.where(lsel, a_r, 0.0), bm_r) - _dot3(jnp.where(lsel, a_i, 0.0), bm_i))
            m_acc[gl] = m_acc[gl] + jnp.where(mask, kmat, 0.0)
    for gl in range(2):
        m_ref[0, gl] = m_acc[gl].astype(BF16)


def _s5_operators(lam_re, lam_im, log_step, b_re, b_im, c_re, c_im, d_skip):
    g = lam_re.shape[1]
    n, p = SSM_STATE, SSM_GROUP
    npair = g // 2
    tc = S5_CHUNK
    rows = tc * p

    def row_form(a):
        return a.reshape(2, npair, 1, 2 * n)

    def col_form(a):
        return a.reshape(2, npair, 2 * n, 1)

    lst = jnp.broadcast_to(log_step[:, :, None], (2, g, n))
    bt = lambda b: b.reshape(2, npair, 2, n, p).transpose(0, 1, 4, 2, 3).reshape(2, npair, p, 2 * n)
    ct = lambda c: jnp.pad(c.reshape(2, npair, 2, p, n).transpose(0, 1, 2, 4, 3).reshape(2, npair, 2 * n, p),
                           ((0, 0), (0, 0), (0, 0), (0, LANES - p)))
    d2 = jnp.broadcast_to(d_skip.reshape(npair, 2, 1, 1, p), (npair, 2, 1, tc, p)).reshape(npair, 2, 1, rows)

    def spec4(shape):
        return pl.BlockSpec((2, 1) + shape, lambda i: (0, i, 0, 0))

    return pl.pallas_call(
        _s5pre_kernel,
        out_shape=[jax.ShapeDtypeStruct((npair, 2, 4, rows, LANES), BF16),
                   jax.ShapeDtypeStruct((npair, 4, LANES, 2 * rows), BF16),
                   jax.ShapeDtypeStruct((npair, 2, rows, rows), BF16),
                   jax.ShapeDtypeStruct((npair, 2, 2, 1, LANES), F32)],
        grid=(npair,),
        in_specs=[spec4((1, LANES)), spec4((1, LANES)), spec4((1, LANES)),
                  spec4((LANES, 1)), spec4((LANES, 1)), spec4((LANES, 1)),
                  spec4((p, LANES)), spec4((p, LANES)),
                  spec4((LANES, LANES)), spec4((LANES, LANES)),
                  pl.BlockSpec((1, 2, 1, rows), lambda i: (i, 0, 0, 0))],
        out_specs=[pl.BlockSpec((1, 2, 4, rows, LANES), lambda i: (i, 0, 0, 0, 0)),
                   pl.BlockSpec((1, 4, LANES, 2 * rows), lambda i: (i, 0, 0, 0)),
                   pl.BlockSpec((1, 2, rows, rows), lambda i: (i, 0, 0, 0)),
                   pl.BlockSpec((1, 2, 2, 1, LANES), lambda i: (i, 0, 0, 0, 0))],
        compiler_params=_cparams(("parallel",)),
        name="s5pre",
    )(row_form(lam_re), row_form(lam_im), row_form(lst),
      col_form(lam_re), col_form(lam_im), col_form(lst),
      bt(b_re), bt(b_im), ct(c_re), ct(c_im), d2)


def _s5state_kernel(u_ref, win_ref, *s_refs):
    for b in range(u_ref.shape[1]):
        for kind in range(4):
            s_refs[kind][:, b * LANES:(b + 1) * LANES] = (_dot(u_ref[0, b], win_ref[0, 0, kind])
                                                          + _dot(u_ref[1, b], win_ref[0, 1, kind]))


def _s5_states(uf, win):
    g, b_, j, rows = uf.shape
    npair = g // 2
    out = jax.ShapeDtypeStruct((j, npair * b_ * LANES), F32)
    ospec = pl.BlockSpec((j, b_ * LANES), lambda pr: (0, pr))
    return pl.pallas_call(
        _s5state_kernel,
        out_shape=[out] * 4,
        grid=(npair,),
        in_specs=[pl.BlockSpec((2, b_, j, rows), lambda pr: (pr, 0, 0, 0)),
                  pl.BlockSpec((1, 2, 4, rows, LANES), lambda pr: (pr, 0, 0, 0, 0))],
        out_specs=[ospec] * 4,
        compiler_params=_cparams(("parallel",)),
        name="s5state",
    )(uf, win)


def _s5scan_kernel(cfr, cfi, cbr, cbi, sfr, sfi, sbr, sbi, dec_ref, hfr, hfi, hbr, hbi):
    jc, jl, w = cfr.shape[0], sfr.shape[0], sfr.shape[1]
    zero = jnp.zeros((1, w), F32)

    def run(sr, si, outs, ar, ai, n, reverse, carry):
        def body(i, hc):
            row = (n - 1 - i) if reverse else i
            hr, hi = hc
            if outs is not None:
                outs[0][pl.ds(row, 1), :] = hr
                outs[1][pl.ds(row, 1), :] = hi
            nr = ar * hr - ai * hi + sr[pl.ds(row, 1), :]
            ni = ar * hi + ai * hr + si[pl.ds(row, 1), :]
            return nr, ni
        return lax.fori_loop(0, n, body, carry)

    afr, afi = dec_ref[0, 0], dec_ref[0, 1]
    abr, abi = dec_ref[1, 0], dec_ref[1, 1]
    c = run(cfr, cfi, None, afr, afi, jc, False, (zero, zero))
    run(sfr, sfi, (hfr, hfi), afr, afi, jl, False, c)
    c = run(cbr, cbi, None, abr, abi, jc, True, (zero, zero))
    run(sbr, sbi, (hbr, hbi), abr, abi, jl, True, c)


def _s5_scan(ctx_states, states, dec_cols):
    jl, wtot = states[0].shape
    jc = ctx_states[0].shape[0]
    tw = WIDE_BLOCK
    spec = pl.BlockSpec((jl, tw), lambda i: (0, i))
    cspec = pl.BlockSpec((jc, tw), lambda i: (0, i))
    return pl.pallas_call(
        _s5scan_kernel,
        out_shape=[jax.ShapeDtypeStruct((jl, wtot), F32)] * 4,
        grid=(wtot // tw,),
        in_specs=[cspec] * 4 + [spec] * 4 + [pl.BlockSpec((2, 2, 1, tw), lambda i: (0, 0, 0, i))],
        out_specs=[spec] * 4,
        compiler_params=_cparams(("parallel",)),
        name="s5scan",
    )(*ctx_states, *states, dec_cols)


def _s5out_kernel(ppt, u_ref, *refs):
    h_refs, (wout_ref, m_ref, o_ref, yf) = refs[:4 * ppt], refs[4 * ppt:]
    tc, p = S5_CHUNK, SSM_GROUP
    rows = u_ref.shape[3]
    j = u_ref.shape[2]
    for pr in range(ppt):
        y2 = None
        for kind in range(4):
            t = _dot(h_refs[4 * pr + kind][...].astype(BF16), wout_ref[pr, kind])
            y2 = t if y2 is None else y2 + t
        for gl in range(2):
            y = y2[:, gl * rows:(gl + 1) * rows] + _dot(u_ref[2 * pr + gl, 0], m_ref[pr, gl])
            yf[2 * pr + gl] = y.astype(BF16)
    for t in range(tc):
        row = jnp.concatenate([yf[gi, :, t * p:(t + 1) * p] for gi in range(2 * ppt)], axis=1)
        o_ref[0, pl.ds(t, j, stride=tc), :] = row.astype(F32)


def _s5_outputs(uf, hprev, wout, m):
    g, b_, j, rows = uf.shape
    gpt = LANES // SSM_GROUP
    ppt = gpt // 2
    hspecs = [pl.BlockSpec((j, LANES), lambda c, b, pr=pr: (0, (c * ppt + pr) * b_ + b))
              for pr in range(ppt) for _ in range(4)]
    hargs = [hprev[kind] for _ in range(ppt) for kind in range(4)]
    return pl.pallas_call(
        functools.partial(_s5out_kernel, ppt),
        out_shape=jax.ShapeDtypeStruct((b_, j * S5_CHUNK, g * SSM_GROUP), F32),
        grid=(g // gpt, b_),
        in_specs=[pl.BlockSpec((gpt, 1, j, rows), lambda c, b: (c, b, 0, 0))] + hspecs +
                 [pl.BlockSpec((ppt, 4, LANES, 2 * rows), lambda c, b: (c, 0, 0, 0)),
                  pl.BlockSpec((ppt, 2, rows, rows), lambda c, b: (c, 0, 0, 0))],
        out_specs=pl.BlockSpec((1, j * S5_CHUNK, LANES), lambda c, b: (b, 0, c)),
        scratch_shapes=[pltpu.VMEM((gpt, j, rows), BF16)],
        compiler_params=_cparams(("parallel", "parallel")),
        name="s5out",
    )(uf, *hargs, wout, m)


def _chunks_per_step(j):
    return max(n for n in (64, 32, 16, 8) if j % n == 0)


def _s5_regroup_kernel(nj, u_ref, o_ref):
    tc, p = S5_CHUNK, SSM_GROUP
    for s in range(tc):
        xs = u_ref[0, pl.ds(s, nj, stride=tc), :].astype(o_ref.dtype)
        for gi in range(LANES // p):
            o_ref[gi, 0, :, s * p:(s + 1) * p] = xs[:, gi * p:(gi + 1) * p]


def _s5_regroup(u):
    b_, l_, w = u.shape
    tc, p = S5_CHUNK, SSM_GROUP
    gpt = LANES // p
    j = l_ // tc
    nj = _chunks_per_step(j)
    return pl.pallas_call(
        functools.partial(_s5_regroup_kernel, nj),
        out_shape=jax.ShapeDtypeStruct((w // p, b_, j, tc * p), BF16),
        grid=(b_, j // nj, w // LANES),
        in_specs=[pl.BlockSpec((1, tc * nj, LANES), lambda b, i, c: (b, i, c))],
        out_specs=pl.BlockSpec((gpt, 1, nj, tc * p), lambda b, i, c: (c, b, i, 0)),
        compiler_params=_cparams(("parallel", "parallel", "parallel")),
        name="s5regroup",
    )(u)


def _s5_mixer(uf, u_c, ops):
    win, wout, m, dec = ops
    g, b_ = uf.shape[0], uf.shape[1]
    npair = g // 2
    states = _s5_states(uf, win)
    ctx_states = _s5_states(_s5_regroup(u_c), win)
    dec_cols = jnp.broadcast_to(dec.transpose(1, 2, 3, 0, 4)[:, :, :, :, None], (2, 2, 1, npair, b_, LANES))
    dec_cols = dec_cols.reshape(2, 2, 1, npair * b_ * LANES)
    hprev = _s5_scan(ctx_states, states, dec_cols)
    return _s5_outputs(uf, hprev, wout, m)


def _attn_kernel(nkv, qt_ref, k_ref, vx_ref, o_ref, s0_ref, s1_ref):
    hd = HEAD_DIM
    kvh = pl.program_id(1)
    tq = qt_ref.shape[2] // ATTN_SUB_BLOCKS
    gq = qt_ref.shape[1] // hd
    kw = k_ref.shape[3]
    vr = vx_ref.shape[3]
    own = (lax.broadcasted_iota(jnp.int32, (kw, tq), 0) >> (hd.bit_length() - 1)) == kvh
    bufs = (s0_ref, s1_ref)
    for sb in range(ATTN_SUB_BLOCKS):
        qp = []
        for g in range(gq):
            qg = qt_ref[0, g * hd:(g + 1) * hd, sb * tq:(sb + 1) * tq].astype(F32)
            q2 = jnp.concatenate([qg] * (kw // hd), axis=0)
            qp.append(jnp.where(own, q2, 0.0).astype(BF16))
        qp = jnp.concatenate(qp, axis=1)
        m0 = jnp.full((1, gq * tq), -jnp.inf, F32)
        acc0 = jnp.zeros((vr, gq * tq), F32)

        def scores(jb, s_ref, qp=qp):
            s = _dot(k_ref[0, jb], qp)
            s_ref[...] = s
            return jnp.max(s, axis=0, keepdims=True)

        def consume(jb, s_ref, cmax, m_prev, acc):
            m_new = jnp.maximum(m_prev, cmax)
            alpha = jnp.exp2(m_prev - m_new)
            p = jnp.exp2(s_ref[...] - m_new).astype(BF16)
            return m_new, alpha * acc + _dot(vx_ref[0, 0, jb], p)

        def steps(j0, count, m, acc, cmax, scores=scores, consume=consume):
            for u in range(count):
                cnext = scores(j0 + u + 1, bufs[(u + 1) % 2])
                m, acc = consume(j0 + u, bufs[u % 2], cmax, m, acc)
                cmax = cnext
            return m, acc, cmax

        unroll = ATTN_STEPS_PER_TRIP
        trips = (nkv - 1) // unroll
        carry = lax.fori_loop(0, trips, lambda i, c, steps=steps: steps(unroll * i, unroll, *c),
                              (m0, acc0, scores(0, s0_ref)))
        m, acc, cmax = steps(trips * unroll, (nkv - 1) - trips * unroll, *carry)
        m, acc = consume(nkv - 1, bufs[(nkv - 1) % 2], cmax, m, acc)
        o = acc[0:hd] / acc[hd:hd + 1]
        ot = jnp.concatenate([o[:, g * tq:(g + 1) * tq] for g in range(gq)], axis=0)
        o_ref[0, sb * tq:(sb + 1) * tq, :] = ot.T.astype(BF16)


def _attention(qt, k4, vx, tq):
    b_, wq, l_ = qt.shape
    _, kvh, nkv, vr, tk = vx.shape
    kw = k4.shape[3]
    gw = wq // kvh
    return pl.pallas_call(
        functools.partial(_attn_kernel, nkv),
        out_shape=jax.ShapeDtypeStruct((b_, l_, wq), BF16),
        grid=(b_, kvh, l_ // tq),
        in_specs=[pl.BlockSpec((1, gw, tq), lambda b, h, i: (b, h, i)),
                  pl.BlockSpec((1, nkv, tk, kw), lambda b, h, i: (b, 0, 0, 0)),
                  pl.BlockSpec((1, 1, nkv, vr, tk), lambda b, h, i: (b, h, 0, 0, 0))],
        out_specs=pl.BlockSpec((1, tq, gw), lambda b, h, i: (b, i, h)),
        scratch_shapes=[pltpu.VMEM((tk, (gw // HEAD_DIM) * tq // ATTN_SUB_BLOCKS), F32)] * 2,
        compiler_params=_cparams(("parallel", "parallel", "parallel")),
        name="attn",
    )(qt, k4, vx)


def _merge_kernel(dm, ys_ref, oa_ref, gl_ref, x_ref, mod_ref, wglu_ref, bglu_ref, wso_ref, wao_ref,
                  wout_ref, n2g_ref, rw_ref, rb_ref, utri_ref, xmix_ref, h2_ref, idx_ref, gate_ref, rank_ref,
                  cnt_out_ref, cnt_ref):
    tm = x_ref.shape[1]
    parts = [pl.ds(r * (tm // MERGE_ROW_PARTS), tm // MERGE_ROW_PARTS) for r in range(MERGE_ROW_PARTS)]
    g1 = mod_ref[0, :, 2 * dm:3 * dm]
    sh2 = mod_ref[0, :, 3 * dm:4 * dm]
    sc2 = mod_ref[0, :, 4 * dm:5 * dm]
    s = [jax.nn.gelu(ys_ref[0, r, :]) for r in parts]
    z = [_dot(sr.astype(BF16), wglu_ref[...]) + bglu_ref[...] for sr in s]
    s = [sr * jax.nn.sigmoid(zr) for sr, zr in zip(s, z)]
    a = [_dot(sr.astype(BF16), wso_ref[...]) for sr in s]
    bq = [_dot(oa_ref[0, r, :], wao_ref[...]) for r in parts]
    merged = []
    for r, ar, br in zip(parts, a, bq):
        gts = jax.nn.sigmoid(gl_ref[0, r, :].astype(F32))
        merged.append((gts[:, 0:dm] * ar + gts[:, dm:2 * dm] * br).astype(BF16))
    out = [_dot(mr, wout_ref[...]) for mr in merged]
    h2_parts = []
    for r, outr in zip(parts, out):
        xm = x_ref[0, r, :] + g1 * outr
        xmix_ref[0, r, :] = xm
        ms = jnp.mean(xm * xm, axis=-1, keepdims=True)
        h2r = xm * lax.rsqrt(ms + EPS) * n2g_ref[...] * (1.0 + sc2) + sh2
        h2_ref[0, r, :] = _pack2(h2r[:, 0:dm // 2], h2r[:, dm // 2:dm])
        h2_parts.append(h2r)
    logits = jnp.concatenate([_dot3(h2r, rw_ref[...]) for h2r in h2_parts], axis=0) + rb_ref[...]
    nep = cnt_ref.shape[0]
    l = logits.T[0:nep, :]
    sub = lax.broadcasted_iota(jnp.int32, l.shape, 0).astype(F32)
    neg = jnp.float32(-jnp.inf)
    vals, idxs = [], []
    for _ in range(TOP_K):
        mx = jnp.max(l, axis=0, keepdims=True)
        ix = jnp.min(jnp.where(l == mx, sub, float(nep)), axis=0, keepdims=True)
        vals.append(mx)
        idxs.append(ix)
        l = jnp.where(sub == ix, neg, l)
    es = [jnp.exp(v - vals[0]) for v in vals]
    den = es[0] + es[1] + es[2] + es[3]
    @pl.when((pl.program_id(0) == 0) & (pl.program_id(1) == 0))
    def _():
        cnt_ref[...] = jnp.zeros(cnt_ref.shape, F32)

    onehot = jnp.zeros(l.shape, F32)
    for k in range(TOP_K):
        onehot = onehot + jnp.where(sub == idxs[k], 1.0, 0.0)
    prior = _dot(onehot.astype(BF16), utri_ref[...]) + cnt_ref[...]
    cnt_ref[...] = cnt_ref[...] + jnp.sum(onehot, axis=1, keepdims=True)
    cnt_out_ref[...] = cnt_ref[...]
    krow = lax.broadcasted_iota(jnp.int32, (8, l.shape[1]), 0)
    idx_out = jnp.zeros(krow.shape, F32)
    gate_out = jnp.zeros(krow.shape, F32)
    rank_out = jnp.zeros(krow.shape, F32)
    for k in range(TOP_K):
        rk = jnp.sum(jnp.where(sub == idxs[k], prior, 0.0), axis=0, keepdims=True)
        idx_out = jnp.where(krow == k, idxs[k], idx_out)
        gate_out = jnp.where(krow == k, es[k] / den, gate_out)
        rank_out = jnp.where(krow == k, rk, rank_out)
    idx_ref[0] = idx_out.astype(jnp.int32)
    gate_ref[0] = jnp.concatenate([gate_out, jnp.zeros((LANES - 8, gate_out.shape[1]), F32)], axis=0).T
    rank_ref[0] = rank_out.astype(jnp.int32)


def _merge(y_ssm, o_attn, glog, x, mod3, w_glu, b_glu, w_ssm_out, w_attn_out, w_out, norm2_g,
           rw_pad, rb_pad, ne, tm):
    b_, l_, dm = x.shape
    wu = dm // 2
    tok = lambda w: pl.BlockSpec((1, tm, w), lambda b, i: (b, i, 0))
    full = lambda r, c: pl.BlockSpec((r, c), lambda b, i: (0, 0))
    utri = (jnp.arange(tm)[:, None] < jnp.arange(tm)[None, :]).astype(BF16)
    nep = -(-ne // 8) * 8
    kmaj = pl.BlockSpec((1, 8, tm), lambda b, i: (b, 0, i))
    return pl.pallas_call(
        functools.partial(_merge_kernel, dm),
        out_shape=[jax.ShapeDtypeStruct((b_, l_, dm), F32),
                   jax.ShapeDtypeStruct((b_, l_, dm // 2), jnp.uint32),
                   jax.ShapeDtypeStruct((b_, 8, l_), jnp.int32),
                   jax.ShapeDtypeStruct((b_, l_, LANES), F32),
                   jax.ShapeDtypeStruct((b_, 8, l_), jnp.int32),
                   jax.ShapeDtypeStruct((nep, 1), F32)],
        grid=(b_, l_ // tm),
        in_specs=[tok(wu), tok(wu), tok(2 * dm), tok(dm),
                  pl.BlockSpec((1, 1, N_MOD * dm), lambda b, i: (b, 0, 0)),
                  full(wu, wu), full(1, wu), full(wu, dm), full(wu, dm), full(dm, dm),
                  full(1, dm), full(dm, LANES), full(1, LANES), full(tm, tm)],
        out_specs=[tok(dm), tok(dm // 2), kmaj, tok(LANES), kmaj, full(nep, 1)],
        scratch_shapes=[pltpu.VMEM((nep, 1), F32)],
        compiler_params=_cparams(("arbitrary", "arbitrary")),
        name="merge",
    )(y_ssm, o_attn, glog, x, mod3, w_glu, b_glu, w_ssm_out, w_attn_out, w_out, norm2_g, rw_pad, rb_pad, utri)


def _expert_kernel(de, be_ref, nv_ref, x_ref, wgu_ref, bgu_ref, wd_ref, bd_ref, y_ref, wgu_b, wd_b):
    i = pl.program_id(0)
    n_valid = nv_ref[i]
    used = n_valid > 0
    new_expert = (i == 0) | (be_ref[i] != be_ref[jnp.maximum(i - 1, 0)])

    @pl.when(used & new_expert)
    def _():
        wgu_b[...] = wgu_ref[0].astype(BF16)
        wd_b[...] = wd_ref[0].astype(BF16)

    @pl.when(used)
    def _():
        row = lax.broadcasted_iota(jnp.int32, (x_ref.shape[0], 1), 0)
        x_lo, x_hi = _unpack2(jnp.where(row < n_valid, x_ref[...], jnp.uint32(0)))
        xb = jnp.concatenate([x_lo, x_hi], axis=1).astype(BF16)
        gu = _dot(xb, wgu_b[...]) + bgu_ref[0]
        gate = jnp.minimum(gu[:, 0:de], SWIGLU_LIMIT)
        up = jnp.clip(gu[:, de:2 * de], -SWIGLU_LIMIT, SWIGLU_LIMIT)
        act = (up + 1.0) * (gate * jax.nn.sigmoid(SWIGLU_ALPHA * gate))
        y = _dot(act.astype(BF16), wd_b[...]) + bd_ref[0]
        half = y.shape[1] // 2
        y_ref[...] = _pack2(y[:, 0:half], y[:, half:])

    @pl.when(jnp.logical_not(used))
    def _():
        y_ref[...] = jnp.zeros(y_ref.shape, y_ref.dtype)


def _experts(xb, block_expert, block_valid, w_gate_up, b_gate_up, w_down, b_down):
    n_slots, dmh = xb.shape
    dm = 2 * dmh
    ne, _, de2 = w_gate_up.shape
    de = de2 // 2
    nb = n_slots // EXPERT_BLOCK
    return pl.pallas_call(
        functools.partial(_expert_kernel, de),
        out_shape=jax.ShapeDtypeStruct((n_slots, dmh), jnp.uint32),
        grid_spec=pltpu.PrefetchScalarGridSpec(
            num_scalar_prefetch=2,
            grid=(nb,),
            in_specs=[pl.BlockSpec((EXPERT_BLOCK, dmh), lambda i, be, nu: (i, 0)),
                      pl.BlockSpec((1, dm, de2), lambda i, be, nu: (be[i], 0, 0)),
                      pl.BlockSpec((1, 1, de2), lambda i, be, nu: (be[i], 0, 0)),
                      pl.BlockSpec((1, de, dm), lambda i, be, nu: (be[i], 0, 0)),
                      pl.BlockSpec((1, 1, dm), lambda i, be, nu: (be[i], 0, 0))],
            out_specs=pl.BlockSpec((EXPERT_BLOCK, dmh), lambda i, be, nu: (i, 0)),
            scratch_shapes=[pltpu.VMEM((dm, de2), BF16), pltpu.VMEM((de, dm), BF16)],
        ),
        compiler_params=_cparams(("arbitrary",)),
        name="expert",
    )(block_expert, block_valid, xb, w_gate_up, b_gate_up.reshape(ne, 1, de2), w_down, b_down.reshape(ne, 1, dm))


def _sc_gather_rows(table, idx):
    n = idx.shape[0]
    w = table.shape[1]
    mesh = plsc.VectorSubcoreMesh(core_axis_name="c", subcore_axis_name="s")
    nc, nw = mesh.num_cores, mesh.num_cores * mesh.num_subcores
    win = SC_GATHER_WINDOW
    assert n % (win * nw) == 0
    per_worker = n // nw

    @functools.partial(
        pl.kernel, out_type=jax.ShapeDtypeStruct((n, w), table.dtype), mesh=mesh,
        scratch_types=[pltpu.VMEM((win,), jnp.int32), pltpu.VMEM((win, w), table.dtype),
                       pltpu.SemaphoreType.DMA])
    def gather(x_hbm, i_hbm, o_hbm, idx_v, rows_v, sem):
        base = (lax.axis_index("s") * nc + lax.axis_index("c")) * per_worker

        @pl.loop(0, per_worker // win)
        def _(j):
            off = base + j * win
            pltpu.sync_copy(i_hbm.at[pl.ds(off, win)], idx_v)
            pltpu.async_copy(x_hbm.at[idx_v], rows_v, sem).wait()
            pltpu.sync_copy(rows_v, o_hbm.at[pl.ds(off, win)])

    return gather(table, idx)


def _sc_scatter_rows(rows, dest, n_out):
    t_, w = rows.shape
    kk = dest.shape[0] // t_
    mesh = plsc.VectorSubcoreMesh(core_axis_name="c", subcore_axis_name="s")
    nc, nw = mesh.num_cores, mesh.num_cores * mesh.num_subcores
    win = SC_GATHER_WINDOW
    assert t_ % (win * nw) == 0
    per_worker = t_ // nw

    @functools.partial(
        pl.kernel, out_type=jax.ShapeDtypeStruct((n_out, w), rows.dtype), mesh=mesh,
        scratch_types=[pltpu.VMEM((win,), jnp.int32)] * kk + [pltpu.VMEM((win, w), rows.dtype),
                                                              pltpu.SemaphoreType.DMA])
    def scatter(x_hbm, d_hbm, o_hbm, *scratch):
        idx_vs, rows_v, sem = scratch[:kk], scratch[kk], scratch[kk + 1]
        base = (lax.axis_index("s") * nc + lax.axis_index("c")) * per_worker

        @pl.loop(0, per_worker // win)
        def _(j):
            off = base + j * win
            pltpu.sync_copy(x_hbm.at[pl.ds(off, win)], rows_v)
            for k in range(kk):
                pltpu.sync_copy(d_hbm.at[pl.ds(k * t_ + off, win)], idx_vs[k])
            copies = [pltpu.async_copy(rows_v, o_hbm.at[idx_vs[k]], sem) for k in range(kk)]
            for cp in copies:
                cp.wait()

    return scatter(rows, dest)


def _final_kernel(dm, yg_ref, gate_ref, xm_ref, mod_ref, fg_ref, *rest):
    o_ref = rest[-1]
    gates = gate_ref[0]
    f_lo = f_hi = None
    for k in range(TOP_K):
        y_lo, y_hi = _unpack2(yg_ref[k, 0])
        gk = gates[:, k:k + 1]
        f_lo = gk * y_lo if f_lo is None else f_lo + gk * y_lo
        f_hi = gk * y_hi if f_hi is None else f_hi + gk * y_hi
    f = jnp.concatenate([f_lo, f_hi], axis=1)
    g2 = mod_ref[0, :, 5 * dm:6 * dm]
    xo = xm_ref[0] + g2 * f
    ms = jnp.mean(xo * xo, axis=-1, keepdims=True)
    o_ref[0] = xo * lax.rsqrt(ms + EPS) * fg_ref[...]


def _final(yg, gates, x_mix, mod3, final_g, tm, sample, prev):
    b_, l_, dm = x_mix.shape
    in_specs = [pl.BlockSpec((TOP_K, 1, tm, dm // 2), lambda i: (0, 0, i, 0)),
                pl.BlockSpec((1, tm, LANES), lambda i: (sample, i, 0)),
                pl.BlockSpec((1, tm, dm), lambda i: (sample, i, 0)),
                pl.BlockSpec((1, 1, N_MOD * dm), lambda i: (sample, 0, 0)),
                pl.BlockSpec((1, dm), lambda i: (0, 0))]
    args = [yg, gates, x_mix, mod3, final_g.reshape(1, dm)]
    aliases = {}
    if prev is not None:
        in_specs.append(pl.BlockSpec(memory_space=pl.ANY))
        args.append(prev)
        aliases = {len(args) - 1: 0}
    return pl.pallas_call(
        functools.partial(_final_kernel, dm),
        out_shape=jax.ShapeDtypeStruct((b_, l_, dm), F32),
        grid=(l_ // tm,),
        in_specs=in_specs,
        out_specs=pl.BlockSpec((1, tm, dm), lambda i: (sample, i, 0)),
        input_output_aliases=aliases,
        compiler_params=_cparams(("parallel",)),
        name="final",
    )(*args)


def _rope_tables(l_):
    half = HEAD_DIM // 2
    inv_freq = ROPE_THETA ** (-jnp.arange(0, half, 2, dtype=F32) / half)
    t = jnp.arange(l_, dtype=jnp.int32)
    row_id, col_id = t // GRID_W, t % GRID_W
    ang = jnp.concatenate([row_id.astype(F32)[:, None] * inv_freq,
                           col_id.astype(F32)[:, None] * inv_freq], axis=-1)
    cos = jnp.repeat(jnp.cos(ang), 2, axis=-1)
    sin = jnp.repeat(jnp.sin(ang), 2, axis=-1)
    sign = jnp.tile(jnp.array([-1.0, 1.0], F32), HEAD_DIM // 2)
    reps = LANES // HEAD_DIM
    return jnp.tile(cos, (1, reps)), jnp.tile(sin * sign, (1, reps))


def _slot_kernel(ne, fs_ref, idx_ref, rank_ref, o_ref):
    idx = idx_ref[0]
    slot = rank_ref[0]
    for e in range(ne):
        slot = slot + jnp.where(idx == e, fs_ref[e], 0)
    o_ref[...] = slot


def _slots(idx, rank, first_slot):
    b_, kr, l_ = idx.shape
    ne = first_slot.shape[0]
    return pl.pallas_call(
        functools.partial(_slot_kernel, ne),
        out_shape=jax.ShapeDtypeStruct((kr, b_ * l_), jnp.int32),
        grid_spec=pltpu.PrefetchScalarGridSpec(
            num_scalar_prefetch=1,
            grid=(b_,),
            in_specs=[pl.BlockSpec((1, kr, l_), lambda b, fs: (b, 0, 0)),
                      pl.BlockSpec((1, kr, l_), lambda b, fs: (b, 0, 0))],
            out_specs=pl.BlockSpec((kr, l_), lambda b, fs: (0, b)),
        ),
        compiler_params=_cparams(("parallel",)),
        name="slots",
    )(first_slot, idx, rank)


def _routing(idx, rank, counts, t_, ne):
    m_ = t_ * TOP_K
    blk = EXPERT_BLOCK
    nb = (m_ + ne * (blk - 1)) // blk
    nb = -(-nb // SLOT_BLOCK_MULTIPLE) * SLOT_BLOCK_MULTIPLE
    blk_counts = (counts + blk - 1) // blk
    blk_ends = jnp.cumsum(blk_counts)
    first_slot = (blk_ends - blk_counts) * blk
    dest = _slots(idx, rank, first_slot.astype(jnp.int32))[:TOP_K]
    blocks = jnp.arange(nb, dtype=jnp.int32)
    block_expert = jnp.minimum(jnp.sum(blocks[:, None] >= blk_ends[None, :], axis=1), ne - 1)
    in_expert = blocks * blk - first_slot[block_expert]
    block_valid = jnp.clip(counts[block_expert] - in_expert, 0, blk)
    block_valid = jnp.where(blocks < blk_ends[-1], block_valid, 0)
    return dest.astype(jnp.int32), block_expert.astype(jnp.int32), block_valid.astype(jnp.int32), nb * blk


def kernel(x, c, ctx, c_ctx, w_mod, b_mod, norm1_g, norm2_g, w_in, s5_lam_re, s5_lam_im, s5_log_step,
           s5_b_re, s5_b_im, s5_c_re, s5_c_im, s5_d, w_glu, b_glu, w_ssm_out, q_norm_g, k_norm_g,
           w_attn_out, w_out, router_w, router_b, w_gate_up, b_gate_up, w_down, b_down, final_norm_g):
    b_, l_, dm = x.shape
    lc = ctx.shape[1]
    depth = w_mod.shape[0]
    assert depth == 1, "single-layer block"
    assert b_ <= 7 and l_ % TOKEN_BLOCK == 0 and l_ % ATTN_Q_BLOCK == 0 and lc % S5_CHUNK == 0
    wu = dm // 2
    wkv = wu // GQA_GROUP
    kvh = wkv // HEAD_DIM
    ne = router_w.shape[-1]
    i = 0

    c8 = jnp.zeros((8, dm), F32).at[:b_].set(c).at[b_].set(c_ctx)
    mod3 = _modulation(c8, w_mod[i], b_mod[i]).reshape(8, 1, N_MOD * dm)

    w_in_b = w_in[i].astype(BF16)
    cos, sin = _rope_tables(l_)
    hsum = jnp.kron(jnp.eye(wu // HEAD_DIM, dtype=F32), jnp.ones((HEAD_DIM, HEAD_DIM), F32)).astype(BF16)
    qg = jnp.tile(q_norm_g[i], wu // HEAD_DIM).reshape(1, wu)
    kg = jnp.tile(k_norm_g[i], wkv // HEAD_DIM).reshape(1, wkv)
    uf, k, vt, qt, glog = _inproj(x, mod3, None, norm1_g[i], w_in_b, cos, sin, qg, kg, hsum, True, TOKEN_BLOCK)
    ones_t = jnp.ones((lc, LANES), F32)
    u_c, k_c, vt_c = _inproj(ctx, mod3, b_, norm1_g[i], w_in_b[:, :wu + 2 * wkv], ones_t, 0.0 * ones_t,
                             qg, kg, hsum, False, lc)

    ops = _s5_operators(s5_lam_re[i], s5_lam_im[i], s5_log_step[i], s5_b_re[i], s5_b_im[i],
                        s5_c_re[i], s5_c_im[i], s5_d[i])
    y_ssm = _s5_mixer(uf, u_c, ops)

    s_all = lc + l_
    tk = max(t for t in range(LANES, ATTN_KEY_CHUNK + 1, LANES) if s_all % t == 0)
    nkv = s_all // tk
    k4 = jnp.concatenate([k_c, k], axis=1).reshape(b_, nkv, tk, wkv)
    vt_all = jnp.concatenate([vt_c, vt], axis=2).reshape(b_, kvh, HEAD_DIM, s_all)
    vx = jnp.concatenate([vt_all, jnp.ones((b_, kvh, 1, s_all), BF16),
                          jnp.zeros((b_, kvh, ATTN_V_ROWS - HEAD_DIM - 1, s_all), BF16)], axis=2)
    vx = vx.reshape(b_, kvh, ATTN_V_ROWS, nkv, tk).transpose(0, 1, 3, 2, 4)
    o_attn = _attention(qt, k4, vx, ATTN_Q_BLOCK)

    rw_pad = jnp.zeros((dm, LANES), F32).at[:, :ne].set(router_w[i])
    rb_pad = jnp.full((1, LANES), -jnp.inf, F32).at[0, :ne].set(router_b[i])
    x_mix, h2, idx, gates, rank, counts = _merge(
        y_ssm, o_attn, glog, x, mod3, w_glu[i].astype(BF16), b_glu[i].reshape(1, wu),
        w_ssm_out[i].astype(BF16), w_attn_out[i].astype(BF16), w_out[i].astype(BF16),
        norm2_g[i].reshape(1, dm), rw_pad, rb_pad, ne, TOKEN_BLOCK)

    t_ = b_ * l_
    dest_kmajor, block_expert, block_valid, n_slots = _routing(idx, rank, counts[:ne, 0].astype(jnp.int32), t_, ne)
    xb = _sc_scatter_rows(h2.reshape(t_, dm // 2), dest_kmajor.reshape(-1), n_slots)
    yb = _experts(xb, block_expert, block_valid, w_gate_up[i], b_gate_up[i], w_down[i], b_down[i])
    out = None
    for b in range(b_):
        yg = _sc_gather_rows(yb, dest_kmajor[:, b * l_:(b + 1) * l_].reshape(-1)).reshape(TOP_K, 1, l_, dm // 2)
        out = _final(yg, gates, x_mix, mod3, final_norm_g, TOKEN_BLOCK, b, out)
    return out
```

```python
import functools

import jax
import jax.numpy as jnp
from jax import lax
from jax.experimental import pallas as pl
from jax.experimental.pallas import tpu as pltpu
from jax.experimental.pallas import tpu_sc as plsc

F32 = jnp.float32
BF16 = jnp.bfloat16

EPS = 1e-6
GRID_W = 64
N_MOD = 6
SSM_GROUP = 16
SSM_STATE = 64
HEAD_DIM = 64
GQA_GROUP = 4
ROPE_THETA = 10000.0
TOP_K = 4
SWIGLU_ALPHA = 1.702
SWIGLU_LIMIT = 7.0
LOG2E = 1.4426950408889634

LANES = 128
S5_CHUNK = 32
S5_POW_ROWS = 56
TOKEN_BLOCK = 512
WIDE_BLOCK = 1024
EXPERT_BLOCK = 512
ATTN_V_ROWS = 80
ATTN_KEY_CHUNK = 384
ATTN_SUB_BLOCKS = 4
ATTN_Q_BLOCK = 256 * ATTN_SUB_BLOCKS
ATTN_STEPS_PER_TRIP = 10
MERGE_ROW_PARTS = 2
SC_GATHER_WINDOW = 128
SC_WORKERS = 32
SLOT_BLOCK_MULTIPLE = SC_GATHER_WINDOW * SC_WORKERS // EXPERT_BLOCK
VMEM_LIMIT = 56 * 1024 * 1024


def _cparams(sem):
    return pltpu.CompilerParams(dimension_semantics=sem, vmem_limit_bytes=VMEM_LIMIT)


def _dot(a, b):
    return jnp.dot(a, b, preferred_element_type=F32)


def _split(a):
    hi = a.astype(BF16)
    lo = (a - hi.astype(F32)).astype(BF16)
    return hi, lo


def _pack2(lo, hi):
    lo_w = lax.bitcast_convert_type(lo.astype(BF16).astype(F32), jnp.uint32) >> 16
    hi_w = lax.bitcast_convert_type(hi.astype(BF16).astype(F32), jnp.uint32) & jnp.uint32(0xFFFF0000)
    return lo_w | hi_w


def _unpack2(words):
    lo = lax.bitcast_convert_type(words << 16, F32)
    hi = lax.bitcast_convert_type(words & jnp.uint32(0xFFFF0000), F32)
    return lo, hi


def _dot3(a, b):
    ah, al = _split(a)
    bh, bl = _split(b)
    return _dot(ah, bh) + _dot(ah, bl) + _dot(al, bh)


def _mod_kernel(c_ref, w_ref, b_ref, o_ref):
    c = c_ref[...]
    s = c * jax.nn.sigmoid(c)
    o_ref[...] = _dot3(s, w_ref[...]) + b_ref[...]


def _modulation(c8, w_mod, b_mod):
    d = c8.shape[1]
    n = w_mod.shape[1]
    tn = WIDE_BLOCK
    return pl.pallas_call(
        _mod_kernel,
        out_shape=jax.ShapeDtypeStruct((8, n), F32),
        grid=(n // tn,),
        in_specs=[pl.BlockSpec((8, d), lambda j: (0, 0)),
                  pl.BlockSpec((d, tn), lambda j: (0, j)),
                  pl.BlockSpec((1, tn), lambda j: (0, j))],
        out_specs=pl.BlockSpec((8, tn), lambda j: (0, j)),
        compiler_params=_cparams(("parallel",)),
        name="mod",
    )(c8, w_mod, b_mod.reshape(1, n))


def _head_norm_rope(z, gain, hsum, cos, sin, scale):
    ss = _dot((z * z).astype(BF16), hsum)
    zn = z * lax.rsqrt(ss * (1.0 / HEAD_DIM) + EPS) * gain
    lane = lax.broadcasted_iota(jnp.int32, (1, LANES), 1)
    even = (lane & 1) == 0
    outs = []
    for ci in range(z.shape[1] // LANES):
        ch = zn[:, ci * LANES:(ci + 1) * LANES]
        nxt = pltpu.roll(ch, LANES - 1, axis=1)
        prv = pltpu.roll(ch, 1, axis=1)
        sw = jnp.where(even, nxt, prv)
        outs.append((ch * cos + sw * sin) * scale)
    return outs


def _inproj_kernel(with_q, dm, x_ref, mod_ref, g_ref, w_ref, cos_ref, sin_ref, qg_ref, kg_ref,
                   hsum_ref, *out_refs):
    x = x_ref[0]
    ms = jnp.mean(x * x, axis=-1, keepdims=True)
    y = x * lax.rsqrt(ms + EPS) * g_ref[...]
    sh = mod_ref[0, :, 0:dm]
    sc = mod_ref[0, :, dm:2 * dm]
    hb = (y * (1.0 + sc) + sh).astype(BF16)
    cos = cos_ref[...]
    sin = sin_ref[...]
    wu = dm // 2
    wkv = wu // GQA_GROUP
    c_k, c_v, c_q, c_g = wu, wu + wkv, wu + 2 * wkv, 2 * wu + 2 * wkv
    u_ref, k_ref, vt_ref = out_refs[:3]
    u = _dot(hb, w_ref[:, 0:c_k])
    if with_q:
        ubuf = out_refs[-1]
        p, gpt, nj = SSM_GROUP, LANES // SSM_GROUP, x.shape[0] // S5_CHUNK
        for ci in range(wu // LANES):
            ubuf[ci] = u[:, ci * LANES:(ci + 1) * LANES]
        for s in range(S5_CHUNK):
            for ci in range(wu // LANES):
                xs = ubuf[ci, pl.ds(s, nj, stride=S5_CHUNK), :].astype(BF16)
                for gi in range(gpt):
                    u_ref[ci * gpt + gi, 0, :, s * p:(s + 1) * p] = xs[:, gi * p:(gi + 1) * p]
    else:
        u_ref[0] = u
    kz = _dot(hb, w_ref[:, c_k:c_v])
    (kr,) = _head_norm_rope(kz, kg_ref[...], hsum_ref[0:wkv, 0:wkv], cos, sin, 1.0)
    k_ref[0] = kr.astype(BF16)
    vt_ref[0] = _dot(hb, w_ref[:, c_v:c_q]).T.astype(BF16)
    if with_q:
        qt_ref, gl_ref = out_refs[3:5]
        qz = _dot(hb, w_ref[:, c_q:c_g])
        qs = _head_norm_rope(qz, qg_ref[...], hsum_ref[...], cos, sin, HEAD_DIM ** -0.5 * LOG2E)
        for ci, qc in enumerate(qs):
            qt_ref[0, ci * LANES:(ci + 1) * LANES, :] = qc.T.astype(BF16)
        gl_ref[0] = _dot(hb, w_ref[:, c_g:]).astype(BF16)


def _inproj(x, mod3, mod_row0, norm_g, w_in_b, cos, sin, qg, kg, hsum, with_q, tm):
    b_, l_, dm = x.shape
    wu = dm // 2
    wkv = wu // GQA_GROUP
    n_in = w_in_b.shape[1]
    outs = [jax.ShapeDtypeStruct((b_, l_, wu), F32),
            jax.ShapeDtypeStruct((b_, l_, wkv), BF16),
            jax.ShapeDtypeStruct((b_, wkv, l_), BF16)]
    ospecs = [pl.BlockSpec((1, tm, wu), lambda b, i: (b, i, 0)),
              pl.BlockSpec((1, tm, wkv), lambda b, i: (b, i, 0)),
              pl.BlockSpec((1, wkv, tm), lambda b, i: (b, 0, i))]
    scratch = []
    if with_q:
        g, rows = wu // SSM_GROUP, S5_CHUNK * SSM_GROUP
        outs[0] = jax.ShapeDtypeStruct((g, b_, l_ // S5_CHUNK, rows), BF16)
        ospecs[0] = pl.BlockSpec((g, 1, tm // S5_CHUNK, rows), lambda b, i: (0, b, i, 0))
        outs += [jax.ShapeDtypeStruct((b_, wu, l_), BF16),
                 jax.ShapeDtypeStruct((b_, l_, 2 * dm), BF16)]
        ospecs += [pl.BlockSpec((1, wu, tm), lambda b, i: (b, 0, i)),
                   pl.BlockSpec((1, tm, 2 * dm), lambda b, i: (b, i, 0))]
        scratch = [pltpu.VMEM((wu // LANES, tm, LANES), F32)]
    if mod_row0 is None:
        mod_map = lambda b, i: (b, 0, 0)
    else:
        mod_map = lambda b, i: (mod_row0, 0, 0)
    return pl.pallas_call(
        functools.partial(_inproj_kernel, with_q, dm),
        out_shape=outs,
        grid=(b_, l_ // tm),
        in_specs=[pl.BlockSpec((1, tm, dm), lambda b, i: (b, i, 0)),
                  pl.BlockSpec((1, 1, N_MOD * dm), mod_map),
                  pl.BlockSpec((1, dm), lambda b, i: (0, 0)),
                  pl.BlockSpec((dm, n_in), lambda b, i: (0, 0)),
                  pl.BlockSpec((tm, LANES), lambda b, i: (i, 0)),
                  pl.BlockSpec((tm, LANES), lambda b, i: (i, 0)),
                  pl.BlockSpec((1, wu), lambda b, i: (0, 0)),
                  pl.BlockSpec((1, wkv), lambda b, i: (0, 0)),
                  pl.BlockSpec((wu, wu), lambda b, i: (0, 0))],
        out_specs=ospecs,
        scratch_shapes=scratch,
        compiler_params=_cparams(("parallel", "parallel")),
        name="inproj_x" if with_q else "inproj_ctx",
    )(x, mod3, norm_g.reshape(1, dm), w_in_b, cos, sin, qg, kg, hsum)


def _cexp_pow(e, lr, li):
    mag = jnp.exp(e * lr)
    ang = e * li
    return mag * jnp.cos(ang), mag * jnp.sin(ang)


def _cmul(ar, ai, br, bi):
    return ar * br - ai * bi, ar * bi + ai * br


def _s5pre_kernel(lamr_r, lami_r, lst_r, lamr_c, lami_c, lst_c, btr_ref, bti_ref, ctr_ref, cti_ref,
                  d_ref, win_ref, wout_ref, m_ref, dec_ref):
    tc = S5_CHUNK
    half = tc // 2
    p = SSM_GROUP
    rows = tc * p
    gsh = SSM_GROUP.bit_length() - 1
    nsh = SSM_STATE.bit_length() - 1
    row_i = lax.broadcasted_iota(jnp.int32, (rows, rows), 0)
    col_i = lax.broadcasted_iota(jnp.int32, (rows, rows), 1)
    lane_g = lax.broadcasted_iota(jnp.int32, (1, LANES), 1) >> nsh
    subl_g = lax.broadcasted_iota(jnp.int32, (LANES, 1), 0) >> nsh
    e_rows = lax.broadcasted_iota(jnp.int32, (S5_POW_ROWS, 1), 0) - half
    e_lanes = lax.broadcasted_iota(jnp.int32, (1, LANES), 1) - half
    sel_e = lax.broadcasted_iota(jnp.int32, (LANES, rows), 0) - half
    sel_t = lax.broadcasted_iota(jnp.int32, (LANES, rows), 1) >> gsh
    sel_p = (lax.broadcasted_iota(jnp.int32, (LANES, rows), 0)
             == (lax.broadcasted_iota(jnp.int32, (LANES, rows), 1) & (p - 1))).astype(BF16)

    def spread_lanes(table, sel):
        hi, lo = _split(table)
        return _dot(hi, sel) + _dot(lo, sel)

    def spread_rows(table, e_of_s):
        return jnp.concatenate([jnp.broadcast_to(table[e_of_s(s) + half:e_of_s(s) + half + 1, :], (p, LANES))
                                for s in range(tc)], axis=0)

    m_acc = [jnp.where(row_i == col_i, d_ref[0, gl], 0.0) for gl in range(2)]
    for dr in range(2):
        step = jnp.exp(lst_r[dr, 0])
        lam_r, lam_i = lamr_r[dr, 0], lami_r[dr, 0]
        lr, li = lam_r * step, lam_i * step
        pw_r, pw_i = _cexp_pow(e_rows.astype(F32), lr, li)
        lb_r, lb_i = pw_r[half + 1:half + 2, :], pw_i[half + 1:half + 2, :]
        den = lam_r * lam_r + lam_i * lam_i
        nr, ni = lb_r - 1.0, lb_i
        cf_r = (nr * lam_r + ni * lam_i) / den
        cf_i = (ni * lam_r - nr * lam_i) / den
        bt_r = jnp.concatenate([btr_ref[dr, 0]] * tc, axis=0)
        bt_i = jnp.concatenate([bti_ref[dr, 0]] * tc, axis=0)
        bb_r, bb_i = _cmul(cf_r, cf_i, bt_r, bt_i)
        e_in = (lambda s: tc - 1 - s) if dr == 0 else (lambda s: s)
        wi_r, wi_i = _cmul(bb_r, bb_i, spread_rows(pw_r, e_in), spread_rows(pw_i, e_in))
        e_a = (lambda s: half - s) if dr == 0 else (lambda s: s - half)
        a_r, a_i = _cmul(bb_r, bb_i, spread_rows(pw_r, e_a), spread_rows(pw_i, e_a))
        dec_ref[0, dr, 0] = pw_r[tc + half:tc + half + 1, :]
        dec_ref[0, dr, 1] = pw_i[tc + half:tc + half + 1, :]
        stepc = jnp.exp(lst_c[dr, 0])
        lrc, lic = lamr_c[dr, 0] * stepc, lami_c[dr, 0] * stepc
        qw_r, qw_i = _cexp_pow(e_lanes.astype(F32), lrc, lic)
        qw_r = jnp.where(e_lanes <= tc, qw_r, 0.0)
        qw_i = jnp.where(e_lanes <= tc, qw_i, 0.0)
        c_r, c_i = spread_lanes(ctr_ref[dr, 0], sel_p), spread_lanes(cti_ref[dr, 0], sel_p)
        sel_b = (sel_e == (sel_t - half if dr == 0 else half - sel_t)).astype(BF16)
        bm_r, bm_i = _cmul(c_r, c_i, spread_lanes(qw_r, sel_b), spread_lanes(qw_i, sel_b))
        sel_o = (sel_e == (sel_t + 1 if dr == 0 else tc - sel_t)).astype(BF16)
        wo_r, wo_i = _cmul(c_r, c_i, spread_lanes(qw_r, sel_o), spread_lanes(qw_i, sel_o))
        if dr == 0:
            mask = (col_i >> gsh) >= (row_i >> gsh)
        else:
            mask = (row_i >> gsh) >= (col_i >> gsh)
        for gl in range(2):
            lsel = lane_g == gl
            ssel = subl_g == gl
            win_ref[0, gl, 2 * dr] = jnp.where(lsel, wi_r, 0.0).astype(BF16)
            win_ref[0, gl, 2 * dr + 1] = jnp.where(lsel, wi_i, 0.0).astype(BF16)
            wout_ref[0, 2 * dr, :, gl * rows:(gl + 1) * rows] = jnp.where(ssel, wo_r, 0.0).astype(BF16)
            wout_ref[0, 2 * dr + 1, :, gl * rows:(gl + 1) * rows] = jnp.where(ssel, -wo_i, 0.0).astype(BF16)
            kmat = (_dot3(jnp.where(lsel, a_r, 0.0), bm_r) - _dot3(jnp.where(lsel, a_i, 0.0), bm_i))
            m_acc[gl] = m_acc[gl] + jnp.where(mask, kmat, 0.0)
    for gl in range(2):
        m_ref[0, gl] = m_acc[gl].astype(BF16)


def _s5_operators(lam_re, lam_im, log_step, b_re, b_im, c_re, c_im, d_skip):
    g = lam_re.shape[1]
    n, p = SSM_STATE, SSM_GROUP
    npair = g // 2
    tc = S5_CHUNK
    rows = tc * p

    def row_form(a):
        return a.reshape(2, npair, 1, 2 * n)

    def col_form(a):
        return a.reshape(2, npair, 2 * n, 1)

    lst = jnp.broadcast_to(log_step[:, :, None], (2, g, n))
    bt = lambda b: b.reshape(2, npair, 2, n, p).transpose(0, 1, 4, 2, 3).reshape(2, npair, p, 2 * n)
    ct = lambda c: jnp.pad(c.reshape(2, npair, 2, p, n).transpose(0, 1, 2, 4, 3).reshape(2, npair, 2 * n, p),
                           ((0, 0), (0, 0), (0, 0), (0, LANES - p)))
    d2 = jnp.broadcast_to(d_skip.reshape(npair, 2, 1, 1, p), (npair, 2, 1, tc, p)).reshape(npair, 2, 1, rows)

    def spec4(shape):
        return pl.BlockSpec((2, 1) + shape, lambda i: (0, i, 0, 0))

    return pl.pallas_call(
        _s5pre_kernel,
        out_shape=[jax.ShapeDtypeStruct((npair, 2, 4, rows, LANES), BF16),
                   jax.ShapeDtypeStruct((npair, 4, LANES, 2 * rows), BF16),
                   jax.ShapeDtypeStruct((npair, 2, rows, rows), BF16),
                   jax.ShapeDtypeStruct((npair, 2, 2, 1, LANES), F32)],
        grid=(npair,),
        in_specs=[spec4((1, LANES)), spec4((1, LANES)), spec4((1, LANES)),
                  spec4((LANES, 1)), spec4((LANES, 1)), spec4((LANES, 1)),
                  spec4((p, LANES)), spec4((p, LANES)),
                  spec4((LANES, LANES)), spec4((LANES, LANES)),
                  pl.BlockSpec((1, 2, 1, rows), lambda i: (i, 0, 0, 0))],
        out_specs=[pl.BlockSpec((1, 2, 4, rows, LANES), lambda i: (i, 0, 0, 0, 0)),
                   pl.BlockSpec((1, 4, LANES, 2 * rows), lambda i: (i, 0, 0, 0)),
                   pl.BlockSpec((1, 2, rows, rows), lambda i: (i, 0, 0, 0)),
                   pl.BlockSpec((1, 2, 2, 1, LANES), lambda i: (i, 0, 0, 0, 0))],
        compiler_params=_cparams(("parallel",)),
        name="s5pre",
    )(row_form(lam_re), row_form(lam_im), row_form(lst),
      col_form(lam_re), col_form(lam_im), col_form(lst),
      bt(b_re), bt(b_im), ct(c_re), ct(c_im), d2)


def _s5state_kernel(u_ref, win_ref, *s_refs):
    w0 = jnp.concatenate([win_ref[0, 0, kind] for kind in range(4)], axis=1)
    w1 = jnp.concatenate([win_ref[0, 1, kind] for kind in range(4)], axis=1)
    for b in range(u_ref.shape[1]):
        s = _dot(u_ref[0, b], w0) + _dot(u_ref[1, b], w1)
        for kind in range(4):
            s_refs[kind][:, b * LANES:(b + 1) * LANES] = s[:, kind * LANES:(kind + 1) * LANES]


def _s5_states(uf, win):
    g, b_, j, rows = uf.shape
    npair = g // 2
    out = jax.ShapeDtypeStruct((j, npair * b_ * LANES), F32)
    ospec = pl.BlockSpec((j, b_ * LANES), lambda pr: (0, pr))
    return pl.pallas_call(
        _s5state_kernel,
        out_shape=[out] * 4,
        grid=(npair,),
        in_specs=[pl.BlockSpec((2, b_, j, rows), lambda pr: (pr, 0, 0, 0)),
                  pl.BlockSpec((1, 2, 4, rows, LANES), lambda pr: (pr, 0, 0, 0, 0))],
        out_specs=[ospec] * 4,
        compiler_params=_cparams(("parallel",)),
        name="s5state",
    )(uf, win)


def _s5scan_kernel(cfr, cfi, cbr, cbi, sfr, sfi, sbr, sbi, dec_ref, hfr, hfi, hbr, hbi):
    jc, jl, w = cfr.shape[0], sfr.shape[0], sfr.shape[1]
    zero = jnp.zeros((1, w), F32)

    def run(sr, si, outs, ar, ai, n, reverse, carry):
        def body(i, hc):
            row = (n - 1 - i) if reverse else i
            hr, hi = hc
            if outs is not None:
                outs[0][pl.ds(row, 1), :] = hr
                outs[1][pl.ds(row, 1), :] = hi
            nr = ar * hr - ai * hi + sr[pl.ds(row, 1), :]
            ni = ar * hi + ai * hr + si[pl.ds(row, 1), :]
            return nr, ni
        return lax.fori_loop(0, n, body, carry)

    afr, afi = dec_ref[0, 0], dec_ref[0, 1]
    abr, abi = dec_ref[1, 0], dec_ref[1, 1]
    c = run(cfr, cfi, None, afr, afi, jc, False, (zero, zero))
    run(sfr, sfi, (hfr, hfi), afr, afi, jl, False, c)
    c = run(cbr, cbi, None, abr, abi, jc, True, (zero, zero))
    run(sbr, sbi, (hbr, hbi), abr, abi, jl, True, c)


def _s5_scan(ctx_states, states, dec_cols):
    jl, wtot = states[0].shape
    jc = ctx_states[0].shape[0]
    tw = WIDE_BLOCK
    spec = pl.BlockSpec((jl, tw), lambda i: (0, i))
    cspec = pl.BlockSpec((jc, tw), lambda i: (0, i))
    return pl.pallas_call(
        _s5scan_kernel,
        out_shape=[jax.ShapeDtypeStruct((jl, wtot), F32)] * 4,
        grid=(wtot // tw,),
        in_specs=[cspec] * 4 + [spec] * 4 + [pl.BlockSpec((2, 2, 1, tw), lambda i: (0, 0, 0, i))],
        out_specs=[spec] * 4,
        compiler_params=_cparams(("parallel",)),
        name="s5scan",
    )(*ctx_states, *states, dec_cols)


def _s5out_kernel(ppt, u_ref, *refs):
    h_refs, (wout_ref, m_ref, o_ref, yf) = refs[:4 * ppt], refs[4 * ppt:]
    tc, p = S5_CHUNK, SSM_GROUP
    rows = u_ref.shape[3]
    j = u_ref.shape[2]
    for pr in range(ppt):
        y2 = None
        for kind in range(4):
            t = _dot(h_refs[4 * pr + kind][...].astype(BF16), wout_ref[pr, kind])
            y2 = t if y2 is None else y2 + t
        for gl in range(2):
            y = y2[:, gl * rows:(gl + 1) * rows] + _dot(u_ref[2 * pr + gl, 0], m_ref[pr, gl])
            yf[2 * pr + gl] = y.astype(BF16)
    for t in range(tc):
        row = jnp.concatenate([yf[gi, :, t * p:(t + 1) * p] for gi in range(2 * ppt)], axis=1)
        o_ref[0, pl.ds(t, j, stride=tc), :] = row.astype(F32)


def _s5_outputs(uf, hprev, wout, m):
    g, b_, j, rows = uf.shape
    gpt = LANES // SSM_GROUP
    ppt = gpt // 2
    hspecs = [pl.BlockSpec((j, LANES), lambda c, b, pr=pr: (0, (c * ppt + pr) * b_ + b))
              for pr in range(ppt) for _ in range(4)]
    hargs = [hprev[kind] for _ in range(ppt) for kind in range(4)]
    return pl.pallas_call(
        functools.partial(_s5out_kernel, ppt),
        out_shape=jax.ShapeDtypeStruct((b_, j * S5_CHUNK, g * SSM_GROUP), F32),
        grid=(g // gpt, b_),
        in_specs=[pl.BlockSpec((gpt, 1, j, rows), lambda c, b: (c, b, 0, 0))] + hspecs +
                 [pl.BlockSpec((ppt, 4, LANES, 2 * rows), lambda c, b: (c, 0, 0, 0)),
                  pl.BlockSpec((ppt, 2, rows, rows), lambda c, b: (c, 0, 0, 0))],
        out_specs=pl.BlockSpec((1, j * S5_CHUNK, LANES), lambda c, b: (b, 0, c)),
        scratch_shapes=[pltpu.VMEM((gpt, j, rows), BF16)],
        compiler_params=_cparams(("parallel", "parallel")),
        name="s5out",
    )(uf, *hargs, wout, m)


def _chunks_per_step(j):
    return max(n for n in (64, 32, 16, 8) if j % n == 0)


def _s5_regroup_kernel(nj, u_ref, o_ref):
    tc, p = S5_CHUNK, SSM_GROUP
    for s in range(tc):
        xs = u_ref[0, pl.ds(s, nj, stride=tc), :].astype(o_ref.dtype)
        for gi in range(LANES // p):
            o_ref[gi, 0, :, s * p:(s + 1) * p] = xs[:, gi * p:(gi + 1) * p]


def _s5_regroup(u):
    b_, l_, w = u.shape
    tc, p = S5_CHUNK, SSM_GROUP
    gpt = LANES // p
    j = l_ // tc
    nj = _chunks_per_step(j)
    return pl.pallas_call(
        functools.partial(_s5_regroup_kernel, nj),
        out_shape=jax.ShapeDtypeStruct((w // p, b_, j, tc * p), BF16),
        grid=(b_, j // nj, w // LANES),
        in_specs=[pl.BlockSpec((1, tc * nj, LANES), lambda b, i, c: (b, i, c))],
        out_specs=pl.BlockSpec((gpt, 1, nj, tc * p), lambda b, i, c: (c, b, i, 0)),
        compiler_params=_cparams(("parallel", "parallel", "parallel")),
        name="s5regroup",
    )(u)


def _s5_mixer(uf, u_c, ops):
    win, wout, m, dec = ops
    g, b_ = uf.shape[0], uf.shape[1]
    npair = g // 2
    states = _s5_states(uf, win)
    ctx_states = _s5_states(_s5_regroup(u_c), win)
    dec_cols = jnp.broadcast_to(dec.transpose(1, 2, 3, 0, 4)[:, :, :, :, None], (2, 2, 1, npair, b_, LANES))
    dec_cols = dec_cols.reshape(2, 2, 1, npair * b_ * LANES)
    hprev = _s5_scan(ctx_states, states, dec_cols)
    return _s5_outputs(uf, hprev, wout, m)


def _attn_kernel(nkv, qt_ref, k_ref, vx_ref, o_ref, s0_ref, s1_ref):
    hd = HEAD_DIM
    kvh = pl.program_id(1)
    tq = qt_ref.shape[2] // ATTN_SUB_BLOCKS
    gq = qt_ref.shape[1] // hd
    kw = k_ref.shape[3]
    vr = vx_ref.shape[3]
    own = (lax.broadcasted_iota(jnp.int32, (kw, tq), 0) >> (hd.bit_length() - 1)) == kvh
    bufs = (s0_ref, s1_ref)
    for sb in range(ATTN_SUB_BLOCKS):
        qp = []
        for g in range(gq):
            qg = qt_ref[0, g * hd:(g + 1) * hd, sb * tq:(sb + 1) * tq].astype(F32)
            q2 = jnp.concatenate([qg] * (kw // hd), axis=0)
            qp.append(jnp.where(own, q2, 0.0).astype(BF16))
        qp = jnp.concatenate(qp, axis=1)
        m0 = jnp.full((1, gq * tq), -jnp.inf, F32)
        acc0 = jnp.zeros((vr, gq * tq), F32)

        def scores(jb, s_ref, qp=qp):
            s = _dot(k_ref[0, jb], qp)
            s_ref[...] = s
            return jnp.max(s, axis=0, keepdims=True)

        def consume(jb, s_ref, cmax, m_prev, acc):
            m_new = jnp.maximum(m_prev, cmax)
            alpha = jnp.exp2(m_prev - m_new)
            p = jnp.exp2(s_ref[...] - m_new).astype(BF16)
            return m_new, alpha * acc + _dot(vx_ref[0, 0, jb], p)

        def steps(j0, count, m, acc, cmax, scores=scores, consume=consume):
            for u in range(count):
                cnext = scores(j0 + u + 1, bufs[(u + 1) % 2])
                m, acc = consume(j0 + u, bufs[u % 2], cmax, m, acc)
                cmax = cnext
            return m, acc, cmax

        unroll = ATTN_STEPS_PER_TRIP
        trips = (nkv - 1) // unroll
        carry = lax.fori_loop(0, trips, lambda i, c, steps=steps: steps(unroll * i, unroll, *c),
                              (m0, acc0, scores(0, s0_ref)))
        m, acc, cmax = steps(trips * unroll, (nkv - 1) - trips * unroll, *carry)
        m, acc = consume(nkv - 1, bufs[(nkv - 1) % 2], cmax, m, acc)
        o = acc[0:hd] / acc[hd:hd + 1]
        ot = jnp.concatenate([o[:, g * tq:(g + 1) * tq] for g in range(gq)], axis=0)
        o_ref[0, sb * tq:(sb + 1) * tq, :] = ot.T.astype(BF16)


def _attention(qt, k4, vx, tq):
    b_, wq, l_ = qt.shape
    _, kvh, nkv, vr, tk = vx.shape
    kw = k4.shape[3]
    gw = wq // kvh
    return pl.pallas_call(
        functools.partial(_attn_kernel, nkv),
        out_shape=jax.ShapeDtypeStruct((b_, l_, wq), BF16),
        grid=(b_, kvh, l_ // tq),
        in_specs=[pl.BlockSpec((1, gw, tq), lambda b, h, i: (b, h, i)),
                  pl.BlockSpec((1, nkv, tk, kw), lambda b, h, i: (b, 0, 0, 0)),
                  pl.BlockSpec((1, 1, nkv, vr, tk), lambda b, h, i: (b, h, 0, 0, 0))],
        out_specs=pl.BlockSpec((1, tq, gw), lambda b, h, i: (b, i, h)),
        scratch_shapes=[pltpu.VMEM((tk, (gw // HEAD_DIM) * tq // ATTN_SUB_BLOCKS), F32)] * 2,
        compiler_params=_cparams(("parallel", "parallel", "parallel")),
        name="attn",
    )(qt, k4, vx)


def _merge_kernel(dm, ys_ref, oa_ref, gl_ref, x_ref, mod_ref, wglu_ref, bglu_ref, wso_ref, wao_ref,
                  wout_ref, n2g_ref, rw_ref, rb_ref, utri_ref, xmix_ref, h2_ref, idx_ref, gate_ref, rank_ref,
                  cnt_out_ref, cnt_ref):
    tm = x_ref.shape[1]
    parts = [pl.ds(r * (tm // MERGE_ROW_PARTS), tm // MERGE_ROW_PARTS) for r in range(MERGE_ROW_PARTS)]
    g1 = mod_ref[0, :, 2 * dm:3 * dm]
    sh2 = mod_ref[0, :, 3 * dm:4 * dm]
    sc2 = mod_ref[0, :, 4 * dm:5 * dm]
    s = [jax.nn.gelu(ys_ref[0, r, :]) for r in parts]
    z = [_dot(sr.astype(BF16), wglu_ref[...]) + bglu_ref[...] for sr in s]
    s = [sr * jax.nn.sigmoid(zr) for sr, zr in zip(s, z)]
    a = [_dot(sr.astype(BF16), wso_ref[...]) for sr in s]
    bq = [_dot(oa_ref[0, r, :], wao_ref[...]) for r in parts]
    merged = []
    for r, ar, br in zip(parts, a, bq):
        gts = jax.nn.sigmoid(gl_ref[0, r, :].astype(F32))
        merged.append((gts[:, 0:dm] * ar + gts[:, dm:2 * dm] * br).astype(BF16))
    out = [_dot(mr, wout_ref[...]) for mr in merged]
    h2_parts = []
    for r, outr in zip(parts, out):
        xm = x_ref[0, r, :] + g1 * outr
        xmix_ref[0, r, :] = xm
        ms = jnp.mean(xm * xm, axis=-1, keepdims=True)
        h2r = xm * lax.rsqrt(ms + EPS) * n2g_ref[...] * (1.0 + sc2) + sh2
        h2_ref[0, r, :] = _pack2(h2r[:, 0:dm // 2], h2r[:, dm // 2:dm])
        h2_parts.append(h2r)
    logits = jnp.concatenate([_dot3(h2r, rw_ref[...]) for h2r in h2_parts], axis=0) + rb_ref[...]
    nep = cnt_ref.shape[0]
    l = logits.T[0:nep, :]
    sub = lax.broadcasted_iota(jnp.int32, l.shape, 0).astype(F32)
    neg = jnp.float32(-jnp.inf)
    vals, idxs = [], []
    for _ in range(TOP_K):
        mx = jnp.max(l, axis=0, keepdims=True)
        ix = jnp.min(jnp.where(l == mx, sub, float(nep)), axis=0, keepdims=True)
        vals.append(mx)
        idxs.append(ix)
        l = jnp.where(sub == ix, neg, l)
    es = [jnp.exp(v - vals[0]) for v in vals]
    den = es[0] + es[1] + es[2] + es[3]
    @pl.when((pl.program_id(0) == 0) & (pl.program_id(1) == 0))
    def _():
        cnt_ref[...] = jnp.zeros(cnt_ref.shape, F32)

    onehot = jnp.zeros(l.shape, F32)
    for k in range(TOP_K):
        onehot = onehot + jnp.where(sub == idxs[k], 1.0, 0.0)
    prior = _dot(onehot.astype(BF16), utri_ref[...]) + cnt_ref[...]
    cnt_ref[...] = cnt_ref[...] + jnp.sum(onehot, axis=1, keepdims=True)
    cnt_out_ref[...] = cnt_ref[...]
    krow = lax.broadcasted_iota(jnp.int32, (8, l.shape[1]), 0)
    idx_out = jnp.zeros(krow.shape, F32)
    gate_out = jnp.zeros(krow.shape, F32)
    rank_out = jnp.zeros(krow.shape, F32)
    for k in range(TOP_K):
        rk = jnp.sum(jnp.where(sub == idxs[k], prior, 0.0), axis=0, keepdims=True)
        idx_out = jnp.where(krow == k, idxs[k], idx_out)
        gate_out = jnp.where(krow == k, es[k] / den, gate_out)
        rank_out = jnp.where(krow == k, rk, rank_out)
    idx_ref[0] = idx_out.astype(jnp.int32)
    gate_ref[0] = jnp.concatenate([gate_out, jnp.zeros((LANES - 8, gate_out.shape[1]), F32)], axis=0).T
    rank_ref[0] = rank_out.astype(jnp.int32)


def _merge(y_ssm, o_attn, glog, x, mod3, w_glu, b_glu, w_ssm_out, w_attn_out, w_out, norm2_g,
           rw_pad, rb_pad, ne, tm):
    b_, l_, dm = x.shape
    wu = dm // 2
    tok = lambda w: pl.BlockSpec((1, tm, w), lambda b, i: (b, i, 0))
    full = lambda r, c: pl.BlockSpec((r, c), lambda b, i: (0, 0))
    utri = (jnp.arange(tm)[:, None] < jnp.arange(tm)[None, :]).astype(BF16)
    nep = -(-ne // 8) * 8
    kmaj = pl.BlockSpec((1, 8, tm), lambda b, i: (b, 0, i))
    return pl.pallas_call(
        functools.partial(_merge_kernel, dm),
        out_shape=[jax.ShapeDtypeStruct((b_, l_, dm), F32),
                   jax.ShapeDtypeStruct((b_, l_, dm // 2), jnp.uint32),
                   jax.ShapeDtypeStruct((b_, 8, l_), jnp.int32),
                   jax.ShapeDtypeStruct((b_, l_, LANES), F32),
                   jax.ShapeDtypeStruct((b_, 8, l_), jnp.int32),
                   jax.ShapeDtypeStruct((nep, 1), F32)],
        grid=(b_, l_ // tm),
        in_specs=[tok(wu), tok(wu), tok(2 * dm), tok(dm),
                  pl.BlockSpec((1, 1, N_MOD * dm), lambda b, i: (b, 0, 0)),
                  full(wu, wu), full(1, wu), full(wu, dm), full(wu, dm), full(dm, dm),
                  full(1, dm), full(dm, LANES), full(1, LANES), full(tm, tm)],
        out_specs=[tok(dm), tok(dm // 2), kmaj, tok(LANES), kmaj, full(nep, 1)],
        scratch_shapes=[pltpu.VMEM((nep, 1), F32)],
        compiler_params=_cparams(("arbitrary", "arbitrary")),
        name="merge",
    )(y_ssm, o_attn, glog, x, mod3, w_glu, b_glu, w_ssm_out, w_attn_out, w_out, norm2_g, rw_pad, rb_pad, utri)


def _expert_kernel(de, be_ref, nv_ref, x_ref, wgu_ref, bgu_ref, wd_ref, bd_ref, y_ref, wgu_b, wd_b):
    i = pl.program_id(0)
    n_valid = nv_ref[i]
    used = n_valid > 0
    new_expert = (i == 0) | (be_ref[i] != be_ref[jnp.maximum(i - 1, 0)])

    @pl.when(used & new_expert)
    def _():
        wgu_b[...] = wgu_ref[0].astype(BF16)
        wd_b[...] = wd_ref[0].astype(BF16)

    @pl.when(used)
    def _():
        row = lax.broadcasted_iota(jnp.int32, (x_ref.shape[0], 1), 0)
        x_lo, x_hi = _unpack2(jnp.where(row < n_valid, x_ref[...], jnp.uint32(0)))
        xb = jnp.concatenate([x_lo, x_hi], axis=1).astype(BF16)
        gu = _dot(xb, wgu_b[...]) + bgu_ref[0]
        gate = jnp.minimum(gu[:, 0:de], SWIGLU_LIMIT)
        up = jnp.clip(gu[:, de:2 * de], -SWIGLU_LIMIT, SWIGLU_LIMIT)
        act = (up + 1.0) * (gate * jax.nn.sigmoid(SWIGLU_ALPHA * gate))
        y = _dot(act.astype(BF16), wd_b[...]) + bd_ref[0]
        half = y.shape[1] // 2
        y_ref[...] = _pack2(y[:, 0:half], y[:, half:])

    @pl.when(jnp.logical_not(used))
    def _():
        y_ref[...] = jnp.zeros(y_ref.shape, y_ref.dtype)


def _experts(xb, block_expert, block_valid, w_gate_up, b_gate_up, w_down, b_down):
    n_slots, dmh = xb.shape
    dm = 2 * dmh
    ne, _, de2 = w_gate_up.shape
    de = de2 // 2
    nb = n_slots // EXPERT_BLOCK
    return pl.pallas_call(
        functools.partial(_expert_kernel, de),
        out_shape=jax.ShapeDtypeStruct((n_slots, dmh), jnp.uint32),
        grid_spec=pltpu.PrefetchScalarGridSpec(
            num_scalar_prefetch=2,
            grid=(nb,),
            in_specs=[pl.BlockSpec((EXPERT_BLOCK, dmh), lambda i, be, nu: (i, 0)),
                      pl.BlockSpec((1, dm, de2), lambda i, be, nu: (be[i], 0, 0)),
                      pl.BlockSpec((1, 1, de2), lambda i, be, nu: (be[i], 0, 0)),
                      pl.BlockSpec((1, de, dm), lambda i, be, nu: (be[i], 0, 0)),
                      pl.BlockSpec((1, 1, dm), lambda i, be, nu: (be[i], 0, 0))],
            out_specs=pl.BlockSpec((EXPERT_BLOCK, dmh), lambda i, be, nu: (i, 0)),
            scratch_shapes=[pltpu.VMEM((dm, de2), BF16), pltpu.VMEM((de, dm), BF16)],
        ),
        compiler_params=_cparams(("arbitrary",)),
        name="expert",
    )(block_expert, block_valid, xb, w_gate_up, b_gate_up.reshape(ne, 1, de2), w_down, b_down.reshape(ne, 1, dm))


def _sc_gather_rows(table, idx):
    n = idx.shape[0]
    w = table.shape[1]
    mesh = plsc.VectorSubcoreMesh(core_axis_name="c", subcore_axis_name="s")
    nc, nw = mesh.num_cores, mesh.num_cores * mesh.num_subcores
    win = SC_GATHER_WINDOW
    assert n % (win * nw) == 0
    per_worker = n // nw

    @functools.partial(
        pl.kernel, out_type=jax.ShapeDtypeStruct((n, w), table.dtype), mesh=mesh,
        scratch_types=[pltpu.VMEM((win,), jnp.int32), pltpu.VMEM((win, w), table.dtype),
                       pltpu.SemaphoreType.DMA])
    def gather(x_hbm, i_hbm, o_hbm, idx_v, rows_v, sem):
        base = (lax.axis_index("s") * nc + lax.axis_index("c")) * per_worker

        @pl.loop(0, per_worker // win)
        def _(j):
            off = base + j * win
            pltpu.sync_copy(i_hbm.at[pl.ds(off, win)], idx_v)
            pltpu.async_copy(x_hbm.at[idx_v], rows_v, sem).wait()
            pltpu.sync_copy(rows_v, o_hbm.at[pl.ds(off, win)])

    return gather(table, idx)


def _sc_scatter_rows(rows, dest, n_out):
    t_, w = rows.shape
    kk = dest.shape[0] // t_
    mesh = plsc.VectorSubcoreMesh(core_axis_name="c", subcore_axis_name="s")
    nc, nw = mesh.num_cores, mesh.num_cores * mesh.num_subcores
    win = SC_GATHER_WINDOW
    assert t_ % (win * nw) == 0
    per_worker = t_ // nw

    @functools.partial(
        pl.kernel, out_type=jax.ShapeDtypeStruct((n_out, w), rows.dtype), mesh=mesh,
        scratch_types=[pltpu.VMEM((win,), jnp.int32)] * kk + [pltpu.VMEM((win, w), rows.dtype),
                                                              pltpu.SemaphoreType.DMA])
    def scatter(x_hbm, d_hbm, o_hbm, *scratch):
        idx_vs, rows_v, sem = scratch[:kk], scratch[kk], scratch[kk + 1]
        base = (lax.axis_index("s") * nc + lax.axis_index("c")) * per_worker

        @pl.loop(0, per_worker // win)
        def _(j):
            off = base + j * win
            pltpu.sync_copy(x_hbm.at[pl.ds(off, win)], rows_v)
            for k in range(kk):
                pltpu.sync_copy(d_hbm.at[pl.ds(k * t_ + off, win)], idx_vs[k])
            copies = [pltpu.async_copy(rows_v, o_hbm.at[idx_vs[k]], sem) for k in range(kk)]
            for cp in copies:
                cp.wait()

    return scatter(rows, dest)


def _final_kernel(dm, yg_ref, gate_ref, xm_ref, mod_ref, fg_ref, *rest):
    o_ref = rest[-1]
    gates = gate_ref[0]
    f_lo = f_hi = None
    for k in range(TOP_K):
        y_lo, y_hi = _unpack2(yg_ref[k, 0])
        gk = gates[:, k:k + 1]
        f_lo = gk * y_lo if f_lo is None else f_lo + gk * y_lo
        f_hi = gk * y_hi if f_hi is None else f_hi + gk * y_hi
    f = jnp.concatenate([f_lo, f_hi], axis=1)
    g2 = mod_ref[0, :, 5 * dm:6 * dm]
    xo = xm_ref[0] + g2 * f
    ms = jnp.mean(xo * xo, axis=-1, keepdims=True)
    o_ref[0] = xo * lax.rsqrt(ms + EPS) * fg_ref[...]


def _final(yg, gates, x_mix, mod3, final_g, tm, sample, prev):
    b_, l_, dm = x_mix.shape
    in_specs = [pl.BlockSpec((TOP_K, 1, tm, dm // 2), lambda i: (0, 0, i, 0)),
                pl.BlockSpec((1, tm, LANES), lambda i: (sample, i, 0)),
                pl.BlockSpec((1, tm, dm), lambda i: (sample, i, 0)),
                pl.BlockSpec((1, 1, N_MOD * dm), lambda i: (sample, 0, 0)),
                pl.BlockSpec((1, dm), lambda i: (0, 0))]
    args = [yg, gates, x_mix, mod3, final_g.reshape(1, dm)]
    aliases = {}
    if prev is not None:
        in_specs.append(pl.BlockSpec(memory_space=pl.ANY))
        args.append(prev)
        aliases = {len(args) - 1: 0}
    return pl.pallas_call(
        functools.partial(_final_kernel, dm),
        out_shape=jax.ShapeDtypeStruct((b_, l_, dm), F32),
        grid=(l_ // tm,),
        in_specs=in_specs,
        out_specs=pl.BlockSpec((1, tm, dm), lambda i: (sample, i, 0)),
        input_output_aliases=aliases,
        compiler_params=_cparams(("parallel",)),
        name="final",
    )(*args)


def _rope_tables(l_):
    half = HEAD_DIM // 2
    inv_freq = ROPE_THETA ** (-jnp.arange(0, half, 2, dtype=F32) / half)
    t = jnp.arange(l_, dtype=jnp.int32)
    row_id, col_id = t // GRID_W, t % GRID_W
    ang = jnp.concatenate([row_id.astype(F32)[:, None] * inv_freq,
                           col_id.astype(F32)[:, None] * inv_freq], axis=-1)
    cos = jnp.repeat(jnp.cos(ang), 2, axis=-1)
    sin = jnp.repeat(jnp.sin(ang), 2, axis=-1)
    sign = jnp.tile(jnp.array([-1.0, 1.0], F32), HEAD_DIM // 2)
    reps = LANES // HEAD_DIM
    return jnp.tile(cos, (1, reps)), jnp.tile(sin * sign, (1, reps))


def _slot_kernel(ne, fs_ref, idx_ref, rank_ref, o_ref):
    idx = idx_ref[0]
    slot = rank_ref[0]
    for e in range(ne):
        slot = slot + jnp.where(idx == e, fs_ref[e], 0)
    o_ref[...] = slot


def _slots(idx, rank, first_slot):
    b_, kr, l_ = idx.shape
    ne = first_slot.shape[0]
    return pl.pallas_call(
        functools.partial(_slot_kernel, ne),
        out_shape=jax.ShapeDtypeStruct((kr, b_ * l_), jnp.int32),
        grid_spec=pltpu.PrefetchScalarGridSpec(
            num_scalar_prefetch=1,
            grid=(b_,),
            in_specs=[pl.BlockSpec((1, kr, l_), lambda b, fs: (b, 0, 0)),
                      pl.BlockSpec((1, kr, l_), lambda b, fs: (b, 0, 0))],
            out_specs=pl.BlockSpec((kr, l_), lambda b, fs: (0, b)),
        ),
        compiler_params=_cparams(("parallel",)),
        name="slots",
    )(first_slot, idx, rank)


def _routing(idx, rank, counts, t_, ne):
    m_ = t_ * TOP_K
    blk = EXPERT_BLOCK
    nb = (m_ + ne * (blk - 1)) // blk
    nb = -(-nb // SLOT_BLOCK_MULTIPLE) * SLOT_BLOCK_MULTIPLE
    blk_counts = (counts + blk - 1) // blk
    blk_ends = jnp.cumsum(blk_counts)
    first_slot = (blk_ends - blk_counts) * blk
    dest = _slots(idx, rank, first_slot.astype(jnp.int32))[:TOP_K]
    blocks = jnp.arange(nb, dtype=jnp.int32)
    block_expert = jnp.minimum(jnp.sum(blocks[:, None] >= blk_ends[None, :], axis=1), ne - 1)
    in_expert = blocks * blk - first_slot[block_expert]
    block_valid = jnp.clip(counts[block_expert] - in_expert, 0, blk)
    block_valid = jnp.where(blocks < blk_ends[-1], block_valid, 0)
    return dest.astype(jnp.int32), block_expert.astype(jnp.int32), block_valid.astype(jnp.int32), nb * blk


def kernel(x, c, ctx, c_ctx, w_mod, b_mod, norm1_g, norm2_g, w_in, s5_lam_re, s5_lam_im, s5_log_step,
           s5_b_re, s5_b_im, s5_c_re, s5_c_im, s5_d, w_glu, b_glu, w_ssm_out, q_norm_g, k_norm_g,
           w_attn_out, w_out, router_w, router_b, w_gate_up, b_gate_up, w_down, b_down, final_norm_g):
    b_, l_, dm = x.shape
    lc = ctx.shape[1]
    depth = w_mod.shape[0]
    assert depth == 1, "single-layer block"
    assert b_ <= 7 and l_ % TOKEN_BLOCK == 0 and l_ % ATTN_Q_BLOCK == 0 and lc % S5_CHUNK == 0
    wu = dm // 2
    wkv = wu // GQA_GROUP
    kvh = wkv // HEAD_DIM
    ne = router_w.shape[-1]
    i = 0

    c8 = jnp.zeros((8, dm), F32).at[:b_].set(c).at[b_].set(c_ctx)
    mod3 = _modulation(c8, w_mod[i], b_mod[i]).reshape(8, 1, N_MOD * dm)

    w_in_b = w_in[i].astype(BF16)
    cos, sin = _rope_tables(l_)
    hsum = jnp.kron(jnp.eye(wu // HEAD_DIM, dtype=F32), jnp.ones((HEAD_DIM, HEAD_DIM), F32)).astype(BF16)
    qg = jnp.tile(q_norm_g[i], wu // HEAD_DIM).reshape(1, wu)
    kg = jnp.tile(k_norm_g[i], wkv // HEAD_DIM).reshape(1, wkv)
    uf, k, vt, qt, glog = _inproj(x, mod3, None, norm1_g[i], w_in_b, cos, sin, qg, kg, hsum, True, TOKEN_BLOCK)
    ones_t = jnp.ones((lc, LANES), F32)
    u_c, k_c, vt_c = _inproj(ctx, mod3, b_, norm1_g[i], w_in_b[:, :wu + 2 * wkv], ones_t, 0.0 * ones_t,
                             qg, kg, hsum, False, lc)

    ops = _s5_operators(s5_lam_re[i], s5_lam_im[i], s5_log_step[i], s5_b_re[i], s5_b_im[i],
                        s5_c_re[i], s5_c_im[i], s5_d[i])
    y_ssm = _s5_mixer(uf, u_c, ops)

    s_all = lc + l_
    tk = max(t for t in range(LANES, ATTN_KEY_CHUNK + 1, LANES) if s_all % t == 0)
    nkv = s_all // tk
    k4 = jnp.concatenate([k_c, k], axis=1).reshape(b_, nkv, tk, wkv)
    vt_all = jnp.concatenate([vt_c, vt], axis=2).reshape(b_, kvh, HEAD_DIM, s_all)
    vx = jnp.concatenate([vt_all, jnp.ones((b_, kvh, 1, s_all), BF16),
                          jnp.zeros((b_, kvh, ATTN_V_ROWS - HEAD_DIM - 1, s_all), BF16)], axis=2)
    vx = vx.reshape(b_, kvh, ATTN_V_ROWS, nkv, tk).transpose(0, 1, 3, 2, 4)
    o_attn = _attention(qt, k4, vx, ATTN_Q_BLOCK)

    rw_pad = jnp.zeros((dm, LANES), F32).at[:, :ne].set(router_w[i])
    rb_pad = jnp.full((1, LANES), -jnp.inf, F32).at[0, :ne].set(router_b[i])
    x_mix, h2, idx, gates, rank, counts = _merge(
        y_ssm, o_attn, glog, x, mod3, w_glu[i].astype(BF16), b_glu[i].reshape(1, wu),
        w_ssm_out[i].astype(BF16), w_attn_out[i].astype(BF16), w_out[i].astype(BF16),
        norm2_g[i].reshape(1, dm), rw_pad, rb_pad, ne, TOKEN_BLOCK)

    t_ = b_ * l_
    dest_kmajor, block_expert, block_valid, n_slots = _routing(idx, rank, counts[:ne, 0].astype(jnp.int32), t_, ne)
    xb = _sc_scatter_rows(h2.reshape(t_, dm // 2), dest_kmajor.reshape(-1), n_slots)
    yb = _experts(xb, block_expert, block_valid, w_gate_up[i], b_gate_up[i], w_down[i], b_down[i])
    out = None
    for b in range(b_):
        yg = _sc_gather_rows(yb, dest_kmajor[:, b * l_:(b + 1) * l_].reshape(-1)).reshape(TOP_K, 1, l_, dm // 2)
        out = _final(yg, gates, x_mix, mod3, final_norm_g, TOKEN_BLOCK, b, out)
    return out
```
